```python
import jax, jax.numpy as jnp
from jax import lax
import numpy as np

D_MODEL = 1024
BATCH = 16
SEQ = 256
DEPTH = 1
DEC_BATCH = 8
DEC_SEQ = 2048
PAST_LEN = 256

GRID_W = 64
D_MIX = D_MODEL
D_A = D_MIX // 2
HEAD_A = 64
H_A = D_A // HEAD_A
D_B = D_MIX - D_A
BLK_B = 64
NB_B = D_B // BLK_B
R_W = 64
R_A = 64
R_G = 128
CONV_W = 4
CONV_PAD_L = 2
LRU_C = 8.0
D_FF = 4 * D_MODEL
N_DIR = 2
EPS = 1e-6
LNX_EPS = 64e-5
SPLIT_SIZES = (D_A, D_A, D_A, N_DIR * R_W, N_DIR * R_A, R_G, D_B, D_B)
D_IN = sum(SPLIT_SIZES)

kernel_name = "hymba_rwkv7_rglru_diffusion_step"


def rms_norm(x, g):
    xf = x.astype(jnp.float32)
    y = xf * lax.rsqrt(jnp.mean(xf * xf, axis=-1, keepdims=True) + EPS)
    return (y * g.astype(jnp.float32)).astype(x.dtype)


def sincos_1d(pos, dim):
    omega = 1.0 / (10000.0 ** (jnp.arange(dim // 2, dtype=jnp.float32) / (dim // 2)))
    ang = pos.astype(jnp.float32)[:, None] * omega[None, :]
    return jnp.concatenate([jnp.sin(ang), jnp.cos(ang)], axis=-1)


def grid_pos_embed(n_tokens):
    rows = n_tokens // GRID_W
    half = D_MODEL // 2
    e_row = sincos_1d(jnp.arange(rows), half)
    e_col = sincos_1d(jnp.arange(GRID_W), half)
    emb = jnp.concatenate([jnp.broadcast_to(e_row[:, None, :], (rows, GRID_W, half)),
                           jnp.broadcast_to(e_col[None, :, :], (rows, GRID_W, half))], axis=-1)
    return emb.reshape(rows * GRID_W, D_MODEL)


def split_proj(z):
    idx = np.cumsum(np.array(SPLIT_SIZES))[:-1].tolist()
    return jnp.split(z, idx, axis=-1)


def rwkv7_scan(r, w, k, v, kk, ka, s0, reverse):
    def step(S, inp):
        r_t, w_t, k_t, v_t, kk_t, ka_t = inp
        sa = jnp.einsum('bhvk,bhk->bhv', S, -kk_t)
        S = (S * w_t[:, :, None, :] + sa[..., :, None] * ka_t[..., None, :]
             + v_t[..., :, None] * k_t[..., None, :])
        y = jnp.einsum('bhvk,bhk->bhv', S, r_t)
        return S, y
    xs = tuple(jnp.moveaxis(a, 1, 0) for a in (r, w, k, v, kk, ka))
    s_fin, ys = lax.scan(step, s0, xs, reverse=reverse)
    return jnp.moveaxis(ys, 0, 1), s_fin


def rwkv7_group(r, k, v, xw, xa, xg, s0, lp):
    B, T, _ = r.shape
    f32 = jnp.float32
    heads = lambda t: t.reshape(B, T, H_A, HEAD_A)
    r, k, v = r.astype(f32), k.astype(f32), v.astype(f32)
    xw = xw.astype(f32).reshape(B, T, N_DIR, R_W)
    xa = xa.astype(f32).reshape(B, T, N_DIR, R_A)
    g = jnp.einsum('btr,rd->btd', jax.nn.sigmoid(xg.astype(f32)), lp['g_up'].astype(f32))
    w_log = -jax.nn.softplus(-(lp['w0'].astype(f32)
                               + jnp.einsum('btnr,nrd->btnd', jnp.tanh(xw), lp['w_up'].astype(f32)))) - 0.5
    decay = jnp.exp(-jnp.exp(w_log))
    a = jax.nn.sigmoid(lp['a0'].astype(f32) + jnp.einsum('btnr,nrd->btnd', xa, lp['a_up'].astype(f32)))
    kk = heads(k * lp['k_k'].astype(f32))
    kk = kk / jnp.maximum(jnp.sqrt(jnp.sum(kk * kk, axis=-1, keepdims=True)), 1e-12)
    r_h, v_h = heads(r), heads(v)
    y_sum = 0.0
    bonus = 0.0
    finals = []
    for d, rev in ((0, False), (1, True)):
        a_d = a[:, :, d]
        k_d = heads(k * (1.0 + (a_d - 1.0) * lp['k_a'].astype(f32)))
        y_d, s_d = rwkv7_scan(r_h, heads(decay[:, :, d]), k_d, v_h, kk, kk * heads(a_d), s0[:, d].astype(f32), rev)
        y_sum = y_sum + y_d
        bonus = bonus + jnp.sum(r_h * k_d * lp['r_k'].astype(f32), axis=-1, keepdims=True) * v_h
        finals.append(s_d)
    mu = jnp.mean(y_sum, axis=-1, keepdims=True)
    var = jnp.mean(jnp.square(y_sum - mu), axis=-1, keepdims=True)
    yn = ((y_sum - mu) * lax.rsqrt(var + LNX_EPS)).reshape(B, T, D_A)
    yn = yn * lp['lnx_g'].astype(f32) + lp['lnx_b'].astype(f32)
    out = (yn + bonus.reshape(B, T, D_A)) * g
    return out, jnp.stack(finals, axis=1)


def conv_centred(x, w, b):
    T = x.shape[1]
    xp = jnp.pad(x, ((0, 0), (CONV_PAD_L, CONV_W - 1 - CONV_PAD_L), (0, 0)))
    out = b
    for j in range(CONV_W):
        out = out + xp[:, j:j + T] * w[j]
    return out


def block_diag(x, w, b):
    B, T, _ = x.shape
    return jnp.einsum('btnc,ncd->btnd', x.reshape(B, T, NB_B, BLK_B), w).reshape(B, T, D_B) + b


def lin_scan(a, bx, h0, reverse):
    def comb(e1, e2):
        a1, b1 = e1
        a2, b2 = e2
        return a1 * a2, a2 * b1 + b2
    a_cum, b_cum = lax.associative_scan(comb, (a, bx), reverse=reverse, axis=1)
    return b_cum + a_cum * h0[:, None, :]


def rglru_group(xb, gb, h0, lp):
    f32 = jnp.float32
    xc = conv_centred(xb.astype(f32), lp['conv_w'].astype(f32), lp['conv_b'].astype(f32))
    gate = jax.nn.gelu(gb.astype(f32), approximate=True)
    y_sum = 0.0
    finals = []
    for d, rev in ((0, False), (1, True)):
        rg = jax.nn.sigmoid(block_diag(xc, lp['wa'][d].astype(f32), lp['ba'][d].astype(f32)))
        ig = jax.nn.sigmoid(block_diag(xc, lp['wx'][d].astype(f32), lp['bx'][d].astype(f32)))
        log_a = -LRU_C * rg * jax.nn.softplus(-lp['lam'][d].astype(f32))
        a_t = jnp.exp(log_a)
        bx = jnp.sqrt(-jnp.expm1(2.0 * log_a)) * (ig * xc)
        h = lin_scan(a_t, bx, h0[:, d].astype(f32), rev)
        y_sum = y_sum + h
        finals.append(h[:, 0] if rev else h[:, -1])
    return y_sum * gate, jnp.stack(finals, axis=1)


def trunk_layer(x, mod, s_rwkv0, s_lru0, lp):
    shift1, scale1, gate1, shift2, scale2, gate2 = jnp.split(mod, 6, axis=-1)
    h = rms_norm(x, lp['g_pre_mix']) * (1.0 + scale1) + shift1
    z = jnp.einsum('btd,de->bte', h, lp['w_in'])
    r, k, v, xw, xa, xg, xb, gb = split_proj(z)
    y_a, s_a = rwkv7_group(r, k, v, xw, xa, xg, s_rwkv0, lp)
    y_b, s_b = rglru_group(xb, gb, s_lru0, lp)
    y = jnp.einsum('bte,ed->btd', jnp.concatenate([y_a, y_b], axis=-1).astype(x.dtype), lp['w_out'])
    x = x + gate1 * rms_norm(y, lp['g_post_mix'])
    h = rms_norm(x, lp['g_pre_mlp']) * (1.0 + scale2) + shift2
    f = jnp.square(jax.nn.relu(jnp.einsum('btd,df->btf', h, lp['w_mlp1'])))
    f = jnp.einsum('btf,fd->btd', f, lp['w_mlp2'])
    x = x + gate2 * rms_norm(f, lp['g_post_mlp'])
    return x, s_a, s_b


def setup_inputs(seed: int = 0) -> dict:
    key = jax.random.key(seed)
    ks = jax.random.split(key, 40)
    nrm = lambda i, shape, s: jax.random.normal(ks[i], shape, jnp.float32) * s
    u = jax.random.uniform(ks[30], (DEPTH, N_DIR, D_B), jnp.float32, 0.9, 0.999)
    p = u ** (1.0 / LRU_C)
    lam = jnp.log(p) - jnp.log1p(-p)
    return {
        "x_prompt": nrm(0, (BATCH, SEQ, D_MODEL), 1.0),
        "x_sample": nrm(1, (DEC_BATCH, DEC_SEQ, D_MODEL), 1.0),
        "c": nrm(2, (DEC_BATCH, D_MODEL), 1.0),
        "state_rwkv": nrm(3, (DEC_BATCH, DEPTH, N_DIR, H_A, HEAD_A, HEAD_A), 0.3),
        "state_lru": nrm(4, (DEC_BATCH, DEPTH, N_DIR, D_B), 0.5),
        "c_ctx": nrm(5, (D_MODEL,), 1.0),
        "w_mod": nrm(6, (DEPTH, D_MODEL, 6 * D_MODEL), 0.5 * D_MODEL ** -0.5),
        "b_mod": nrm(7, (DEPTH, 6 * D_MODEL), 0.01),
        "g_pre_mix": 1.0 + nrm(8, (DEPTH, D_MODEL), 0.02),
        "g_post_mix": 1.0 + nrm(9, (DEPTH, D_MODEL), 0.02),
        "g_pre_mlp": 1.0 + nrm(10, (DEPTH, D_MODEL), 0.02),
        "g_post_mlp": 1.0 + nrm(11, (DEPTH, D_MODEL), 0.02),
        "w_in": nrm(12, (DEPTH, D_MODEL, D_IN), D_MODEL ** -0.5),
        "rwkv_w0": jax.random.uniform(ks[13], (DEPTH, N_DIR, D_A), jnp.float32, -6.0, 1.0),
        "rwkv_w_up": nrm(14, (DEPTH, N_DIR, R_W, D_A), 0.1),
        "rwkv_a0": nrm(15, (DEPTH, N_DIR, D_A), 0.1),
        "rwkv_a_up": nrm(16, (DEPTH, N_DIR, R_A, D_A), 0.1),
        "rwkv_g_up": nrm(17, (DEPTH, R_G, D_A), R_G ** -0.5),
        "rwkv_k_k": 0.85 + nrm(18, (DEPTH, D_A), 0.02),
        "rwkv_k_a": 1.0 + nrm(19, (DEPTH, D_A), 0.02),
        "rwkv_r_k": nrm(20, (DEPTH, H_A, HEAD_A), 0.1),
        "rwkv_lnx_g": 1.0 + nrm(21, (DEPTH, D_A), 0.02),
        "rwkv_lnx_b": nrm(22, (DEPTH, D_A), 0.01),
        "lru_conv_w": nrm(23, (DEPTH, CONV_W, D_B), CONV_W ** -0.5),
        "lru_conv_b": nrm(24, (DEPTH, D_B), 0.01),
        "lru_wa": nrm(25, (DEPTH, N_DIR, NB_B, BLK_B, BLK_B), BLK_B ** -0.5),
        "lru_ba": nrm(26, (DEPTH, N_DIR, D_B), 0.01),
        "lru_wx": nrm(27, (DEPTH, N_DIR, NB_B, BLK_B, BLK_B), BLK_B ** -0.5),
        "lru_bx": nrm(28, (DEPTH, N_DIR, D_B), 0.01),
        "lru_lambda": lam,
        "w_out": nrm(31, (DEPTH, D_MIX, D_MODEL), D_MIX ** -0.5),
        "w_mlp1": nrm(32, (DEPTH, D_MODEL, D_FF), D_MODEL ** -0.5),
        "w_mlp2": nrm(33, (DEPTH, D_FF, D_MODEL), D_FF ** -0.5),
    }


def reference(x_prompt, x_sample, c, state_rwkv, state_lru, c_ctx, w_mod, b_mod,
              g_pre_mix, g_post_mix, g_pre_mlp, g_post_mlp, w_in,
              rwkv_w0, rwkv_w_up, rwkv_a0, rwkv_a_up, rwkv_g_up, rwkv_k_k, rwkv_k_a, rwkv_r_k,
              rwkv_lnx_g, rwkv_lnx_b, lru_conv_w, lru_conv_b, lru_wa, lru_ba, lru_wx, lru_bx,
              lru_lambda, w_out, w_mlp1, w_mlp2):
    n_ctx = x_prompt.shape[0]
    xp = x_prompt
    xs = x_sample + grid_pos_embed(x_sample.shape[1]).astype(x_sample.dtype)[None]
    zero_rwkv = jnp.zeros((n_ctx, N_DIR, H_A, HEAD_A, HEAD_A), jnp.float32)
    zero_lru = jnp.zeros((n_ctx, N_DIR, D_B), jnp.float32)
    new_rwkv, new_lru = [], []
    for l in range(DEPTH):
        lp = {
            'g_pre_mix': g_pre_mix[l], 'g_post_mix': g_post_mix[l],
            'g_pre_mlp': g_pre_mlp[l], 'g_post_mlp': g_post_mlp[l],
            'w_in': w_in[l], 'w_out': w_out[l], 'w_mlp1': w_mlp1[l], 'w_mlp2': w_mlp2[l],
            'w0': rwkv_w0[l], 'w_up': rwkv_w_up[l], 'a0': rwkv_a0[l], 'a_up': rwkv_a_up[l],
            'g_up': rwkv_g_up[l], 'k_k': rwkv_k_k[l], 'k_a': rwkv_k_a[l], 'r_k': rwkv_r_k[l],
            'lnx_g': rwkv_lnx_g[l], 'lnx_b': rwkv_lnx_b[l],
            'conv_w': lru_conv_w[l], 'conv_b': lru_conv_b[l],
            'wa': lru_wa[l], 'ba': lru_ba[l], 'wx': lru_wx[l], 'bx': lru_bx[l], 'lam': lru_lambda[l],
        }
        mod_ctx = (jax.nn.silu(c_ctx) @ w_mod[l] + b_mod[l])[None, None, :]
        mod_lat = (jax.nn.silu(c) @ w_mod[l] + b_mod[l])[:, None, :]
        xp, s_r, s_l = trunk_layer(xp, mod_ctx, zero_rwkv, zero_lru, lp)
        new_rwkv.append(s_r.astype(x_prompt.dtype))
        new_lru.append(s_l.astype(x_prompt.dtype))
        xs, _, _ = trunk_layer(xs, mod_lat, state_rwkv[:, l], state_lru[:, l], lp)
    new_state_rwkv = jnp.stack(new_rwkv, axis=1)
    new_state_lru = jnp.stack(new_lru, axis=1)
    return (xp, xs, new_state_rwkv, new_state_lru)
```

```python
import functools

import numpy as np
import jax
import jax.numpy as jnp
from jax import lax
from jax.experimental import pallas as pl
from jax.experimental.pallas import tpu as pltpu

F32 = jnp.float32
BF16 = jnp.bfloat16

D_MODEL = 1024
D_A = 512
D_B = 512
HEAD = 64
N_HEAD = 8
R_W = 64
R_A = 64
R_G = 128
D_FF = 4096
D_IN = 2944
GRID_W = 64
CONV_W = 4
LRU_C = 8.0
EPS = 1e-6
LNX_EPS = 64e-5

CHUNK = 64
GROUP = 256
HEADS_PER_GROUP = GROUP // HEAD
N_GROUP = D_A // GROUP
HALO = 8
TOKEN_TILE = 256
VMEM_LIMIT = 56 * 1024 * 1024

_O_R, _O_K, _O_V, _O_XW, _O_XA, _O_XG, _O_XB, _O_GB = 0, 512, 1024, 1536, 1664, 1792, 1920, 2432

_M_STRICT, _M_INCL, _M_LEV0 = 0, 1, 2
_LEVELS = (1, 2, 4, 8, 16, 32)
_M_PER_DIR = 2 + len(_LEVELS)
_M_EYE = 2 * _M_PER_DIR


def _dot(a, b):
    return jnp.dot(a.astype(BF16), b.astype(BF16), preferred_element_type=F32)


def _dot_nt(a, b):
    return lax.dot_general(a.astype(BF16), b.astype(BF16), (((1,), (1,)), ((), ())),
                           preferred_element_type=F32)


def _dot_tn(a, b):
    return lax.dot_general(a.astype(BF16), b.astype(BF16), (((0,), (0,)), ((), ())),
                           preferred_element_type=F32)


def _split3(x):
    hi = x.astype(BF16)
    r1 = x - hi.astype(F32)
    mid = r1.astype(BF16)
    lo = (r1 - mid.astype(F32)).astype(BF16)
    return hi, mid, lo


def _mod_kernel(c_ref, w_ref, b_ref, o_ref):
    c = c_ref[...]
    s = c * jax.nn.sigmoid(c)
    o_ref[...] = _dot(s, w_ref[...]) + b_ref[...]


def _mod_call(c_all, w_mod, b_mod):
    m = c_all.shape[0]
    n = w_mod.shape[1]
    tn = 1536
    return pl.pallas_call(
        _mod_kernel,
        grid=(n // tn,),
        in_specs=[pl.BlockSpec((m, D_MODEL), lambda j: (0, 0)),
                  pl.BlockSpec((D_MODEL, tn), lambda j: (0, j)),
                  pl.BlockSpec((1, tn), lambda j: (0, j))],
        out_specs=pl.BlockSpec((m, tn), lambda j: (0, j)),
        out_shape=jax.ShapeDtypeStruct((m, n), F32),
        compiler_params=pltpu.CompilerParams(dimension_semantics=("parallel",),
                                             vmem_limit_bytes=VMEM_LIMIT),
        name="mod",
    )(c_all, w_mod, b_mod.reshape(1, n))


def _proj_kernel(has_pos, *refs):
    if has_pos:
        x_ref, pos_ref = refs[0], refs[1]
        refs = refs[2:]
    else:
        x_ref, pos_ref = refs[0], None
        refs = refs[1:]
    (mod_ref, gpre_ref, win_ref, wup_ref, w0_ref, aup_ref, a0_ref, gup_ref,
     kk_ref, ka_ref, rk_ref, seg_ref,
     r_o, k_o, v_o, kk_o, lw_o, a_o, xb_o, g_o, bonus_o, gate_o) = refs

    x = x_ref[0]
    if has_pos:
        x = x + pos_ref[...]
    mod = mod_ref[0]
    shift1, scale1 = mod[0:1], mod[1:2]
    ms = jnp.mean(x * x, axis=-1, keepdims=True)
    h = (x * lax.rsqrt(ms + EPS)) * gpre_ref[...]
    h = h * (1.0 + scale1) + shift1
    z = _dot(h, win_ref[...])

    r = z[:, _O_R:_O_R + D_A]
    k = z[:, _O_K:_O_K + D_A]
    v = z[:, _O_V:_O_V + D_A]
    xw = z[:, _O_XW:_O_XW + 2 * R_W]
    xa = z[:, _O_XA:_O_XA + 2 * R_A]
    xg = z[:, _O_XG:_O_XG + R_G]

    g_o[0] = _dot(jax.nn.sigmoid(xg), gup_ref[...])
    wl = w0_ref[...] + _dot(jnp.tanh(xw), wup_ref[...])
    lw = -jax.nn.sigmoid(wl) * float(np.exp(-0.5))
    a = jax.nn.sigmoid(a0_ref[...] + _dot(xa, aup_ref[...]))
    lw_o[0, 0] = lw[:, :D_A]
    lw_o[1, 0] = lw[:, D_A:]
    a_o[0, 0] = a[:, :D_A]
    a_o[1, 0] = a[:, D_A:]

    kks = k * kk_ref[...]
    ss = _dot(kks * kks, seg_ref[...])
    kk = kks / jnp.maximum(jnp.sqrt(ss), 1e-12)

    ka = ka_ref[...]
    ksum = k * ((1.0 + (a[:, :D_A] - 1.0) * ka) + (1.0 + (a[:, D_A:] - 1.0) * ka))
    bonus_o[0] = _dot(r * ksum * rk_ref[...], seg_ref[...]) * v

    r_o[0] = r
    k_o[0] = k
    v_o[0] = v
    kk_o[0] = kk
    xb_o[0] = z[:, _O_XB:_O_XB + D_B]
    gate_o[0] = jax.nn.gelu(z[:, _O_GB:_O_GB + D_B], approximate=True)


def _proj_call(x, pos, mod, mod_row, p):
    bsz, t, _ = x.shape
    tm = TOKEN_TILE
    nt = t // tm
    has_pos = pos is not None

    def const(shape):
        return pl.BlockSpec(shape, lambda i, b: (0,) * len(shape))

    tok = lambda width: pl.BlockSpec((1, tm, width), lambda i, b: (b, i, 0))
    in_specs = [tok(D_MODEL)]
    args = [x]
    if has_pos:
        in_specs.append(pl.BlockSpec((tm, D_MODEL), lambda i, b: (i, 0)))
        args.append(pos)
    in_specs += [
        pl.BlockSpec((1, 6, D_MODEL), lambda i, b: (mod_row(b), 0, 0)),
        const((1, D_MODEL)), const((D_MODEL, D_IN)),
        const((2 * R_W, 2 * D_A)), const((1, 2 * D_A)),
        const((2 * R_A, 2 * D_A)), const((1, 2 * D_A)),
        const((R_G, D_A)),
        const((1, D_A)), const((1, D_A)), const((1, D_A)), const((D_A, D_A)),
    ]
    args += [mod, p["g_pre_mix"], p["w_in"], p["wup_bd"], p["w0"], p["aup_bd"], p["a0"], p["g_up"],
             p["k_k"], p["k_a"], p["r_k"], p["seg_ones"]]
    tok_shape = jax.ShapeDtypeStruct((bsz, t, D_A), F32)
    dir_shape = jax.ShapeDtypeStruct((2, bsz, t, D_A), F32)
    dir_spec = pl.BlockSpec((2, 1, tm, D_A), lambda i, b: (0, b, i, 0))
    out_shape = [tok_shape] * 4 + [dir_shape] * 2 + [tok_shape] * 4
    out_specs = [tok(D_A)] * 4 + [dir_spec] * 2 + [tok(D_A)] * 4
    return pl.pallas_call(
        functools.partial(_proj_kernel, has_pos),
        grid=(nt, bsz),
        in_specs=in_specs,
        out_specs=out_specs,
        out_shape=out_shape,
        compiler_params=pltpu.CompilerParams(dimension_semantics=("parallel", "parallel"),
                                             vmem_limit_bytes=VMEM_LIMIT),
        name="proj",
    )(*args)


def _scan_masks():
    t = np.arange(CHUNK)[:, None]
    s = (np.arange(GROUP) % CHUNK)[None, :]
    rows = []
    for d in (0, 1):
        before = (s < t) if d == 0 else (s > t)
        rows.append(before)
        rows.append(before | (s == t))
        for b in _LEVELS:
            same = (t // (2 * b)) == (s // (2 * b))
            if d == 0:
                rows.append(same & ((t // b) % 2 == 1) & ((s // b) % 2 == 0))
            else:
                rows.append(same & ((t // b) % 2 == 0) & ((s // b) % 2 == 1))
    rows.append(s == t)
    return np.stack(rows).astype(np.float32)


def _cumsum_mats():
    t = np.arange(CHUNK)[:, None]
    s = np.arange(CHUNK)[None, :]
    fwd = (s <= t).astype(np.float32)
    bwd = (s >= t).astype(np.float32)
    return np.stack([np.tile(fwd, (1, 3)), np.tile(bwd, (1, 3))])


def _group_blockdiag_mask():
    i = np.arange(GROUP)
    return ((i[:, None] // HEAD) == (i[None, :] // HEAD)).astype(np.float32)


def _scan_kernel(nc, *refs):
    fwd_in, bwd_in = refs[0:9], refs[9:18]
    (s0_ref, l0_ref, masks_ref, mtri_ref, bdm_ref, bdmf_ref,
     ka_ref, convw_ref, convb_ref, wa_ref, ba_ref, wx_ref, bx_ref, lam_ref,
     y_ref, hs_ref, sbd_ref, hl_ref) = refs[18:]

    i = pl.program_id(1)

    @pl.when(i == 0)
    def _init():
        y_ref[...] = jnp.zeros_like(y_ref)
        hs_ref[...] = jnp.zeros_like(hs_ref)
        sbd_ref[...] = s0_ref[...]
        hl_ref[...] = l0_ref[...]

    bdm = bdm_ref[...]
    bdm_f = bdmf_ref[...]

    def blockdiag(x):
        xb = x.astype(BF16)
        return jnp.concatenate([xb] * HEADS_PER_GROUP, axis=0) * bdm

    def bdot(lhs, rhs):
        return jnp.dot(lhs.astype(BF16), blockdiag(rhs), preferred_element_type=F32)

    row = lax.broadcasted_iota(jnp.int32, (CHUNK, D_B), 0)

    for d, ins in ((0, fwd_in), (1, bwd_in)):
        r_ref, k_ref, v_ref, kk_ref, lw_ref, a_ref, xb_ref, xp_ref, xn_ref = ins
        ci = i if d == 0 else nc - 1 - i
        rows = pl.ds(pl.multiple_of(ci * CHUNK, CHUNK), CHUNK)
        m0 = d * _M_PER_DIR
        last = CHUNK - 1 if d == 0 else 0

        r, k, v, kk = r_ref[0], k_ref[0], v_ref[0], kk_ref[0]
        lw, a = lw_ref[0, 0], a_ref[0, 0]
        hi, mid, lo = _split3(lw)
        lc = jnp.dot(mtri_ref[d], jnp.concatenate([hi, mid, lo], axis=0),
                     preferred_element_type=F32)
        ltot = lc[last:last + 1, :]
        e_pos = jnp.exp(lc)
        e_neg = jnp.exp(-lc)
        g_tot = jnp.exp(ltot)
        kd = k * (1.0 + (a - 1.0) * ka_ref[...])
        kb = kk * a
        a_t = -kk * jnp.exp(lc - lw)
        r_t = r * e_pos
        b_t = kb * e_neg
        k_t = kd * e_neg
        b_h = b_t * g_tot
        k_h = k_t * g_tot

        y_parts = []
        for g in range(N_GROUP):
            sl = slice(g * GROUP, (g + 1) * GROUP)
            ar = jnp.concatenate([a_t[:, sl], r_t[:, sl]], axis=0)
            rhs = jnp.concatenate([blockdiag(b_t[:, sl]), blockdiag(k_t[:, sl])], axis=0)
            sc = _dot_nt(ar, rhs)
            n_ab = sc[:CHUNK, :GROUP] * masks_ref[m0 + _M_STRICT]
            n_ak = sc[:CHUNK, GROUP:] * masks_ref[m0 + _M_STRICT]
            n_rb = sc[CHUNK:, :GROUP] * masks_ref[m0 + _M_INCL]
            n_rk = sc[CHUNK:, GROUP:] * masks_ref[m0 + _M_INCL]

            tinv = masks_ref[_M_EYE] + n_ab * masks_ref[m0 + _M_LEV0]
            for li in range(1, len(_LEVELS)):
                n_l = n_ab * masks_ref[m0 + _M_LEV0 + li]
                tinv = tinv + bdot(tinv, bdot(n_l, tinv))

            vg = v[:, sl]
            s_old = sbd_ref[0, d, g]
            ars = _dot_nt(ar, s_old)
            u = bdot(tinv, ars[:CHUNK] + bdot(n_ak, vg))
            y_parts.append(ars[CHUNK:] + bdot(n_rb, u) + bdot(n_rk, vg))
            upd = _dot_tn(jnp.concatenate([u, vg], axis=0),
                          jnp.concatenate([b_h[:, sl], k_h[:, sl]], axis=0))
            sbd_ref[0, d, g] = s_old * g_tot[:, sl] + upd * bdm_f
        y_ref[0, rows, :] += jnp.concatenate(y_parts, axis=1)

        xprev = xp_ref[0] * jnp.where(ci > 0, 1.0, 0.0)
        xnext = xn_ref[0] * jnp.where(ci < nc - 1, 1.0, 0.0)
        ext = jnp.concatenate([xprev, xb_ref[0], xnext], axis=0)
        n_ext = CHUNK + 2 * HALO
        xc = convb_ref[...]
        for j in range(CONV_W):
            sh = (2 - j) % n_ext
            tap = ext if sh == 0 else pltpu.roll(ext, sh, 0)
            xc = xc + tap[HALO:HALO + CHUNK] * convw_ref[j:j + 1, :]
        rg_parts, ig_parts = [], []
        for g in range(N_GROUP):
            sl = slice(g * GROUP, (g + 1) * GROUP)
            rg_parts.append(_dot(xc[:, sl], wa_ref[d, g]))
            ig_parts.append(_dot(xc[:, sl], wx_ref[d, g]))
        rg = jax.nn.sigmoid(jnp.concatenate(rg_parts, axis=1) + ba_ref[d:d + 1, :])
        ig = jax.nn.sigmoid(jnp.concatenate(ig_parts, axis=1) + bx_ref[d:d + 1, :])
        log_a = -LRU_C * rg * jax.nn.softplus(-lam_ref[d:d + 1, :])
        ac = jnp.exp(log_a)
        bc = jnp.sqrt(-jnp.tanh(log_a) * (ac * ac + 1.0)) * (ig * xc)
        for s in _LEVELS:
            if d == 0:
                sh, ok = s, row >= s
            else:
                sh, ok = CHUNK - s, row < CHUNK - s
            a_sh = jnp.where(ok, pltpu.roll(ac, sh, 0), 1.0)
            b_sh = jnp.where(ok, pltpu.roll(bc, sh, 0), 0.0)
            bc = ac * b_sh + bc
            ac = ac * a_sh
        hseq = bc + ac * hl_ref[0, d:d + 1, :]
        hl_ref[0, d:d + 1, :] = hseq[last:last + 1, :]
        hs_ref[0, rows, :] += hseq


def _scan_call(feats, s0_bd, l0, p):
    r, k, v, kk, lw, a, xb = feats
    bsz, t, _ = r.shape
    nc = t // CHUNK
    hpc = CHUNK // HALO

    def fwd_c(i):
        return i

    def bwd_c(i):
        return nc - 1 - i

    def dir_specs(d, cm):
        tok = pl.BlockSpec((1, CHUNK, D_A), lambda b, i: (b, cm(i), 0))
        per_dir = pl.BlockSpec((1, 1, CHUNK, D_A), lambda b, i: (d, b, cm(i), 0))
        prev = pl.BlockSpec((1, HALO, D_B), lambda b, i: (b, jnp.maximum(cm(i) * hpc - 1, 0), 0))
        nxt = pl.BlockSpec((1, HALO, D_B),
                           lambda b, i: (b, jnp.minimum((cm(i) + 1) * hpc, t // HALO - 1), 0))
        return [tok, tok, tok, tok, per_dir, per_dir, tok, prev, nxt]

    def const(shape):
        return pl.BlockSpec(shape, lambda b, i: (0,) * len(shape))

    in_specs = dir_specs(0, fwd_c) + dir_specs(1, bwd_c)
    args = [r, k, v, kk, lw, a, xb, xb, xb] * 2
    n_masks = 2 * _M_PER_DIR + 1
    in_specs += [
        pl.BlockSpec((1, 2, N_GROUP, GROUP, GROUP), lambda b, i: (b, 0, 0, 0, 0)),
        pl.BlockSpec((1, 2, D_B), lambda b, i: (b, 0, 0)),
        const((n_masks, CHUNK, GROUP)), const((2, CHUNK, 3 * CHUNK)),
        const((GROUP, GROUP)), const((GROUP, GROUP)),
        const((1, D_A)), const((CONV_W, D_B)), const((1, D_B)),
        const((2, N_GROUP, GROUP, GROUP)), const((2, D_B)),
        const((2, N_GROUP, GROUP, GROUP)), const((2, D_B)), const((2, D_B)),
    ]
    args += [s0_bd, l0, p["scan_masks"], p["cumsum_mats"], p["bdm_bf16"], p["bdm_f32"],
             p["k_a"], p["conv_w"], p["conv_b"], p["wa_bd"], p["ba"], p["wx_bd"], p["bx"], p["lam"]]
    seq_spec = pl.BlockSpec((1, t, D_A), lambda b, i: (b, 0, 0))
    out_specs = [seq_spec, seq_spec,
                 pl.BlockSpec((1, 2, N_GROUP, GROUP, GROUP), lambda b, i: (b, 0, 0, 0, 0)),
                 pl.BlockSpec((1, 2, D_B), lambda b, i: (b, 0, 0))]
    out_shape = [jax.ShapeDtypeStruct((bsz, t, D_A), F32), jax.ShapeDtypeStruct((bsz, t, D_B), F32),
                 jax.ShapeDtypeStruct((bsz, 2, N_GROUP, GROUP, GROUP), F32),
                 jax.ShapeDtypeStruct((bsz, 2, D_B), F32)]
    return pl.pallas_call(
        functools.partial(_scan_kernel, nc),
        grid=(bsz, nc),
        in_specs=in_specs,
        out_specs=out_specs,
        out_shape=out_shape,
        compiler_params=pltpu.CompilerParams(dimension_semantics=("parallel", "arbitrary"),
                                             vmem_limit_bytes=VMEM_LIMIT),
        name="scan",
    )(*args)


def _out_kernel(has_pos, *refs):
    if has_pos:
        x_ref, pos_ref = refs[0], refs[1]
        refs = refs[2:]
    else:
        x_ref, pos_ref = refs[0], None
        refs = refs[1:]
    (y_ref, hs_ref, g_ref, bonus_ref, gate_ref, mod_ref,
     avg_ref, lnxg_ref, lnxb_ref, wout_ref, gpost_ref, gpre2_ref, w1_ref, w2_ref, gpost2_ref,
     o_ref) = refs

    x = x_ref[0]
    if has_pos:
        x = x + pos_ref[...]
    mod = mod_ref[0]
    gate1, shift2, scale2, gate2 = mod[2:3], mod[3:4], mod[4:5], mod[5:6]

    y = y_ref[0]
    avg = avg_ref[...]
    y_hi = y.astype(BF16)
    y_lo = (y - y_hi.astype(F32)).astype(BF16)
    mu = jnp.dot(y_hi, avg, preferred_element_type=F32) + jnp.dot(y_lo, avg, preferred_element_type=F32)
    yc = y - mu
    var = _dot(yc * yc, avg)
    yn = yc * lax.rsqrt(var + LNX_EPS) * lnxg_ref[...] + lnxb_ref[...]
    out_a = (yn + bonus_ref[0]) * g_ref[0]
    out_b = hs_ref[0] * gate_ref[0]
    mix = _dot(jnp.concatenate([out_a, out_b], axis=1), wout_ref[...])
    ms = jnp.mean(mix * mix, axis=-1, keepdims=True)
    x = x + gate1 * ((mix * lax.rsqrt(ms + EPS)) * gpost_ref[...])

    ms = jnp.mean(x * x, axis=-1, keepdims=True)
    h = (x * lax.rsqrt(ms + EPS)) * gpre2_ref[...]
    h = h * (1.0 + scale2) + shift2
    f = _dot(h, w1_ref[...])
    f = jnp.square(jnp.maximum(f, 0.0))
    f = _dot(f, w2_ref[...])
    ms = jnp.mean(f * f, axis=-1, keepdims=True)
    o_ref[0] = x + gate2 * ((f * lax.rsqrt(ms + EPS)) * gpost2_ref[...])


def _out_call(x, pos, mod, mod_row, y, hs, g, bonus, gate, p):
    bsz, t, _ = x.shape
    tm = TOKEN_TILE
    nt = t // tm
    has_pos = pos is not None

    def const(shape):
        return pl.BlockSpec(shape, lambda i, b: (0,) * len(shape))

    tok = lambda width: pl.BlockSpec((1, tm, width), lambda i, b: (b, i, 0))
    in_specs = [tok(D_MODEL)]
    args = [x]
    if has_pos:
        in_specs.append(pl.BlockSpec((tm, D_MODEL), lambda i, b: (i, 0)))
        args.append(pos)
    in_specs += [tok(D_A)] * 5
    in_specs += [
        pl.BlockSpec((1, 6, D_MODEL), lambda i, b: (mod_row(b), 0, 0)),
        const((D_A, D_A)), const((1, D_A)), const((1, D_A)),
        const((D_MODEL, D_MODEL)), const((1, D_MODEL)), const((1, D_MODEL)),
        const((D_MODEL, D_FF)), const((D_FF, D_MODEL)), const((1, D_MODEL)),
    ]
    args += [y, hs, g, bonus, gate, mod,
             p["seg_avg"], p["lnx_g"], p["lnx_b"], p["w_out"], p["g_post_mix"], p["g_pre_mlp"],
             p["w_mlp1"], p["w_mlp2"], p["g_post_mlp"]]
    return pl.pallas_call(
        functools.partial(_out_kernel, has_pos),
        grid=(nt, bsz),
        in_specs=in_specs,
        out_specs=tok(D_MODEL),
        out_shape=jax.ShapeDtypeStruct((bsz, t, D_MODEL), F32),
        compiler_params=pltpu.CompilerParams(dimension_semantics=("parallel", "parallel"),
                                             vmem_limit_bytes=VMEM_LIMIT),
        name="out",
    )(*args)


def _sincos_1d(pos, dim):
    omega = 1.0 / (10000.0 ** (jnp.arange(dim // 2, dtype=F32) / (dim // 2)))
    ang = pos.astype(F32)[:, None] * omega[None, :]
    return jnp.concatenate([jnp.sin(ang), jnp.cos(ang)], axis=-1)


def _grid_pos_embed(n_tokens):
    rows = n_tokens // GRID_W
    half = D_MODEL // 2
    e_row = _sincos_1d(jnp.arange(rows), half)
    e_col = _sincos_1d(jnp.arange(GRID_W), half)
    emb = jnp.concatenate([jnp.broadcast_to(e_row[:, None, :], (rows, GRID_W, half)),
                           jnp.broadcast_to(e_col[None, :, :], (rows, GRID_W, half))], axis=-1)
    return emb.reshape(rows * GRID_W, D_MODEL)


def _blockdiag_pairs(w):
    z = jnp.zeros_like(w[0])
    return jnp.concatenate([jnp.concatenate([w[0], z], axis=1),
                            jnp.concatenate([z, w[1]], axis=1)], axis=0)


def _heads_to_blockdiag(w):
    lead = w.shape[:-3]
    w = w.reshape(lead + (N_GROUP, HEADS_PER_GROUP, HEAD, HEAD))
    eye = jnp.eye(HEADS_PER_GROUP, dtype=w.dtype)
    bd = jnp.einsum('...ghab,hj->...ghajb', w, eye)
    return bd.reshape(lead + (N_GROUP, GROUP, GROUP))


def _blockdiag_to_heads(bd):
    lead = bd.shape[:-3]
    bd = bd.reshape(lead + (N_GROUP, HEADS_PER_GROUP, HEAD, HEADS_PER_GROUP, HEAD))
    eye = jnp.eye(HEADS_PER_GROUP, dtype=bd.dtype)
    w = jnp.einsum('...ghajb,hj->...ghab', bd, eye)
    return w.reshape(lead + (N_HEAD, HEAD, HEAD))


def kernel(x_prompt, x_sample, c, state_rwkv, state_lru, c_ctx, w_mod, b_mod, g_pre_mix, g_post_mix,
           g_pre_mlp, g_post_mlp, w_in, rwkv_w0, rwkv_w_up, rwkv_a0, rwkv_a_up, rwkv_g_up, rwkv_k_k,
           rwkv_k_a, rwkv_r_k, rwkv_lnx_g, rwkv_lnx_b, lru_conv_w, lru_conv_b, lru_wa, lru_ba, lru_wx,
           lru_bx, lru_lambda, w_out, w_mlp1, w_mlp2):
    n_ctx = x_prompt.shape[0]
    n_lat = x_sample.shape[0]
    l = 0
    seg = _group_blockdiag_mask()
    seg512 = np.kron(np.eye(N_GROUP, dtype=np.float32), seg)
    p = {
        "g_pre_mix": g_pre_mix[l][None], "g_post_mix": g_post_mix[l][None],
        "g_pre_mlp": g_pre_mlp[l][None], "g_post_mlp": g_post_mlp[l][None],
        "w_in": w_in[l].astype(BF16), "w_out": w_out[l].astype(BF16),
        "w_mlp1": w_mlp1[l].astype(BF16), "w_mlp2": w_mlp2[l].astype(BF16),
        "w0": rwkv_w0[l].reshape(1, 2 * D_A), "a0": rwkv_a0[l].reshape(1, 2 * D_A),
        "wup_bd": _blockdiag_pairs(rwkv_w_up[l]).astype(BF16),
        "aup_bd": _blockdiag_pairs(rwkv_a_up[l]).astype(BF16),
        "g_up": rwkv_g_up[l].astype(BF16),
        "k_k": rwkv_k_k[l][None], "k_a": rwkv_k_a[l][None], "r_k": rwkv_r_k[l].reshape(1, D_A),
        "lnx_g": rwkv_lnx_g[l][None], "lnx_b": rwkv_lnx_b[l][None],
        "conv_w": lru_conv_w[l], "conv_b": lru_conv_b[l][None],
        "wa_bd": _heads_to_blockdiag(lru_wa[l]).astype(BF16), "ba": lru_ba[l],
        "wx_bd": _heads_to_blockdiag(lru_wx[l]).astype(BF16), "bx": lru_bx[l],
        "lam": lru_lambda[l],
        "seg_ones": jnp.asarray(seg512, BF16), "seg_avg": jnp.asarray(seg512 / HEAD, BF16),
        "scan_masks": jnp.asarray(_scan_masks()), "cumsum_mats": jnp.asarray(_cumsum_mats(), BF16),
        "bdm_bf16": jnp.asarray(seg, BF16), "bdm_f32": jnp.asarray(seg),
    }

    m_rows = 16
    c_all = jnp.concatenate([c_ctx[None], c, jnp.zeros((m_rows - 1 - n_lat, D_MODEL), F32)], axis=0)
    mod = _mod_call(c_all, w_mod[l], b_mod[l]).reshape(m_rows, 6, D_MODEL)

    pos = _grid_pos_embed(x_sample.shape[1]).astype(x_sample.dtype)
    ctx_row = lambda b: 0
    lat_row = lambda b: b + 1

    feats = _proj_call(x_prompt, None, mod, ctx_row, p)
    y, hs, s_ctx, l_ctx = _scan_call(
        feats[:7], jnp.zeros((n_ctx, 2, N_GROUP, GROUP, GROUP), F32), jnp.zeros((n_ctx, 2, D_B), F32), p)
    y_prompt = _out_call(x_prompt, None, mod, ctx_row, y, hs, feats[7], feats[8], feats[9], p)

    feats = _proj_call(x_sample, pos, mod, lat_row, p)
    y, hs, _, _ = _scan_call(feats[:7], _heads_to_blockdiag(state_rwkv[:, l]), state_lru[:, l], p)
    y_sample = _out_call(x_sample, pos, mod, lat_row, y, hs, feats[7], feats[8], feats[9], p)

    new_state_rwkv = _blockdiag_to_heads(s_ctx)[:, None].astype(x_prompt.dtype)
    new_state_lru = l_ctx[:, None].astype(x_prompt.dtype)
    return (y_prompt, y_sample, new_state_rwkv, new_state_lru)
```

```python
import functools

import numpy as np
import jax
import jax.numpy as jnp
from jax import lax
from jax.experimental import pallas as pl
from jax.experimental.pallas import tpu as pltpu

F32 = jnp.float32
BF16 = jnp.bfloat16

D_MODEL = 1024
D_A = 512
D_B = 512
HEAD = 64
N_HEAD = 8
R_W = 64
R_A = 64
R_G = 128
D_FF = 4096
D_IN = 2944
GRID_W = 64
CONV_W = 4
LRU_C = 8.0
EPS = 1e-6
LNX_EPS = 64e-5

CHUNK = 64
GROUP = 256
HEADS_PER_GROUP = GROUP // HEAD
N_GROUP = D_A // GROUP
HALO = 8
TOKEN_TILE = 256
CPT = TOKEN_TILE // CHUNK
VMEM_LIMIT = 60 * 1024 * 1024

_O_R, _O_K, _O_V, _O_XW, _O_XA, _O_XG, _O_XB, _O_GB = 0, 512, 1024, 1536, 1664, 1792, 1920, 2432

_M_STRICT, _M_INCL, _M_LEV0 = 0, 1, 2
_LEVELS = (1, 2, 4, 8, 16, 32)
_M_PER_DIR = 2 + len(_LEVELS)
_M_EYE = 2 * _M_PER_DIR


def _dot(a, b):
    return jnp.dot(a.astype(BF16), b.astype(BF16), preferred_element_type=F32)


def _dot_nt(a, b):
    return lax.dot_general(a.astype(BF16), b.astype(BF16), (((1,), (1,)), ((), ())),
                           preferred_element_type=F32)


def _dot_tn(a, b):
    return lax.dot_general(a.astype(BF16), b.astype(BF16), (((0,), (0,)), ((), ())),
                           preferred_element_type=F32)


def _split3(x):
    hi = x.astype(BF16)
    r1 = x - hi.astype(F32)
    mid = r1.astype(BF16)
    lo = (r1 - mid.astype(F32)).astype(BF16)
    return hi, mid, lo


def _blockdiag(x, bdm):
    xb = x.astype(BF16)
    return jnp.concatenate([xb] * HEADS_PER_GROUP, axis=0) * bdm


def _mod_kernel(c_ref, w_ref, b_ref, o_ref):
    c = c_ref[...]
    s = c * jax.nn.sigmoid(c)
    o_ref[...] = _dot(s, w_ref[...]) + b_ref[...]


def _mod_call(c_all, w_mod, b_mod):
    m = c_all.shape[0]
    n = w_mod.shape[1]
    tn = 1536
    return pl.pallas_call(
        _mod_kernel,
        grid=(n // tn,),
        in_specs=[pl.BlockSpec((m, D_MODEL), lambda j: (0, 0)),
                  pl.BlockSpec((D_MODEL, tn), lambda j: (0, j)),
                  pl.BlockSpec((1, tn), lambda j: (0, j))],
        out_specs=pl.BlockSpec((m, tn), lambda j: (0, j)),
        out_shape=jax.ShapeDtypeStruct((m, n), F32),
        compiler_params=pltpu.CompilerParams(dimension_semantics=("parallel",),
                                             vmem_limit_bytes=VMEM_LIMIT),
        name="mod",
    )(c_all, w_mod, b_mod.reshape(1, n))


def _chunk_masks():
    t = np.arange(CHUNK)[:, None]
    s = (np.arange(GROUP) % CHUNK)[None, :]
    rows = []
    for d in (0, 1):
        before = (s < t) if d == 0 else (s > t)
        rows.append(before)
        rows.append(before | (s == t))
        for b in _LEVELS:
            same = (t // (2 * b)) == (s // (2 * b))
            if d == 0:
                rows.append(same & ((t // b) % 2 == 1) & ((s // b) % 2 == 0))
            else:
                rows.append(same & ((t // b) % 2 == 0) & ((s // b) % 2 == 1))
    rows.append(s == t)
    return np.stack(rows).astype(np.float32)


def _cumsum_mats():
    t = np.arange(TOKEN_TILE)[:, None]
    s = np.arange(TOKEN_TILE)[None, :]
    same = (t // CHUNK) == (s // CHUNK)
    out = []
    for d in (0, 1):
        inc = same & ((s <= t) if d == 0 else (s >= t))
        m = np.concatenate([inc, same], axis=0).astype(np.float32)
        out.append(np.tile(m, (1, 3)))
    return np.stack(out)


def _group_blockdiag_mask():
    i = np.arange(GROUP)
    return ((i[:, None] // HEAD) == (i[None, :] // HEAD)).astype(np.float32)


def _feat_kernel(has_pos, nt, *refs):
    if has_pos:
        x_ref, xp_ref, xn_ref, pos_ref, pp_ref, pn_ref = refs[:6]
        refs = refs[6:]
    else:
        x_ref, xp_ref, xn_ref = refs[:3]
        pos_ref = pp_ref = pn_ref = None
        refs = refs[3:]
    (mod_ref, gpre_ref, win_ref, wup_ref, w0_ref, aup_ref, a0_ref, gup_ref,
     kk_ref, ka_ref, rk_ref, seg_ref,
     convw_ref, convb_ref, wa_ref, ba_ref, wx_ref, bx_ref, lam_ref,
     masks_ref, csum_ref, bdm_ref,
     p_o, q_o, m_o, n_o, gt_o, ac_o, bs_o, bl_o, g_o, bonus_o, gate_o) = refs

    i = pl.program_id(0)
    tm = TOKEN_TILE
    mod = mod_ref[0]
    shift1, scale1 = mod[0:1], mod[1:2]

    def normmod(xv):
        ms = jnp.mean(xv * xv, axis=-1, keepdims=True)
        hv = (xv * lax.rsqrt(ms + EPS)) * gpre_ref[...]
        return hv * (1.0 + scale1) + shift1

    x = x_ref[0]
    halo = jnp.concatenate([xp_ref[0], xn_ref[0]], axis=0)
    if has_pos:
        x = x + pos_ref[...]
        halo = halo + jnp.concatenate([pp_ref[...], pn_ref[...]], axis=0)
    z = _dot(normmod(x), win_ref[...])
    zh = _dot(normmod(halo), win_ref[:, _O_XB:_O_XB + D_B])

    r = z[:, _O_R:_O_R + D_A]
    k = z[:, _O_K:_O_K + D_A]
    v = z[:, _O_V:_O_V + D_A]
    xw = z[:, _O_XW:_O_XW + 2 * R_W]
    xa = z[:, _O_XA:_O_XA + 2 * R_A]
    xg = z[:, _O_XG:_O_XG + R_G]

    g_o[0] = _dot(jax.nn.sigmoid(xg), gup_ref[...])
    gate_o[0] = jax.nn.gelu(z[:, _O_GB:_O_GB + D_B], approximate=True)
    wl = w0_ref[...] + _dot(jnp.tanh(xw), wup_ref[...])
    lw2 = -jax.nn.sigmoid(wl) * float(np.exp(-0.5))
    a2 = jax.nn.sigmoid(a0_ref[...] + _dot(xa, aup_ref[...]))

    kks = k * kk_ref[...]
    ss = _dot(kks * kks, seg_ref[...])
    kk = kks / jnp.maximum(jnp.sqrt(ss), 1e-12)
    ka = ka_ref[...]
    kd2 = [k * (1.0 + (a2[:, d * D_A:(d + 1) * D_A] - 1.0) * ka) for d in (0, 1)]
    bonus_o[0] = _dot(r * (kd2[0] + kd2[1]) * rk_ref[...], seg_ref[...]) * v

    m_prev = jnp.where(i > 0, 1.0, 0.0)
    m_next = jnp.where(i < nt - 1, 1.0, 0.0)
    ext = jnp.concatenate([zh[:HALO] * m_prev, z[:, _O_XB:_O_XB + D_B], zh[HALO:] * m_next], axis=0)
    n_ext = tm + 2 * HALO
    xc = convb_ref[...]
    for j in range(CONV_W):
        sh = (2 - j) % n_ext
        tap = ext if sh == 0 else pltpu.roll(ext, sh, 0)
        xc = xc + tap[HALO:HALO + tm] * convw_ref[j:j + 1, :]
    row = lax.broadcasted_iota(jnp.int32, (tm, D_B), 0) & (CHUNK - 1)
    bsum = None
    for d in (0, 1):
        rg = jnp.concatenate([_dot(xc[:, g * GROUP:(g + 1) * GROUP], wa_ref[d, g]) for g in range(N_GROUP)], 1)
        ig = jnp.concatenate([_dot(xc[:, g * GROUP:(g + 1) * GROUP], wx_ref[d, g]) for g in range(N_GROUP)], 1)
        rg = jax.nn.sigmoid(rg + ba_ref[d:d + 1, :])
        ig = jax.nn.sigmoid(ig + bx_ref[d:d + 1, :])
        log_a = -LRU_C * rg * jax.nn.softplus(-lam_ref[d:d + 1, :])
        ac = jnp.exp(log_a)
        bc = jnp.sqrt(-jnp.tanh(log_a) * (ac * ac + 1.0)) * (ig * xc)
        for s in _LEVELS:
            if d == 0:
                sh, ok = s, row >= s
            else:
                sh, ok = tm - s, row < CHUNK - s
            a_sh = jnp.where(ok, pltpu.roll(ac, sh, 0), 1.0)
            b_sh = jnp.where(ok, pltpu.roll(bc, sh, 0), 0.0)
            bc = ac * b_sh + bc
            ac = ac * a_sh
        ac_o[d, 0] = ac
        bsum = bc if bsum is None else bsum + bc
        for c in range(CPT):
            last = c * CHUNK + (CHUNK - 1 if d == 0 else 0)
            bl_o[d, 0, c] = bc[last:last + 1, :]
    bs_o[0] = bsum

    bdm = bdm_ref[...]
    bd = lambda xv: _blockdiag(xv, bdm)
    bdot = lambda lhs, rhs: jnp.dot(lhs.astype(BF16), bd(rhs), preferred_element_type=F32)

    a_t, r_t, b_t, k_t, b_h, k_h = [], [], [], [], [], []
    for d in (0, 1):
        lw = lw2[:, d * D_A:(d + 1) * D_A]
        hi, mid, lo = _split3(lw)
        cs = jnp.dot(csum_ref[d], jnp.concatenate([hi, mid, lo], axis=0), preferred_element_type=F32)
        lc, ltot = cs[:tm], cs[tm:]
        e_pos = jnp.exp(lc)
        e_neg = jnp.exp(-lc)
        g_tot = jnp.exp(ltot)
        for c in range(CPT):
            gt_o[d, 0, c] = g_tot[c * CHUNK:c * CHUNK + 1, :]
        kb = kk * a2[:, d * D_A:(d + 1) * D_A]
        a_t.append(-kk * jnp.exp(lc - lw))
        r_t.append(r * e_pos)
        b_t.append(kb * e_neg)
        k_t.append(kd2[d] * e_neg)
        b_h.append(b_t[d] * g_tot)
        k_h.append(k_t[d] * g_tot)

    combos = [(d, c, g) for d in (0, 1) for c in range(CPT) for g in range(N_GROUP)]

    def cut(arr, c, g):
        return arr[c * CHUNK:(c + 1) * CHUNK, g * GROUP:(g + 1) * GROUP]

    sc = {}
    for key in combos:
        d, c, g = key
        ar = jnp.concatenate([cut(a_t[d], c, g), cut(r_t[d], c, g)], axis=0)
        rhs = jnp.concatenate([bd(cut(b_t[d], c, g)), bd(cut(k_t[d], c, g))], axis=0)
        sc[key] = _dot_nt(ar, rhs)
    n_ab, n_ak, n_rb, n_rk, tinv = {}, {}, {}, {}, {}
    for key in combos:
        m0 = key[0] * _M_PER_DIR
        n_ab[key] = sc[key][:CHUNK, :GROUP] * masks_ref[m0 + _M_STRICT]
        n_ak[key] = sc[key][:CHUNK, GROUP:] * masks_ref[m0 + _M_STRICT]
        n_rb[key] = sc[key][CHUNK:, :GROUP] * masks_ref[m0 + _M_INCL]
        n_rk[key] = sc[key][CHUNK:, GROUP:] * masks_ref[m0 + _M_INCL]
        tinv[key] = masks_ref[_M_EYE] + n_ab[key] * masks_ref[m0 + _M_LEV0]
    for li in range(1, len(_LEVELS)):
        pm = {}
        for key in combos:
            pm[key] = bdot(n_ab[key] * masks_ref[key[0] * _M_PER_DIR + _M_LEV0 + li], tinv[key])
        for key in combos:
            tinv[key] = tinv[key] + bdot(tinv[key], pm[key])

    akv = {key: bdot(n_ak[key], cut(v, key[1], key[2])) for key in combos}
    wu = {}
    for key in combos:
        d, c, g = key
        rhs = jnp.concatenate([bd(cut(a_t[d], c, g)), bd(akv[key])], axis=1)
        wu[key] = jnp.dot(tinv[key].astype(BF16), rhs, preferred_element_type=F32)
    pq = {}
    for key in combos:
        rhs = jnp.concatenate([bd(wu[key][:, :GROUP]), bd(wu[key][:, GROUP:])], axis=1)
        pq[key] = jnp.dot(n_rb[key].astype(BF16), rhs, preferred_element_type=F32)
    rkv = {key: bdot(n_rk[key], cut(v, key[1], key[2])) for key in combos}

    lane = lax.broadcasted_iota(jnp.int32, (CHUNK, GROUP), 1)

    def fold(full):
        s0, s1, s2, s3 = (full[h * HEAD:(h + 1) * HEAD] for h in range(HEADS_PER_GROUP))
        return jnp.where(lane < HEAD, s0, jnp.where(lane < 2 * HEAD, s1, jnp.where(lane < 3 * HEAD, s2, s3)))

    for key in combos:
        d, c, g = key
        rs = slice(c * CHUNK, (c + 1) * CHUNK)
        ls = slice(g * GROUP, (g + 1) * GROUP)
        w_, u0 = wu[key][:, :GROUP], wu[key][:, GROUP:]
        vg = cut(v, c, g)
        bh, kh = cut(b_h[d], c, g), cut(k_h[d], c, g)
        p_o[d, 0, rs, ls] = cut(r_t[d], c, g) + pq[key][:, :GROUP]
        m_o[d, 0, rs, ls] = fold(_dot_tn(w_, bh))
        n_o[d, 0, rs, ls] = fold(_dot_tn(jnp.concatenate([u0, vg], axis=0), jnp.concatenate([bh, kh], axis=0)))
        if d == 1:
            other = (0, c, g)
            q_o[0, rs, ls] = (pq[other][:, GROUP:] + rkv[other]) + (pq[key][:, GROUP:] + rkv[key])


def _feat_call(x, pos, mod, mod_row, p):
    bsz, t, _ = x.shape
    tm = TOKEN_TILE
    nt = t // tm
    nc = t // CHUNK
    hpt = tm // HALO
    has_pos = pos is not None

    def const(shape):
        return pl.BlockSpec(shape, lambda i, b: (0,) * len(shape))

    tok = lambda width: pl.BlockSpec((1, tm, width), lambda i, b: (b, i, 0))
    prev_i = lambda i: jnp.maximum(i * hpt - 1, 0)
    next_i = lambda i: jnp.minimum((i + 1) * hpt, t // HALO - 1)
    in_specs = [tok(D_MODEL),
                pl.BlockSpec((1, HALO, D_MODEL), lambda i, b: (b, prev_i(i), 0)),
                pl.BlockSpec((1, HALO, D_MODEL), lambda i, b: (b, next_i(i), 0))]
    args = [x, x, x]
    if has_pos:
        in_specs += [pl.BlockSpec((tm, D_MODEL), lambda i, b: (i, 0)),
                     pl.BlockSpec((HALO, D_MODEL), lambda i, b: (prev_i(i), 0)),
                     pl.BlockSpec((HALO, D_MODEL), lambda i, b: (next_i(i), 0))]
        args += [pos, pos, pos]
    n_masks = 2 * _M_PER_DIR + 1
    in_specs += [
        pl.BlockSpec((1, 6, D_MODEL), lambda i, b: (mod_row(b), 0, 0)),
        const((1, D_MODEL)), const((D_MODEL, D_IN)),
        const((2 * R_W, 2 * D_A)), const((1, 2 * D_A)),
        const((2 * R_A, 2 * D_A)), const((1, 2 * D_A)),
        const((R_G, D_A)),
        const((1, D_A)), const((1, D_A)), const((1, D_A)), const((D_A, D_A)),
        const((CONV_W, D_B)), const((1, D_B)),
        const((2, N_GROUP, GROUP, GROUP)), const((2, D_B)),
        const((2, N_GROUP, GROUP, GROUP)), const((2, D_B)), const((2, D_B)),
        const((n_masks, CHUNK, GROUP)), const((2, 2 * tm, 3 * tm)), const((GROUP, GROUP)),
    ]
    args += [mod, p["g_pre_mix"], p["w_in"], p["wup_bd"], p["w0"], p["aup_bd"], p["a0"], p["g_up"],
             p["k_k"], p["k_a"], p["r_k"], p["seg_ones"],
             p["conv_w"], p["conv_b"], p["wa_bd"], p["ba"], p["wx_bd"], p["bx"], p["lam"],
             p["chunk_masks"], p["cumsum_mats"], p["bdm_bf16"]]
    tok_shape = jax.ShapeDtypeStruct((bsz, t, D_A), F32)
    dir_shape = jax.ShapeDtypeStruct((2, bsz, t, D_A), F32)
    row_shape = jax.ShapeDtypeStruct((2, bsz, nc, 1, D_A), F32)
    dir_spec = pl.BlockSpec((2, 1, tm, D_A), lambda i, b: (0, b, i, 0))
    row_spec = pl.BlockSpec((2, 1, CPT, 1, D_A), lambda i, b: (0, b, i, 0, 0))
    out_shape = [dir_shape, tok_shape, dir_shape, dir_shape, row_shape, dir_shape, tok_shape, row_shape,
                 tok_shape, tok_shape, tok_shape]
    out_specs = [dir_spec, tok(D_A), dir_spec, dir_spec, row_spec, dir_spec, tok(D_B), row_spec,
                 tok(D_A), tok(D_A), tok(D_B)]
    return pl.pallas_call(
        functools.partial(_feat_kernel, has_pos, nt),
        grid=(nt, bsz),
        in_specs=in_specs,
        out_specs=out_specs,
        out_shape=out_shape,
        compiler_params=pltpu.CompilerParams(dimension_semantics=("parallel", "parallel"),
                                             vmem_limit_bytes=VMEM_LIMIT),
        name="feat",
    )(*args)


def _scan_kernel(nt, *refs):
    (pf_ref, mf_ref, nf_ref, gtf_ref, acf_ref, blf_ref, q_ref, bs_ref,
     pb_ref, mb_ref, nb_ref, gtb_ref, acb_ref, blb_ref,
     s0_ref, l0_ref, bdm_ref,
     y_ref, hs_ref, s_ref, hl_ref) = refs

    i = pl.program_id(1)

    @pl.when(i == 0)
    def _init():
        y_ref[...] = jnp.zeros_like(y_ref)
        hs_ref[...] = jnp.zeros_like(hs_ref)
        s_ref[...] = s0_ref[...]
        hl_ref[...] = l0_ref[...]

    bdm = bdm_ref[...]
    per_dir = ((0, pf_ref, mf_ref, nf_ref, gtf_ref, acf_ref, blf_ref),
               (1, pb_ref, mb_ref, nb_ref, gtb_ref, acb_ref, blb_ref))
    for d, p_ref, m_ref, n_ref, gt_ref, ac_ref, bl_ref in per_dir:
        tile = i if d == 0 else nt - 1 - i
        state = s_ref[0, d]
        h0 = hl_ref[0, d:d + 1, :]
        for cc in range(CPT):
            c = cc if d == 0 else CPT - 1 - cc
            rs = slice(c * CHUNK, (c + 1) * CHUNK)
            rows = pl.ds(pl.multiple_of(tile * TOKEN_TILE + c * CHUNK, CHUNK), CHUNK)
            gt = gt_ref[0, 0, c]
            y_parts, s_parts = [], []
            for g in range(N_GROUP):
                ls = slice(g * GROUP, (g + 1) * GROUP)
                sg = state[:, ls]
                y_parts.append(_dot_nt(p_ref[0, 0, rs, ls], _blockdiag(sg, bdm)))
                s_parts.append(sg * gt[:, ls]
                               + jnp.dot(sg.astype(BF16), _blockdiag(m_ref[0, 0, rs, ls], bdm),
                                         preferred_element_type=F32)
                               + n_ref[0, 0, rs, ls])
            state = jnp.concatenate(s_parts, axis=1)
            y_c = jnp.concatenate(y_parts, axis=1)
            acum = ac_ref[0, 0, rs, :]
            h_c = acum * h0
            if d == 0:
                y_c = y_c + q_ref[0, rs, :]
                h_c = h_c + bs_ref[0, rs, :]
            y_ref[0, rows, :] += y_c
            hs_ref[0, rows, :] += h_c
            last = CHUNK - 1 if d == 0 else 0
            h0 = bl_ref[0, 0, c] + acum[last:last + 1, :] * h0
        s_ref[0, d] = state
        hl_ref[0, d:d + 1, :] = h0


def _scan_call(feats, s0, l0, p):
    pm, q, mm, nm, gt, ac, bs, bl = feats
    _, bsz, t, _ = pm.shape
    nt = t // TOKEN_TILE

    def dir_specs(d, tile_of):
        big = pl.BlockSpec((1, 1, TOKEN_TILE, D_A), lambda b, i: (d, b, tile_of(i), 0))
        small = pl.BlockSpec((1, 1, CPT, 1, D_A), lambda b, i: (d, b, tile_of(i), 0, 0))
        return big, small

    big_f, small_f = dir_specs(0, lambda i: i)
    big_b, small_b = dir_specs(1, lambda i: nt - 1 - i)
    tok_f = pl.BlockSpec((1, TOKEN_TILE, D_A), lambda b, i: (b, i, 0))
    in_specs = [big_f, big_f, big_f, small_f, big_f, small_f, tok_f, tok_f,
                big_b, big_b, big_b, small_b, big_b, small_b,
                pl.BlockSpec((1, 2, HEAD, D_A), lambda b, i: (b, 0, 0, 0)),
                pl.BlockSpec((1, 2, D_B), lambda b, i: (b, 0, 0)),
                pl.BlockSpec((GROUP, GROUP), lambda b, i: (0, 0))]
    args = [pm, mm, nm, gt, ac, bl, q, bs, pm, mm, nm, gt, ac, bl, s0, l0, p["bdm_bf16"]]
    seq_spec = pl.BlockSpec((1, t, D_A), lambda b, i: (b, 0, 0))
    out_specs = [seq_spec, seq_spec,
                 pl.BlockSpec((1, 2, HEAD, D_A), lambda b, i: (b, 0, 0, 0)),
                 pl.BlockSpec((1, 2, D_B), lambda b, i: (b, 0, 0))]
    out_shape = [jax.ShapeDtypeStruct((bsz, t, D_A), F32), jax.ShapeDtypeStruct((bsz, t, D_B), F32),
                 jax.ShapeDtypeStruct((bsz, 2, HEAD, D_A), F32),
                 jax.ShapeDtypeStruct((bsz, 2, D_B), F32)]
    return pl.pallas_call(
        functools.partial(_scan_kernel, nt),
        grid=(bsz, nt),
        in_specs=in_specs,
        out_specs=out_specs,
        out_shape=out_shape,
        compiler_params=pltpu.CompilerParams(dimension_semantics=("parallel", "arbitrary"),
                                             vmem_limit_bytes=VMEM_LIMIT),
        name="scan",
    )(*args)


def _out_kernel(has_pos, *refs):
    if has_pos:
        x_ref, pos_ref = refs[0], refs[1]
        refs = refs[2:]
    else:
        x_ref, pos_ref = refs[0], None
        refs = refs[1:]
    (y_ref, hs_ref, g_ref, bonus_ref, gate_ref, mod_ref,
     avg_ref, lnxg_ref, lnxb_ref, wout_ref, gpost_ref, gpre2_ref, w1_ref, w2_ref, gpost2_ref,
     o_ref) = refs

    x = x_ref[0]
    if has_pos:
        x = x + pos_ref[...]
    mod = mod_ref[0]
    gate1, shift2, scale2, gate2 = mod[2:3], mod[3:4], mod[4:5], mod[5:6]

    y = y_ref[0]
    avg = avg_ref[...]
    y_hi = y.astype(BF16)
    y_lo = (y - y_hi.astype(F32)).astype(BF16)
    mu = jnp.dot(y_hi, avg, preferred_element_type=F32) + jnp.dot(y_lo, avg, preferred_element_type=F32)
    yc = y - mu
    var = _dot(yc * yc, avg)
    yn = yc * lax.rsqrt(var + LNX_EPS) * lnxg_ref[...] + lnxb_ref[...]
    out_a = (yn + bonus_ref[0]) * g_ref[0]
    out_b = hs_ref[0] * gate_ref[0]
    mix = _dot(jnp.concatenate([out_a, out_b], axis=1), wout_ref[...])
    ms = jnp.mean(mix * mix, axis=-1, keepdims=True)
    x = x + gate1 * ((mix * lax.rsqrt(ms + EPS)) * gpost_ref[...])

    ms = jnp.mean(x * x, axis=-1, keepdims=True)
    h = (x * lax.rsqrt(ms + EPS)) * gpre2_ref[...]
    h = h * (1.0 + scale2) + shift2
    f = _dot(h, w1_ref[...])
    f = jnp.square(jnp.maximum(f, 0.0))
    f = _dot(f, w2_ref[...])
    ms = jnp.mean(f * f, axis=-1, keepdims=True)
    o_ref[0] = x + gate2 * ((f * lax.rsqrt(ms + EPS)) * gpost2_ref[...])


def _out_call(x, pos, mod, mod_row, y, hs, g, bonus, gate, p):
    bsz, t, _ = x.shape
    tm = TOKEN_TILE
    nt = t // tm
    has_pos = pos is not None

    def const(shape):
        return pl.BlockSpec(shape, lambda i, b: (0,) * len(shape))

    tok = lambda width: pl.BlockSpec((1, tm, width), lambda i, b: (b, i, 0))
    in_specs = [tok(D_MODEL)]
    args = [x]
    if has_pos:
        in_specs.append(pl.BlockSpec((tm, D_MODEL), lambda i, b: (i, 0)))
        args.append(pos)
    in_specs += [tok(D_A)] * 5
    in_specs += [
        pl.BlockSpec((1, 6, D_MODEL), lambda i, b: (mod_row(b), 0, 0)),
        const((D_A, D_A)), const((1, D_A)), const((1, D_A)),
        const((D_MODEL, D_MODEL)), const((1, D_MODEL)), const((1, D_MODEL)),
        const((D_MODEL, D_FF)), const((D_FF, D_MODEL)), const((1, D_MODEL)),
    ]
    args += [y, hs, g, bonus, gate, mod,
             p["seg_avg"], p["lnx_g"], p["lnx_b"], p["w_out"], p["g_post_mix"], p["g_pre_mlp"],
             p["w_mlp1"], p["w_mlp2"], p["g_post_mlp"]]
    return pl.pallas_call(
        functools.partial(_out_kernel, has_pos),
        grid=(nt, bsz),
        in_specs=in_specs,
        out_specs=tok(D_MODEL),
        out_shape=jax.ShapeDtypeStruct((bsz, t, D_MODEL), F32),
        compiler_params=pltpu.CompilerParams(dimension_semantics=("parallel", "parallel"),
                                             vmem_limit_bytes=VMEM_LIMIT),
        name="out",
    )(*args)


def _sincos_1d(pos, dim):
    omega = 1.0 / (10000.0 ** (jnp.arange(dim // 2, dtype=F32) / (dim // 2)))
    ang = pos.astype(F32)[:, None] * omega[None, :]
    return jnp.concatenate([jnp.sin(ang), jnp.cos(ang)], axis=-1)


def _grid_pos_embed(n_tokens):
    rows = n_tokens // GRID_W
    half = D_MODEL // 2
    e_row = _sincos_1d(jnp.arange(rows), half)
    e_col = _sincos_1d(jnp.arange(GRID_W), half)
    emb = jnp.concatenate([jnp.broadcast_to(e_row[:, None, :], (rows, GRID_W, half)),
                           jnp.broadcast_to(e_col[None, :, :], (rows, GRID_W, half))], axis=-1)
    return emb.reshape(rows * GRID_W, D_MODEL)


def _blockdiag_pairs(w):
    z = jnp.zeros_like(w[0])
    return jnp.concatenate([jnp.concatenate([w[0], z], axis=1),
                            jnp.concatenate([z, w[1]], axis=1)], axis=0)


def _heads_to_blockdiag(w):
    lead = w.shape[:-3]
    w = w.reshape(lead + (N_GROUP, HEADS_PER_GROUP, HEAD, HEAD))
    eye = jnp.eye(HEADS_PER_GROUP, dtype=w.dtype)
    bd = jnp.einsum('...ghab,hj->...ghajb', w, eye)
    return bd.reshape(lead + (N_GROUP, GROUP, GROUP))


def _state_to_lanes(s):
    b = s.shape[0]
    return jnp.transpose(s, (0, 1, 3, 2, 4)).reshape(b, 2, HEAD, D_A)


def _state_from_lanes(s):
    b = s.shape[0]
    return jnp.transpose(s.reshape(b, 2, HEAD, N_HEAD, HEAD), (0, 1, 3, 2, 4))


def kernel(x_prompt, x_sample, c, state_rwkv, state_lru, c_ctx, w_mod, b_mod, g_pre_mix, g_post_mix,
           g_pre_mlp, g_post_mlp, w_in, rwkv_w0, rwkv_w_up, rwkv_a0, rwkv_a_up, rwkv_g_up, rwkv_k_k,
           rwkv_k_a, rwkv_r_k, rwkv_lnx_g, rwkv_lnx_b, lru_conv_w, lru_conv_b, lru_wa, lru_ba, lru_wx,
           lru_bx, lru_lambda, w_out, w_mlp1, w_mlp2):
    n_ctx = x_prompt.shape[0]
    n_lat = x_sample.shape[0]
    l = 0
    seg = _group_blockdiag_mask()
    seg512 = np.kron(np.eye(N_GROUP, dtype=np.float32), seg)
    p = {
        "g_pre_mix": g_pre_mix[l][None], "g_post_mix": g_post_mix[l][None],
        "g_pre_mlp": g_pre_mlp[l][None], "g_post_mlp": g_post_mlp[l][None],
        "w_in": w_in[l].astype(BF16), "w_out": w_out[l].astype(BF16),
        "w_mlp1": w_mlp1[l].astype(BF16), "w_mlp2": w_mlp2[l].astype(BF16),
        "w0": rwkv_w0[l].reshape(1, 2 * D_A), "a0": rwkv_a0[l].reshape(1, 2 * D_A),
        "wup_bd": _blockdiag_pairs(rwkv_w_up[l]).astype(BF16),
        "aup_bd": _blockdiag_pairs(rwkv_a_up[l]).astype(BF16),
        "g_up": rwkv_g_up[l].astype(BF16),
        "k_k": rwkv_k_k[l][None], "k_a": rwkv_k_a[l][None], "r_k": rwkv_r_k[l].reshape(1, D_A),
        "lnx_g": rwkv_lnx_g[l][None], "lnx_b": rwkv_lnx_b[l][None],
        "conv_w": lru_conv_w[l], "conv_b": lru_conv_b[l][None],
        "wa_bd": _heads_to_blockdiag(lru_wa[l]).astype(BF16), "ba": lru_ba[l],
        "wx_bd": _heads_to_blockdiag(lru_wx[l]).astype(BF16), "bx": lru_bx[l],
        "lam": lru_lambda[l],
        "seg_ones": jnp.asarray(seg512, BF16), "seg_avg": jnp.asarray(seg512 / HEAD, BF16),
        "chunk_masks": jnp.asarray(_chunk_masks()), "cumsum_mats": jnp.asarray(_cumsum_mats(), BF16),
        "bdm_bf16": jnp.asarray(seg, BF16),
    }

    m_rows = 16
    c_all = jnp.concatenate([c_ctx[None], c, jnp.zeros((m_rows - 1 - n_lat, D_MODEL), F32)], axis=0)
    mod = _mod_call(c_all, w_mod[l], b_mod[l]).reshape(m_rows, 6, D_MODEL)

    pos = _grid_pos_embed(x_sample.shape[1]).astype(x_sample.dtype)
    ctx_row = lambda b: 0
    lat_row = lambda b: b + 1

    feats = _feat_call(x_prompt, None, mod, ctx_row, p)
    y, hs, s_ctx, l_ctx = _scan_call(
        feats[:8], jnp.zeros((n_ctx, 2, HEAD, D_A), F32), jnp.zeros((n_ctx, 2, D_B), F32), p)
    y_prompt = _out_call(x_prompt, None, mod, ctx_row, y, hs, feats[8], feats[9], feats[10], p)

    feats = _feat_call(x_sample, pos, mod, lat_row, p)
    y, hs, _, _ = _scan_call(feats[:8], _state_to_lanes(state_rwkv[:, l]), state_lru[:, l], p)
    y_sample = _out_call(x_sample, pos, mod, lat_row, y, hs, feats[8], feats[9], feats[10], p)

    new_state_rwkv = _state_from_lanes(s_ctx)[:, None].astype(x_prompt.dtype)
    new_state_lru = l_ctx[:, None].astype(x_prompt.dtype)
    return (y_prompt, y_sample, new_state_rwkv, new_state_lru)
```

```python
import functools

import numpy as np
import jax
import jax.numpy as jnp
from jax import lax
from jax.experimental import pallas as pl
from jax.experimental.pallas import tpu as pltpu

F32 = jnp.float32
BF16 = jnp.bfloat16

D_MODEL = 1024
D_A = 512
D_B = 512
HEAD = 64
N_HEAD = 8
R_W = 64
R_A = 64
R_G = 128
D_FF = 4096
D_IN = 2944
GRID_W = 64
CONV_W = 4
LRU_C = 8.0
EPS = 1e-6
LNX_EPS = 64e-5

CHUNK = 64
GROUP = 128
HEADS_PER_GROUP = GROUP // HEAD
N_GROUP = D_A // GROUP
SUB = 8
LANE = 128
N_SLAB = D_A // LANE
HALO = 8
TOKEN_TILE = 256
CPT = TOKEN_TILE // CHUNK
VMEM_LIMIT = 60 * 1024 * 1024

_O_R, _O_K, _O_V, _O_XW, _O_XA, _O_XG, _O_XB, _O_GB = 0, 512, 1024, 1536, 1664, 1792, 1920, 2432

_M_STRICT, _M_INCL, _M_LEV0 = 0, 1, 2
_LEVELS = (1, 2, 4, 8, 16, 32)
_M_PER_DIR = 2 + len(_LEVELS)
_M_EYE = 2 * _M_PER_DIR


def _dot(a, b):
    return jnp.dot(a.astype(BF16), b.astype(BF16), preferred_element_type=F32)


def _dot_nt(a, b):
    return lax.dot_general(a.astype(BF16), b.astype(BF16), (((1,), (1,)), ((), ())),
                           preferred_element_type=F32)


def _dot_tn(a, b):
    return lax.dot_general(a.astype(BF16), b.astype(BF16), (((0,), (0,)), ((), ())),
                           preferred_element_type=F32)


def _strided_rows(ref, lead, base):
    return [ref[lead + (pl.ds(base + j, SUB, stride=SUB), slice(None))] for j in range(SUB)]


def _natural_rows(ref, lead, base, pieces):
    for j, piece in enumerate(pieces):
        ref[lead + (pl.ds(base + SUB * j, SUB), slice(None))] = piece
    return jnp.concatenate(_strided_rows(ref, lead, base), axis=0)


def _sublane_shift(x, steps, reverse, fill):
    sub = lax.broadcasted_iota(jnp.int32, x.shape, 0)
    if reverse:
        return jnp.where(sub < SUB - steps, pltpu.roll(x, SUB - steps, 0), fill)
    return jnp.where(sub >= steps, pltpu.roll(x, steps, 0), fill)


def _scan_affine(a, b, reverse):
    a, b = list(a), list(b)
    order = range(SUB - 2, -1, -1) if reverse else range(1, SUB)
    for j in order:
        p = j + 1 if reverse else j - 1
        b[j] = a[j] * b[p] + b[j]
        a[j] = a[j] * a[p]
    ta, tb = (a[0], b[0]) if reverse else (a[SUB - 1], b[SUB - 1])
    s = 1
    while s < SUB:
        tb = ta * _sublane_shift(tb, s, reverse, 0.0) + tb
        ta = ta * _sublane_shift(ta, s, reverse, 1.0)
        s *= 2
    ea = _sublane_shift(ta, 1, reverse, 1.0)
    eb = _sublane_shift(tb, 1, reverse, 0.0)
    return [x * ea for x in a], [x * eb + y for x, y in zip(a, b)]


def _scan_sum(x, reverse):
    x = list(x)
    order = range(SUB - 2, -1, -1) if reverse else range(1, SUB)
    for j in order:
        x[j] = x[j] + x[j + 1 if reverse else j - 1]
    t = x[0] if reverse else x[SUB - 1]
    s = 1
    while s < SUB:
        t = t + _sublane_shift(t, s, reverse, 0.0)
        s *= 2
    e = _sublane_shift(t, 1, reverse, 0.0)
    return [y + e for y in x]


def _blockdiag(x, bdm):
    xb = x.astype(BF16)
    return jnp.concatenate([xb] * HEADS_PER_GROUP, axis=0) * bdm


def _mod_kernel(c_ref, w_ref, b_ref, o_ref):
    c = c_ref[...]
    s = c * jax.nn.sigmoid(c)
    o_ref[...] = _dot(s, w_ref[...]) + b_ref[...]


def _mod_call(c_all, w_mod, b_mod):
    m = c_all.shape[0]
    n = w_mod.shape[1]
    tn = 1536
    return pl.pallas_call(
        _mod_kernel,
        grid=(n // tn,),
        in_specs=[pl.BlockSpec((m, D_MODEL), lambda j: (0, 0)),
                  pl.BlockSpec((D_MODEL, tn), lambda j: (0, j)),
                  pl.BlockSpec((1, tn), lambda j: (0, j))],
        out_specs=pl.BlockSpec((m, tn), lambda j: (0, j)),
        out_shape=jax.ShapeDtypeStruct((m, n), F32),
        compiler_params=pltpu.CompilerParams(dimension_semantics=("parallel",),
                                             vmem_limit_bytes=VMEM_LIMIT),
        name="mod",
    )(c_all, w_mod, b_mod.reshape(1, n))


def _chunk_masks():
    t = np.arange(CHUNK)[:, None]
    s = (np.arange(GROUP) % CHUNK)[None, :]
    rows = []
    for d in (0, 1):
        before = (s < t) if d == 0 else (s > t)
        rows.append(before)
        rows.append(before | (s == t))
        for b in _LEVELS:
            same = (t // (2 * b)) == (s // (2 * b))
            if d == 0:
                rows.append(same & ((t // b) % 2 == 1) & ((s // b) % 2 == 0))
            else:
                rows.append(same & ((t // b) % 2 == 0) & ((s // b) % 2 == 1))
    rows.append(s == t)
    return np.stack(rows).astype(np.float32)


def _group_blockdiag_mask():
    i = np.arange(GROUP)
    return ((i[:, None] // HEAD) == (i[None, :] // HEAD)).astype(np.float32)


def _feat_kernel(has_pos, nt, *refs):
    if has_pos:
        x_ref, xp_ref, xn_ref, pos_ref, pp_ref, pn_ref = refs[:6]
        refs = refs[6:]
    else:
        x_ref, xp_ref, xn_ref = refs[:3]
        pos_ref = pp_ref = pn_ref = None
        refs = refs[3:]
    (mod_ref, gpre_ref, win_ref, wup_ref, w0_ref, aup_ref, a0_ref, gup_ref,
     kk_ref, ka_ref, rk_ref, seg_ref,
     convw_ref, convb_ref, wa_ref, ba_ref, wx_ref, bx_ref, lam_ref,
     masks_ref, bdm_ref,
     p_o, q_o, m_o, n_o, gt_o, ac_o, bs_o, bl_o, g_o, bonus_o, gate_o,
     lw_s, lc_s, la_s, lb_s, lo_s) = refs

    i = pl.program_id(0)
    tm = TOKEN_TILE
    mod = mod_ref[0]
    shift1, scale1 = mod[0:1], mod[1:2]

    def normmod(xv):
        ms = jnp.mean(xv * xv, axis=-1, keepdims=True)
        hv = (xv * lax.rsqrt(ms + EPS)) * gpre_ref[...]
        return hv * (1.0 + scale1) + shift1

    x = x_ref[0]
    halo = jnp.concatenate([xp_ref[0], xn_ref[0]], axis=0)
    if has_pos:
        x = x + pos_ref[...]
        halo = halo + jnp.concatenate([pp_ref[...], pn_ref[...]], axis=0)
    z = _dot(normmod(x), win_ref[...])
    zh = _dot(normmod(halo), win_ref[:, _O_XB:_O_XB + D_B])

    r = z[:, _O_R:_O_R + D_A]
    k = z[:, _O_K:_O_K + D_A]
    v = z[:, _O_V:_O_V + D_A]
    xw = z[:, _O_XW:_O_XW + 2 * R_W]
    xa = z[:, _O_XA:_O_XA + 2 * R_A]
    xg = z[:, _O_XG:_O_XG + R_G]

    g_o[0] = _dot(jax.nn.sigmoid(xg), gup_ref[...])
    gate_o[0] = jax.nn.gelu(z[:, _O_GB:_O_GB + D_B], approximate=True)
    wl = w0_ref[...] + _dot(jnp.tanh(xw), wup_ref[...])
    lw2 = -jax.nn.sigmoid(wl) * float(np.exp(-0.5))
    a2 = jax.nn.sigmoid(a0_ref[...] + _dot(xa, aup_ref[...]))

    kks = k * kk_ref[...]
    ss = _dot(kks * kks, seg_ref[...])
    kk = kks / jnp.maximum(jnp.sqrt(ss), 1e-12)
    ka = ka_ref[...]
    kd2 = [k * (1.0 + (a2[:, d * D_A:(d + 1) * D_A] - 1.0) * ka) for d in (0, 1)]
    bonus_o[0] = _dot(r * (kd2[0] + kd2[1]) * rk_ref[...], seg_ref[...]) * v

    m_prev = jnp.where(i > 0, 1.0, 0.0)
    m_next = jnp.where(i < nt - 1, 1.0, 0.0)
    ext = jnp.concatenate([zh[:HALO] * m_prev, z[:, _O_XB:_O_XB + D_B], zh[HALO:] * m_next], axis=0)
    n_ext = tm + 2 * HALO
    xc = convb_ref[...]
    for j in range(CONV_W):
        sh = (2 - j) % n_ext
        tap = ext if sh == 0 else pltpu.roll(ext, sh, 0)
        xc = xc + tap[HALO:HALO + tm] * convw_ref[j:j + 1, :]
    for d in (0, 1):
        rg = jnp.concatenate([_dot(xc[:, g * GROUP:(g + 1) * GROUP], wa_ref[d, g]) for g in range(N_GROUP)], 1)
        ig = jnp.concatenate([_dot(xc[:, g * GROUP:(g + 1) * GROUP], wx_ref[d, g]) for g in range(N_GROUP)], 1)
        rg = jax.nn.sigmoid(rg + ba_ref[d:d + 1, :])
        ig = jax.nn.sigmoid(ig + bx_ref[d:d + 1, :])
        log_a = -LRU_C * rg * jax.nn.softplus(-lam_ref[d:d + 1, :])
        a_lru = jnp.exp(log_a)
        b_lru = jnp.sqrt(-jnp.tanh(log_a) * (a_lru * a_lru + 1.0)) * (ig * xc)
        for q in range(N_SLAB):
            la_s[d, q] = a_lru[:, q * LANE:(q + 1) * LANE]
            lb_s[d, q] = b_lru[:, q * LANE:(q + 1) * LANE]
    for c in range(CPT):
        base = c * CHUNK
        rs = slice(base, base + CHUNK)
        for q in range(N_SLAB):
            ls = slice(q * LANE, (q + 1) * LANE)
            bsum = None
            for d in (0, 1):
                acum, bcum = _scan_affine(_strided_rows(la_s, (d, q), base), _strided_rows(lb_s, (d, q), base),
                                          d == 1)
                ac_o[d, 0, rs, ls] = _natural_rows(lo_s, (d, q), base, acum)
                bsum = bcum if bsum is None else [x + y for x, y in zip(bsum, bcum)]
                bl_o[d, 0, c, :, ls] = bcum[0][0:1, :] if d == 1 else bcum[SUB - 1][SUB - 1:SUB, :]
            bs_o[0, rs, ls] = _natural_rows(lo_s, (2, q), base, bsum)

    bdm = bdm_ref[...]
    bd = lambda xv: _blockdiag(xv, bdm)
    bdot = lambda lhs, rhs: jnp.dot(lhs.astype(BF16), bd(rhs), preferred_element_type=F32)

    a_t, r_t, b_t, k_t, g_tot = [], [], [], [], []
    for d in (0, 1):
        lw = lw2[:, d * D_A:(d + 1) * D_A]
        for q in range(N_SLAB):
            lw_s[d, q] = lw[:, q * LANE:(q + 1) * LANE]
        g_tot.append([])
        lc_rows = []
        for c in range(CPT):
            base = c * CHUNK
            blocks, totals = [], []
            for q in range(N_SLAB):
                pieces = _scan_sum(_strided_rows(lw_s, (d, q), base), d == 1)
                totals.append(pieces[0][0:1, :] if d == 1 else pieces[SUB - 1][SUB - 1:SUB, :])
                blocks.append(_natural_rows(lc_s, (d, q), base, pieces))
            lc_rows.append(jnp.concatenate(blocks, axis=1))
            g_tot[d].append(jnp.exp(jnp.concatenate(totals, axis=1)))
            gt_o[d, 0, c] = g_tot[d][c]
        lc = jnp.concatenate(lc_rows, axis=0)
        e_neg = jnp.exp(-lc)
        a_t.append(-kk * jnp.exp(lc - lw))
        r_t.append(r * jnp.exp(lc))
        b_t.append(kk * a2[:, d * D_A:(d + 1) * D_A] * e_neg)
        k_t.append(kd2[d] * e_neg)

    combos = [(d, c, g) for d in (0, 1) for c in range(CPT) for g in range(N_GROUP)]

    def cut(arr, c, g):
        return arr[c * CHUNK:(c + 1) * CHUNK, g * GROUP:(g + 1) * GROUP]

    sc = {}
    for key in combos:
        d, c, g = key
        ar = jnp.concatenate([cut(a_t[d], c, g), cut(r_t[d], c, g)], axis=0)
        rhs = jnp.concatenate([bd(cut(b_t[d], c, g)), bd(cut(k_t[d], c, g))], axis=0)
        sc[key] = _dot_nt(ar, rhs)
    n_ab, n_ak, n_rb, n_rk, tinv = {}, {}, {}, {}, {}
    for key in combos:
        m0 = key[0] * _M_PER_DIR
        n_ab[key] = sc[key][:CHUNK, :GROUP] * masks_ref[m0 + _M_STRICT]
        n_ak[key] = sc[key][:CHUNK, GROUP:] * masks_ref[m0 + _M_STRICT]
        n_rb[key] = sc[key][CHUNK:, :GROUP] * masks_ref[m0 + _M_INCL]
        n_rk[key] = sc[key][CHUNK:, GROUP:] * masks_ref[m0 + _M_INCL]
        tinv[key] = masks_ref[_M_EYE] + n_ab[key] * masks_ref[m0 + _M_LEV0]
    for li in range(1, len(_LEVELS)):
        pm = {}
        for key in combos:
            pm[key] = bdot(n_ab[key] * masks_ref[key[0] * _M_PER_DIR + _M_LEV0 + li], tinv[key])
        for key in combos:
            tinv[key] = tinv[key] + bdot(tinv[key], pm[key])

    kv = {key: bdot(jnp.concatenate([n_ak[key], n_rk[key]], axis=0), cut(v, key[1], key[2])) for key in combos}
    gm = {key: bdot(n_rb[key], tinv[key]) for key in combos}
    wp = {}
    for key in combos:
        d, c, g = key
        lhs = jnp.concatenate([tinv[key], gm[key]], axis=0)
        rhs = jnp.concatenate([bd(cut(a_t[d], c, g)), bd(kv[key][:CHUNK])], axis=1)
        wp[key] = jnp.dot(lhs.astype(BF16), rhs, preferred_element_type=F32)

    lane = lax.broadcasted_iota(jnp.int32, (CHUNK, GROUP), 1)

    def fold(full):
        out = full[(HEADS_PER_GROUP - 1) * HEAD:]
        for h in range(HEADS_PER_GROUP - 2, -1, -1):
            out = jnp.where(lane < (h + 1) * HEAD, full[h * HEAD:(h + 1) * HEAD], out)
        return out

    for key in combos:
        d, c, g = key
        rs = slice(c * CHUNK, (c + 1) * CHUNK)
        ls = slice(g * GROUP, (g + 1) * GROUP)
        w_, u0 = wp[key][:CHUNK, :GROUP], wp[key][:CHUNK, GROUP:]
        vg = cut(v, c, g)
        gt = g_tot[d][c][:, ls]
        bh, kh = cut(b_t[d], c, g) * gt, cut(k_t[d], c, g) * gt
        p_o[d, 0, rs, ls] = cut(r_t[d], c, g) + wp[key][CHUNK:, :GROUP]
        m_o[d, 0, rs, ls] = fold(_dot_tn(w_, bh))
        n_o[d, 0, rs, ls] = fold(_dot_tn(jnp.concatenate([u0, vg], axis=0), jnp.concatenate([bh, kh], axis=0)))
        if d == 1:
            other = (0, c, g)
            q_o[0, rs, ls] = ((wp[other][CHUNK:, GROUP:] + kv[other][CHUNK:])
                              + (wp[key][CHUNK:, GROUP:] + kv[key][CHUNK:]))


def _feat_call(x, pos, mod, mod_row, p):
    bsz, t, _ = x.shape
    tm = TOKEN_TILE
    nt = t // tm
    nc = t // CHUNK
    hpt = tm // HALO
    has_pos = pos is not None

    def const(shape):
        return pl.BlockSpec(shape, lambda i, b: (0,) * len(shape))

    tok = lambda width: pl.BlockSpec((1, tm, width), lambda i, b: (b, i, 0))
    prev_i = lambda i: jnp.maximum(i * hpt - 1, 0)
    next_i = lambda i: jnp.minimum((i + 1) * hpt, t // HALO - 1)
    in_specs = [tok(D_MODEL),
                pl.BlockSpec((1, HALO, D_MODEL), lambda i, b: (b, prev_i(i), 0)),
                pl.BlockSpec((1, HALO, D_MODEL), lambda i, b: (b, next_i(i), 0))]
    args = [x, x, x]
    if has_pos:
        in_specs += [pl.BlockSpec((tm, D_MODEL), lambda i, b: (i, 0)),
                     pl.BlockSpec((HALO, D_MODEL), lambda i, b: (prev_i(i), 0)),
                     pl.BlockSpec((HALO, D_MODEL), lambda i, b: (next_i(i), 0))]
        args += [pos, pos, pos]
    n_masks = 2 * _M_PER_DIR + 1
    in_specs += [
        pl.BlockSpec((1, 6, D_MODEL), lambda i, b: (mod_row(b), 0, 0)),
        const((1, D_MODEL)), const((D_MODEL, D_IN)),
        const((2 * R_W, 2 * D_A)), const((1, 2 * D_A)),
        const((2 * R_A, 2 * D_A)), const((1, 2 * D_A)),
        const((R_G, D_A)),
        const((1, D_A)), const((1, D_A)), const((1, D_A)), const((D_A, D_A)),
        const((CONV_W, D_B)), const((1, D_B)),
        const((2, N_GROUP, GROUP, GROUP)), const((2, D_B)),
        const((2, N_GROUP, GROUP, GROUP)), const((2, D_B)), const((2, D_B)),
        const((n_masks, CHUNK, GROUP)), const((GROUP, GROUP)),
    ]
    args += [mod, p["g_pre_mix"], p["w_in"], p["wup_bd"], p["w0"], p["aup_bd"], p["a0"], p["g_up"],
             p["k_k"], p["k_a"], p["r_k"], p["seg_ones"],
             p["conv_w"], p["conv_b"], p["wa_bd"], p["ba"], p["wx_bd"], p["bx"], p["lam"],
             p["chunk_masks"], p["bdm_bf16"]]
    tok_shape = jax.ShapeDtypeStruct((bsz, t, D_A), F32)
    dir_shape = jax.ShapeDtypeStruct((2, bsz, t, D_A), F32)
    row_shape = jax.ShapeDtypeStruct((2, bsz, nc, 1, D_A), F32)
    dir_spec = pl.BlockSpec((2, 1, tm, D_A), lambda i, b: (0, b, i, 0))
    row_spec = pl.BlockSpec((2, 1, CPT, 1, D_A), lambda i, b: (0, b, i, 0, 0))
    out_shape = [dir_shape, tok_shape, dir_shape, dir_shape, row_shape, dir_shape, tok_shape, row_shape,
                 tok_shape, tok_shape, tok_shape]
    out_specs = [dir_spec, tok(D_A), dir_spec, dir_spec, row_spec, dir_spec, tok(D_B), row_spec,
                 tok(D_A), tok(D_A), tok(D_B)]
    return pl.pallas_call(
        functools.partial(_feat_kernel, has_pos, nt),
        grid=(nt, bsz),
        in_specs=in_specs,
        out_specs=out_specs,
        out_shape=out_shape,
        scratch_shapes=[pltpu.VMEM((2, N_SLAB, tm, LANE), F32)] * 4 + [pltpu.VMEM((3, N_SLAB, tm, LANE), F32)],
        compiler_params=pltpu.CompilerParams(dimension_semantics=("parallel", "parallel"),
                                             vmem_limit_bytes=VMEM_LIMIT),
        name="feat",
    )(*args)


def _scan_kernel(nt, *refs):
    (pf_ref, mf_ref, nf_ref, gtf_ref, acf_ref, blf_ref, q_ref, bs_ref,
     pb_ref, mb_ref, nb_ref, gtb_ref, acb_ref, blb_ref,
     s0_ref, l0_ref, bdm_ref,
     y_ref, hs_ref, s_ref, hl_ref) = refs

    i = pl.program_id(1)

    @pl.when(i == 0)
    def _init():
        y_ref[...] = jnp.zeros_like(y_ref)
        hs_ref[...] = jnp.zeros_like(hs_ref)
        s_ref[...] = s0_ref[...]
        hl_ref[...] = l0_ref[...]

    bdm = bdm_ref[...]
    per_dir = ((0, pf_ref, mf_ref, nf_ref, gtf_ref, acf_ref, blf_ref),
               (1, pb_ref, mb_ref, nb_ref, gtb_ref, acb_ref, blb_ref))
    for d, p_ref, m_ref, n_ref, gt_ref, ac_ref, bl_ref in per_dir:
        tile = i if d == 0 else nt - 1 - i
        state = s_ref[0, d]
        h0 = hl_ref[0, d:d + 1, :]
        for cc in range(CPT):
            c = cc if d == 0 else CPT - 1 - cc
            rs = slice(c * CHUNK, (c + 1) * CHUNK)
            rows = pl.ds(pl.multiple_of(tile * TOKEN_TILE + c * CHUNK, CHUNK), CHUNK)
            gt = gt_ref[0, 0, c]
            y_parts, s_parts = [], []
            for g in range(N_GROUP):
                ls = slice(g * GROUP, (g + 1) * GROUP)
                sg = state[:, ls]
                y_parts.append(_dot_nt(p_ref[0, 0, rs, ls], _blockdiag(sg, bdm)))
                s_parts.append(sg * gt[:, ls]
                               + jnp.dot(sg.astype(BF16), _blockdiag(m_ref[0, 0, rs, ls], bdm),
                                         preferred_element_type=F32)
                               + n_ref[0, 0, rs, ls])
            state = jnp.concatenate(s_parts, axis=1)
            y_c = jnp.concatenate(y_parts, axis=1)
            acum = ac_ref[0, 0, rs, :]
            h_c = acum * h0
            if d == 0:
                y_c = y_c + q_ref[0, rs, :]
                h_c = h_c + bs_ref[0, rs, :]
            y_ref[0, rows, :] += y_c
            hs_ref[0, rows, :] += h_c
            last = CHUNK - 1 if d == 0 else 0
            h0 = bl_ref[0, 0, c] + acum[last:last + 1, :] * h0
        s_ref[0, d] = state
        hl_ref[0, d:d + 1, :] = h0


def _scan_call(feats, s0, l0, p):
    pm, q, mm, nm, gt, ac, bs, bl = feats
    _, bsz, t, _ = pm.shape
    nt = t // TOKEN_TILE

    def dir_specs(d, tile_of):
        big = pl.BlockSpec((1, 1, TOKEN_TILE, D_A), lambda b, i: (d, b, tile_of(i), 0))
        small = pl.BlockSpec((1, 1, CPT, 1, D_A), lambda b, i: (d, b, tile_of(i), 0, 0))
        return big, small

    big_f, small_f = dir_specs(0, lambda i: i)
    big_b, small_b = dir_specs(1, lambda i: nt - 1 - i)
    tok_f = pl.BlockSpec((1, TOKEN_TILE, D_A), lambda b, i: (b, i, 0))
    in_specs = [big_f, big_f, big_f, small_f, big_f, small_f, tok_f, tok_f,
                big_b, big_b, big_b, small_b, big_b, small_b,
                pl.BlockSpec((1, 2, HEAD, D_A), lambda b, i: (b, 0, 0, 0)),
                pl.BlockSpec((1, 2, D_B), lambda b, i: (b, 0, 0)),
                pl.BlockSpec((GROUP, GROUP), lambda b, i: (0, 0))]
    args = [pm, mm, nm, gt, ac, bl, q, bs, pm, mm, nm, gt, ac, bl, s0, l0, p["bdm_bf16"]]
    seq_spec = pl.BlockSpec((1, t, D_A), lambda b, i: (b, 0, 0))
    out_specs = [seq_spec, seq_spec,
                 pl.BlockSpec((1, 2, HEAD, D_A), lambda b, i: (b, 0, 0, 0)),
                 pl.BlockSpec((1, 2, D_B), lambda b, i: (b, 0, 0))]
    out_shape = [jax.ShapeDtypeStruct((bsz, t, D_A), F32), jax.ShapeDtypeStruct((bsz, t, D_B), F32),
                 jax.ShapeDtypeStruct((bsz, 2, HEAD, D_A), F32),
                 jax.ShapeDtypeStruct((bsz, 2, D_B), F32)]
    return pl.pallas_call(
        functools.partial(_scan_kernel, nt),
        grid=(bsz, nt),
        in_specs=in_specs,
        out_specs=out_specs,
        out_shape=out_shape,
        compiler_params=pltpu.CompilerParams(dimension_semantics=("parallel", "arbitrary"),
                                             vmem_limit_bytes=VMEM_LIMIT),
        name="scan",
    )(*args)


def _out_kernel(has_pos, *refs):
    if has_pos:
        x_ref, pos_ref = refs[0], refs[1]
        refs = refs[2:]
    else:
        x_ref, pos_ref = refs[0], None
        refs = refs[1:]
    (y_ref, hs_ref, g_ref, bonus_ref, gate_ref, mod_ref,
     avg_ref, lnxg_ref, lnxb_ref, wout_ref, gpost_ref, gpre2_ref, w1_ref, w2_ref, gpost2_ref,
     o_ref) = refs

    x = x_ref[0]
    if has_pos:
        x = x + pos_ref[...]
    mod = mod_ref[0]
    gate1, shift2, scale2, gate2 = mod[2:3], mod[3:4], mod[4:5], mod[5:6]

    y = y_ref[0]
    avg = avg_ref[...]
    y_hi = y.astype(BF16)
    y_lo = (y - y_hi.astype(F32)).astype(BF16)
    mu = jnp.dot(y_hi, avg, preferred_element_type=F32) + jnp.dot(y_lo, avg, preferred_element_type=F32)
    yc = y - mu
    var = _dot(yc * yc, avg)
    yn = yc * lax.rsqrt(var + LNX_EPS) * lnxg_ref[...] + lnxb_ref[...]
    out_a = (yn + bonus_ref[0]) * g_ref[0]
    out_b = hs_ref[0] * gate_ref[0]
    mix = _dot(jnp.concatenate([out_a, out_b], axis=1), wout_ref[...])
    ms = jnp.mean(mix * mix, axis=-1, keepdims=True)
    x = x + gate1 * ((mix * lax.rsqrt(ms + EPS)) * gpost_ref[...])

    ms = jnp.mean(x * x, axis=-1, keepdims=True)
    h = (x * lax.rsqrt(ms + EPS)) * gpre2_ref[...]
    h = h * (1.0 + scale2) + shift2
    f = _dot(h, w1_ref[...])
    f = jnp.square(jnp.maximum(f, 0.0))
    f = _dot(f, w2_ref[...])
    ms = jnp.mean(f * f, axis=-1, keepdims=True)
    o_ref[0] = x + gate2 * ((f * lax.rsqrt(ms + EPS)) * gpost2_ref[...])


def _out_call(x, pos, mod, mod_row, y, hs, g, bonus, gate, p):
    bsz, t, _ = x.shape
    tm = TOKEN_TILE
    nt = t // tm
    has_pos = pos is not None

    def const(shape):
        return pl.BlockSpec(shape, lambda i, b: (0,) * len(shape))

    tok = lambda width: pl.BlockSpec((1, tm, width), lambda i, b: (b, i, 0))
    in_specs = [tok(D_MODEL)]
    args = [x]
    if has_pos:
        in_specs.append(pl.BlockSpec((tm, D_MODEL), lambda i, b: (i, 0)))
        args.append(pos)
    in_specs += [tok(D_A)] * 5
    in_specs += [
        pl.BlockSpec((1, 6, D_MODEL), lambda i, b: (mod_row(b), 0, 0)),
        const((D_A, D_A)), const((1, D_A)), const((1, D_A)),
        const((D_MODEL, D_MODEL)), const((1, D_MODEL)), const((1, D_MODEL)),
        const((D_MODEL, D_FF)), const((D_FF, D_MODEL)), const((1, D_MODEL)),
    ]
    args += [y, hs, g, bonus, gate, mod,
             p["seg_avg"], p["lnx_g"], p["lnx_b"], p["w_out"], p["g_post_mix"], p["g_pre_mlp"],
             p["w_mlp1"], p["w_mlp2"], p["g_post_mlp"]]
    return pl.pallas_call(
        functools.partial(_out_kernel, has_pos),
        grid=(nt, bsz),
        in_specs=in_specs,
        out_specs=tok(D_MODEL),
        out_shape=jax.ShapeDtypeStruct((bsz, t, D_MODEL), F32),
        compiler_params=pltpu.CompilerParams(dimension_semantics=("parallel", "parallel"),
                                             vmem_limit_bytes=VMEM_LIMIT),
        name="out",
    )(*args)


def _sincos_1d(pos, dim):
    omega = 1.0 / (10000.0 ** (jnp.arange(dim // 2, dtype=F32) / (dim // 2)))
    ang = pos.astype(F32)[:, None] * omega[None, :]
    return jnp.concatenate([jnp.sin(ang), jnp.cos(ang)], axis=-1)


def _grid_pos_embed(n_tokens):
    rows = n_tokens // GRID_W
    half = D_MODEL // 2
    e_row = _sincos_1d(jnp.arange(rows), half)
    e_col = _sincos_1d(jnp.arange(GRID_W), half)
    emb = jnp.concatenate([jnp.broadcast_to(e_row[:, None, :], (rows, GRID_W, half)),
                           jnp.broadcast_to(e_col[None, :, :], (rows, GRID_W, half))], axis=-1)
    return emb.reshape(rows * GRID_W, D_MODEL)


def _blockdiag_pairs(w):
    z = jnp.zeros_like(w[0])
    return jnp.concatenate([jnp.concatenate([w[0], z], axis=1),
                            jnp.concatenate([z, w[1]], axis=1)], axis=0)


def _heads_to_blockdiag(w):
    lead = w.shape[:-3]
    w = w.reshape(lead + (N_GROUP, HEADS_PER_GROUP, HEAD, HEAD))
    eye = jnp.eye(HEADS_PER_GROUP, dtype=w.dtype)
    bd = jnp.einsum('...ghab,hj->...ghajb', w, eye)
    return bd.reshape(lead + (N_GROUP, GROUP, GROUP))


def _state_to_lanes(s):
    b = s.shape[0]
    return jnp.transpose(s, (0, 1, 3, 2, 4)).reshape(b, 2, HEAD, D_A)


def _state_from_lanes(s):
    b = s.shape[0]
    return jnp.transpose(s.reshape(b, 2, HEAD, N_HEAD, HEAD), (0, 1, 3, 2, 4))


def kernel(x_prompt, x_sample, c, state_rwkv, state_lru, c_ctx, w_mod, b_mod, g_pre_mix, g_post_mix,
           g_pre_mlp, g_post_mlp, w_in, rwkv_w0, rwkv_w_up, rwkv_a0, rwkv_a_up, rwkv_g_up, rwkv_k_k,
           rwkv_k_a, rwkv_r_k, rwkv_lnx_g, rwkv_lnx_b, lru_conv_w, lru_conv_b, lru_wa, lru_ba, lru_wx,
           lru_bx, lru_lambda, w_out, w_mlp1, w_mlp2):
    n_ctx = x_prompt.shape[0]
    n_lat = x_sample.shape[0]
    l = 0
    seg = _group_blockdiag_mask()
    seg512 = np.kron(np.eye(N_GROUP, dtype=np.float32), seg)
    p = {
        "g_pre_mix": g_pre_mix[l][None], "g_post_mix": g_post_mix[l][None],
        "g_pre_mlp": g_pre_mlp[l][None], "g_post_mlp": g_post_mlp[l][None],
        "w_in": w_in[l].astype(BF16), "w_out": w_out[l].astype(BF16),
        "w_mlp1": w_mlp1[l].astype(BF16), "w_mlp2": w_mlp2[l].astype(BF16),
        "w0": rwkv_w0[l].reshape(1, 2 * D_A), "a0": rwkv_a0[l].reshape(1, 2 * D_A),
        "wup_bd": _blockdiag_pairs(rwkv_w_up[l]).astype(BF16),
        "aup_bd": _blockdiag_pairs(rwkv_a_up[l]).astype(BF16),
        "g_up": rwkv_g_up[l].astype(BF16),
        "k_k": rwkv_k_k[l][None], "k_a": rwkv_k_a[l][None], "r_k": rwkv_r_k[l].reshape(1, D_A),
        "lnx_g": rwkv_lnx_g[l][None], "lnx_b": rwkv_lnx_b[l][None],
        "conv_w": lru_conv_w[l], "conv_b": lru_conv_b[l][None],
        "wa_bd": _heads_to_blockdiag(lru_wa[l]).astype(BF16), "ba": lru_ba[l],
        "wx_bd": _heads_to_blockdiag(lru_wx[l]).astype(BF16), "bx": lru_bx[l],
        "lam": lru_lambda[l],
        "seg_ones": jnp.asarray(seg512, BF16), "seg_avg": jnp.asarray(seg512 / HEAD, BF16),
        "chunk_masks": jnp.asarray(_chunk_masks()),
        "bdm_bf16": jnp.asarray(seg, BF16),
    }

    m_rows = 16
    c_all = jnp.concatenate([c_ctx[None], c, jnp.zeros((m_rows - 1 - n_lat, D_MODEL), F32)], axis=0)
    mod = _mod_call(c_all, w_mod[l], b_mod[l]).reshape(m_rows, 6, D_MODEL)

    pos = _grid_pos_embed(x_sample.shape[1]).astype(x_sample.dtype)
    ctx_row = lambda b: 0
    lat_row = lambda b: b + 1

    feats = _feat_call(x_prompt, None, mod, ctx_row, p)
    y, hs, s_ctx, l_ctx = _scan_call(
        feats[:8], jnp.zeros((n_ctx, 2, HEAD, D_A), F32), jnp.zeros((n_ctx, 2, D_B), F32), p)
    y_prompt = _out_call(x_prompt, None, mod, ctx_row, y, hs, feats[8], feats[9], feats[10], p)

    feats = _feat_call(x_sample, pos, mod, lat_row, p)
    y, hs, _, _ = _scan_call(feats[:8], _state_to_lanes(state_rwkv[:, l]), state_lru[:, l], p)
    y_sample = _out_call(x_sample, pos, mod, lat_row, y, hs, feats[8], feats[9], feats[10], p)

    new_state_rwkv = _state_from_lanes(s_ctx)[:, None].astype(x_prompt.dtype)
    new_state_lru = l_ctx[:, None].astype(x_prompt.dtype)
    return (y_prompt, y_sample, new_state_rwkv, new_state_lru)
```

```python
import functools

import numpy as np
import jax
import jax.numpy as jnp
from jax import lax
from jax.experimental import pallas as pl
from jax.experimental.pallas import tpu as pltpu

F32 = jnp.float32
BF16 = jnp.bfloat16

D_MODEL = 1024
D_A = 512
D_B = 512
HEAD = 64
N_HEAD = 8
R_W = 64
R_A = 64
R_G = 128
D_FF = 4096
D_IN = 2944
GRID_W = 64
CONV_W = 4
LRU_C = 8.0
EPS = 1e-6
LNX_EPS = 64e-5

CHUNK = 64
GROUP = 128
HEADS_PER_GROUP = GROUP // HEAD
N_GROUP = D_A // GROUP
SUB = 8
LANE = 128
N_SLAB = D_A // LANE
HALO = 8
TOKEN_TILE = 256
CPT = TOKEN_TILE // CHUNK
VMEM_LIMIT = 60 * 1024 * 1024

_O_R, _O_K, _O_V, _O_XW, _O_XA, _O_XG, _O_XB, _O_GB = 0, 512, 1024, 1536, 1664, 1792, 1920, 2432

_M_STRICT, _M_INCL, _M_LEV0 = 0, 1, 2
_LEVELS = (1, 2, 4, 8, 16, 32)
_M_PER_DIR = 2 + len(_LEVELS)
_M_EYE = 2 * _M_PER_DIR


def _dot(a, b):
    return jnp.dot(a.astype(BF16), b.astype(BF16), preferred_element_type=F32)


def _dot_nt(a, b):
    return lax.dot_general(a.astype(BF16), b.astype(BF16), (((1,), (1,)), ((), ())),
                           preferred_element_type=F32)


def _dot_tn(a, b):
    return lax.dot_general(a.astype(BF16), b.astype(BF16), (((0,), (0,)), ((), ())),
                           preferred_element_type=F32)


def _strided_rows(ref, lead, base):
    return [ref[lead + (pl.ds(base + j, SUB, stride=SUB), slice(None))] for j in range(SUB)]


def _natural_rows(ref, lead, base, pieces):
    for j, piece in enumerate(pieces):
        ref[lead + (pl.ds(base + SUB * j, SUB), slice(None))] = piece
    return jnp.concatenate(_strided_rows(ref, lead, base), axis=0)


def _sublane_shift(x, steps, reverse, fill):
    sub = lax.broadcasted_iota(jnp.int32, x.shape, 0)
    if reverse:
        return jnp.where(sub < SUB - steps, pltpu.roll(x, SUB - steps, 0), fill)
    return jnp.where(sub >= steps, pltpu.roll(x, steps, 0), fill)


def _scan_affine(a, b, reverse):
    a, b = list(a), list(b)
    order = range(SUB - 2, -1, -1) if reverse else range(1, SUB)
    for j in order:
        p = j + 1 if reverse else j - 1
        b[j] = a[j] * b[p] + b[j]
        a[j] = a[j] * a[p]
    ta, tb = (a[0], b[0]) if reverse else (a[SUB - 1], b[SUB - 1])
    s = 1
    while s < SUB:
        tb = ta * _sublane_shift(tb, s, reverse, 0.0) + tb
        ta = ta * _sublane_shift(ta, s, reverse, 1.0)
        s *= 2
    ea = _sublane_shift(ta, 1, reverse, 1.0)
    eb = _sublane_shift(tb, 1, reverse, 0.0)
    return [x * ea for x in a], [x * eb + y for x, y in zip(a, b)]


def _scan_sum(x, reverse):
    x = list(x)
    order = range(SUB - 2, -1, -1) if reverse else range(1, SUB)
    for j in order:
        x[j] = x[j] + x[j + 1 if reverse else j - 1]
    t = x[0] if reverse else x[SUB - 1]
    s = 1
    while s < SUB:
        t = t + _sublane_shift(t, s, reverse, 0.0)
        s *= 2
    e = _sublane_shift(t, 1, reverse, 0.0)
    return [y + e for y in x]


def _blockdiag(x, bdm):
    xb = x.astype(BF16)
    return jnp.concatenate([xb] * HEADS_PER_GROUP, axis=0) * bdm


def _mod_kernel(c_ref, w_ref, b_ref, o_ref):
    c = c_ref[...]
    s = c * jax.nn.sigmoid(c)
    o_ref[...] = _dot(s, w_ref[...]) + b_ref[...]


def _mod_call(c_all, w_mod, b_mod):
    m = c_all.shape[0]
    n = w_mod.shape[1]
    tn = 1536
    return pl.pallas_call(
        _mod_kernel,
        grid=(n // tn,),
        in_specs=[pl.BlockSpec((m, D_MODEL), lambda j: (0, 0)),
                  pl.BlockSpec((D_MODEL, tn), lambda j: (0, j)),
                  pl.BlockSpec((1, tn), lambda j: (0, j))],
        out_specs=pl.BlockSpec((m, tn), lambda j: (0, j)),
        out_shape=jax.ShapeDtypeStruct((m, n), F32),
        compiler_params=pltpu.CompilerParams(dimension_semantics=("parallel",),
                                             vmem_limit_bytes=VMEM_LIMIT),
        name="mod",
    )(c_all, w_mod, b_mod.reshape(1, n))


def _chunk_masks():
    t = np.arange(CHUNK)[:, None]
    s = (np.arange(GROUP) % CHUNK)[None, :]
    rows = []
    for d in (0, 1):
        before = (s < t) if d == 0 else (s > t)
        rows.append(before)
        rows.append(before | (s == t))
        for b in _LEVELS:
            same = (t // (2 * b)) == (s // (2 * b))
            if d == 0:
                rows.append(same & ((t // b) % 2 == 1) & ((s // b) % 2 == 0))
            else:
                rows.append(same & ((t // b) % 2 == 0) & ((s // b) % 2 == 1))
    rows.append(s == t)
    return np.stack(rows).astype(np.float32)


def _group_blockdiag_mask():
    i = np.arange(GROUP)
    return ((i[:, None] // HEAD) == (i[None, :] // HEAD)).astype(np.float32)


def _feat_kernel(has_pos, nt, *refs):
    if has_pos:
        x_ref, xp_ref, xn_ref, pos_ref, pp_ref, pn_ref = refs[:6]
        refs = refs[6:]
    else:
        x_ref, xp_ref, xn_ref = refs[:3]
        pos_ref = pp_ref = pn_ref = None
        refs = refs[3:]
    (mod_ref, gpre_ref, win_ref, wup_ref, w0_ref, aup_ref, a0_ref, gup_ref,
     kk_ref, ka_ref, rk_ref, seg_ref,
     convw_ref, convb_ref, wa_ref, ba_ref, wx_ref, bx_ref, lam_ref,
     masks_ref, bdm_ref,
     p_o, q_o, m_o, n_o, gt_o, ac_o, bs_o, bl_o, g_o, bonus_o, gate_o,
     lw_s, lc_s, la_s, lb_s, lo_s) = refs

    i = pl.program_id(0)
    tm = TOKEN_TILE
    mod = mod_ref[0]
    shift1, scale1 = mod[0:1], mod[1:2]

    def normmod(xv):
        ms = jnp.mean(xv * xv, axis=-1, keepdims=True)
        hv = (xv * lax.rsqrt(ms + EPS)) * gpre_ref[...]
        return hv * (1.0 + scale1) + shift1

    x = x_ref[0]
    halo = jnp.concatenate([xp_ref[0], xn_ref[0]], axis=0)
    if has_pos:
        x = x + pos_ref[...]
        halo = halo + jnp.concatenate([pp_ref[...], pn_ref[...]], axis=0)
    z = _dot(normmod(x), win_ref[...])
    zh = _dot(normmod(halo), win_ref[:, _O_XB:_O_XB + D_B])

    r = z[:, _O_R:_O_R + D_A]
    k = z[:, _O_K:_O_K + D_A]
    v = z[:, _O_V:_O_V + D_A]
    xw = z[:, _O_XW:_O_XW + 2 * R_W]
    xa = z[:, _O_XA:_O_XA + 2 * R_A]
    xg = z[:, _O_XG:_O_XG + R_G]

    g_o[0] = _dot(jax.nn.sigmoid(xg), gup_ref[...])
    gate_o[0] = jax.nn.gelu(z[:, _O_GB:_O_GB + D_B], approximate=True)
    wl = w0_ref[...] + _dot(jnp.tanh(xw), wup_ref[...])
    lw2 = -jax.nn.sigmoid(wl) * float(np.exp(-0.5))
    a2 = jax.nn.sigmoid(a0_ref[...] + _dot(xa, aup_ref[...]))

    kks = k * kk_ref[...]
    ss = _dot(kks * kks, seg_ref[...])
    kk = kks * lax.rsqrt(jnp.maximum(ss, 1e-24))
    ka = ka_ref[...]
    kd2 = [k * (1.0 + (a2[:, d * D_A:(d + 1) * D_A] - 1.0) * ka) for d in (0, 1)]
    bonus_o[0] = _dot(r * (kd2[0] + kd2[1]) * rk_ref[...], seg_ref[...]) * v

    m_prev = jnp.where(i > 0, 1.0, 0.0)
    m_next = jnp.where(i < nt - 1, 1.0, 0.0)
    ext = jnp.concatenate([zh[:HALO] * m_prev, z[:, _O_XB:_O_XB + D_B], zh[HALO:] * m_next], axis=0)
    n_ext = tm + 2 * HALO
    xc = convb_ref[...]
    for j in range(CONV_W):
        sh = (2 - j) % n_ext
        tap = ext if sh == 0 else pltpu.roll(ext, sh, 0)
        xc = xc + tap[HALO:HALO + tm] * convw_ref[j:j + 1, :]
    for d in (0, 1):
        rg = jnp.concatenate([_dot(xc[:, g * GROUP:(g + 1) * GROUP], wa_ref[d, g]) for g in range(N_GROUP)], 1)
        ig = jnp.concatenate([_dot(xc[:, g * GROUP:(g + 1) * GROUP], wx_ref[d, g]) for g in range(N_GROUP)], 1)
        rg = jax.nn.sigmoid(rg + ba_ref[d:d + 1, :])
        ig = jax.nn.sigmoid(ig + bx_ref[d:d + 1, :])
        log_a = -LRU_C * rg * jax.nn.softplus(-lam_ref[d:d + 1, :])
        a_lru = jnp.exp(log_a)
        b_lru = jnp.sqrt(-jnp.tanh(log_a) * (a_lru * a_lru + 1.0)) * (ig * xc)
        for q in range(N_SLAB):
            la_s[d, q] = a_lru[:, q * LANE:(q + 1) * LANE]
            lb_s[d, q] = b_lru[:, q * LANE:(q + 1) * LANE]
    for c in range(CPT):
        base = c * CHUNK
        rs = slice(base, base + CHUNK)
        for q in range(N_SLAB):
            ls = slice(q * LANE, (q + 1) * LANE)
            bsum = None
            for d in (0, 1):
                acum, bcum = _scan_affine(_strided_rows(la_s, (d, q), base), _strided_rows(lb_s, (d, q), base),
                                          d == 1)
                ac_o[d, 0, rs, ls] = _natural_rows(lo_s, (d, q), base, acum)
                bsum = bcum if bsum is None else [x + y for x, y in zip(bsum, bcum)]
                bl_o[d, 0, c, :, ls] = bcum[0][0:1, :] if d == 1 else bcum[SUB - 1][SUB - 1:SUB, :]
            bs_o[0, rs, ls] = _natural_rows(lo_s, (2, q), base, bsum)

    bdm = bdm_ref[...]
    bd = lambda xv: _blockdiag(xv, bdm)
    bdot = lambda lhs, rhs: jnp.dot(lhs.astype(BF16), bd(rhs), preferred_element_type=F32)

    a_t, r_t, b_t, k_t, g_tot = [], [], [], [], []
    for d in (0, 1):
        lw = lw2[:, d * D_A:(d + 1) * D_A]
        for q in range(N_SLAB):
            lw_s[d, q] = lw[:, q * LANE:(q + 1) * LANE]
        g_tot.append([])
        lc_rows = []
        for c in range(CPT):
            base = c * CHUNK
            blocks, totals = [], []
            for q in range(N_SLAB):
                pieces = _scan_sum(_strided_rows(lw_s, (d, q), base), d == 1)
                totals.append(pieces[0][0:1, :] if d == 1 else pieces[SUB - 1][SUB - 1:SUB, :])
                blocks.append(_natural_rows(lc_s, (d, q), base, pieces))
            lc_rows.append(jnp.concatenate(blocks, axis=1))
            g_tot[d].append(jnp.exp(jnp.concatenate(totals, axis=1)))
            gt_o[d, 0, c] = g_tot[d][c]
        lc = jnp.concatenate(lc_rows, axis=0)
        e_neg = jnp.exp(-lc)
        a_t.append(-kk * jnp.exp(lc - lw))
        r_t.append(r * jnp.exp(lc))
        b_t.append(kk * a2[:, d * D_A:(d + 1) * D_A] * e_neg)
        k_t.append(kd2[d] * e_neg)

    combos = [(d, c, g) for d in (0, 1) for c in range(CPT) for g in range(N_GROUP)]

    def cut(arr, c, g):
        return arr[c * CHUNK:(c + 1) * CHUNK, g * GROUP:(g + 1) * GROUP]

    sc = {}
    for key in combos:
        d, c, g = key
        ar = jnp.concatenate([cut(a_t[d], c, g), cut(r_t[d], c, g)], axis=0)
        rhs = jnp.concatenate([bd(cut(b_t[d], c, g)), bd(cut(k_t[d], c, g))], axis=0)
        sc[key] = _dot_nt(ar, rhs)
    n_ab, n_ak, n_rb, n_rk, tinv = {}, {}, {}, {}, {}
    for key in combos:
        m0 = key[0] * _M_PER_DIR
        n_ab[key] = sc[key][:CHUNK, :GROUP] * masks_ref[m0 + _M_STRICT]
        n_ak[key] = sc[key][:CHUNK, GROUP:] * masks_ref[m0 + _M_STRICT]
        n_rb[key] = sc[key][CHUNK:, :GROUP] * masks_ref[m0 + _M_INCL]
        n_rk[key] = sc[key][CHUNK:, GROUP:] * masks_ref[m0 + _M_INCL]
        tinv[key] = masks_ref[_M_EYE] + n_ab[key] * masks_ref[m0 + _M_LEV0]
    for li in range(1, len(_LEVELS)):
        pm = {}
        for key in combos:
            pm[key] = bdot(n_ab[key] * masks_ref[key[0] * _M_PER_DIR + _M_LEV0 + li], tinv[key])
        for key in combos:
            tinv[key] = tinv[key] + bdot(tinv[key], pm[key])

    kv = {key: bdot(jnp.concatenate([n_ak[key], n_rk[key]], axis=0), cut(v, key[1], key[2])) for key in combos}
    gm = {key: bdot(n_rb[key], tinv[key]) for key in combos}
    wp = {}
    for key in combos:
        d, c, g = key
        lhs = jnp.concatenate([tinv[key], gm[key]], axis=0)
        rhs = jnp.concatenate([bd(cut(a_t[d], c, g)), bd(kv[key][:CHUNK])], axis=1)
        wp[key] = jnp.dot(lhs.astype(BF16), rhs, preferred_element_type=F32)

    lane = lax.broadcasted_iota(jnp.int32, (CHUNK, GROUP), 1)

    def fold(full):
        out = full[(HEADS_PER_GROUP - 1) * HEAD:]
        for h in range(HEADS_PER_GROUP - 2, -1, -1):
            out = jnp.where(lane < (h + 1) * HEAD, full[h * HEAD:(h + 1) * HEAD], out)
        return out

    for key in combos:
        d, c, g = key
        rs = slice(c * CHUNK, (c + 1) * CHUNK)
        ls = slice(g * GROUP, (g + 1) * GROUP)
        w_, u0 = wp[key][:CHUNK, :GROUP], wp[key][:CHUNK, GROUP:]
        vg = cut(v, c, g)
        gt = g_tot[d][c][:, ls]
        bh, kh = cut(b_t[d], c, g) * gt, cut(k_t[d], c, g) * gt
        p_o[d, 0, rs, ls] = (cut(r_t[d], c, g) + wp[key][CHUNK:, :GROUP]).astype(BF16)
        m_o[d, 0, rs, ls] = fold(_dot_tn(w_, bh)).astype(BF16)
        n_o[d, 0, rs, ls] = fold(_dot_tn(jnp.concatenate([u0, vg], axis=0), jnp.concatenate([bh, kh], axis=0)))
        if d == 1:
            other = (0, c, g)
            q_o[0, rs, ls] = ((wp[other][CHUNK:, GROUP:] + kv[other][CHUNK:])
                              + (wp[key][CHUNK:, GROUP:] + kv[key][CHUNK:]))


def _feat_call(x, pos, mod, mod_row, p):
    bsz, t, _ = x.shape
    tm = TOKEN_TILE
    nt = t // tm
    nc = t // CHUNK
    hpt = tm // HALO
    has_pos = pos is not None

    def const(shape):
        return pl.BlockSpec(shape, lambda i, b: (0,) * len(shape))

    tok = lambda width: pl.BlockSpec((1, tm, width), lambda i, b: (b, i, 0))
    prev_i = lambda i: jnp.maximum(i * hpt - 1, 0)
    next_i = lambda i: jnp.minimum((i + 1) * hpt, t // HALO - 1)
    in_specs = [tok(D_MODEL),
                pl.BlockSpec((1, HALO, D_MODEL), lambda i, b: (b, prev_i(i), 0)),
                pl.BlockSpec((1, HALO, D_MODEL), lambda i, b: (b, next_i(i), 0))]
    args = [x, x, x]
    if has_pos:
        in_specs += [pl.BlockSpec((tm, D_MODEL), lambda i, b: (i, 0)),
                     pl.BlockSpec((HALO, D_MODEL), lambda i, b: (prev_i(i), 0)),
                     pl.BlockSpec((HALO, D_MODEL), lambda i, b: (next_i(i), 0))]
        args += [pos, pos, pos]
    n_masks = 2 * _M_PER_DIR + 1
    in_specs += [
        pl.BlockSpec((1, 6, D_MODEL), lambda i, b: (mod_row(b), 0, 0)),
        const((1, D_MODEL)), const((D_MODEL, D_IN)),
        const((2 * R_W, 2 * D_A)), const((1, 2 * D_A)),
        const((2 * R_A, 2 * D_A)), const((1, 2 * D_A)),
        const((R_G, D_A)),
        const((1, D_A)), const((1, D_A)), const((1, D_A)), const((D_A, D_A)),
        const((CONV_W, D_B)), const((1, D_B)),
        const((2, N_GROUP, GROUP, GROUP)), const((2, D_B)),
        const((2, N_GROUP, GROUP, GROUP)), const((2, D_B)), const((2, D_B)),
        const((n_masks, CHUNK, GROUP)), const((GROUP, GROUP)),
    ]
    args += [mod, p["g_pre_mix"], p["w_in"], p["wup_bd"], p["w0"], p["aup_bd"], p["a0"], p["g_up"],
             p["k_k"], p["k_a"], p["r_k"], p["seg_ones"],
             p["conv_w"], p["conv_b"], p["wa_bd"], p["ba"], p["wx_bd"], p["bx"], p["lam"],
             p["chunk_masks"], p["bdm_bf16"]]
    tok_shape = jax.ShapeDtypeStruct((bsz, t, D_A), F32)
    dir_shape = jax.ShapeDtypeStruct((2, bsz, t, D_A), F32)
    row_shape = jax.ShapeDtypeStruct((2, bsz, nc, 1, D_A), F32)
    mxu_shape = jax.ShapeDtypeStruct((2, bsz, t, D_A), BF16)
    dir_spec = pl.BlockSpec((2, 1, tm, D_A), lambda i, b: (0, b, i, 0))
    row_spec = pl.BlockSpec((2, 1, CPT, 1, D_A), lambda i, b: (0, b, i, 0, 0))
    out_shape = [mxu_shape, tok_shape, mxu_shape, dir_shape, row_shape, dir_shape, tok_shape, row_shape,
                 tok_shape, tok_shape, tok_shape]
    out_specs = [dir_spec, tok(D_A), dir_spec, dir_spec, row_spec, dir_spec, tok(D_B), row_spec,
                 tok(D_A), tok(D_A), tok(D_B)]
    return pl.pallas_call(
        functools.partial(_feat_kernel, has_pos, nt),
        grid=(nt, bsz),
        in_specs=in_specs,
        out_specs=out_specs,
        out_shape=out_shape,
        scratch_shapes=[pltpu.VMEM((2, N_SLAB, tm, LANE), F32)] * 4 + [pltpu.VMEM((3, N_SLAB, tm, LANE), F32)],
        compiler_params=pltpu.CompilerParams(dimension_semantics=("parallel", "parallel"),
                                             vmem_limit_bytes=VMEM_LIMIT),
        name="feat",
    )(*args)


def _scan_kernel(nt, *refs):
    (pf_ref, mf_ref, nf_ref, gtf_ref, acf_ref, blf_ref, q_ref, bs_ref,
     pb_ref, mb_ref, nb_ref, gtb_ref, acb_ref, blb_ref,
     s0_ref, l0_ref, bdm_ref,
     y_ref, hs_ref, s_ref, hl_ref) = refs

    i = pl.program_id(1)

    @pl.when(i == 0)
    def _init():
        y_ref[...] = jnp.zeros_like(y_ref)
        hs_ref[...] = jnp.zeros_like(hs_ref)
        s_ref[...] = s0_ref[...]
        hl_ref[...] = l0_ref[...]

    bdm = bdm_ref[...]
    per_dir = ((0, pf_ref, mf_ref, nf_ref, gtf_ref, acf_ref, blf_ref),
               (1, pb_ref, mb_ref, nb_ref, gtb_ref, acb_ref, blb_ref))
    tiles = (i, nt - 1 - i)
    state = [s_ref[0, d] for d in (0, 1)]
    h0 = [hl_ref[0, d:d + 1, :] for d in (0, 1)]
    for cc in range(CPT):
        for d, p_ref, m_ref, n_ref, gt_ref, ac_ref, bl_ref in per_dir:
            c = cc if d == 0 else CPT - 1 - cc
            rs = slice(c * CHUNK, (c + 1) * CHUNK)
            rows = pl.ds(pl.multiple_of(tiles[d] * TOKEN_TILE + c * CHUNK, CHUNK), CHUNK)
            gt = gt_ref[0, 0, c]
            y_parts, s_parts = [], []
            for g in range(N_GROUP):
                ls = slice(g * GROUP, (g + 1) * GROUP)
                sg = state[d][:, ls]
                y_parts.append(_dot_nt(p_ref[0, 0, rs, ls], _blockdiag(sg, bdm)))
                s_parts.append(sg * gt[:, ls]
                               + jnp.dot(sg.astype(BF16), _blockdiag(m_ref[0, 0, rs, ls], bdm),
                                         preferred_element_type=F32)
                               + n_ref[0, 0, rs, ls])
            state[d] = jnp.concatenate(s_parts, axis=1)
            y_c = jnp.concatenate(y_parts, axis=1)
            acum = ac_ref[0, 0, rs, :]
            h_c = acum * h0[d]
            if d == 0:
                y_c = y_c + q_ref[0, rs, :]
                h_c = h_c + bs_ref[0, rs, :]
            y_ref[0, rows, :] += y_c
            hs_ref[0, rows, :] += h_c
            last = CHUNK - 1 if d == 0 else 0
            h0[d] = bl_ref[0, 0, c] + acum[last:last + 1, :] * h0[d]
    for d in (0, 1):
        s_ref[0, d] = state[d]
        hl_ref[0, d:d + 1, :] = h0[d]


def _scan_call(feats, s0, l0, p):
    pm, q, mm, nm, gt, ac, bs, bl = feats
    _, bsz, t, _ = pm.shape
    nt = t // TOKEN_TILE

    def dir_specs(d, tile_of):
        big = pl.BlockSpec((1, 1, TOKEN_TILE, D_A), lambda b, i: (d, b, tile_of(i), 0))
        small = pl.BlockSpec((1, 1, CPT, 1, D_A), lambda b, i: (d, b, tile_of(i), 0, 0))
        return big, small

    big_f, small_f = dir_specs(0, lambda i: i)
    big_b, small_b = dir_specs(1, lambda i: nt - 1 - i)
    tok_f = pl.BlockSpec((1, TOKEN_TILE, D_A), lambda b, i: (b, i, 0))
    in_specs = [big_f, big_f, big_f, small_f, big_f, small_f, tok_f, tok_f,
                big_b, big_b, big_b, small_b, big_b, small_b,
                pl.BlockSpec((1, 2, HEAD, D_A), lambda b, i: (b, 0, 0, 0)),
                pl.BlockSpec((1, 2, D_B), lambda b, i: (b, 0, 0)),
                pl.BlockSpec((GROUP, GROUP), lambda b, i: (0, 0))]
    args = [pm, mm, nm, gt, ac, bl, q, bs, pm, mm, nm, gt, ac, bl, s0, l0, p["bdm_bf16"]]
    seq_spec = pl.BlockSpec((1, t, D_A), lambda b, i: (b, 0, 0))
    out_specs = [seq_spec, seq_spec,
                 pl.BlockSpec((1, 2, HEAD, D_A), lambda b, i: (b, 0, 0, 0)),
                 pl.BlockSpec((1, 2, D_B), lambda b, i: (b, 0, 0))]
    out_shape = [jax.ShapeDtypeStruct((bsz, t, D_A), F32), jax.ShapeDtypeStruct((bsz, t, D_B), F32),
                 jax.ShapeDtypeStruct((bsz, 2, HEAD, D_A), F32),
                 jax.ShapeDtypeStruct((bsz, 2, D_B), F32)]
    return pl.pallas_call(
        functools.partial(_scan_kernel, nt),
        grid=(bsz, nt),
        in_specs=in_specs,
        out_specs=out_specs,
        out_shape=out_shape,
        compiler_params=pltpu.CompilerParams(dimension_semantics=("parallel", "arbitrary"),
                                             vmem_limit_bytes=VMEM_LIMIT),
        name="scan",
    )(*args)


def _out_kernel(has_pos, *refs):
    if has_pos:
        x_ref, pos_ref = refs[0], refs[1]
        refs = refs[2:]
    else:
        x_ref, pos_ref = refs[0], None
        refs = refs[1:]
    (y_ref, hs_ref, g_ref, bonus_ref, gate_ref, mod_ref,
     avg_ref, lnxg_ref, lnxb_ref, wout_ref, gpost_ref, gpre2_ref, w1_ref, w2_ref, gpost2_ref,
     o_ref) = refs

    x = x_ref[0]
    if has_pos:
        x = x + pos_ref[...]
    mod = mod_ref[0]
    gate1, shift2, scale2, gate2 = mod[2:3], mod[3:4], mod[4:5], mod[5:6]

    y = y_ref[0]
    avg = avg_ref[...]
    y_hi = y.astype(BF16)
    y_lo = (y - y_hi.astype(F32)).astype(BF16)
    mu = jnp.dot(y_hi, avg, preferred_element_type=F32) + jnp.dot(y_lo, avg, preferred_element_type=F32)
    yc = y - mu
    var = _dot(yc * yc, avg)
    yn = yc * lax.rsqrt(var + LNX_EPS) * lnxg_ref[...] + lnxb_ref[...]
    out_a = (yn + bonus_ref[0]) * g_ref[0]
    out_b = hs_ref[0] * gate_ref[0]
    mix = _dot(jnp.concatenate([out_a, out_b], axis=1), wout_ref[...])
    ms = jnp.mean(mix * mix, axis=-1, keepdims=True)
    x = x + gate1 * ((mix * lax.rsqrt(ms + EPS)) * gpost_ref[...])

    ms = jnp.mean(x * x, axis=-1, keepdims=True)
    h = (x * lax.rsqrt(ms + EPS)) * gpre2_ref[...]
    h = h * (1.0 + scale2) + shift2
    f = _dot(h, w1_ref[...])
    f = jnp.square(jnp.maximum(f, 0.0))
    f = _dot(f, w2_ref[...])
    ms = jnp.mean(f * f, axis=-1, keepdims=True)
    o_ref[0] = x + gate2 * ((f * lax.rsqrt(ms + EPS)) * gpost2_ref[...])


def _out_call(x, pos, mod, mod_row, y, hs, g, bonus, gate, p):
    bsz, t, _ = x.shape
    tm = TOKEN_TILE
    nt = t // tm
    has_pos = pos is not None

    def const(shape):
        return pl.BlockSpec(shape, lambda i, b: (0,) * len(shape))

    tok = lambda width: pl.BlockSpec((1, tm, width), lambda i, b: (b, i, 0))
    in_specs = [tok(D_MODEL)]
    args = [x]
    if has_pos:
        in_specs.append(pl.BlockSpec((tm, D_MODEL), lambda i, b: (i, 0)))
        args.append(pos)
    in_specs += [tok(D_A)] * 5
    in_specs += [
        pl.BlockSpec((1, 6, D_MODEL), lambda i, b: (mod_row(b), 0, 0)),
        const((D_A, D_A)), const((1, D_A)), const((1, D_A)),
        const((D_MODEL, D_MODEL)), const((1, D_MODEL)), const((1, D_MODEL)),
        const((D_MODEL, D_FF)), const((D_FF, D_MODEL)), const((1, D_MODEL)),
    ]
    args += [y, hs, g, bonus, gate, mod,
             p["seg_avg"], p["lnx_g"], p["lnx_b"], p["w_out"], p["g_post_mix"], p["g_pre_mlp"],
             p["w_mlp1"], p["w_mlp2"], p["g_post_mlp"]]
    return pl.pallas_call(
        functools.partial(_out_kernel, has_pos),
        grid=(nt, bsz),
        in_specs=in_specs,
        out_specs=tok(D_MODEL),
        out_shape=jax.ShapeDtypeStruct((bsz, t, D_MODEL), F32),
        compiler_params=pltpu.CompilerParams(dimension_semantics=("parallel", "parallel"),
                                             vmem_limit_bytes=VMEM_LIMIT),
        name="out",
    )(*args)


def _sincos_1d(pos, dim):
    omega = 1.0 / (10000.0 ** (jnp.arange(dim // 2, dtype=F32) / (dim // 2)))
    ang = pos.astype(F32)[:, None] * omega[None, :]
    return jnp.concatenate([jnp.sin(ang), jnp.cos(ang)], axis=-1)


def _grid_pos_embed(n_tokens):
    rows = n_tokens // GRID_W
    half = D_MODEL // 2
    e_row = _sincos_1d(jnp.arange(rows), half)
    e_col = _sincos_1d(jnp.arange(GRID_W), half)
    emb = jnp.concatenate([jnp.broadcast_to(e_row[:, None, :], (rows, GRID_W, half)),
                           jnp.broadcast_to(e_col[None, :, :], (rows, GRID_W, half))], axis=-1)
    return emb.reshape(rows * GRID_W, D_MODEL)


def _blockdiag_pairs(w):
    z = jnp.zeros_like(w[0])
    return jnp.concatenate([jnp.concatenate([w[0], z], axis=1),
                            jnp.concatenate([z, w[1]], axis=1)], axis=0)


def _heads_to_blockdiag(w):
    lead = w.shape[:-3]
    w = w.reshape(lead + (N_GROUP, HEADS_PER_GROUP, HEAD, HEAD))
    eye = jnp.eye(HEADS_PER_GROUP, dtype=w.dtype)
    bd = jnp.einsum('...ghab,hj->...ghajb', w, eye)
    return bd.reshape(lead + (N_GROUP, GROUP, GROUP))


def _state_to_lanes(s):
    b = s.shape[0]
    return jnp.transpose(s, (0, 1, 3, 2, 4)).reshape(b, 2, HEAD, D_A)


def _state_from_lanes(s):
    b = s.shape[0]
    return jnp.transpose(s.reshape(b, 2, HEAD, N_HEAD, HEAD), (0, 1, 3, 2, 4))


def kernel(x_prompt, x_sample, c, state_rwkv, state_lru, c_ctx, w_mod, b_mod, g_pre_mix, g_post_mix,
           g_pre_mlp, g_post_mlp, w_in, rwkv_w0, rwkv_w_up, rwkv_a0, rwkv_a_up, rwkv_g_up, rwkv_k_k,
           rwkv_k_a, rwkv_r_k, rwkv_lnx_g, rwkv_lnx_b, lru_conv_w, lru_conv_b, lru_wa, lru_ba, lru_wx,
           lru_bx, lru_lambda, w_out, w_mlp1, w_mlp2):
    n_ctx = x_prompt.shape[0]
    n_lat = x_sample.shape[0]
    l = 0
    seg = _group_blockdiag_mask()
    seg512 = np.kron(np.eye(N_GROUP, dtype=np.float32), seg)
    p = {
        "g_pre_mix": g_pre_mix[l][None], "g_post_mix": g_post_mix[l][None],
        "g_pre_mlp": g_pre_mlp[l][None], "g_post_mlp": g_post_mlp[l][None],
        "w_in": w_in[l].astype(BF16), "w_out": w_out[l].astype(BF16),
        "w_mlp1": w_mlp1[l].astype(BF16), "w_mlp2": w_mlp2[l].astype(BF16),
        "w0": rwkv_w0[l].reshape(1, 2 * D_A), "a0": rwkv_a0[l].reshape(1, 2 * D_A),
        "wup_bd": _blockdiag_pairs(rwkv_w_up[l]).astype(BF16),
        "aup_bd": _blockdiag_pairs(rwkv_a_up[l]).astype(BF16),
        "g_up": rwkv_g_up[l].astype(BF16),
        "k_k": rwkv_k_k[l][None], "k_a": rwkv_k_a[l][None], "r_k": rwkv_r_k[l].reshape(1, D_A),
        "lnx_g": rwkv_lnx_g[l][None], "lnx_b": rwkv_lnx_b[l][None],
        "conv_w": lru_conv_w[l], "conv_b": lru_conv_b[l][None],
        "wa_bd": _heads_to_blockdiag(lru_wa[l]).astype(BF16), "ba": lru_ba[l],
        "wx_bd": _heads_to_blockdiag(lru_wx[l]).astype(BF16), "bx": lru_bx[l],
        "lam": lru_lambda[l],
        "seg_ones": jnp.asarray(seg512, BF16), "seg_avg": jnp.asarray(seg512 / HEAD, BF16),
        "chunk_masks": jnp.asarray(_chunk_masks()),
        "bdm_bf16": jnp.asarray(seg, BF16),
    }

    m_rows = 16
    c_all = jnp.concatenate([c_ctx[None], c, jnp.zeros((m_rows - 1 - n_lat, D_MODEL), F32)], axis=0)
    mod = _mod_call(c_all, w_mod[l], b_mod[l]).reshape(m_rows, 6, D_MODEL)

    pos = _grid_pos_embed(x_sample.shape[1]).astype(x_sample.dtype)
    ctx_row = lambda b: 0
    lat_row = lambda b: b + 1

    feats = _feat_call(x_prompt, None, mod, ctx_row, p)
    y, hs, s_ctx, l_ctx = _scan_call(
        feats[:8], jnp.zeros((n_ctx, 2, HEAD, D_A), F32), jnp.zeros((n_ctx, 2, D_B), F32), p)
    y_prompt = _out_call(x_prompt, None, mod, ctx_row, y, hs, feats[8], feats[9], feats[10], p)

    feats = _feat_call(x_sample, pos, mod, lat_row, p)
    y, hs, _, _ = _scan_call(feats[:8], _state_to_lanes(state_rwkv[:, l]), state_lru[:, l], p)
    y_sample = _out_call(x_sample, pos, mod, lat_row, y, hs, feats[8], feats[9], feats[10], p)

    new_state_rwkv = _state_from_lanes(s_ctx)[:, None].astype(x_prompt.dtype)
    new_state_lru = l_ctx[:, None].astype(x_prompt.dtype)
    return (y_prompt, y_sample, new_state_rwkv, new_state_lru)
```

```python
import functools

import numpy as np
import jax
import jax.numpy as jnp
from jax import lax
from jax.experimental import pallas as pl
from jax.experimental.pallas import tpu as pltpu

F32 = jnp.float32
BF16 = jnp.bfloat16

D_MODEL = 1024
D_A = 512
D_B = 512
HEAD = 64
N_HEAD = 8
R_W = 64
R_A = 64
R_G = 128
D_FF = 4096
D_IN = 2944
GRID_W = 64
CONV_W = 4
LRU_C = 8.0
EPS = 1e-6
LNX_EPS = 64e-5

CHUNK = 64
GROUP = 128
HEADS_PER_GROUP = GROUP // HEAD
N_GROUP = D_A // GROUP
SUB = 8
LANE = 128
N_SLAB = D_A // LANE
HALO = 8
TOKEN_TILE = 256
CPT = TOKEN_TILE // CHUNK
OUT_TILE = 512
VMEM_LIMIT = 60 * 1024 * 1024

_O_R, _O_K, _O_V, _O_XW, _O_XA, _O_XG, _O_XB, _O_GB = 0, 512, 1024, 1536, 1664, 1792, 1920, 2432

_M_STRICT, _M_INCL, _M_LEV0 = 0, 1, 2
_LEVELS = (1, 2, 4, 8, 16, 32)
_M_PER_DIR = 2 + len(_LEVELS)
_M_EYE = 2 * _M_PER_DIR


def _dot(a, b):
    return jnp.dot(a.astype(BF16), b.astype(BF16), preferred_element_type=F32)


def _dot_nt(a, b):
    return lax.dot_general(a.astype(BF16), b.astype(BF16), (((1,), (1,)), ((), ())),
                           preferred_element_type=F32)


def _dot_tn(a, b):
    return lax.dot_general(a.astype(BF16), b.astype(BF16), (((0,), (0,)), ((), ())),
                           preferred_element_type=F32)


def _strided_rows(ref, lead, base):
    return [ref[lead + (pl.ds(base + j, SUB, stride=SUB), slice(None))] for j in range(SUB)]


def _natural_rows(ref, lead, base, pieces):
    for j, piece in enumerate(pieces):
        ref[lead + (pl.ds(base + SUB * j, SUB), slice(None))] = piece
    return jnp.concatenate(_strided_rows(ref, lead, base), axis=0)


def _sublane_shift(x, steps, reverse, fill):
    sub = lax.broadcasted_iota(jnp.int32, x.shape, 0)
    if reverse:
        return jnp.where(sub < SUB - steps, pltpu.roll(x, SUB - steps, 0), fill)
    return jnp.where(sub >= steps, pltpu.roll(x, steps, 0), fill)


def _scan_affine(a, b, reverse):
    a, b = list(a), list(b)
    order = range(SUB - 2, -1, -1) if reverse else range(1, SUB)
    for j in order:
        p = j + 1 if reverse else j - 1
        b[j] = a[j] * b[p] + b[j]
        a[j] = a[j] * a[p]
    ta, tb = (a[0], b[0]) if reverse else (a[SUB - 1], b[SUB - 1])
    s = 1
    while s < SUB:
        tb = ta * _sublane_shift(tb, s, reverse, 0.0) + tb
        ta = ta * _sublane_shift(ta, s, reverse, 1.0)
        s *= 2
    ea = _sublane_shift(ta, 1, reverse, 1.0)
    eb = _sublane_shift(tb, 1, reverse, 0.0)
    return [x * ea for x in a], [x * eb + y for x, y in zip(a, b)]


def _scan_sum(x, reverse):
    x = list(x)
    order = range(SUB - 2, -1, -1) if reverse else range(1, SUB)
    for j in order:
        x[j] = x[j] + x[j + 1 if reverse else j - 1]
    t = x[0] if reverse else x[SUB - 1]
    s = 1
    while s < SUB:
        t = t + _sublane_shift(t, s, reverse, 0.0)
        s *= 2
    e = _sublane_shift(t, 1, reverse, 0.0)
    return [y + e for y in x]


def _sigmoid(x):
    return 0.5 * jnp.tanh(0.5 * x) + 0.5


def _blockdiag(x, bdm):
    xb = x.astype(BF16)
    return jnp.concatenate([xb] * HEADS_PER_GROUP, axis=0) * bdm


def _mod_kernel(c_ref, w_ref, b_ref, o_ref):
    c = c_ref[...]
    s = c * _sigmoid(c)
    o_ref[...] = _dot(s, w_ref[...]) + b_ref[...]


def _mod_call(c_all, w_mod, b_mod):
    m = c_all.shape[0]
    n = w_mod.shape[1]
    tn = 1536
    return pl.pallas_call(
        _mod_kernel,
        grid=(n // tn,),
        in_specs=[pl.BlockSpec((m, D_MODEL), lambda j: (0, 0)),
                  pl.BlockSpec((D_MODEL, tn), lambda j: (0, j)),
                  pl.BlockSpec((1, tn), lambda j: (0, j))],
        out_specs=pl.BlockSpec((m, tn), lambda j: (0, j)),
        out_shape=jax.ShapeDtypeStruct((m, n), F32),
        compiler_params=pltpu.CompilerParams(dimension_semantics=("parallel",),
                                             vmem_limit_bytes=VMEM_LIMIT),
        name="mod",
    )(c_all, w_mod, b_mod.reshape(1, n))


def _chunk_masks():
    t = np.arange(CHUNK)[:, None]
    s = (np.arange(GROUP) % CHUNK)[None, :]
    rows = []
    for d in (0, 1):
        before = (s < t) if d == 0 else (s > t)
        rows.append(before)
        rows.append(before | (s == t))
        for b in _LEVELS:
            same = (t // (2 * b)) == (s // (2 * b))
            if d == 0:
                rows.append(same & ((t // b) % 2 == 1) & ((s // b) % 2 == 0))
            else:
                rows.append(same & ((t // b) % 2 == 0) & ((s // b) % 2 == 1))
    rows.append(s == t)
    return np.stack(rows).astype(np.float32)


def _group_blockdiag_mask():
    i = np.arange(GROUP)
    return ((i[:, None] // HEAD) == (i[None, :] // HEAD)).astype(np.float32)


def _feat_kernel(has_pos, nt, *refs):
    if has_pos:
        x_ref, xp_ref, xn_ref, pos_ref, pp_ref, pn_ref = refs[:6]
        refs = refs[6:]
    else:
        x_ref, xp_ref, xn_ref = refs[:3]
        pos_ref = pp_ref = pn_ref = None
        refs = refs[3:]
    (mod_ref, gpre_ref, win_ref, wup_ref, w0_ref, aup_ref, a0_ref, gup_ref,
     kk_ref, ka_ref, rk_ref, seg_ref,
     convw_ref, convb_ref, wa_ref, ba_ref, wx_ref, bx_ref, lam_ref,
     masks_ref, bdm_ref,
     p_o, q_o, m_o, n_o, gt_o, ac_o, bs_o, bl_o, g_o, bonus_o, gate_o,
     lw_s, lc_s, la_s, lb_s, lo_s) = refs

    i = pl.program_id(0)
    tm = TOKEN_TILE
    mod = mod_ref[0]
    shift1, scale1 = mod[0:1], mod[1:2]

    def normmod(xv):
        ms = jnp.mean(xv * xv, axis=-1, keepdims=True)
        hv = (xv * lax.rsqrt(ms + EPS)) * gpre_ref[...]
        return hv * (1.0 + scale1) + shift1

    x = x_ref[0]
    halo = jnp.concatenate([xp_ref[0], xn_ref[0]], axis=0)
    if has_pos:
        x = x + pos_ref[...]
        halo = halo + jnp.concatenate([pp_ref[...], pn_ref[...]], axis=0)
    z = _dot(normmod(x), win_ref[...])
    zh = _dot(normmod(halo), win_ref[:, _O_XB:_O_XB + D_B])

    r = z[:, _O_R:_O_R + D_A]
    k = z[:, _O_K:_O_K + D_A]
    v = z[:, _O_V:_O_V + D_A]
    xw = z[:, _O_XW:_O_XW + 2 * R_W]
    xa = z[:, _O_XA:_O_XA + 2 * R_A]
    xg = z[:, _O_XG:_O_XG + R_G]

    g_o[0] = _dot(_sigmoid(xg), gup_ref[...])
    gate_o[0] = jax.nn.gelu(z[:, _O_GB:_O_GB + D_B], approximate=True)
    wl = w0_ref[...] + _dot(jnp.tanh(xw), wup_ref[...])
    lw2 = -_sigmoid(wl) * float(np.exp(-0.5))
    a2 = _sigmoid(a0_ref[...] + _dot(xa, aup_ref[...]))

    kks = k * kk_ref[...]
    ss = _dot(kks * kks, seg_ref[...])
    kk = kks * lax.rsqrt(jnp.maximum(ss, 1e-24))
    ka = ka_ref[...]
    kd2 = [k * (1.0 + (a2[:, d * D_A:(d + 1) * D_A] - 1.0) * ka) for d in (0, 1)]
    bonus_o[0] = _dot(r * (kd2[0] + kd2[1]) * rk_ref[...], seg_ref[...]) * v

    m_prev = jnp.where(i > 0, 1.0, 0.0)
    m_next = jnp.where(i < nt - 1, 1.0, 0.0)
    ext = jnp.concatenate([zh[:HALO] * m_prev, z[:, _O_XB:_O_XB + D_B], zh[HALO:] * m_next], axis=0)
    n_ext = tm + 2 * HALO
    xc = convb_ref[...]
    for j in range(CONV_W):
        sh = (2 - j) % n_ext
        tap = ext if sh == 0 else pltpu.roll(ext, sh, 0)
        xc = xc + tap[HALO:HALO + tm] * convw_ref[j:j + 1, :]
    for d in (0, 1):
        rg = jnp.concatenate([_dot(xc[:, g * GROUP:(g + 1) * GROUP], wa_ref[d, g]) for g in range(N_GROUP)], 1)
        ig = jnp.concatenate([_dot(xc[:, g * GROUP:(g + 1) * GROUP], wx_ref[d, g]) for g in range(N_GROUP)], 1)
        rg = _sigmoid(rg + ba_ref[d:d + 1, :])
        ig = _sigmoid(ig + bx_ref[d:d + 1, :])
        log_a = -LRU_C * rg * jax.nn.softplus(-lam_ref[d:d + 1, :])
        a_lru = jnp.exp(log_a)
        b_lru = jnp.sqrt(-jnp.tanh(log_a) * (a_lru * a_lru + 1.0)) * (ig * xc)
        for q in range(N_SLAB):
            la_s[d, q] = a_lru[:, q * LANE:(q + 1) * LANE]
            lb_s[d, q] = b_lru[:, q * LANE:(q + 1) * LANE]
    for c in range(CPT):
        base = c * CHUNK
        rs = slice(base, base + CHUNK)
        for q in range(N_SLAB):
            ls = slice(q * LANE, (q + 1) * LANE)
            bsum = None
            for d in (0, 1):
                acum, bcum = _scan_affine(_strided_rows(la_s, (d, q), base), _strided_rows(lb_s, (d, q), base),
                                          d == 1)
                ac_o[d, 0, rs, ls] = _natural_rows(lo_s, (d, q), base, acum)
                bsum = bcum if bsum is None else [x + y for x, y in zip(bsum, bcum)]
                bl_o[d, 0, c, :, ls] = bcum[0][0:1, :] if d == 1 else bcum[SUB - 1][SUB - 1:SUB, :]
            bs_o[0, rs, ls] = _natural_rows(lo_s, (2, q), base, bsum)

    bdm = bdm_ref[...]
    bd = lambda xv: _blockdiag(xv, bdm)
    bdot = lambda lhs, rhs: jnp.dot(lhs.astype(BF16), bd(rhs), preferred_element_type=F32)

    a_t, r_t, b_t, k_t, g_tot = [], [], [], [], []
    for d in (0, 1):
        lw = lw2[:, d * D_A:(d + 1) * D_A]
        for q in range(N_SLAB):
            lw_s[d, q] = lw[:, q * LANE:(q + 1) * LANE]
        g_tot.append([])
        lc_rows = []
        for c in range(CPT):
            base = c * CHUNK
            blocks, totals = [], []
            for q in range(N_SLAB):
                pieces = _scan_sum(_strided_rows(lw_s, (d, q), base), d == 1)
                totals.append(pieces[0][0:1, :] if d == 1 else pieces[SUB - 1][SUB - 1:SUB, :])
                blocks.append(_natural_rows(lc_s, (d, q), base, pieces))
            lc_rows.append(jnp.concatenate(blocks, axis=1))
            g_tot[d].append(jnp.exp(jnp.concatenate(totals, axis=1)))
            gt_o[d, 0, c] = g_tot[d][c]
        lc = jnp.concatenate(lc_rows, axis=0)
        e_neg = jnp.exp(-lc)
        a_t.append(-kk * jnp.exp(lc - lw))
        r_t.append(r * jnp.exp(lc))
        b_t.append(kk * a2[:, d * D_A:(d + 1) * D_A] * e_neg)
        k_t.append(kd2[d] * e_neg)

    combos = [(d, c, g) for d in (0, 1) for c in range(CPT) for g in range(N_GROUP)]

    def cut(arr, c, g):
        return arr[c * CHUNK:(c + 1) * CHUNK, g * GROUP:(g + 1) * GROUP]

    sc = {}
    for key in combos:
        d, c, g = key
        ar = jnp.concatenate([cut(a_t[d], c, g), cut(r_t[d], c, g)], axis=0)
        rhs = jnp.concatenate([bd(cut(b_t[d], c, g)), bd(cut(k_t[d], c, g))], axis=0)
        sc[key] = _dot_nt(ar, rhs)
    n_ab, n_ak, n_rb, n_rk, tinv = {}, {}, {}, {}, {}
    for key in combos:
        m0 = key[0] * _M_PER_DIR
        n_ab[key] = sc[key][:CHUNK, :GROUP] * masks_ref[m0 + _M_STRICT]
        n_ak[key] = sc[key][:CHUNK, GROUP:] * masks_ref[m0 + _M_STRICT]
        n_rb[key] = sc[key][CHUNK:, :GROUP] * masks_ref[m0 + _M_INCL]
        n_rk[key] = sc[key][CHUNK:, GROUP:] * masks_ref[m0 + _M_INCL]
        tinv[key] = masks_ref[_M_EYE] + n_ab[key] * masks_ref[m0 + _M_LEV0]
    for li in range(1, len(_LEVELS)):
        pm = {}
        for key in combos:
            pm[key] = bdot(n_ab[key] * masks_ref[key[0] * _M_PER_DIR + _M_LEV0 + li], tinv[key])
        for key in combos:
            tinv[key] = tinv[key] + bdot(tinv[key], pm[key])

    kv = {key: bdot(jnp.concatenate([n_ak[key], n_rk[key]], axis=0), cut(v, key[1], key[2])) for key in combos}
    gm = {key: bdot(n_rb[key], tinv[key]) for key in combos}
    wp = {}
    for key in combos:
        d, c, g = key
        lhs = jnp.concatenate([tinv[key], gm[key]], axis=0)
        rhs = jnp.concatenate([bd(cut(a_t[d], c, g)), bd(kv[key][:CHUNK])], axis=1)
        wp[key] = jnp.dot(lhs.astype(BF16), rhs, preferred_element_type=F32)

    lane = lax.broadcasted_iota(jnp.int32, (CHUNK, GROUP), 1)

    def fold(full):
        out = full[(HEADS_PER_GROUP - 1) * HEAD:]
        for h in range(HEADS_PER_GROUP - 2, -1, -1):
            out = jnp.where(lane < (h + 1) * HEAD, full[h * HEAD:(h + 1) * HEAD], out)
        return out

    for key in combos:
        d, c, g = key
        rs = slice(c * CHUNK, (c + 1) * CHUNK)
        ls = slice(g * GROUP, (g + 1) * GROUP)
        w_, u0 = wp[key][:CHUNK, :GROUP], wp[key][:CHUNK, GROUP:]
        vg = cut(v, c, g)
        gt = g_tot[d][c][:, ls]
        bh, kh = cut(b_t[d], c, g) * gt, cut(k_t[d], c, g) * gt
        p_o[d, 0, rs, ls] = (cut(r_t[d], c, g) + wp[key][CHUNK:, :GROUP]).astype(BF16)
        m_o[d, 0, rs, ls] = fold(_dot_tn(w_, bh)).astype(BF16)
        n_o[d, 0, rs, ls] = fold(_dot_tn(jnp.concatenate([u0, vg], axis=0), jnp.concatenate([bh, kh], axis=0)))
        if d == 1:
            other = (0, c, g)
            q_o[0, rs, ls] = ((wp[other][CHUNK:, GROUP:] + kv[other][CHUNK:])
                              + (wp[key][CHUNK:, GROUP:] + kv[key][CHUNK:]))


def _feat_call(x, pos, mod, mod_row, p):
    bsz, t, _ = x.shape
    tm = TOKEN_TILE
    nt = t // tm
    nc = t // CHUNK
    hpt = tm // HALO
    has_pos = pos is not None

    def const(shape):
        return pl.BlockSpec(shape, lambda i, b: (0,) * len(shape))

    tok = lambda width: pl.BlockSpec((1, tm, width), lambda i, b: (b, i, 0))
    prev_i = lambda i: jnp.maximum(i * hpt - 1, 0)
    next_i = lambda i: jnp.minimum((i + 1) * hpt, t // HALO - 1)
    in_specs = [tok(D_MODEL),
                pl.BlockSpec((1, HALO, D_MODEL), lambda i, b: (b, prev_i(i), 0)),
                pl.BlockSpec((1, HALO, D_MODEL), lambda i, b: (b, next_i(i), 0))]
    args = [x, x, x]
    if has_pos:
        in_specs += [pl.BlockSpec((tm, D_MODEL), lambda i, b: (i, 0)),
                     pl.BlockSpec((HALO, D_MODEL), lambda i, b: (prev_i(i), 0)),
                     pl.BlockSpec((HALO, D_MODEL), lambda i, b: (next_i(i), 0))]
        args += [pos, pos, pos]
    n_masks = 2 * _M_PER_DIR + 1
    in_specs += [
        pl.BlockSpec((1, 6, D_MODEL), lambda i, b: (mod_row(b), 0, 0)),
        const((1, D_MODEL)), const((D_MODEL, D_IN)),
        const((2 * R_W, 2 * D_A)), const((1, 2 * D_A)),
        const((2 * R_A, 2 * D_A)), const((1, 2 * D_A)),
        const((R_G, D_A)),
        const((1, D_A)), const((1, D_A)), const((1, D_A)), const((D_A, D_A)),
        const((CONV_W, D_B)), const((1, D_B)),
        const((2, N_GROUP, GROUP, GROUP)), const((2, D_B)),
        const((2, N_GROUP, GROUP, GROUP)), const((2, D_B)), const((2, D_B)),
        const((n_masks, CHUNK, GROUP)), const((GROUP, GROUP)),
    ]
    args += [mod, p["g_pre_mix"], p["w_in"], p["wup_bd"], p["w0"], p["aup_bd"], p["a0"], p["g_up"],
             p["k_k"], p["k_a"], p["r_k"], p["seg_ones"],
             p["conv_w"], p["conv_b"], p["wa_bd"], p["ba"], p["wx_bd"], p["bx"], p["lam"],
             p["chunk_masks"], p["bdm_bf16"]]
    tok_shape = jax.ShapeDtypeStruct((bsz, t, D_A), F32)
    dir_shape = jax.ShapeDtypeStruct((2, bsz, t, D_A), F32)
    row_shape = jax.ShapeDtypeStruct((2, bsz, nc, 1, D_A), F32)
    mxu_shape = jax.ShapeDtypeStruct((2, bsz, t, D_A), BF16)
    dir_spec = pl.BlockSpec((2, 1, tm, D_A), lambda i, b: (0, b, i, 0))
    row_spec = pl.BlockSpec((2, 1, CPT, 1, D_A), lambda i, b: (0, b, i, 0, 0))
    out_shape = [mxu_shape, tok_shape, mxu_shape, dir_shape, row_shape, dir_shape, tok_shape, row_shape,
                 tok_shape, tok_shape, tok_shape]
    out_specs = [dir_spec, tok(D_A), dir_spec, dir_spec, row_spec, dir_spec, tok(D_B), row_spec,
                 tok(D_A), tok(D_A), tok(D_B)]
    return pl.pallas_call(
        functools.partial(_feat_kernel, has_pos, nt),
        grid=(nt, bsz),
        in_specs=in_specs,
        out_specs=out_specs,
        out_shape=out_shape,
        scratch_shapes=[pltpu.VMEM((2, N_SLAB, tm, LANE), F32)] * 4 + [pltpu.VMEM((3, N_SLAB, tm, LANE), F32)],
        compiler_params=pltpu.CompilerParams(dimension_semantics=("parallel", "parallel"),
                                             vmem_limit_bytes=VMEM_LIMIT),
        name="feat",
    )(*args)


def _scan_kernel(nt, *refs):
    (pf_ref, mf_ref, nf_ref, gtf_ref, acf_ref, blf_ref, q_ref, bs_ref,
     pb_ref, mb_ref, nb_ref, gtb_ref, acb_ref, blb_ref,
     s0_ref, l0_ref, bdm_ref,
     y_ref, hs_ref, s_ref, hl_ref) = refs

    i = pl.program_id(1)

    @pl.when(i == 0)
    def _init():
        y_ref[...] = jnp.zeros_like(y_ref)
        hs_ref[...] = jnp.zeros_like(hs_ref)
        s_ref[...] = s0_ref[...]
        hl_ref[...] = l0_ref[...]

    bdm = bdm_ref[...]
    per_dir = ((0, pf_ref, mf_ref, nf_ref, gtf_ref, acf_ref, blf_ref),
               (1, pb_ref, mb_ref, nb_ref, gtb_ref, acb_ref, blb_ref))
    tiles = (i, nt - 1 - i)
    state = [s_ref[0, d] for d in (0, 1)]
    h0 = [hl_ref[0, d:d + 1, :] for d in (0, 1)]
    for cc in range(CPT):
        for d, p_ref, m_ref, n_ref, gt_ref, ac_ref, bl_ref in per_dir:
            c = cc if d == 0 else CPT - 1 - cc
            rs = slice(c * CHUNK, (c + 1) * CHUNK)
            rows = pl.ds(pl.multiple_of(tiles[d] * TOKEN_TILE + c * CHUNK, CHUNK), CHUNK)
            gt = gt_ref[0, 0, c]
            y_parts, s_parts = [], []
            for g in range(N_GROUP):
                ls = slice(g * GROUP, (g + 1) * GROUP)
                sg = state[d][:, ls]
                y_parts.append(_dot_nt(p_ref[0, 0, rs, ls], _blockdiag(sg, bdm)))
                s_parts.append(sg * gt[:, ls]
                               + jnp.dot(sg.astype(BF16), _blockdiag(m_ref[0, 0, rs, ls], bdm),
                                         preferred_element_type=F32)
                               + n_ref[0, 0, rs, ls])
            state[d] = jnp.concatenate(s_parts, axis=1)
            y_c = jnp.concatenate(y_parts, axis=1)
            acum = ac_ref[0, 0, rs, :]
            h_c = acum * h0[d]
            if d == 0:
                y_c = y_c + q_ref[0, rs, :]
                h_c = h_c + bs_ref[0, rs, :]
            y_ref[0, rows, :] += y_c
            hs_ref[0, rows, :] += h_c
            last = CHUNK - 1 if d == 0 else 0
            h0[d] = bl_ref[0, 0, c] + acum[last:last + 1, :] * h0[d]
    for d in (0, 1):
        s_ref[0, d] = state[d]
        hl_ref[0, d:d + 1, :] = h0[d]


def _scan_call(feats, s0, l0, p):
    pm, q, mm, nm, gt, ac, bs, bl = feats
    _, bsz, t, _ = pm.shape
    nt = t // TOKEN_TILE

    def dir_specs(d, tile_of):
        big = pl.BlockSpec((1, 1, TOKEN_TILE, D_A), lambda b, i: (d, b, tile_of(i), 0))
        small = pl.BlockSpec((1, 1, CPT, 1, D_A), lambda b, i: (d, b, tile_of(i), 0, 0))
        return big, small

    big_f, small_f = dir_specs(0, lambda i: i)
    big_b, small_b = dir_specs(1, lambda i: nt - 1 - i)
    tok_f = pl.BlockSpec((1, TOKEN_TILE, D_A), lambda b, i: (b, i, 0))
    in_specs = [big_f, big_f, big_f, small_f, big_f, small_f, tok_f, tok_f,
                big_b, big_b, big_b, small_b, big_b, small_b,
                pl.BlockSpec((1, 2, HEAD, D_A), lambda b, i: (b, 0, 0, 0)),
                pl.BlockSpec((1, 2, D_B), lambda b, i: (b, 0, 0)),
                pl.BlockSpec((GROUP, GROUP), lambda b, i: (0, 0))]
    args = [pm, mm, nm, gt, ac, bl, q, bs, pm, mm, nm, gt, ac, bl, s0, l0, p["bdm_bf16"]]
    seq_spec = pl.BlockSpec((1, t, D_A), lambda b, i: (b, 0, 0))
    out_specs = [seq_spec, seq_spec,
                 pl.BlockSpec((1, 2, HEAD, D_A), lambda b, i: (b, 0, 0, 0)),
                 pl.BlockSpec((1, 2, D_B), lambda b, i: (b, 0, 0))]
    out_shape = [jax.ShapeDtypeStruct((bsz, t, D_A), F32), jax.ShapeDtypeStruct((bsz, t, D_B), F32),
                 jax.ShapeDtypeStruct((bsz, 2, HEAD, D_A), F32),
                 jax.ShapeDtypeStruct((bsz, 2, D_B), F32)]
    return pl.pallas_call(
        functools.partial(_scan_kernel, nt),
        grid=(bsz, nt),
        in_specs=in_specs,
        out_specs=out_specs,
        out_shape=out_shape,
        compiler_params=pltpu.CompilerParams(dimension_semantics=("parallel", "arbitrary"),
                                             vmem_limit_bytes=VMEM_LIMIT),
        name="scan",
    )(*args)


def _out_kernel(has_pos, *refs):
    if has_pos:
        x_ref, pos_ref = refs[0], refs[1]
        refs = refs[2:]
    else:
        x_ref, pos_ref = refs[0], None
        refs = refs[1:]
    (y_ref, hs_ref, g_ref, bonus_ref, gate_ref, mod_ref,
     avg_ref, lnxg_ref, lnxb_ref, wout_ref, gpost_ref, gpre2_ref, w1_ref, w2_ref, gpost2_ref,
     o_ref) = refs

    x = x_ref[0]
    if has_pos:
        x = x + pos_ref[...]
    mod = mod_ref[0]
    gate1, shift2, scale2, gate2 = mod[2:3], mod[3:4], mod[4:5], mod[5:6]

    y = y_ref[0]
    avg = avg_ref[...]
    y_hi = y.astype(BF16)
    y_lo = (y - y_hi.astype(F32)).astype(BF16)
    mu = jnp.dot(y_hi, avg, preferred_element_type=F32) + jnp.dot(y_lo, avg, preferred_element_type=F32)
    yc = y - mu
    var = _dot(yc * yc, avg)
    yn = yc * lax.rsqrt(var + LNX_EPS) * lnxg_ref[...] + lnxb_ref[...]
    out_a = (yn + bonus_ref[0]) * g_ref[0]
    out_b = hs_ref[0] * gate_ref[0]
    mix = _dot(jnp.concatenate([out_a, out_b], axis=1), wout_ref[...])
    ms = jnp.mean(mix * mix, axis=-1, keepdims=True)
    x = x + gate1 * ((mix * lax.rsqrt(ms + EPS)) * gpost_ref[...])

    ms = jnp.mean(x * x, axis=-1, keepdims=True)
    h = (x * lax.rsqrt(ms + EPS)) * gpre2_ref[...]
    h = h * (1.0 + scale2) + shift2
    f = _dot(h, w1_ref[...])
    f = jnp.square(jnp.maximum(f, 0.0))
    f = _dot(f, w2_ref[...])
    ms = jnp.mean(f * f, axis=-1, keepdims=True)
    o_ref[0] = x + gate2 * ((f * lax.rsqrt(ms + EPS)) * gpost2_ref[...])


def _out_call(x, pos, mod, mod_row, y, hs, g, bonus, gate, p):
    bsz, t, _ = x.shape
    tm = min(OUT_TILE, t)
    nt = t // tm
    has_pos = pos is not None

    def const(shape):
        return pl.BlockSpec(shape, lambda i, b: (0,) * len(shape))

    tok = lambda width: pl.BlockSpec((1, tm, width), lambda i, b: (b, i, 0))
    in_specs = [tok(D_MODEL)]
    args = [x]
    if has_pos:
        in_specs.append(pl.BlockSpec((tm, D_MODEL), lambda i, b: (i, 0)))
        args.append(pos)
    in_specs += [tok(D_A)] * 5
    in_specs += [
        pl.BlockSpec((1, 6, D_MODEL), lambda i, b: (mod_row(b), 0, 0)),
        const((D_A, D_A)), const((1, D_A)), const((1, D_A)),
        const((D_MODEL, D_MODEL)), const((1, D_MODEL)), const((1, D_MODEL)),
        const((D_MODEL, D_FF)), const((D_FF, D_MODEL)), const((1, D_MODEL)),
    ]
    args += [y, hs, g, bonus, gate, mod,
             p["seg_avg"], p["lnx_g"], p["lnx_b"], p["w_out"], p["g_post_mix"], p["g_pre_mlp"],
             p["w_mlp1"], p["w_mlp2"], p["g_post_mlp"]]
    return pl.pallas_call(
        functools.partial(_out_kernel, has_pos),
        grid=(nt, bsz),
        in_specs=in_specs,
        out_specs=tok(D_MODEL),
        out_shape=jax.ShapeDtypeStruct((bsz, t, D_MODEL), F32),
        compiler_params=pltpu.CompilerParams(dimension_semantics=("parallel", "parallel"),
                                             vmem_limit_bytes=VMEM_LIMIT),
        name="out",
    )(*args)


def _sincos_1d(pos, dim):
    omega = 1.0 / (10000.0 ** (jnp.arange(dim // 2, dtype=F32) / (dim // 2)))
    ang = pos.astype(F32)[:, None] * omega[None, :]
    return jnp.concatenate([jnp.sin(ang), jnp.cos(ang)], axis=-1)


def _grid_pos_embed(n_tokens):
    rows = n_tokens // GRID_W
    half = D_MODEL // 2
    e_row = _sincos_1d(jnp.arange(rows), half)
    e_col = _sincos_1d(jnp.arange(GRID_W), half)
    emb = jnp.concatenate([jnp.broadcast_to(e_row[:, None, :], (rows, GRID_W, half)),
                           jnp.broadcast_to(e_col[None, :, :], (rows, GRID_W, half))], axis=-1)
    return emb.reshape(rows * GRID_W, D_MODEL)


def _blockdiag_pairs(w):
    z = jnp.zeros_like(w[0])
    return jnp.concatenate([jnp.concatenate([w[0], z], axis=1),
                            jnp.concatenate([z, w[1]], axis=1)], axis=0)


def _heads_to_blockdiag(w):
    lead = w.shape[:-3]
    w = w.reshape(lead + (N_GROUP, HEADS_PER_GROUP, HEAD, HEAD))
    eye = jnp.eye(HEADS_PER_GROUP, dtype=w.dtype)
    bd = jnp.einsum('...ghab,hj->...ghajb', w, eye)
    return bd.reshape(lead + (N_GROUP, GROUP, GROUP))


def _state_to_lanes(s):
    b = s.shape[0]
    return jnp.transpose(s, (0, 1, 3, 2, 4)).reshape(b, 2, HEAD, D_A)


def _state_from_lanes(s):
    b = s.shape[0]
    return jnp.transpose(s.reshape(b, 2, HEAD, N_HEAD, HEAD), (0, 1, 3, 2, 4))


def kernel(x_prompt, x_sample, c, state_rwkv, state_lru, c_ctx, w_mod, b_mod, g_pre_mix, g_post_mix,
           g_pre_mlp, g_post_mlp, w_in, rwkv_w0, rwkv_w_up, rwkv_a0, rwkv_a_up, rwkv_g_up, rwkv_k_k,
           rwkv_k_a, rwkv_r_k, rwkv_lnx_g, rwkv_lnx_b, lru_conv_w, lru_conv_b, lru_wa, lru_ba, lru_wx,
           lru_bx, lru_lambda, w_out, w_mlp1, w_mlp2):
    n_ctx = x_prompt.shape[0]
    n_lat = x_sample.shape[0]
    l = 0
    seg = _group_blockdiag_mask()
    seg512 = np.kron(np.eye(N_GROUP, dtype=np.float32), seg)
    p = {
        "g_pre_mix": g_pre_mix[l][None], "g_post_mix": g_post_mix[l][None],
        "g_pre_mlp": g_pre_mlp[l][None], "g_post_mlp": g_post_mlp[l][None],
        "w_in": w_in[l].astype(BF16), "w_out": w_out[l].astype(BF16),
        "w_mlp1": w_mlp1[l].astype(BF16), "w_mlp2": w_mlp2[l].astype(BF16),
        "w0": rwkv_w0[l].reshape(1, 2 * D_A), "a0": rwkv_a0[l].reshape(1, 2 * D_A),
        "wup_bd": _blockdiag_pairs(rwkv_w_up[l]).astype(BF16),
        "aup_bd": _blockdiag_pairs(rwkv_a_up[l]).astype(BF16),
        "g_up": rwkv_g_up[l].astype(BF16),
        "k_k": rwkv_k_k[l][None], "k_a": rwkv_k_a[l][None], "r_k": rwkv_r_k[l].reshape(1, D_A),
        "lnx_g": rwkv_lnx_g[l][None], "lnx_b": rwkv_lnx_b[l][None],
        "conv_w": lru_conv_w[l], "conv_b": lru_conv_b[l][None],
        "wa_bd": _heads_to_blockdiag(lru_wa[l]).astype(BF16), "ba": lru_ba[l],
        "wx_bd": _heads_to_blockdiag(lru_wx[l]).astype(BF16), "bx": lru_bx[l],
        "lam": lru_lambda[l],
        "seg_ones": jnp.asarray(seg512, BF16), "seg_avg": jnp.asarray(seg512 / HEAD, BF16),
        "chunk_masks": jnp.asarray(_chunk_masks()),
        "bdm_bf16": jnp.asarray(seg, BF16),
    }

    m_rows = 16
    c_all = jnp.concatenate([c_ctx[None], c, jnp.zeros((m_rows - 1 - n_lat, D_MODEL), F32)], axis=0)
    mod = _mod_call(c_all, w_mod[l], b_mod[l]).reshape(m_rows, 6, D_MODEL)

    pos = _grid_pos_embed(x_sample.shape[1]).astype(x_sample.dtype)
    ctx_row = lambda b: 0
    lat_row = lambda b: b + 1

    feats = _feat_call(x_prompt, None, mod, ctx_row, p)
    y, hs, s_ctx, l_ctx = _scan_call(
        feats[:8], jnp.zeros((n_ctx, 2, HEAD, D_A), F32), jnp.zeros((n_ctx, 2, D_B), F32), p)
    y_prompt = _out_call(x_prompt, None, mod, ctx_row, y, hs, feats[8], feats[9], feats[10], p)

    feats = _feat_call(x_sample, pos, mod, lat_row, p)
    y, hs, _, _ = _scan_call(feats[:8], _state_to_lanes(state_rwkv[:, l]), state_lru[:, l], p)
    y_sample = _out_call(x_sample, pos, mod, lat_row, y, hs, feats[8], feats[9], feats[10], p)

    new_state_rwkv = _state_from_lanes(s_ctx)[:, None].astype(x_prompt.dtype)
    new_state_lru = l_ctx[:, None].astype(x_prompt.dtype)
    return (y_prompt, y_sample, new_state_rwkv, new_state_lru)
```

```python
import functools

import numpy as np
import jax
import jax.numpy as jnp
from jax import lax
from jax.experimental import pallas as pl
from jax.experimental.pallas import tpu as pltpu

F32 = jnp.float32
BF16 = jnp.bfloat16

D_MODEL = 1024
D_A = 512
D_B = 512
HEAD = 64
N_HEAD = 8
R_W = 64
R_A = 64
R_G = 128
D_FF = 4096
D_IN = 2944
GRID_W = 64
CONV_W = 4
LRU_C = 8.0
EPS = 1e-6
LNX_EPS = 64e-5

CHUNK = 64
GROUP = 128
HEADS_PER_GROUP = GROUP // HEAD
N_GROUP = D_A // GROUP
SUB = 8
LANE = 128
N_SLAB = D_A // LANE
HALO = 8
TOKEN_TILE = 256
CPT = TOKEN_TILE // CHUNK
OUT_TILE = 512
MXU_TILE = 256
VMEM_LIMIT = 60 * 1024 * 1024

_O_R, _O_K, _O_V, _O_XW, _O_XA, _O_XG, _O_XB, _O_GB = 0, 512, 1024, 1536, 1664, 1792, 1920, 2432

_M_STRICT, _M_INCL, _M_LEV0 = 0, 1, 2
_LEVELS = (1, 2, 4, 8, 16, 32)
_M_PER_DIR = 2 + len(_LEVELS)
_M_EYE = 2 * _M_PER_DIR


def _dot(a, b):
    return jnp.dot(a.astype(BF16), b.astype(BF16), preferred_element_type=F32)


def _dot_nt(a, b):
    return lax.dot_general(a.astype(BF16), b.astype(BF16), (((1,), (1,)), ((), ())),
                           preferred_element_type=F32)


def _dot_tn(a, b):
    return lax.dot_general(a.astype(BF16), b.astype(BF16), (((0,), (0,)), ((), ())),
                           preferred_element_type=F32)


def _strided_rows(ref, lead, base):
    return [ref[lead + (pl.ds(base + j, SUB, stride=SUB), slice(None))] for j in range(SUB)]


def _natural_rows(ref, lead, base, pieces):
    for j, piece in enumerate(pieces):
        ref[lead + (pl.ds(base + SUB * j, SUB), slice(None))] = piece
    return jnp.concatenate(_strided_rows(ref, lead, base), axis=0)


def _sublane_shift(x, steps, reverse, fill):
    sub = lax.broadcasted_iota(jnp.int32, x.shape, 0)
    if reverse:
        return jnp.where(sub < SUB - steps, pltpu.roll(x, SUB - steps, 0), fill)
    return jnp.where(sub >= steps, pltpu.roll(x, steps, 0), fill)


def _scan_affine(a, b, reverse):
    a, b = list(a), list(b)
    order = range(SUB - 2, -1, -1) if reverse else range(1, SUB)
    for j in order:
        p = j + 1 if reverse else j - 1
        b[j] = a[j] * b[p] + b[j]
        a[j] = a[j] * a[p]
    ta, tb = (a[0], b[0]) if reverse else (a[SUB - 1], b[SUB - 1])
    s = 1
    while s < SUB:
        tb = ta * _sublane_shift(tb, s, reverse, 0.0) + tb
        ta = ta * _sublane_shift(ta, s, reverse, 1.0)
        s *= 2
    ea = _sublane_shift(ta, 1, reverse, 1.0)
    eb = _sublane_shift(tb, 1, reverse, 0.0)
    return [x * ea for x in a], [x * eb + y for x, y in zip(a, b)]


def _scan_sum(x, reverse):
    x = list(x)
    order = range(SUB - 2, -1, -1) if reverse else range(1, SUB)
    for j in order:
        x[j] = x[j] + x[j + 1 if reverse else j - 1]
    t = x[0] if reverse else x[SUB - 1]
    s = 1
    while s < SUB:
        t = t + _sublane_shift(t, s, reverse, 0.0)
        s *= 2
    e = _sublane_shift(t, 1, reverse, 0.0)
    return [y + e for y in x]


def _sigmoid(x):
    return 0.5 * jnp.tanh(0.5 * x) + 0.5


def _blockdiag(x, bdm):
    xb = x.astype(BF16)
    return jnp.concatenate([xb] * HEADS_PER_GROUP, axis=0) * bdm


def _mod_kernel(c_ref, w_ref, b_ref, o_ref):
    c = c_ref[...]
    s = c * _sigmoid(c)
    o_ref[...] = _dot(s, w_ref[...]) + b_ref[...]


def _mod_call(c_all, w_mod, b_mod):
    m = c_all.shape[0]
    n = w_mod.shape[1]
    tn = 1536
    return pl.pallas_call(
        _mod_kernel,
        grid=(n // tn,),
        in_specs=[pl.BlockSpec((m, D_MODEL), lambda j: (0, 0)),
                  pl.BlockSpec((D_MODEL, tn), lambda j: (0, j)),
                  pl.BlockSpec((1, tn), lambda j: (0, j))],
        out_specs=pl.BlockSpec((m, tn), lambda j: (0, j)),
        out_shape=jax.ShapeDtypeStruct((m, n), F32),
        compiler_params=pltpu.CompilerParams(dimension_semantics=("parallel",),
                                             vmem_limit_bytes=VMEM_LIMIT),
        name="mod",
    )(c_all, w_mod, b_mod.reshape(1, n))


def _chunk_masks():
    t = np.arange(CHUNK)[:, None]
    s = (np.arange(GROUP) % CHUNK)[None, :]
    rows = []
    for d in (0, 1):
        before = (s < t) if d == 0 else (s > t)
        rows.append(before)
        rows.append(before | (s == t))
        for b in _LEVELS:
            same = (t // (2 * b)) == (s // (2 * b))
            if d == 0:
                rows.append(same & ((t // b) % 2 == 1) & ((s // b) % 2 == 0))
            else:
                rows.append(same & ((t // b) % 2 == 0) & ((s // b) % 2 == 1))
    rows.append(s == t)
    return np.stack(rows).astype(np.float32)


def _group_blockdiag_mask():
    i = np.arange(GROUP)
    return ((i[:, None] // HEAD) == (i[None, :] // HEAD)).astype(np.float32)


def _feat_kernel(has_pos, nt, *refs):
    if has_pos:
        x_ref, xp_ref, xn_ref, pos_ref, pp_ref, pn_ref = refs[:6]
        refs = refs[6:]
    else:
        x_ref, xp_ref, xn_ref = refs[:3]
        pos_ref = pp_ref = pn_ref = None
        refs = refs[3:]
    (mod_ref, gpre_ref, win_ref, wup_ref, w0_ref, aup_ref, a0_ref, gup_ref,
     kk_ref, ka_ref, rk_ref, seg_ref,
     convw_ref, convb_ref, wa_ref, ba_ref, wx_ref, bx_ref, lam_ref,
     masks_ref, bdm_ref,
     p_o, q_o, m_o, n_o, gt_o, ac_o, bs_o, bl_o, g_o, bonus_o, gate_o,
     lw_s, lc_s, la_s, lb_s, lo_s) = refs

    i = pl.program_id(0)
    tm = TOKEN_TILE
    mod = mod_ref[0]
    shift1, scale1 = mod[0:1], mod[1:2]

    def normmod(xv):
        ms = jnp.mean(xv * xv, axis=-1, keepdims=True)
        hv = (xv * lax.rsqrt(ms + EPS)) * gpre_ref[...]
        return hv * (1.0 + scale1) + shift1

    x = x_ref[0]
    halo = jnp.concatenate([xp_ref[0], xn_ref[0]], axis=0)
    if has_pos:
        x = x + pos_ref[...]
        halo = halo + jnp.concatenate([pp_ref[...], pn_ref[...]], axis=0)
    z = _dot(normmod(x), win_ref[...])
    zh = _dot(normmod(halo), win_ref[:, _O_XB:_O_XB + D_B])

    r = z[:, _O_R:_O_R + D_A]
    k = z[:, _O_K:_O_K + D_A]
    v = z[:, _O_V:_O_V + D_A]
    xw = z[:, _O_XW:_O_XW + 2 * R_W]
    xa = z[:, _O_XA:_O_XA + 2 * R_A]
    xg = z[:, _O_XG:_O_XG + R_G]

    g_o[0] = _dot(_sigmoid(xg), gup_ref[...])
    gate_o[0] = jax.nn.gelu(z[:, _O_GB:_O_GB + D_B], approximate=True)
    wl = w0_ref[...] + _dot(jnp.tanh(xw), wup_ref[...])
    lw2 = -_sigmoid(wl) * float(np.exp(-0.5))
    a2 = _sigmoid(a0_ref[...] + _dot(xa, aup_ref[...]))

    kks = k * kk_ref[...]
    ss = _dot(kks * kks, seg_ref[...])
    kk = kks * lax.rsqrt(jnp.maximum(ss, 1e-24))
    ka = ka_ref[...]
    kd2 = [k * (1.0 + (a2[:, d * D_A:(d + 1) * D_A] - 1.0) * ka) for d in (0, 1)]
    bonus_o[0] = _dot(r * (kd2[0] + kd2[1]) * rk_ref[...], seg_ref[...]) * v

    m_prev = jnp.where(i > 0, 1.0, 0.0)
    m_next = jnp.where(i < nt - 1, 1.0, 0.0)
    ext = jnp.concatenate([zh[:HALO] * m_prev, z[:, _O_XB:_O_XB + D_B], zh[HALO:] * m_next], axis=0)
    n_ext = tm + 2 * HALO
    xc = convb_ref[...]
    for j in range(CONV_W):
        sh = (2 - j) % n_ext
        tap = ext if sh == 0 else pltpu.roll(ext, sh, 0)
        xc = xc + tap[HALO:HALO + tm] * convw_ref[j:j + 1, :]
    for d in (0, 1):
        rg = jnp.concatenate([_dot(xc[:, g * GROUP:(g + 1) * GROUP], wa_ref[d, g]) for g in range(N_GROUP)], 1)
        ig = jnp.concatenate([_dot(xc[:, g * GROUP:(g + 1) * GROUP], wx_ref[d, g]) for g in range(N_GROUP)], 1)
        rg = _sigmoid(rg + ba_ref[d:d + 1, :])
        ig = _sigmoid(ig + bx_ref[d:d + 1, :])
        log_a = -LRU_C * rg * jax.nn.softplus(-lam_ref[d:d + 1, :])
        a_lru = jnp.exp(log_a)
        b_lru = jnp.sqrt(-jnp.tanh(log_a) * (a_lru * a_lru + 1.0)) * (ig * xc)
        for q in range(N_SLAB):
            la_s[d, q] = a_lru[:, q * LANE:(q + 1) * LANE]
            lb_s[d, q] = b_lru[:, q * LANE:(q + 1) * LANE]
    for c in range(CPT):
        base = c * CHUNK
        rs = slice(base, base + CHUNK)
        for q in range(N_SLAB):
            ls = slice(q * LANE, (q + 1) * LANE)
            bsum = None
            for d in (0, 1):
                acum, bcum = _scan_affine(_strided_rows(la_s, (d, q), base), _strided_rows(lb_s, (d, q), base),
                                          d == 1)
                ac_o[d, 0, rs, ls] = _natural_rows(lo_s, (d, q), base, acum)
                bsum = bcum if bsum is None else [x + y for x, y in zip(bsum, bcum)]
                bl_o[d, 0, c, :, ls] = bcum[0][0:1, :] if d == 1 else bcum[SUB - 1][SUB - 1:SUB, :]
            bs_o[0, rs, ls] = _natural_rows(lo_s, (2, q), base, bsum)

    bdm = bdm_ref[...]
    bd = lambda xv: _blockdiag(xv, bdm)
    bdot = lambda lhs, rhs: jnp.dot(lhs.astype(BF16), bd(rhs), preferred_element_type=F32)

    a_t, r_t, b_t, k_t, g_tot = [], [], [], [], []
    for d in (0, 1):
        lw = lw2[:, d * D_A:(d + 1) * D_A]
        for q in range(N_SLAB):
            lw_s[d, q] = lw[:, q * LANE:(q + 1) * LANE]
        g_tot.append([])
        lc_rows = []
        for c in range(CPT):
            base = c * CHUNK
            blocks, totals = [], []
            for q in range(N_SLAB):
                pieces = _scan_sum(_strided_rows(lw_s, (d, q), base), d == 1)
                totals.append(pieces[0][0:1, :] if d == 1 else pieces[SUB - 1][SUB - 1:SUB, :])
                blocks.append(_natural_rows(lc_s, (d, q), base, pieces))
            lc_rows.append(jnp.concatenate(blocks, axis=1))
            g_tot[d].append(jnp.exp(jnp.concatenate(totals, axis=1)))
            gt_o[d, 0, c] = g_tot[d][c]
        lc = jnp.concatenate(lc_rows, axis=0)
        e_neg = jnp.exp(-lc)
        a_t.append(-kk * jnp.exp(lc - lw))
        r_t.append(r * jnp.exp(lc))
        b_t.append(kk * a2[:, d * D_A:(d + 1) * D_A] * e_neg)
        k_t.append(kd2[d] * e_neg)

    combos = [(d, c, g) for d in (0, 1) for c in range(CPT) for g in range(N_GROUP)]

    def cut(arr, c, g):
        return arr[c * CHUNK:(c + 1) * CHUNK, g * GROUP:(g + 1) * GROUP]

    sc = {}
    for key in combos:
        d, c, g = key
        ar = jnp.concatenate([cut(a_t[d], c, g), cut(r_t[d], c, g)], axis=0)
        rhs = jnp.concatenate([bd(cut(b_t[d], c, g)), bd(cut(k_t[d], c, g))], axis=0)
        sc[key] = _dot_nt(ar, rhs)
    n_ab, n_ak, n_rb, n_rk, tinv = {}, {}, {}, {}, {}
    for key in combos:
        m0 = key[0] * _M_PER_DIR
        n_ab[key] = sc[key][:CHUNK, :GROUP] * masks_ref[m0 + _M_STRICT]
        n_ak[key] = sc[key][:CHUNK, GROUP:] * masks_ref[m0 + _M_STRICT]
        n_rb[key] = sc[key][CHUNK:, :GROUP] * masks_ref[m0 + _M_INCL]
        n_rk[key] = sc[key][CHUNK:, GROUP:] * masks_ref[m0 + _M_INCL]
        tinv[key] = masks_ref[_M_EYE] + n_ab[key] * masks_ref[m0 + _M_LEV0]
    for li in range(1, len(_LEVELS)):
        pm = {}
        for key in combos:
            pm[key] = bdot(n_ab[key] * masks_ref[key[0] * _M_PER_DIR + _M_LEV0 + li], tinv[key])
        for key in combos:
            tinv[key] = tinv[key] + bdot(tinv[key], pm[key])

    kv = {key: bdot(jnp.concatenate([n_ak[key], n_rk[key]], axis=0), cut(v, key[1], key[2])) for key in combos}
    gm = {key: bdot(n_rb[key], tinv[key]) for key in combos}
    wp = {}
    for key in combos:
        d, c, g = key
        lhs = jnp.concatenate([tinv[key], gm[key]], axis=0)
        rhs = jnp.concatenate([bd(cut(a_t[d], c, g)), bd(kv[key][:CHUNK])], axis=1)
        wp[key] = jnp.dot(lhs.astype(BF16), rhs, preferred_element_type=F32)

    lane = lax.broadcasted_iota(jnp.int32, (CHUNK, GROUP), 1)

    def fold(full):
        out = full[(HEADS_PER_GROUP - 1) * HEAD:]
        for h in range(HEADS_PER_GROUP - 2, -1, -1):
            out = jnp.where(lane < (h + 1) * HEAD, full[h * HEAD:(h + 1) * HEAD], out)
        return out

    for key in combos:
        d, c, g = key
        rs = slice(c * CHUNK, (c + 1) * CHUNK)
        ls = slice(g * GROUP, (g + 1) * GROUP)
        w_, u0 = wp[key][:CHUNK, :GROUP], wp[key][:CHUNK, GROUP:]
        vg = cut(v, c, g)
        gt = g_tot[d][c][:, ls]
        bh, kh = cut(b_t[d], c, g) * gt, cut(k_t[d], c, g) * gt
        p_o[d, 0, rs, ls] = (cut(r_t[d], c, g) + wp[key][CHUNK:, :GROUP]).astype(BF16)
        m_o[d, 0, rs, ls] = fold(_dot_tn(w_, bh)).astype(BF16)
        n_o[d, 0, rs, ls] = fold(_dot_tn(jnp.concatenate([u0, vg], axis=0), jnp.concatenate([bh, kh], axis=0)))
        if d == 1:
            other = (0, c, g)
            q_o[0, rs, ls] = ((wp[other][CHUNK:, GROUP:] + kv[other][CHUNK:])
                              + (wp[key][CHUNK:, GROUP:] + kv[key][CHUNK:]))


def _feat_call(x, pos, mod, mod_row, p):
    bsz, t, _ = x.shape
    tm = TOKEN_TILE
    nt = t // tm
    nc = t // CHUNK
    hpt = tm // HALO
    has_pos = pos is not None

    def const(shape):
        return pl.BlockSpec(shape, lambda i, b: (0,) * len(shape))

    tok = lambda width: pl.BlockSpec((1, tm, width), lambda i, b: (b, i, 0))
    prev_i = lambda i: jnp.maximum(i * hpt - 1, 0)
    next_i = lambda i: jnp.minimum((i + 1) * hpt, t // HALO - 1)
    in_specs = [tok(D_MODEL),
                pl.BlockSpec((1, HALO, D_MODEL), lambda i, b: (b, prev_i(i), 0)),
                pl.BlockSpec((1, HALO, D_MODEL), lambda i, b: (b, next_i(i), 0))]
    args = [x, x, x]
    if has_pos:
        in_specs += [pl.BlockSpec((tm, D_MODEL), lambda i, b: (i, 0)),
                     pl.BlockSpec((HALO, D_MODEL), lambda i, b: (prev_i(i), 0)),
                     pl.BlockSpec((HALO, D_MODEL), lambda i, b: (next_i(i), 0))]
        args += [pos, pos, pos]
    n_masks = 2 * _M_PER_DIR + 1
    in_specs += [
        pl.BlockSpec((1, 6, D_MODEL), lambda i, b: (mod_row(b), 0, 0)),
        const((1, D_MODEL)), const((D_MODEL, D_IN)),
        const((2 * R_W, 2 * D_A)), const((1, 2 * D_A)),
        const((2 * R_A, 2 * D_A)), const((1, 2 * D_A)),
        const((R_G, D_A)),
        const((1, D_A)), const((1, D_A)), const((1, D_A)), const((D_A, D_A)),
        const((CONV_W, D_B)), const((1, D_B)),
        const((2, N_GROUP, GROUP, GROUP)), const((2, D_B)),
        const((2, N_GROUP, GROUP, GROUP)), const((2, D_B)), const((2, D_B)),
        const((n_masks, CHUNK, GROUP)), const((GROUP, GROUP)),
    ]
    args += [mod, p["g_pre_mix"], p["w_in"], p["wup_bd"], p["w0"], p["aup_bd"], p["a0"], p["g_up"],
             p["k_k"], p["k_a"], p["r_k"], p["seg_ones"],
             p["conv_w"], p["conv_b"], p["wa_bd"], p["ba"], p["wx_bd"], p["bx"], p["lam"],
             p["chunk_masks"], p["bdm_bf16"]]
    tok_shape = jax.ShapeDtypeStruct((bsz, t, D_A), F32)
    dir_shape = jax.ShapeDtypeStruct((2, bsz, t, D_A), F32)
    row_shape = jax.ShapeDtypeStruct((2, bsz, nc, 1, D_A), F32)
    mxu_shape = jax.ShapeDtypeStruct((2, bsz, t, D_A), BF16)
    dir_spec = pl.BlockSpec((2, 1, tm, D_A), lambda i, b: (0, b, i, 0))
    row_spec = pl.BlockSpec((2, 1, CPT, 1, D_A), lambda i, b: (0, b, i, 0, 0))
    out_shape = [mxu_shape, tok_shape, mxu_shape, dir_shape, row_shape, dir_shape, tok_shape, row_shape,
                 tok_shape, tok_shape, tok_shape]
    out_specs = [dir_spec, tok(D_A), dir_spec, dir_spec, row_spec, dir_spec, tok(D_B), row_spec,
                 tok(D_A), tok(D_A), tok(D_B)]
    return pl.pallas_call(
        functools.partial(_feat_kernel, has_pos, nt),
        grid=(nt, bsz),
        in_specs=in_specs,
        out_specs=out_specs,
        out_shape=out_shape,
        scratch_shapes=[pltpu.VMEM((2, N_SLAB, tm, LANE), F32)] * 4 + [pltpu.VMEM((3, N_SLAB, tm, LANE), F32)],
        compiler_params=pltpu.CompilerParams(dimension_semantics=("parallel", "parallel"),
                                             vmem_limit_bytes=VMEM_LIMIT),
        name="feat",
    )(*args)


def _scan_kernel(nt, *refs):
    (pf_ref, mf_ref, nf_ref, gtf_ref, acf_ref, blf_ref, q_ref, bs_ref,
     pb_ref, mb_ref, nb_ref, gtb_ref, acb_ref, blb_ref,
     s0_ref, l0_ref, bdm_ref,
     y_ref, hs_ref, s_ref, hl_ref) = refs

    i = pl.program_id(1)

    @pl.when(i == 0)
    def _init():
        y_ref[...] = jnp.zeros_like(y_ref)
        hs_ref[...] = jnp.zeros_like(hs_ref)
        s_ref[...] = s0_ref[...]
        hl_ref[...] = l0_ref[...]

    bdm = bdm_ref[...]
    per_dir = ((0, pf_ref, mf_ref, nf_ref, gtf_ref, acf_ref, blf_ref),
               (1, pb_ref, mb_ref, nb_ref, gtb_ref, acb_ref, blb_ref))
    tiles = (i, nt - 1 - i)
    state = [s_ref[0, d] for d in (0, 1)]
    h0 = [hl_ref[0, d:d + 1, :] for d in (0, 1)]
    for cc in range(CPT):
        for d, p_ref, m_ref, n_ref, gt_ref, ac_ref, bl_ref in per_dir:
            c = cc if d == 0 else CPT - 1 - cc
            rs = slice(c * CHUNK, (c + 1) * CHUNK)
            rows = pl.ds(pl.multiple_of(tiles[d] * TOKEN_TILE + c * CHUNK, CHUNK), CHUNK)
            gt = gt_ref[0, 0, c]
            y_parts, s_parts = [], []
            for g in range(N_GROUP):
                ls = slice(g * GROUP, (g + 1) * GROUP)
                sg = state[d][:, ls]
                y_parts.append(_dot_nt(p_ref[0, 0, rs, ls], _blockdiag(sg, bdm)))
                s_parts.append(sg * gt[:, ls]
                               + jnp.dot(sg.astype(BF16), _blockdiag(m_ref[0, 0, rs, ls], bdm),
                                         preferred_element_type=F32)
                               + n_ref[0, 0, rs, ls])
            state[d] = jnp.concatenate(s_parts, axis=1)
            y_c = jnp.concatenate(y_parts, axis=1)
            acum = ac_ref[0, 0, rs, :]
            h_c = acum * h0[d]
            if d == 0:
                y_c = y_c + q_ref[0, rs, :]
                h_c = h_c + bs_ref[0, rs, :]
            y_ref[0, rows, :] += y_c
            hs_ref[0, rows, :] += h_c
            last = CHUNK - 1 if d == 0 else 0
            h0[d] = bl_ref[0, 0, c] + acum[last:last + 1, :] * h0[d]
    for d in (0, 1):
        s_ref[0, d] = state[d]
        hl_ref[0, d:d + 1, :] = h0[d]


def _scan_call(feats, s0, l0, p):
    pm, q, mm, nm, gt, ac, bs, bl = feats
    _, bsz, t, _ = pm.shape
    nt = t // TOKEN_TILE

    def dir_specs(d, tile_of):
        big = pl.BlockSpec((1, 1, TOKEN_TILE, D_A), lambda b, i: (d, b, tile_of(i), 0))
        small = pl.BlockSpec((1, 1, CPT, 1, D_A), lambda b, i: (d, b, tile_of(i), 0, 0))
        return big, small

    big_f, small_f = dir_specs(0, lambda i: i)
    big_b, small_b = dir_specs(1, lambda i: nt - 1 - i)
    tok_f = pl.BlockSpec((1, TOKEN_TILE, D_A), lambda b, i: (b, i, 0))
    in_specs = [big_f, big_f, big_f, small_f, big_f, small_f, tok_f, tok_f,
                big_b, big_b, big_b, small_b, big_b, small_b,
                pl.BlockSpec((1, 2, HEAD, D_A), lambda b, i: (b, 0, 0, 0)),
                pl.BlockSpec((1, 2, D_B), lambda b, i: (b, 0, 0)),
                pl.BlockSpec((GROUP, GROUP), lambda b, i: (0, 0))]
    args = [pm, mm, nm, gt, ac, bl, q, bs, pm, mm, nm, gt, ac, bl, s0, l0, p["bdm_bf16"]]
    seq_spec = pl.BlockSpec((1, t, D_A), lambda b, i: (b, 0, 0))
    out_specs = [seq_spec, seq_spec,
                 pl.BlockSpec((1, 2, HEAD, D_A), lambda b, i: (b, 0, 0, 0)),
                 pl.BlockSpec((1, 2, D_B), lambda b, i: (b, 0, 0))]
    out_shape = [jax.ShapeDtypeStruct((bsz, t, D_A), F32), jax.ShapeDtypeStruct((bsz, t, D_B), F32),
                 jax.ShapeDtypeStruct((bsz, 2, HEAD, D_A), F32),
                 jax.ShapeDtypeStruct((bsz, 2, D_B), F32)]
    return pl.pallas_call(
        functools.partial(_scan_kernel, nt),
        grid=(bsz, nt),
        in_specs=in_specs,
        out_specs=out_specs,
        out_shape=out_shape,
        compiler_params=pltpu.CompilerParams(dimension_semantics=("parallel", "arbitrary"),
                                             vmem_limit_bytes=VMEM_LIMIT),
        name="scan",
    )(*args)


def _out_kernel(has_pos, *refs):
    if has_pos:
        x_ref, pos_ref = refs[0], refs[1]
        refs = refs[2:]
    else:
        x_ref, pos_ref = refs[0], None
        refs = refs[1:]
    (y_ref, hs_ref, g_ref, bonus_ref, gate_ref, mod_ref,
     avg_ref, lnxg_ref, lnxb_ref, wout_ref, gpost_ref, gpre2_ref, w1_ref, w2_ref, gpost2_ref,
     o_ref) = refs

    x = x_ref[0]
    if has_pos:
        x = x + pos_ref[...]
    mod = mod_ref[0]
    gate1, shift2, scale2, gate2 = mod[2:3], mod[3:4], mod[4:5], mod[5:6]

    y = y_ref[0]
    avg = avg_ref[...]

    def head_mean(a):
        return jnp.concatenate([_dot(a[:, t * MXU_TILE:(t + 1) * MXU_TILE], avg)
                                for t in range(D_A // MXU_TILE)], axis=1)

    y_hi = y.astype(BF16)
    y_lo = (y - y_hi.astype(F32)).astype(BF16)
    mu = head_mean(y_hi) + head_mean(y_lo)
    yc = y - mu
    var = head_mean(yc * yc)
    yn = yc * lax.rsqrt(var + LNX_EPS) * lnxg_ref[...] + lnxb_ref[...]
    out_a = (yn + bonus_ref[0]) * g_ref[0]
    out_b = hs_ref[0] * gate_ref[0]
    mix = _dot(jnp.concatenate([out_a, out_b], axis=1), wout_ref[...])
    ms = jnp.mean(mix * mix, axis=-1, keepdims=True)
    x = x + gate1 * ((mix * lax.rsqrt(ms + EPS)) * gpost_ref[...])

    ms = jnp.mean(x * x, axis=-1, keepdims=True)
    h = (x * lax.rsqrt(ms + EPS)) * gpre2_ref[...]
    h = h * (1.0 + scale2) + shift2
    f = _dot(h, w1_ref[...])
    f = jnp.square(jnp.maximum(f, 0.0))
    f = _dot(f, w2_ref[...])
    ms = jnp.mean(f * f, axis=-1, keepdims=True)
    o_ref[0] = x + gate2 * ((f * lax.rsqrt(ms + EPS)) * gpost2_ref[...])


def _out_call(x, pos, mod, mod_row, y, hs, g, bonus, gate, p):
    bsz, t, _ = x.shape
    tm = min(OUT_TILE, t)
    nt = t // tm
    has_pos = pos is not None

    def const(shape):
        return pl.BlockSpec(shape, lambda i, b: (0,) * len(shape))

    tok = lambda width: pl.BlockSpec((1, tm, width), lambda i, b: (b, i, 0))
    in_specs = [tok(D_MODEL)]
    args = [x]
    if has_pos:
        in_specs.append(pl.BlockSpec((tm, D_MODEL), lambda i, b: (i, 0)))
        args.append(pos)
    in_specs += [tok(D_A)] * 5
    in_specs += [
        pl.BlockSpec((1, 6, D_MODEL), lambda i, b: (mod_row(b), 0, 0)),
        const((MXU_TILE, MXU_TILE)), const((1, D_A)), const((1, D_A)),
        const((D_MODEL, D_MODEL)), const((1, D_MODEL)), const((1, D_MODEL)),
        const((D_MODEL, D_FF)), const((D_FF, D_MODEL)), const((1, D_MODEL)),
    ]
    args += [y, hs, g, bonus, gate, mod,
             p["seg_avg"], p["lnx_g"], p["lnx_b"], p["w_out"], p["g_post_mix"], p["g_pre_mlp"],
             p["w_mlp1"], p["w_mlp2"], p["g_post_mlp"]]
    return pl.pallas_call(
        functools.partial(_out_kernel, has_pos),
        grid=(nt, bsz),
        in_specs=in_specs,
        out_specs=tok(D_MODEL),
        out_shape=jax.ShapeDtypeStruct((bsz, t, D_MODEL), F32),
        compiler_params=pltpu.CompilerParams(dimension_semantics=("parallel", "parallel"),
                                             vmem_limit_bytes=VMEM_LIMIT),
        name="out",
    )(*args)


def _sincos_1d(pos, dim):
    omega = 1.0 / (10000.0 ** (jnp.arange(dim // 2, dtype=F32) / (dim // 2)))
    ang = pos.astype(F32)[:, None] * omega[None, :]
    return jnp.concatenate([jnp.sin(ang), jnp.cos(ang)], axis=-1)


def _grid_pos_embed(n_tokens):
    rows = n_tokens // GRID_W
    half = D_MODEL // 2
    e_row = _sincos_1d(jnp.arange(rows), half)
    e_col = _sincos_1d(jnp.arange(GRID_W), half)
    emb = jnp.concatenate([jnp.broadcast_to(e_row[:, None, :], (rows, GRID_W, half)),
                           jnp.broadcast_to(e_col[None, :, :], (rows, GRID_W, half))], axis=-1)
    return emb.reshape(rows * GRID_W, D_MODEL)


def _blockdiag_pairs(w):
    z = jnp.zeros_like(w[0])
    return jnp.concatenate([jnp.concatenate([w[0], z], axis=1),
                            jnp.concatenate([z, w[1]], axis=1)], axis=0)


def _heads_to_blockdiag(w):
    lead = w.shape[:-3]
    w = w.reshape(lead + (N_GROUP, HEADS_PER_GROUP, HEAD, HEAD))
    eye = jnp.eye(HEADS_PER_GROUP, dtype=w.dtype)
    bd = jnp.einsum('...ghab,hj->...ghajb', w, eye)
    return bd.reshape(lead + (N_GROUP, GROUP, GROUP))


def _state_to_lanes(s):
    b = s.shape[0]
    return jnp.transpose(s, (0, 1, 3, 2, 4)).reshape(b, 2, HEAD, D_A)


def _state_from_lanes(s):
    b = s.shape[0]
    return jnp.transpose(s.reshape(b, 2, HEAD, N_HEAD, HEAD), (0, 1, 3, 2, 4))


def kernel(x_prompt, x_sample, c, state_rwkv, state_lru, c_ctx, w_mod, b_mod, g_pre_mix, g_post_mix,
           g_pre_mlp, g_post_mlp, w_in, rwkv_w0, rwkv_w_up, rwkv_a0, rwkv_a_up, rwkv_g_up, rwkv_k_k,
           rwkv_k_a, rwkv_r_k, rwkv_lnx_g, rwkv_lnx_b, lru_conv_w, lru_conv_b, lru_wa, lru_ba, lru_wx,
           lru_bx, lru_lambda, w_out, w_mlp1, w_mlp2):
    n_ctx = x_prompt.shape[0]
    n_lat = x_sample.shape[0]
    l = 0
    seg = _group_blockdiag_mask()
    seg512 = np.kron(np.eye(N_GROUP, dtype=np.float32), seg)
    p = {
        "g_pre_mix": g_pre_mix[l][None], "g_post_mix": g_post_mix[l][None],
        "g_pre_mlp": g_pre_mlp[l][None], "g_post_mlp": g_post_mlp[l][None],
        "w_in": w_in[l].astype(BF16), "w_out": w_out[l].astype(BF16),
        "w_mlp1": w_mlp1[l].astype(BF16), "w_mlp2": w_mlp2[l].astype(BF16),
        "w0": rwkv_w0[l].reshape(1, 2 * D_A), "a0": rwkv_a0[l].reshape(1, 2 * D_A),
        "wup_bd": _blockdiag_pairs(rwkv_w_up[l]).astype(BF16),
        "aup_bd": _blockdiag_pairs(rwkv_a_up[l]).astype(BF16),
        "g_up": rwkv_g_up[l].astype(BF16),
        "k_k": rwkv_k_k[l][None], "k_a": rwkv_k_a[l][None], "r_k": rwkv_r_k[l].reshape(1, D_A),
        "lnx_g": rwkv_lnx_g[l][None], "lnx_b": rwkv_lnx_b[l][None],
        "conv_w": lru_conv_w[l], "conv_b": lru_conv_b[l][None],
        "wa_bd": _heads_to_blockdiag(lru_wa[l]).astype(BF16), "ba": lru_ba[l],
        "wx_bd": _heads_to_blockdiag(lru_wx[l]).astype(BF16), "bx": lru_bx[l],
        "lam": lru_lambda[l],
        "seg_ones": jnp.asarray(seg512, BF16),
        "seg_avg": jnp.asarray(np.kron(np.eye(MXU_TILE // GROUP, dtype=np.float32), seg) / HEAD, BF16),
        "chunk_masks": jnp.asarray(_chunk_masks()),
        "bdm_bf16": jnp.asarray(seg, BF16),
    }

    m_rows = 16
    c_all = jnp.concatenate([c_ctx[None], c, jnp.zeros((m_rows - 1 - n_lat, D_MODEL), F32)], axis=0)
    mod = _mod_call(c_all, w_mod[l], b_mod[l]).reshape(m_rows, 6, D_MODEL)

    pos = _grid_pos_embed(x_sample.shape[1]).astype(x_sample.dtype)
    ctx_row = lambda b: 0
    lat_row = lambda b: b + 1

    feats = _feat_call(x_prompt, None, mod, ctx_row, p)
    y, hs, s_ctx, l_ctx = _scan_call(
        feats[:8], jnp.zeros((n_ctx, 2, HEAD, D_A), F32), jnp.zeros((n_ctx, 2, D_B), F32), p)
    y_prompt = _out_call(x_prompt, None, mod, ctx_row, y, hs, feats[8], feats[9], feats[10], p)

    feats = _feat_call(x_sample, pos, mod, lat_row, p)
    y, hs, _, _ = _scan_call(feats[:8], _state_to_lanes(state_rwkv[:, l]), state_lru[:, l], p)
    y_sample = _out_call(x_sample, pos, mod, lat_row, y, hs, feats[8], feats[9], feats[10], p)

    new_state_rwkv = _state_from_lanes(s_ctx)[:, None].astype(x_prompt.dtype)
    new_state_lru = l_ctx[:, None].astype(x_prompt.dtype)
    return (y_prompt, y_sample, new_state_rwkv, new_state_lru)
```

```python
import functools

import numpy as np
import jax
import jax.numpy as jnp
from jax import lax
from jax.experimental import pallas as pl
from jax.experimental.pallas import tpu as pltpu

F32 = jnp.float32
BF16 = jnp.bfloat16

D_MODEL = 1024
D_A = 512
D_B = 512
HEAD = 64
N_HEAD = 8
R_W = 64
R_A = 64
R_G = 128
D_FF = 4096
D_IN = 2944
GRID_W = 64
CONV_W = 4
LRU_C = 8.0
EPS = 1e-6
LNX_EPS = 64e-5

CHUNK = 64
GROUP = 128
HEADS_PER_GROUP = GROUP // HEAD
N_GROUP = D_A // GROUP
SUB = 8
LANE = 128
N_SLAB = D_A // LANE
HALO = 8
TOKEN_TILE = 256
CPT = TOKEN_TILE // CHUNK
OUT_TILE = 512
MXU_TILE = 256
VMEM_LIMIT = 60 * 1024 * 1024

_O_R, _O_K, _O_V, _O_XW, _O_XA, _O_XG, _O_XB, _O_GB = 0, 512, 1024, 1536, 1664, 1792, 1920, 2432

_M_STRICT, _M_INCL, _M_LEV0 = 0, 1, 2
_LEVELS = (1, 2, 4, 8, 16, 32)
_M_PER_DIR = 2 + len(_LEVELS)
_M_EYE = 2 * _M_PER_DIR


def _dot(a, b):
    return jnp.dot(a.astype(BF16), b.astype(BF16), preferred_element_type=F32)


def _dot_nt(a, b):
    return lax.dot_general(a.astype(BF16), b.astype(BF16), (((1,), (1,)), ((), ())),
                           preferred_element_type=F32)


def _dot_tn(a, b):
    return lax.dot_general(a.astype(BF16), b.astype(BF16), (((0,), (0,)), ((), ())),
                           preferred_element_type=F32)


def _strided_rows(ref, lead, base):
    return [ref[lead + (pl.ds(base + j, SUB, stride=SUB), slice(None))] for j in range(SUB)]


def _natural_rows(ref, lead, base, pieces):
    for j, piece in enumerate(pieces):
        ref[lead + (pl.ds(base + SUB * j, SUB), slice(None))] = piece
    return jnp.concatenate(_strided_rows(ref, lead, base), axis=0)


def _sublane_shift(x, steps, reverse, fill):
    sub = lax.broadcasted_iota(jnp.int32, x.shape, 0)
    if reverse:
        return jnp.where(sub < SUB - steps, pltpu.roll(x, SUB - steps, 0), fill)
    return jnp.where(sub >= steps, pltpu.roll(x, steps, 0), fill)


def _scan_affine(a, b, reverse):
    a, b = list(a), list(b)
    order = range(SUB - 2, -1, -1) if reverse else range(1, SUB)
    for j in order:
        p = j + 1 if reverse else j - 1
        b[j] = a[j] * b[p] + b[j]
        a[j] = a[j] * a[p]
    ta, tb = (a[0], b[0]) if reverse else (a[SUB - 1], b[SUB - 1])
    s = 1
    while s < SUB:
        tb = ta * _sublane_shift(tb, s, reverse, 0.0) + tb
        ta = ta * _sublane_shift(ta, s, reverse, 1.0)
        s *= 2
    ea = _sublane_shift(ta, 1, reverse, 1.0)
    eb = _sublane_shift(tb, 1, reverse, 0.0)
    return [x * ea for x in a], [x * eb + y for x, y in zip(a, b)]


def _scan_sum(x, reverse):
    x = list(x)
    order = range(SUB - 2, -1, -1) if reverse else range(1, SUB)
    for j in order:
        x[j] = x[j] + x[j + 1 if reverse else j - 1]
    t = x[0] if reverse else x[SUB - 1]
    s = 1
    while s < SUB:
        t = t + _sublane_shift(t, s, reverse, 0.0)
        s *= 2
    e = _sublane_shift(t, 1, reverse, 0.0)
    return [y + e for y in x]


def _sigmoid(x):
    return 0.5 * jnp.tanh(0.5 * x) + 0.5


def _blockdiag(x, bdm):
    xb = x.astype(BF16)
    return jnp.concatenate([xb] * HEADS_PER_GROUP, axis=0) * bdm


def _mod_kernel(c_ref, w_ref, b_ref, o_ref):
    c = c_ref[...]
    s = c * _sigmoid(c)
    o_ref[...] = _dot(s, w_ref[...]) + b_ref[...]


def _mod_call(c_all, w_mod, b_mod):
    m = c_all.shape[0]
    n = w_mod.shape[1]
    tn = 1536
    return pl.pallas_call(
        _mod_kernel,
        grid=(n // tn,),
        in_specs=[pl.BlockSpec((m, D_MODEL), lambda j: (0, 0)),
                  pl.BlockSpec((D_MODEL, tn), lambda j: (0, j)),
                  pl.BlockSpec((1, tn), lambda j: (0, j))],
        out_specs=pl.BlockSpec((m, tn), lambda j: (0, j)),
        out_shape=jax.ShapeDtypeStruct((m, n), F32),
        compiler_params=pltpu.CompilerParams(dimension_semantics=("parallel",),
                                             vmem_limit_bytes=VMEM_LIMIT),
        name="mod",
    )(c_all, w_mod, b_mod.reshape(1, n))


def _chunk_masks():
    t = np.arange(CHUNK)[:, None]
    s = (np.arange(GROUP) % CHUNK)[None, :]
    rows = []
    for d in (0, 1):
        before = (s < t) if d == 0 else (s > t)
        rows.append(before)
        rows.append(before | (s == t))
        for b in _LEVELS:
            same = (t // (2 * b)) == (s // (2 * b))
            if d == 0:
                rows.append(same & ((t // b) % 2 == 1) & ((s // b) % 2 == 0))
            else:
                rows.append(same & ((t // b) % 2 == 0) & ((s // b) % 2 == 1))
    rows.append(s == t)
    return np.stack(rows).astype(np.float32)


def _group_blockdiag_mask():
    i = np.arange(GROUP)
    return ((i[:, None] // HEAD) == (i[None, :] // HEAD)).astype(np.float32)


def _feat_kernel(has_pos, nt, *refs):
    if has_pos:
        x_ref, xp_ref, xn_ref, pos_ref, pp_ref, pn_ref = refs[:6]
        refs = refs[6:]
    else:
        x_ref, xp_ref, xn_ref = refs[:3]
        pos_ref = pp_ref = pn_ref = None
        refs = refs[3:]
    (mod_ref, gpre_ref, win_ref, wup_ref, w0_ref, aup_ref, a0_ref, gup_ref,
     kk_ref, ka_ref, rk_ref, seg_ref,
     convw_ref, convb_ref, wa_ref, ba_ref, wx_ref, bx_ref, lam_ref,
     masks_ref, bdm_ref,
     p_o, q_o, m_o, n_o, gt_o, ac_o, bs_o, bl_o, g_o, bonus_o, gate_o,
     lw_s, lc_s, la_s, lb_s, lo_s) = refs

    i = pl.program_id(0)
    tm = TOKEN_TILE
    mod = mod_ref[0]
    shift1, scale1 = mod[0:1], mod[1:2]

    def normmod(xv):
        ms = jnp.mean(xv * xv, axis=-1, keepdims=True)
        hv = (xv * lax.rsqrt(ms + EPS)) * gpre_ref[...]
        return hv * (1.0 + scale1) + shift1

    x = x_ref[0]
    halo = jnp.concatenate([xp_ref[0], xn_ref[0]], axis=0)
    if has_pos:
        x = x + pos_ref[...]
        halo = halo + jnp.concatenate([pp_ref[...], pn_ref[...]], axis=0)
    z = _dot(normmod(x), win_ref[...])
    zh = _dot(normmod(halo), win_ref[:, _O_XB:_O_XB + D_B])

    r = z[:, _O_R:_O_R + D_A]
    k = z[:, _O_K:_O_K + D_A]
    v = z[:, _O_V:_O_V + D_A]
    xw = z[:, _O_XW:_O_XW + 2 * R_W]
    xa = z[:, _O_XA:_O_XA + 2 * R_A]
    xg = z[:, _O_XG:_O_XG + R_G]

    g_o[0] = _dot(_sigmoid(xg), gup_ref[...])
    gate_o[0] = jax.nn.gelu(z[:, _O_GB:_O_GB + D_B], approximate=True)
    wl = w0_ref[...] + _dot(jnp.tanh(xw), wup_ref[...])
    lw2 = -_sigmoid(wl) * float(np.exp(-0.5))
    a2 = _sigmoid(a0_ref[...] + _dot(xa, aup_ref[...]))

    kks = k * kk_ref[...]
    ss = _dot(kks * kks, seg_ref[...])
    kk = kks * lax.rsqrt(jnp.maximum(ss, 1e-24))
    ka = ka_ref[...]
    kd2 = [k * (1.0 + (a2[:, d * D_A:(d + 1) * D_A] - 1.0) * ka) for d in (0, 1)]
    bonus_o[0] = _dot(r * (kd2[0] + kd2[1]) * rk_ref[...], seg_ref[...]) * v

    m_prev = jnp.where(i > 0, 1.0, 0.0)
    m_next = jnp.where(i < nt - 1, 1.0, 0.0)
    ext = jnp.concatenate([zh[:HALO] * m_prev, z[:, _O_XB:_O_XB + D_B], zh[HALO:] * m_next], axis=0)
    n_ext = tm + 2 * HALO
    xc = convb_ref[...]
    for j in range(CONV_W):
        sh = (2 - j) % n_ext
        tap = ext if sh == 0 else pltpu.roll(ext, sh, 0)
        xc = xc + tap[HALO:HALO + tm] * convw_ref[j:j + 1, :]
    for d in (0, 1):
        rg = jnp.concatenate([_dot(xc[:, g * GROUP:(g + 1) * GROUP], wa_ref[d, g]) for g in range(N_GROUP)], 1)
        ig = jnp.concatenate([_dot(xc[:, g * GROUP:(g + 1) * GROUP], wx_ref[d, g]) for g in range(N_GROUP)], 1)
        rg = _sigmoid(rg + ba_ref[d:d + 1, :])
        ig = _sigmoid(ig + bx_ref[d:d + 1, :])
        log_a = -LRU_C * rg * jax.nn.softplus(-lam_ref[d:d + 1, :])
        a_lru = jnp.exp(log_a)
        b_lru = jnp.sqrt(-jnp.tanh(log_a) * (a_lru * a_lru + 1.0)) * (ig * xc)
        for q in range(N_SLAB):
            la_s[d, q] = a_lru[:, q * LANE:(q + 1) * LANE]
            lb_s[d, q] = b_lru[:, q * LANE:(q + 1) * LANE]
    for c in range(CPT):
        base = c * CHUNK
        rs = slice(base, base + CHUNK)
        for q in range(N_SLAB):
            ls = slice(q * LANE, (q + 1) * LANE)
            bsum = None
            for d in (0, 1):
                acum, bcum = _scan_affine(_strided_rows(la_s, (d, q), base), _strided_rows(lb_s, (d, q), base),
                                          d == 1)
                ac_o[d, 0, rs, ls] = _natural_rows(lo_s, (d, q), base, acum)
                bsum = bcum if bsum is None else [x + y for x, y in zip(bsum, bcum)]
                bl_o[d, 0, c, :, ls] = bcum[0][0:1, :] if d == 1 else bcum[SUB - 1][SUB - 1:SUB, :]
            bs_o[0, rs, ls] = _natural_rows(lo_s, (2, q), base, bsum)

    bdm = bdm_ref[...]
    bd = lambda xv: _blockdiag(xv, bdm)
    bdot = lambda lhs, rhs: jnp.dot(lhs.astype(BF16), bd(rhs), preferred_element_type=F32)

    a_t, r_t, b_t, k_t, g_tot = [], [], [], [], []
    for d in (0, 1):
        lw = lw2[:, d * D_A:(d + 1) * D_A]
        for q in range(N_SLAB):
            lw_s[d, q] = lw[:, q * LANE:(q + 1) * LANE]
        g_tot.append([])
        lc_rows = []
        for c in range(CPT):
            base = c * CHUNK
            blocks, totals = [], []
            for q in range(N_SLAB):
                pieces = _scan_sum(_strided_rows(lw_s, (d, q), base), d == 1)
                totals.append(pieces[0][0:1, :] if d == 1 else pieces[SUB - 1][SUB - 1:SUB, :])
                blocks.append(_natural_rows(lc_s, (d, q), base, pieces))
            lc_rows.append(jnp.concatenate(blocks, axis=1))
            g_tot[d].append(jnp.exp(jnp.concatenate(totals, axis=1)))
            gt_o[d, 0, c] = g_tot[d][c]
        lc = jnp.concatenate(lc_rows, axis=0)
        e_neg = jnp.exp(-lc)
        a_t.append(-kk * jnp.exp(lc - lw))
        r_t.append(r * jnp.exp(lc))
        b_t.append(kk * a2[:, d * D_A:(d + 1) * D_A] * e_neg)
        k_t.append(kd2[d] * e_neg)

    combos = [(d, c, g) for d in (0, 1) for c in range(CPT) for g in range(N_GROUP)]

    def cut(arr, c, g):
        return arr[c * CHUNK:(c + 1) * CHUNK, g * GROUP:(g + 1) * GROUP]

    sc = {}
    for key in combos:
        d, c, g = key
        ar = jnp.concatenate([cut(a_t[d], c, g), cut(r_t[d], c, g)], axis=0)
        rhs = jnp.concatenate([bd(cut(b_t[d], c, g)), bd(cut(k_t[d], c, g))], axis=0)
        sc[key] = _dot_nt(ar, rhs)
    n_ab, n_ak, n_rb, n_rk, tinv = {}, {}, {}, {}, {}
    for key in combos:
        m0 = key[0] * _M_PER_DIR
        n_ab[key] = sc[key][:CHUNK, :GROUP] * masks_ref[m0 + _M_STRICT]
        n_ak[key] = sc[key][:CHUNK, GROUP:] * masks_ref[m0 + _M_STRICT]
        n_rb[key] = sc[key][CHUNK:, :GROUP] * masks_ref[m0 + _M_INCL]
        n_rk[key] = sc[key][CHUNK:, GROUP:] * masks_ref[m0 + _M_INCL]
        tinv[key] = masks_ref[_M_EYE] + n_ab[key] * masks_ref[m0 + _M_LEV0]
    for li in range(1, len(_LEVELS)):
        pm = {}
        for key in combos:
            pm[key] = bdot(n_ab[key] * masks_ref[key[0] * _M_PER_DIR + _M_LEV0 + li], tinv[key])
        for key in combos:
            tinv[key] = tinv[key] + bdot(tinv[key], pm[key])

    kv = {key: bdot(jnp.concatenate([n_ak[key], n_rk[key]], axis=0), cut(v, key[1], key[2])) for key in combos}
    gm = {key: bdot(n_rb[key], tinv[key]) for key in combos}
    wp = {}
    for key in combos:
        d, c, g = key
        lhs = jnp.concatenate([tinv[key], gm[key]], axis=0)
        rhs = jnp.concatenate([bd(cut(a_t[d], c, g)), bd(kv[key][:CHUNK])], axis=1)
        wp[key] = jnp.dot(lhs.astype(BF16), rhs, preferred_element_type=F32)

    lane = lax.broadcasted_iota(jnp.int32, (CHUNK, GROUP), 1)

    def fold(full):
        out = full[(HEADS_PER_GROUP - 1) * HEAD:]
        for h in range(HEADS_PER_GROUP - 2, -1, -1):
            out = jnp.where(lane < (h + 1) * HEAD, full[h * HEAD:(h + 1) * HEAD], out)
        return out

    for key in combos:
        d, c, g = key
        rs = slice(c * CHUNK, (c + 1) * CHUNK)
        ls = slice(g * GROUP, (g + 1) * GROUP)
        w_, u0 = wp[key][:CHUNK, :GROUP], wp[key][:CHUNK, GROUP:]
        vg = cut(v, c, g)
        gt = g_tot[d][c][:, ls]
        bh, kh = cut(b_t[d], c, g) * gt, cut(k_t[d], c, g) * gt
        p_o[d, 0, rs, ls] = (cut(r_t[d], c, g) + wp[key][CHUNK:, :GROUP]).astype(BF16)
        m_o[d, 0, rs, ls] = fold(_dot_tn(w_, bh)).astype(BF16)
        n_o[d, 0, rs, ls] = fold(_dot_tn(jnp.concatenate([u0, vg], axis=0), jnp.concatenate([bh, kh], axis=0)))
        if d == 1:
            other = (0, c, g)
            q_o[0, rs, ls] = ((wp[other][CHUNK:, GROUP:] + kv[other][CHUNK:])
                              + (wp[key][CHUNK:, GROUP:] + kv[key][CHUNK:]))


def _feat_call(x, pos, mod, mod_row, p):
    bsz, t, _ = x.shape
    tm = TOKEN_TILE
    nt = t // tm
    nc = t // CHUNK
    hpt = tm // HALO
    has_pos = pos is not None

    def const(shape):
        return pl.BlockSpec(shape, lambda i, b: (0,) * len(shape))

    tok = lambda width: pl.BlockSpec((1, tm, width), lambda i, b: (b, i, 0))
    prev_i = lambda i: jnp.maximum(i * hpt - 1, 0)
    next_i = lambda i: jnp.minimum((i + 1) * hpt, t // HALO - 1)
    in_specs = [tok(D_MODEL),
                pl.BlockSpec((1, HALO, D_MODEL), lambda i, b: (b, prev_i(i), 0)),
                pl.BlockSpec((1, HALO, D_MODEL), lambda i, b: (b, next_i(i), 0))]
    args = [x, x, x]
    if has_pos:
        in_specs += [pl.BlockSpec((tm, D_MODEL), lambda i, b: (i, 0)),
                     pl.BlockSpec((HALO, D_MODEL), lambda i, b: (prev_i(i), 0)),
                     pl.BlockSpec((HALO, D_MODEL), lambda i, b: (next_i(i), 0))]
        args += [pos, pos, pos]
    n_masks = 2 * _M_PER_DIR + 1
    in_specs += [
        pl.BlockSpec((1, 6, D_MODEL), lambda i, b: (mod_row(b), 0, 0)),
        const((1, D_MODEL)), const((D_MODEL, D_IN)),
        const((2 * R_W, 2 * D_A)), const((1, 2 * D_A)),
        const((2 * R_A, 2 * D_A)), const((1, 2 * D_A)),
        const((R_G, D_A)),
        const((1, D_A)), const((1, D_A)), const((1, D_A)), const((D_A, D_A)),
        const((CONV_W, D_B)), const((1, D_B)),
        const((2, N_GROUP, GROUP, GROUP)), const((2, D_B)),
        const((2, N_GROUP, GROUP, GROUP)), const((2, D_B)), const((2, D_B)),
        const((n_masks, CHUNK, GROUP)), const((GROUP, GROUP)),
    ]
    args += [mod, p["g_pre_mix"], p["w_in"], p["wup_bd"], p["w0"], p["aup_bd"], p["a0"], p["g_up"],
             p["k_k"], p["k_a"], p["r_k"], p["seg_ones"],
             p["conv_w"], p["conv_b"], p["wa_bd"], p["ba"], p["wx_bd"], p["bx"], p["lam"],
             p["chunk_masks"], p["bdm_bf16"]]
    tok_shape = jax.ShapeDtypeStruct((bsz, t, D_A), F32)
    dir_shape = jax.ShapeDtypeStruct((2, bsz, t, D_A), F32)
    row_shape = jax.ShapeDtypeStruct((2, bsz, nc, 1, D_A), F32)
    mxu_shape = jax.ShapeDtypeStruct((2, bsz, t, D_A), BF16)
    dir_spec = pl.BlockSpec((2, 1, tm, D_A), lambda i, b: (0, b, i, 0))
    row_spec = pl.BlockSpec((2, 1, CPT, 1, D_A), lambda i, b: (0, b, i, 0, 0))
    out_shape = [mxu_shape, tok_shape, mxu_shape, dir_shape, row_shape, dir_shape, tok_shape, row_shape,
                 tok_shape, tok_shape, tok_shape]
    out_specs = [dir_spec, tok(D_A), dir_spec, dir_spec, row_spec, dir_spec, tok(D_B), row_spec,
                 tok(D_A), tok(D_A), tok(D_B)]
    return pl.pallas_call(
        functools.partial(_feat_kernel, has_pos, nt),
        grid=(nt, bsz),
        in_specs=in_specs,
        out_specs=out_specs,
        out_shape=out_shape,
        scratch_shapes=[pltpu.VMEM((2, N_SLAB, tm, LANE), F32)] * 4 + [pltpu.VMEM((3, N_SLAB, tm, LANE), F32)],
        compiler_params=pltpu.CompilerParams(dimension_semantics=("parallel", "parallel"),
                                             vmem_limit_bytes=VMEM_LIMIT),
        name="feat",
    )(*args)


def _scan_kernel(nt, *refs):
    (pf_ref, mf_ref, nf_ref, gtf_ref, acf_ref, blf_ref,
     pb_ref, mb_ref, nb_ref, gtb_ref, acb_ref, blb_ref,
     s0_ref, l0_ref, bdm_ref,
     y_ref, hs_ref, s_ref, hl_ref) = refs

    i = pl.program_id(1)

    @pl.when(i == 0)
    def _init():
        y_ref[...] = jnp.zeros_like(y_ref)
        hs_ref[...] = jnp.zeros_like(hs_ref)
        s_ref[...] = s0_ref[...]
        hl_ref[...] = l0_ref[...]

    bdm = bdm_ref[...]
    per_dir = ((0, pf_ref, mf_ref, nf_ref, gtf_ref, acf_ref, blf_ref),
               (1, pb_ref, mb_ref, nb_ref, gtb_ref, acb_ref, blb_ref))
    tiles = (i, nt - 1 - i)
    state = [s_ref[0, d] for d in (0, 1)]
    h0 = [hl_ref[0, d:d + 1, :] for d in (0, 1)]
    for cc in range(CPT):
        for d, p_ref, m_ref, n_ref, gt_ref, ac_ref, bl_ref in per_dir:
            c = cc if d == 0 else CPT - 1 - cc
            rs = slice(c * CHUNK, (c + 1) * CHUNK)
            rows = pl.ds(pl.multiple_of(tiles[d] * TOKEN_TILE + c * CHUNK, CHUNK), CHUNK)
            gt = gt_ref[0, 0, c]
            y_parts, s_parts = [], []
            for g in range(N_GROUP):
                ls = slice(g * GROUP, (g + 1) * GROUP)
                sg = state[d][:, ls]
                y_parts.append(_dot_nt(p_ref[0, 0, rs, ls], _blockdiag(sg, bdm)))
                s_parts.append(sg * gt[:, ls]
                               + jnp.dot(sg.astype(BF16), _blockdiag(m_ref[0, 0, rs, ls], bdm),
                                         preferred_element_type=F32)
                               + n_ref[0, 0, rs, ls])
            state[d] = jnp.concatenate(s_parts, axis=1)
            y_c = jnp.concatenate(y_parts, axis=1)
            acum = ac_ref[0, 0, rs, :]
            y_ref[0, rows, :] += y_c
            hs_ref[0, rows, :] += acum * h0[d]
            last = CHUNK - 1 if d == 0 else 0
            h0[d] = bl_ref[0, 0, c] + acum[last:last + 1, :] * h0[d]
    for d in (0, 1):
        s_ref[0, d] = state[d]
        hl_ref[0, d:d + 1, :] = h0[d]


def _scan_call(feats, s0, l0, p):
    pm, mm, nm, gt, ac, bl = feats
    _, bsz, t, _ = pm.shape
    nt = t // TOKEN_TILE

    def dir_specs(d, tile_of):
        big = pl.BlockSpec((1, 1, TOKEN_TILE, D_A), lambda b, i: (d, b, tile_of(i), 0))
        small = pl.BlockSpec((1, 1, CPT, 1, D_A), lambda b, i: (d, b, tile_of(i), 0, 0))
        return big, small

    big_f, small_f = dir_specs(0, lambda i: i)
    big_b, small_b = dir_specs(1, lambda i: nt - 1 - i)
    in_specs = [big_f, big_f, big_f, small_f, big_f, small_f,
                big_b, big_b, big_b, small_b, big_b, small_b,
                pl.BlockSpec((1, 2, HEAD, D_A), lambda b, i: (b, 0, 0, 0)),
                pl.BlockSpec((1, 2, D_B), lambda b, i: (b, 0, 0)),
                pl.BlockSpec((GROUP, GROUP), lambda b, i: (0, 0))]
    args = [pm, mm, nm, gt, ac, bl, pm, mm, nm, gt, ac, bl, s0, l0, p["bdm_bf16"]]
    seq_spec = pl.BlockSpec((1, t, D_A), lambda b, i: (b, 0, 0))
    out_specs = [seq_spec, seq_spec,
                 pl.BlockSpec((1, 2, HEAD, D_A), lambda b, i: (b, 0, 0, 0)),
                 pl.BlockSpec((1, 2, D_B), lambda b, i: (b, 0, 0))]
    out_shape = [jax.ShapeDtypeStruct((bsz, t, D_A), F32), jax.ShapeDtypeStruct((bsz, t, D_B), F32),
                 jax.ShapeDtypeStruct((bsz, 2, HEAD, D_A), F32),
                 jax.ShapeDtypeStruct((bsz, 2, D_B), F32)]
    return pl.pallas_call(
        functools.partial(_scan_kernel, nt),
        grid=(bsz, nt),
        in_specs=in_specs,
        out_specs=out_specs,
        out_shape=out_shape,
        compiler_params=pltpu.CompilerParams(dimension_semantics=("parallel", "arbitrary"),
                                             vmem_limit_bytes=VMEM_LIMIT),
        name="scan",
    )(*args)


def _out_kernel(has_pos, *refs):
    if has_pos:
        x_ref, pos_ref = refs[0], refs[1]
        refs = refs[2:]
    else:
        x_ref, pos_ref = refs[0], None
        refs = refs[1:]
    (y_ref, q_ref, hs_ref, bs_ref, g_ref, bonus_ref, gate_ref, mod_ref,
     avg_ref, lnxg_ref, lnxb_ref, wout_ref, gpost_ref, gpre2_ref, w1_ref, w2_ref, gpost2_ref,
     o_ref) = refs

    x = x_ref[0]
    if has_pos:
        x = x + pos_ref[...]
    mod = mod_ref[0]
    gate1, shift2, scale2, gate2 = mod[2:3], mod[3:4], mod[4:5], mod[5:6]

    y = y_ref[0] + q_ref[0]
    avg = avg_ref[...]

    def head_mean(a):
        return jnp.concatenate([_dot(a[:, t * MXU_TILE:(t + 1) * MXU_TILE], avg)
                                for t in range(D_A // MXU_TILE)], axis=1)

    y_hi = y.astype(BF16)
    y_lo = (y - y_hi.astype(F32)).astype(BF16)
    mu = head_mean(y_hi) + head_mean(y_lo)
    yc = y - mu
    var = head_mean(yc * yc)
    yn = yc * lax.rsqrt(var + LNX_EPS) * lnxg_ref[...] + lnxb_ref[...]
    out_a = (yn + bonus_ref[0]) * g_ref[0]
    out_b = (hs_ref[0] + bs_ref[0]) * gate_ref[0]
    mix = _dot(jnp.concatenate([out_a, out_b], axis=1), wout_ref[...])
    ms = jnp.mean(mix * mix, axis=-1, keepdims=True)
    x = x + gate1 * ((mix * lax.rsqrt(ms + EPS)) * gpost_ref[...])

    ms = jnp.mean(x * x, axis=-1, keepdims=True)
    h = (x * lax.rsqrt(ms + EPS)) * gpre2_ref[...]
    h = h * (1.0 + scale2) + shift2
    f = _dot(h, w1_ref[...])
    f = jnp.square(jnp.maximum(f, 0.0))
    f = _dot(f, w2_ref[...])
    ms = jnp.mean(f * f, axis=-1, keepdims=True)
    o_ref[0] = x + gate2 * ((f * lax.rsqrt(ms + EPS)) * gpost2_ref[...])


def _out_call(x, pos, mod, mod_row, y, q, hs, bs, g, bonus, gate, p):
    bsz, t, _ = x.shape
    tm = min(OUT_TILE, t)
    nt = t // tm
    has_pos = pos is not None

    def const(shape):
        return pl.BlockSpec(shape, lambda i, b: (0,) * len(shape))

    tok = lambda width: pl.BlockSpec((1, tm, width), lambda i, b: (b, i, 0))
    in_specs = [tok(D_MODEL)]
    args = [x]
    if has_pos:
        in_specs.append(pl.BlockSpec((tm, D_MODEL), lambda i, b: (i, 0)))
        args.append(pos)
    in_specs += [tok(D_A)] * 7
    in_specs += [
        pl.BlockSpec((1, 6, D_MODEL), lambda i, b: (mod_row(b), 0, 0)),
        const((MXU_TILE, MXU_TILE)), const((1, D_A)), const((1, D_A)),
        const((D_MODEL, D_MODEL)), const((1, D_MODEL)), const((1, D_MODEL)),
        const((D_MODEL, D_FF)), const((D_FF, D_MODEL)), const((1, D_MODEL)),
    ]
    args += [y, q, hs, bs, g, bonus, gate, mod,
             p["seg_avg"], p["lnx_g"], p["lnx_b"], p["w_out"], p["g_post_mix"], p["g_pre_mlp"],
             p["w_mlp1"], p["w_mlp2"], p["g_post_mlp"]]
    return pl.pallas_call(
        functools.partial(_out_kernel, has_pos),
        grid=(nt, bsz),
        in_specs=in_specs,
        out_specs=tok(D_MODEL),
        out_shape=jax.ShapeDtypeStruct((bsz, t, D_MODEL), F32),
        compiler_params=pltpu.CompilerParams(dimension_semantics=("parallel", "parallel"),
                                             vmem_limit_bytes=VMEM_LIMIT),
        name="out",
    )(*args)


def _sincos_1d(pos, dim):
    omega = 1.0 / (10000.0 ** (jnp.arange(dim // 2, dtype=F32) / (dim // 2)))
    ang = pos.astype(F32)[:, None] * omega[None, :]
    return jnp.concatenate([jnp.sin(ang), jnp.cos(ang)], axis=-1)


def _grid_pos_embed(n_tokens):
    rows = n_tokens // GRID_W
    half = D_MODEL // 2
    e_row = _sincos_1d(jnp.arange(rows), half)
    e_col = _sincos_1d(jnp.arange(GRID_W), half)
    emb = jnp.concatenate([jnp.broadcast_to(e_row[:, None, :], (rows, GRID_W, half)),
                           jnp.broadcast_to(e_col[None, :, :], (rows, GRID_W, half))], axis=-1)
    return emb.reshape(rows * GRID_W, D_MODEL)


def _blockdiag_pairs(w):
    z = jnp.zeros_like(w[0])
    return jnp.concatenate([jnp.concatenate([w[0], z], axis=1),
                            jnp.concatenate([z, w[1]], axis=1)], axis=0)


def _heads_to_blockdiag(w):
    lead = w.shape[:-3]
    w = w.reshape(lead + (N_GROUP, HEADS_PER_GROUP, HEAD, HEAD))
    eye = jnp.eye(HEADS_PER_GROUP, dtype=w.dtype)
    bd = jnp.einsum('...ghab,hj->...ghajb', w, eye)
    return bd.reshape(lead + (N_GROUP, GROUP, GROUP))


def _state_to_lanes(s):
    b = s.shape[0]
    return jnp.transpose(s, (0, 1, 3, 2, 4)).reshape(b, 2, HEAD, D_A)


def _state_from_lanes(s):
    b = s.shape[0]
    return jnp.transpose(s.reshape(b, 2, HEAD, N_HEAD, HEAD), (0, 1, 3, 2, 4))


def kernel(x_prompt, x_sample, c, state_rwkv, state_lru, c_ctx, w_mod, b_mod, g_pre_mix, g_post_mix,
           g_pre_mlp, g_post_mlp, w_in, rwkv_w0, rwkv_w_up, rwkv_a0, rwkv_a_up, rwkv_g_up, rwkv_k_k,
           rwkv_k_a, rwkv_r_k, rwkv_lnx_g, rwkv_lnx_b, lru_conv_w, lru_conv_b, lru_wa, lru_ba, lru_wx,
           lru_bx, lru_lambda, w_out, w_mlp1, w_mlp2):
    n_ctx = x_prompt.shape[0]
    n_lat = x_sample.shape[0]
    l = 0
    seg = _group_blockdiag_mask()
    seg512 = np.kron(np.eye(N_GROUP, dtype=np.float32), seg)
    p = {
        "g_pre_mix": g_pre_mix[l][None], "g_post_mix": g_post_mix[l][None],
        "g_pre_mlp": g_pre_mlp[l][None], "g_post_mlp": g_post_mlp[l][None],
        "w_in": w_in[l].astype(BF16), "w_out": w_out[l].astype(BF16),
        "w_mlp1": w_mlp1[l].astype(BF16), "w_mlp2": w_mlp2[l].astype(BF16),
        "w0": rwkv_w0[l].reshape(1, 2 * D_A), "a0": rwkv_a0[l].reshape(1, 2 * D_A),
        "wup_bd": _blockdiag_pairs(rwkv_w_up[l]).astype(BF16),
        "aup_bd": _blockdiag_pairs(rwkv_a_up[l]).astype(BF16),
        "g_up": rwkv_g_up[l].astype(BF16),
        "k_k": rwkv_k_k[l][None], "k_a": rwkv_k_a[l][None], "r_k": rwkv_r_k[l].reshape(1, D_A),
        "lnx_g": rwkv_lnx_g[l][None], "lnx_b": rwkv_lnx_b[l][None],
        "conv_w": lru_conv_w[l], "conv_b": lru_conv_b[l][None],
        "wa_bd": _heads_to_blockdiag(lru_wa[l]).astype(BF16), "ba": lru_ba[l],
        "wx_bd": _heads_to_blockdiag(lru_wx[l]).astype(BF16), "bx": lru_bx[l],
        "lam": lru_lambda[l],
        "seg_ones": jnp.asarray(seg512, BF16),
        "seg_avg": jnp.asarray(np.kron(np.eye(MXU_TILE // GROUP, dtype=np.float32), seg) / HEAD, BF16),
        "chunk_masks": jnp.asarray(_chunk_masks()),
        "bdm_bf16": jnp.asarray(seg, BF16),
    }

    m_rows = 16
    c_all = jnp.concatenate([c_ctx[None], c, jnp.zeros((m_rows - 1 - n_lat, D_MODEL), F32)], axis=0)
    mod = _mod_call(c_all, w_mod[l], b_mod[l]).reshape(m_rows, 6, D_MODEL)

    pos = _grid_pos_embed(x_sample.shape[1]).astype(x_sample.dtype)
    ctx_row = lambda b: 0
    lat_row = lambda b: b + 1

    pm, q, mm, nm, gt, ac, bs, bl, g, bonus, gate = _feat_call(x_prompt, None, mod, ctx_row, p)
    y, hs, s_ctx, l_ctx = _scan_call(
        (pm, mm, nm, gt, ac, bl), jnp.zeros((n_ctx, 2, HEAD, D_A), F32), jnp.zeros((n_ctx, 2, D_B), F32), p)
    y_prompt = _out_call(x_prompt, None, mod, ctx_row, y, q, hs, bs, g, bonus, gate, p)

    pm, q, mm, nm, gt, ac, bs, bl, g, bonus, gate = _feat_call(x_sample, pos, mod, lat_row, p)
    y, hs, _, _ = _scan_call((pm, mm, nm, gt, ac, bl), _state_to_lanes(state_rwkv[:, l]), state_lru[:, l], p)
    y_sample = _out_call(x_sample, pos, mod, lat_row, y, q, hs, bs, g, bonus, gate, p)

    new_state_rwkv = _state_from_lanes(s_ctx)[:, None].astype(x_prompt.dtype)
    new_state_lru = l_ctx[:, None].astype(x_prompt.dtype)
    return (y_prompt, y_sample, new_state_rwkv, new_state_lru)
```

```python
import functools

import numpy as np
import jax
import jax.numpy as jnp
from jax import lax
from jax.experimental import pallas as pl
from jax.experimental.pallas import tpu as pltpu

F32 = jnp.float32
BF16 = jnp.bfloat16

D_MODEL = 1024
D_A = 512
D_B = 512
HEAD = 64
N_HEAD = 8
R_W = 64
R_A = 64
R_G = 128
D_FF = 4096
D_IN = 2944
GRID_W = 64
CONV_W = 4
LRU_C = 8.0
EPS = 1e-6
LNX_EPS = 64e-5

CHUNK = 64
GROUP = 128
HEADS_PER_GROUP = GROUP // HEAD
N_GROUP = D_A // GROUP
SUB = 8
LANE = 128
N_SLAB = D_A // LANE
HALO = 8
TOKEN_TILE = 256
CPT = TOKEN_TILE // CHUNK
OUT_TILE = 512
MXU_TILE = 256
VMEM_LIMIT = 60 * 1024 * 1024

_O_R, _O_K, _O_V, _O_XW, _O_XA, _O_XG, _O_XB, _O_GB = 0, 512, 1024, 1536, 1664, 1792, 1920, 2432

_M_STRICT, _M_INCL, _M_LEV0 = 0, 1, 2
_LEVELS = (1, 2, 4, 8, 16, 32)
_M_PER_DIR = 2 + len(_LEVELS)
_M_EYE = 2 * _M_PER_DIR


def _dot(a, b):
    return jnp.dot(a.astype(BF16), b.astype(BF16), preferred_element_type=F32)


def _dot_nt(a, b):
    return lax.dot_general(a.astype(BF16), b.astype(BF16), (((1,), (1,)), ((), ())),
                           preferred_element_type=F32)


def _dot_tn(a, b):
    return lax.dot_general(a.astype(BF16), b.astype(BF16), (((0,), (0,)), ((), ())),
                           preferred_element_type=F32)


def _strided_rows(ref, lead, base):
    return [ref[lead + (pl.ds(base + j, SUB, stride=SUB), slice(None))] for j in range(SUB)]


def _natural_rows(ref, lead, base, pieces):
    for j, piece in enumerate(pieces):
        ref[lead + (pl.ds(base + SUB * j, SUB), slice(None))] = piece
    return jnp.concatenate(_strided_rows(ref, lead, base), axis=0)


def _sublane_shift(x, steps, reverse, fill):
    sub = lax.broadcasted_iota(jnp.int32, x.shape, 0)
    if reverse:
        return jnp.where(sub < SUB - steps, pltpu.roll(x, SUB - steps, 0), fill)
    return jnp.where(sub >= steps, pltpu.roll(x, steps, 0), fill)


def _scan_affine(a, b, reverse):
    a, b = list(a), list(b)
    order = range(SUB - 2, -1, -1) if reverse else range(1, SUB)
    for j in order:
        p = j + 1 if reverse else j - 1
        b[j] = a[j] * b[p] + b[j]
        a[j] = a[j] * a[p]
    ta, tb = (a[0], b[0]) if reverse else (a[SUB - 1], b[SUB - 1])
    s = 1
    while s < SUB:
        tb = ta * _sublane_shift(tb, s, reverse, 0.0) + tb
        ta = ta * _sublane_shift(ta, s, reverse, 1.0)
        s *= 2
    ea = _sublane_shift(ta, 1, reverse, 1.0)
    eb = _sublane_shift(tb, 1, reverse, 0.0)
    return [x * ea for x in a], [x * eb + y for x, y in zip(a, b)]


def _scan_sum(x, reverse):
    x = list(x)
    order = range(SUB - 2, -1, -1) if reverse else range(1, SUB)
    for j in order:
        x[j] = x[j] + x[j + 1 if reverse else j - 1]
    t = x[0] if reverse else x[SUB - 1]
    s = 1
    while s < SUB:
        t = t + _sublane_shift(t, s, reverse, 0.0)
        s *= 2
    e = _sublane_shift(t, 1, reverse, 0.0)
    return [y + e for y in x]


def _sigmoid(x):
    return 0.5 * jnp.tanh(0.5 * x) + 0.5


def _blockdiag(x, bdm):
    xb = x.astype(BF16)
    return jnp.concatenate([xb] * HEADS_PER_GROUP, axis=0) * bdm


def _mod_kernel(c_ref, w_ref, b_ref, o_ref):
    c = c_ref[...]
    s = c * _sigmoid(c)
    o_ref[...] = _dot(s, w_ref[...]) + b_ref[...]


def _mod_call(c_all, w_mod, b_mod):
    m = c_all.shape[0]
    n = w_mod.shape[1]
    tn = 1536
    return pl.pallas_call(
        _mod_kernel,
        grid=(n // tn,),
        in_specs=[pl.BlockSpec((m, D_MODEL), lambda j: (0, 0)),
                  pl.BlockSpec((D_MODEL, tn), lambda j: (0, j)),
                  pl.BlockSpec((1, tn), lambda j: (0, j))],
        out_specs=pl.BlockSpec((m, tn), lambda j: (0, j)),
        out_shape=jax.ShapeDtypeStruct((m, n), F32),
        compiler_params=pltpu.CompilerParams(dimension_semantics=("parallel",),
                                             vmem_limit_bytes=VMEM_LIMIT),
        name="mod",
    )(c_all, w_mod, b_mod.reshape(1, n))


def _chunk_masks():
    t = np.arange(CHUNK)[:, None]
    s = (np.arange(GROUP) % CHUNK)[None, :]
    rows = []
    for d in (0, 1):
        before = (s < t) if d == 0 else (s > t)
        rows.append(before)
        rows.append(before | (s == t))
        for b in _LEVELS:
            same = (t // (2 * b)) == (s // (2 * b))
            if d == 0:
                rows.append(same & ((t // b) % 2 == 1) & ((s // b) % 2 == 0))
            else:
                rows.append(same & ((t // b) % 2 == 0) & ((s // b) % 2 == 1))
    rows.append(s == t)
    return np.stack(rows).astype(np.float32)


def _group_blockdiag_mask():
    i = np.arange(GROUP)
    return ((i[:, None] // HEAD) == (i[None, :] // HEAD)).astype(np.float32)


def _feat_kernel(has_pos, nt, *refs):
    if has_pos:
        x_ref, xp_ref, xn_ref, pos_ref, pp_ref, pn_ref = refs[:6]
        refs = refs[6:]
    else:
        x_ref, xp_ref, xn_ref = refs[:3]
        pos_ref = pp_ref = pn_ref = None
        refs = refs[3:]
    (mod_ref, gpre_ref, win_ref, wup_ref, w0_ref, aup_ref, a0_ref, gup_ref,
     kk_ref, ka_ref, rk_ref, seg_ref,
     convw_ref, convb_ref, wa_ref, ba_ref, wx_ref, bx_ref, lam_ref,
     masks_ref, bdm_ref,
     p_o, q_o, m_o, n_o, gt_o, ac_o, bs_o, bl_o, al_o, g_o, bonus_o, gate_o,
     lw_s, lc_s, la_s, lb_s, lo_s) = refs

    i = pl.program_id(0)
    tm = TOKEN_TILE
    mod = mod_ref[0]
    shift1, scale1 = mod[0:1], mod[1:2]

    def normmod(xv):
        ms = jnp.mean(xv * xv, axis=-1, keepdims=True)
        hv = (xv * lax.rsqrt(ms + EPS)) * gpre_ref[...]
        return hv * (1.0 + scale1) + shift1

    x = x_ref[0]
    halo = jnp.concatenate([xp_ref[0], xn_ref[0]], axis=0)
    if has_pos:
        x = x + pos_ref[...]
        halo = halo + jnp.concatenate([pp_ref[...], pn_ref[...]], axis=0)
    z = _dot(normmod(x), win_ref[...])
    zh = _dot(normmod(halo), win_ref[:, _O_XB:_O_XB + D_B])

    r = z[:, _O_R:_O_R + D_A]
    k = z[:, _O_K:_O_K + D_A]
    v = z[:, _O_V:_O_V + D_A]
    xw = z[:, _O_XW:_O_XW + 2 * R_W]
    xa = z[:, _O_XA:_O_XA + 2 * R_A]
    xg = z[:, _O_XG:_O_XG + R_G]

    g_o[0] = _dot(_sigmoid(xg), gup_ref[...])
    gate_o[0] = jax.nn.gelu(z[:, _O_GB:_O_GB + D_B], approximate=True)
    wl = w0_ref[...] + _dot(jnp.tanh(xw), wup_ref[...])
    lw2 = -_sigmoid(wl) * float(np.exp(-0.5))
    a2 = _sigmoid(a0_ref[...] + _dot(xa, aup_ref[...]))

    kks = k * kk_ref[...]
    ss = _dot(kks * kks, seg_ref[...])
    kk = kks * lax.rsqrt(jnp.maximum(ss, 1e-24))
    ka = ka_ref[...]
    kd2 = [k * (1.0 + (a2[:, d * D_A:(d + 1) * D_A] - 1.0) * ka) for d in (0, 1)]
    bonus_o[0] = _dot(r * (kd2[0] + kd2[1]) * rk_ref[...], seg_ref[...]) * v

    m_prev = jnp.where(i > 0, 1.0, 0.0)
    m_next = jnp.where(i < nt - 1, 1.0, 0.0)
    ext = jnp.concatenate([zh[:HALO] * m_prev, z[:, _O_XB:_O_XB + D_B], zh[HALO:] * m_next], axis=0)
    n_ext = tm + 2 * HALO
    xc = convb_ref[...]
    for j in range(CONV_W):
        sh = (2 - j) % n_ext
        tap = ext if sh == 0 else pltpu.roll(ext, sh, 0)
        xc = xc + tap[HALO:HALO + tm] * convw_ref[j:j + 1, :]
    for d in (0, 1):
        rg = jnp.concatenate([_dot(xc[:, g * GROUP:(g + 1) * GROUP], wa_ref[d, g]) for g in range(N_GROUP)], 1)
        ig = jnp.concatenate([_dot(xc[:, g * GROUP:(g + 1) * GROUP], wx_ref[d, g]) for g in range(N_GROUP)], 1)
        rg = _sigmoid(rg + ba_ref[d:d + 1, :])
        ig = _sigmoid(ig + bx_ref[d:d + 1, :])
        log_a = -LRU_C * rg * jax.nn.softplus(-lam_ref[d:d + 1, :])
        a_lru = jnp.exp(log_a)
        b_lru = jnp.sqrt(-jnp.tanh(log_a) * (a_lru * a_lru + 1.0)) * (ig * xc)
        for q in range(N_SLAB):
            la_s[d, q] = a_lru[:, q * LANE:(q + 1) * LANE]
            lb_s[d, q] = b_lru[:, q * LANE:(q + 1) * LANE]
    for c in range(CPT):
        base = c * CHUNK
        rs = slice(base, base + CHUNK)
        for q in range(N_SLAB):
            ls = slice(q * LANE, (q + 1) * LANE)
            bsum = None
            for d in (0, 1):
                acum, bcum = _scan_affine(_strided_rows(la_s, (d, q), base), _strided_rows(lb_s, (d, q), base),
                                          d == 1)
                ac_o[d, 0, rs, ls] = _natural_rows(lo_s, (d, q), base, acum)
                bsum = bcum if bsum is None else [x + y for x, y in zip(bsum, bcum)]
                bl_o[d, 0, c, :, ls] = bcum[0][0:1, :] if d == 1 else bcum[SUB - 1][SUB - 1:SUB, :]
                al_o[d, 0, c, :, ls] = acum[0][0:1, :] if d == 1 else acum[SUB - 1][SUB - 1:SUB, :]
            bs_o[0, rs, ls] = _natural_rows(lo_s, (2, q), base, bsum)

    bdm = bdm_ref[...]
    bd = lambda xv: _blockdiag(xv, bdm)
    bdot = lambda lhs, rhs: jnp.dot(lhs.astype(BF16), bd(rhs), preferred_element_type=F32)

    a_t, r_t, b_t, k_t, g_tot = [], [], [], [], []
    for d in (0, 1):
        lw = lw2[:, d * D_A:(d + 1) * D_A]
        for q in range(N_SLAB):
            lw_s[d, q] = lw[:, q * LANE:(q + 1) * LANE]
        g_tot.append([])
        lc_rows = []
        for c in range(CPT):
            base = c * CHUNK
            blocks, totals = [], []
            for q in range(N_SLAB):
                pieces = _scan_sum(_strided_rows(lw_s, (d, q), base), d == 1)
                totals.append(pieces[0][0:1, :] if d == 1 else pieces[SUB - 1][SUB - 1:SUB, :])
                blocks.append(_natural_rows(lc_s, (d, q), base, pieces))
            lc_rows.append(jnp.concatenate(blocks, axis=1))
            g_tot[d].append(jnp.exp(jnp.concatenate(totals, axis=1)))
            gt_o[d, 0, c] = g_tot[d][c]
        lc = jnp.concatenate(lc_rows, axis=0)
        e_neg = jnp.exp(-lc)
        a_t.append(-kk * jnp.exp(lc - lw))
        r_t.append(r * jnp.exp(lc))
        b_t.append(kk * a2[:, d * D_A:(d + 1) * D_A] * e_neg)
        k_t.append(kd2[d] * e_neg)

    combos = [(d, c, g) for d in (0, 1) for c in range(CPT) for g in range(N_GROUP)]

    def cut(arr, c, g):
        return arr[c * CHUNK:(c + 1) * CHUNK, g * GROUP:(g + 1) * GROUP]

    sc = {}
    for key in combos:
        d, c, g = key
        ar = jnp.concatenate([cut(a_t[d], c, g), cut(r_t[d], c, g)], axis=0)
        rhs = jnp.concatenate([bd(cut(b_t[d], c, g)), bd(cut(k_t[d], c, g))], axis=0)
        sc[key] = _dot_nt(ar, rhs)
    n_ab, n_ak, n_rb, n_rk, tinv = {}, {}, {}, {}, {}
    for key in combos:
        m0 = key[0] * _M_PER_DIR
        n_ab[key] = sc[key][:CHUNK, :GROUP] * masks_ref[m0 + _M_STRICT]
        n_ak[key] = sc[key][:CHUNK, GROUP:] * masks_ref[m0 + _M_STRICT]
        n_rb[key] = sc[key][CHUNK:, :GROUP] * masks_ref[m0 + _M_INCL]
        n_rk[key] = sc[key][CHUNK:, GROUP:] * masks_ref[m0 + _M_INCL]
        tinv[key] = masks_ref[_M_EYE] + n_ab[key] * masks_ref[m0 + _M_LEV0]
    for li in range(1, len(_LEVELS)):
        pm = {}
        for key in combos:
            pm[key] = bdot(n_ab[key] * masks_ref[key[0] * _M_PER_DIR + _M_LEV0 + li], tinv[key])
        for key in combos:
            tinv[key] = tinv[key] + bdot(tinv[key], pm[key])

    kv = {key: bdot(jnp.concatenate([n_ak[key], n_rk[key]], axis=0), cut(v, key[1], key[2])) for key in combos}
    gm = {key: bdot(n_rb[key], tinv[key]) for key in combos}
    wp = {}
    for key in combos:
        d, c, g = key
        lhs = jnp.concatenate([tinv[key], gm[key]], axis=0)
        rhs = jnp.concatenate([bd(cut(a_t[d], c, g)), bd(kv[key][:CHUNK])], axis=1)
        wp[key] = jnp.dot(lhs.astype(BF16), rhs, preferred_element_type=F32)

    lane = lax.broadcasted_iota(jnp.int32, (CHUNK, GROUP), 1)

    def fold(full):
        out = full[(HEADS_PER_GROUP - 1) * HEAD:]
        for h in range(HEADS_PER_GROUP - 2, -1, -1):
            out = jnp.where(lane < (h + 1) * HEAD, full[h * HEAD:(h + 1) * HEAD], out)
        return out

    for key in combos:
        d, c, g = key
        rs = slice(c * CHUNK, (c + 1) * CHUNK)
        ls = slice(g * GROUP, (g + 1) * GROUP)
        w_, u0 = wp[key][:CHUNK, :GROUP], wp[key][:CHUNK, GROUP:]
        vg = cut(v, c, g)
        gt = g_tot[d][c][:, ls]
        bh, kh = cut(b_t[d], c, g) * gt, cut(k_t[d], c, g) * gt
        p_o[d, 0, rs, ls] = (cut(r_t[d], c, g) + wp[key][CHUNK:, :GROUP]).astype(BF16)
        m_o[d, 0, rs, ls] = fold(_dot_tn(w_, bh)).astype(BF16)
        n_o[d, 0, rs, ls] = fold(_dot_tn(jnp.concatenate([u0, vg], axis=0), jnp.concatenate([bh, kh], axis=0)))
        if d == 1:
            other = (0, c, g)
            q_o[0, rs, ls] = ((wp[other][CHUNK:, GROUP:] + kv[other][CHUNK:])
                              + (wp[key][CHUNK:, GROUP:] + kv[key][CHUNK:]))


def _feat_call(x, pos, mod, mod_row, p):
    bsz, t, _ = x.shape
    tm = TOKEN_TILE
    nt = t // tm
    nc = t // CHUNK
    hpt = tm // HALO
    has_pos = pos is not None

    def const(shape):
        return pl.BlockSpec(shape, lambda i, b: (0,) * len(shape))

    tok = lambda width: pl.BlockSpec((1, tm, width), lambda i, b: (b, i, 0))
    prev_i = lambda i: jnp.maximum(i * hpt - 1, 0)
    next_i = lambda i: jnp.minimum((i + 1) * hpt, t // HALO - 1)
    in_specs = [tok(D_MODEL),
                pl.BlockSpec((1, HALO, D_MODEL), lambda i, b: (b, prev_i(i), 0)),
                pl.BlockSpec((1, HALO, D_MODEL), lambda i, b: (b, next_i(i), 0))]
    args = [x, x, x]
    if has_pos:
        in_specs += [pl.BlockSpec((tm, D_MODEL), lambda i, b: (i, 0)),
                     pl.BlockSpec((HALO, D_MODEL), lambda i, b: (prev_i(i), 0)),
                     pl.BlockSpec((HALO, D_MODEL), lambda i, b: (next_i(i), 0))]
        args += [pos, pos, pos]
    n_masks = 2 * _M_PER_DIR + 1
    in_specs += [
        pl.BlockSpec((1, 6, D_MODEL), lambda i, b: (mod_row(b), 0, 0)),
        const((1, D_MODEL)), const((D_MODEL, D_IN)),
        const((2 * R_W, 2 * D_A)), const((1, 2 * D_A)),
        const((2 * R_A, 2 * D_A)), const((1, 2 * D_A)),
        const((R_G, D_A)),
        const((1, D_A)), const((1, D_A)), const((1, D_A)), const((D_A, D_A)),
        const((CONV_W, D_B)), const((1, D_B)),
        const((2, N_GROUP, GROUP, GROUP)), const((2, D_B)),
        const((2, N_GROUP, GROUP, GROUP)), const((2, D_B)), const((2, D_B)),
        const((n_masks, CHUNK, GROUP)), const((GROUP, GROUP)),
    ]
    args += [mod, p["g_pre_mix"], p["w_in"], p["wup_bd"], p["w0"], p["aup_bd"], p["a0"], p["g_up"],
             p["k_k"], p["k_a"], p["r_k"], p["seg_ones"],
             p["conv_w"], p["conv_b"], p["wa_bd"], p["ba"], p["wx_bd"], p["bx"], p["lam"],
             p["chunk_masks"], p["bdm_bf16"]]
    tok_shape = jax.ShapeDtypeStruct((bsz, t, D_A), F32)
    dir_shape = jax.ShapeDtypeStruct((2, bsz, t, D_A), F32)
    row_shape = jax.ShapeDtypeStruct((2, bsz, nc, 1, D_A), F32)
    mxu_shape = jax.ShapeDtypeStruct((2, bsz, t, D_A), BF16)
    dir_spec = pl.BlockSpec((2, 1, tm, D_A), lambda i, b: (0, b, i, 0))
    row_spec = pl.BlockSpec((2, 1, CPT, 1, D_A), lambda i, b: (0, b, i, 0, 0))
    out_shape = [mxu_shape, tok_shape, mxu_shape, dir_shape, row_shape, dir_shape, tok_shape, row_shape, row_shape,
                 tok_shape, tok_shape, tok_shape]
    out_specs = [dir_spec, tok(D_A), dir_spec, dir_spec, row_spec, dir_spec, tok(D_B), row_spec, row_spec,
                 tok(D_A), tok(D_A), tok(D_B)]
    return pl.pallas_call(
        functools.partial(_feat_kernel, has_pos, nt),
        grid=(nt, bsz),
        in_specs=in_specs,
        out_specs=out_specs,
        out_shape=out_shape,
        scratch_shapes=[pltpu.VMEM((2, N_SLAB, tm, LANE), F32)] * 4 + [pltpu.VMEM((3, N_SLAB, tm, LANE), F32)],
        compiler_params=pltpu.CompilerParams(dimension_semantics=("parallel", "parallel"),
                                             vmem_limit_bytes=VMEM_LIMIT),
        name="feat",
    )(*args)


def _scan_kernel(nt, *refs):
    (pf_ref, mf_ref, nf_ref, gtf_ref, alf_ref, blf_ref,
     pb_ref, mb_ref, nb_ref, gtb_ref, alb_ref, blb_ref,
     s0_ref, l0_ref, bdm_ref,
     y_ref, hc_ref, s_ref, hl_ref) = refs

    i = pl.program_id(1)

    @pl.when(i == 0)
    def _init():
        y_ref[...] = jnp.zeros_like(y_ref)
        s_ref[...] = s0_ref[...]
        hl_ref[...] = l0_ref[...]

    bdm = bdm_ref[...]
    per_dir = ((0, pf_ref, mf_ref, nf_ref, gtf_ref, alf_ref, blf_ref),
               (1, pb_ref, mb_ref, nb_ref, gtb_ref, alb_ref, blb_ref))
    tiles = (i, nt - 1 - i)
    state = [s_ref[0, d] for d in (0, 1)]
    h0 = [hl_ref[0, d:d + 1, :] for d in (0, 1)]
    for cc in range(CPT):
        for d, p_ref, m_ref, n_ref, gt_ref, al_ref, bl_ref in per_dir:
            c = cc if d == 0 else CPT - 1 - cc
            rs = slice(c * CHUNK, (c + 1) * CHUNK)
            rows = pl.ds(pl.multiple_of(tiles[d] * TOKEN_TILE + c * CHUNK, CHUNK), CHUNK)
            gt = gt_ref[0, 0, c]
            y_parts, s_parts = [], []
            for g in range(N_GROUP):
                ls = slice(g * GROUP, (g + 1) * GROUP)
                sg = state[d][:, ls]
                y_parts.append(_dot_nt(p_ref[0, 0, rs, ls], _blockdiag(sg, bdm)))
                s_parts.append(sg * gt[:, ls]
                               + jnp.dot(sg.astype(BF16), _blockdiag(m_ref[0, 0, rs, ls], bdm),
                                         preferred_element_type=F32)
                               + n_ref[0, 0, rs, ls])
            state[d] = jnp.concatenate(s_parts, axis=1)
            y_ref[0, rows, :] += jnp.concatenate(y_parts, axis=1)
            hc_ref[d, 0, tiles[d] * CPT + c] = h0[d]
            h0[d] = bl_ref[0, 0, c] + al_ref[0, 0, c] * h0[d]
    for d in (0, 1):
        s_ref[0, d] = state[d]
        hl_ref[0, d:d + 1, :] = h0[d]


def _scan_call(feats, s0, l0, p):
    pm, mm, nm, gt, al, bl = feats
    _, bsz, t, _ = pm.shape
    nt = t // TOKEN_TILE
    nc = t // CHUNK

    def dir_specs(d, tile_of):
        big = pl.BlockSpec((1, 1, TOKEN_TILE, D_A), lambda b, i: (d, b, tile_of(i), 0))
        small = pl.BlockSpec((1, 1, CPT, 1, D_A), lambda b, i: (d, b, tile_of(i), 0, 0))
        return big, small

    big_f, small_f = dir_specs(0, lambda i: i)
    big_b, small_b = dir_specs(1, lambda i: nt - 1 - i)
    in_specs = [big_f, big_f, big_f, small_f, small_f, small_f,
                big_b, big_b, big_b, small_b, small_b, small_b,
                pl.BlockSpec((1, 2, HEAD, D_A), lambda b, i: (b, 0, 0, 0)),
                pl.BlockSpec((1, 2, D_B), lambda b, i: (b, 0, 0)),
                pl.BlockSpec((GROUP, GROUP), lambda b, i: (0, 0))]
    args = [pm, mm, nm, gt, al, bl, pm, mm, nm, gt, al, bl, s0, l0, p["bdm_bf16"]]
    out_specs = [pl.BlockSpec((1, t, D_A), lambda b, i: (b, 0, 0)),
                 pl.BlockSpec((2, 1, nc, 1, D_B), lambda b, i: (0, b, 0, 0, 0)),
                 pl.BlockSpec((1, 2, HEAD, D_A), lambda b, i: (b, 0, 0, 0)),
                 pl.BlockSpec((1, 2, D_B), lambda b, i: (b, 0, 0))]
    out_shape = [jax.ShapeDtypeStruct((bsz, t, D_A), F32),
                 jax.ShapeDtypeStruct((2, bsz, nc, 1, D_B), F32),
                 jax.ShapeDtypeStruct((bsz, 2, HEAD, D_A), F32),
                 jax.ShapeDtypeStruct((bsz, 2, D_B), F32)]
    return pl.pallas_call(
        functools.partial(_scan_kernel, nt),
        grid=(bsz, nt),
        in_specs=in_specs,
        out_specs=out_specs,
        out_shape=out_shape,
        compiler_params=pltpu.CompilerParams(dimension_semantics=("parallel", "arbitrary"),
                                             vmem_limit_bytes=VMEM_LIMIT),
        name="scan",
    )(*args)


def _out_kernel(has_pos, *refs):
    if has_pos:
        x_ref, pos_ref = refs[0], refs[1]
        refs = refs[2:]
    else:
        x_ref, pos_ref = refs[0], None
        refs = refs[1:]
    (y_ref, q_ref, ac_ref, hc_ref, bs_ref, g_ref, bonus_ref, gate_ref, mod_ref,
     avg_ref, lnxg_ref, lnxb_ref, wout_ref, gpost_ref, gpre2_ref, w1_ref, w2_ref, gpost2_ref,
     o_ref) = refs

    x = x_ref[0]
    if has_pos:
        x = x + pos_ref[...]
    mod = mod_ref[0]
    gate1, shift2, scale2, gate2 = mod[2:3], mod[3:4], mod[4:5], mod[5:6]

    y = y_ref[0] + q_ref[0]
    avg = avg_ref[...]

    def head_mean(a):
        return jnp.concatenate([_dot(a[:, t * MXU_TILE:(t + 1) * MXU_TILE], avg)
                                for t in range(D_A // MXU_TILE)], axis=1)

    y_hi = y.astype(BF16)
    y_lo = (y - y_hi.astype(F32)).astype(BF16)
    mu = head_mean(y_hi) + head_mean(y_lo)
    yc = y - mu
    var = head_mean(yc * yc)
    yn = yc * lax.rsqrt(var + LNX_EPS) * lnxg_ref[...] + lnxb_ref[...]
    out_a = (yn + bonus_ref[0]) * g_ref[0]
    hs = jnp.concatenate(
        [bs_ref[0, c * CHUNK:(c + 1) * CHUNK, :]
         + ac_ref[0, 0, c * CHUNK:(c + 1) * CHUNK, :] * hc_ref[0, 0, c]
         + ac_ref[1, 0, c * CHUNK:(c + 1) * CHUNK, :] * hc_ref[1, 0, c]
         for c in range(x_ref.shape[1] // CHUNK)], axis=0)
    out_b = hs * gate_ref[0]
    mix = _dot(jnp.concatenate([out_a, out_b], axis=1), wout_ref[...])
    ms = jnp.mean(mix * mix, axis=-1, keepdims=True)
    x = x + gate1 * ((mix * lax.rsqrt(ms + EPS)) * gpost_ref[...])

    ms = jnp.mean(x * x, axis=-1, keepdims=True)
    h = (x * lax.rsqrt(ms + EPS)) * gpre2_ref[...]
    h = h * (1.0 + scale2) + shift2
    f = _dot(h, w1_ref[...])
    f = jnp.square(jnp.maximum(f, 0.0))
    f = _dot(f, w2_ref[...])
    ms = jnp.mean(f * f, axis=-1, keepdims=True)
    o_ref[0] = x + gate2 * ((f * lax.rsqrt(ms + EPS)) * gpost2_ref[...])


def _out_call(x, pos, mod, mod_row, y, q, ac, hc, bs, g, bonus, gate, p):
    bsz, t, _ = x.shape
    tm = min(OUT_TILE, t)
    nt = t // tm
    has_pos = pos is not None

    def const(shape):
        return pl.BlockSpec(shape, lambda i, b: (0,) * len(shape))

    tok = lambda width: pl.BlockSpec((1, tm, width), lambda i, b: (b, i, 0))
    in_specs = [tok(D_MODEL)]
    args = [x]
    if has_pos:
        in_specs.append(pl.BlockSpec((tm, D_MODEL), lambda i, b: (i, 0)))
        args.append(pos)
    in_specs += [tok(D_A), tok(D_A),
                 pl.BlockSpec((2, 1, tm, D_B), lambda i, b: (0, b, i, 0)),
                 pl.BlockSpec((2, 1, tm // CHUNK, 1, D_B), lambda i, b: (0, b, i, 0, 0))]
    in_specs += [tok(D_A)] * 4
    in_specs += [
        pl.BlockSpec((1, 6, D_MODEL), lambda i, b: (mod_row(b), 0, 0)),
        const((MXU_TILE, MXU_TILE)), const((1, D_A)), const((1, D_A)),
        const((D_MODEL, D_MODEL)), const((1, D_MODEL)), const((1, D_MODEL)),
        const((D_MODEL, D_FF)), const((D_FF, D_MODEL)), const((1, D_MODEL)),
    ]
    args += [y, q, ac, hc, bs, g, bonus, gate, mod,
             p["seg_avg"], p["lnx_g"], p["lnx_b"], p["w_out"], p["g_post_mix"], p["g_pre_mlp"],
             p["w_mlp1"], p["w_mlp2"], p["g_post_mlp"]]
    return pl.pallas_call(
        functools.partial(_out_kernel, has_pos),
        grid=(nt, bsz),
        in_specs=in_specs,
        out_specs=tok(D_MODEL),
        out_shape=jax.ShapeDtypeStruct((bsz, t, D_MODEL), F32),
        compiler_params=pltpu.CompilerParams(dimension_semantics=("parallel", "parallel"),
                                             vmem_limit_bytes=VMEM_LIMIT),
        name="out",
    )(*args)


def _sincos_1d(pos, dim):
    omega = 1.0 / (10000.0 ** (jnp.arange(dim // 2, dtype=F32) / (dim // 2)))
    ang = pos.astype(F32)[:, None] * omega[None, :]
    return jnp.concatenate([jnp.sin(ang), jnp.cos(ang)], axis=-1)


def _grid_pos_embed(n_tokens):
    rows = n_tokens // GRID_W
    half = D_MODEL // 2
    e_row = _sincos_1d(jnp.arange(rows), half)
    e_col = _sincos_1d(jnp.arange(GRID_W), half)
    emb = jnp.concatenate([jnp.broadcast_to(e_row[:, None, :], (rows, GRID_W, half)),
                           jnp.broadcast_to(e_col[None, :, :], (rows, GRID_W, half))], axis=-1)
    return emb.reshape(rows * GRID_W, D_MODEL)


def _blockdiag_pairs(w):
    z = jnp.zeros_like(w[0])
    return jnp.concatenate([jnp.concatenate([w[0], z], axis=1),
                            jnp.concatenate([z, w[1]], axis=1)], axis=0)


def _heads_to_blockdiag(w):
    lead = w.shape[:-3]
    w = w.reshape(lead + (N_GROUP, HEADS_PER_GROUP, HEAD, HEAD))
    eye = jnp.eye(HEADS_PER_GROUP, dtype=w.dtype)
    bd = jnp.einsum('...ghab,hj->...ghajb', w, eye)
    return bd.reshape(lead + (N_GROUP, GROUP, GROUP))


def _state_to_lanes(s):
    b = s.shape[0]
    return jnp.transpose(s, (0, 1, 3, 2, 4)).reshape(b, 2, HEAD, D_A)


def _state_from_lanes(s):
    b = s.shape[0]
    return jnp.transpose(s.reshape(b, 2, HEAD, N_HEAD, HEAD), (0, 1, 3, 2, 4))


def kernel(x_prompt, x_sample, c, state_rwkv, state_lru, c_ctx, w_mod, b_mod, g_pre_mix, g_post_mix,
           g_pre_mlp, g_post_mlp, w_in, rwkv_w0, rwkv_w_up, rwkv_a0, rwkv_a_up, rwkv_g_up, rwkv_k_k,
           rwkv_k_a, rwkv_r_k, rwkv_lnx_g, rwkv_lnx_b, lru_conv_w, lru_conv_b, lru_wa, lru_ba, lru_wx,
           lru_bx, lru_lambda, w_out, w_mlp1, w_mlp2):
    n_ctx = x_prompt.shape[0]
    n_lat = x_sample.shape[0]
    l = 0
    seg = _group_blockdiag_mask()
    seg512 = np.kron(np.eye(N_GROUP, dtype=np.float32), seg)
    p = {
        "g_pre_mix": g_pre_mix[l][None], "g_post_mix": g_post_mix[l][None],
        "g_pre_mlp": g_pre_mlp[l][None], "g_post_mlp": g_post_mlp[l][None],
        "w_in": w_in[l].astype(BF16), "w_out": w_out[l].astype(BF16),
        "w_mlp1": w_mlp1[l].astype(BF16), "w_mlp2": w_mlp2[l].astype(BF16),
        "w0": rwkv_w0[l].reshape(1, 2 * D_A), "a0": rwkv_a0[l].reshape(1, 2 * D_A),
        "wup_bd": _blockdiag_pairs(rwkv_w_up[l]).astype(BF16),
        "aup_bd": _blockdiag_pairs(rwkv_a_up[l]).astype(BF16),
        "g_up": rwkv_g_up[l].astype(BF16),
        "k_k": rwkv_k_k[l][None], "k_a": rwkv_k_a[l][None], "r_k": rwkv_r_k[l].reshape(1, D_A),
        "lnx_g": rwkv_lnx_g[l][None], "lnx_b": rwkv_lnx_b[l][None],
        "conv_w": lru_conv_w[l], "conv_b": lru_conv_b[l][None],
        "wa_bd": _heads_to_blockdiag(lru_wa[l]).astype(BF16), "ba": lru_ba[l],
        "wx_bd": _heads_to_blockdiag(lru_wx[l]).astype(BF16), "bx": lru_bx[l],
        "lam": lru_lambda[l],
        "seg_ones": jnp.asarray(seg512, BF16),
        "seg_avg": jnp.asarray(np.kron(np.eye(MXU_TILE // GROUP, dtype=np.float32), seg) / HEAD, BF16),
        "chunk_masks": jnp.asarray(_chunk_masks()),
        "bdm_bf16": jnp.asarray(seg, BF16),
    }

    m_rows = 16
    c_all = jnp.concatenate([c_ctx[None], c, jnp.zeros((m_rows - 1 - n_lat, D_MODEL), F32)], axis=0)
    mod = _mod_call(c_all, w_mod[l], b_mod[l]).reshape(m_rows, 6, D_MODEL)

    pos = _grid_pos_embed(x_sample.shape[1]).astype(x_sample.dtype)
    ctx_row = lambda b: 0
    lat_row = lambda b: b + 1

    pm, q, mm, nm, gt, ac, bs, bl, al, g, bonus, gate = _feat_call(x_prompt, None, mod, ctx_row, p)
    y, hc, s_ctx, l_ctx = _scan_call(
        (pm, mm, nm, gt, al, bl), jnp.zeros((n_ctx, 2, HEAD, D_A), F32), jnp.zeros((n_ctx, 2, D_B), F32), p)
    y_prompt = _out_call(x_prompt, None, mod, ctx_row, y, q, ac, hc, bs, g, bonus, gate, p)

    pm, q, mm, nm, gt, ac, bs, bl, al, g, bonus, gate = _feat_call(x_sample, pos, mod, lat_row, p)
    y, hc, _, _ = _scan_call((pm, mm, nm, gt, al, bl), _state_to_lanes(state_rwkv[:, l]), state_lru[:, l], p)
    y_sample = _out_call(x_sample, pos, mod, lat_row, y, q, ac, hc, bs, g, bonus, gate, p)

    new_state_rwkv = _state_from_lanes(s_ctx)[:, None].astype(x_prompt.dtype)
    new_state_lru = l_ctx[:, None].astype(x_prompt.dtype)
    return (y_prompt, y_sample, new_state_rwkv, new_state_lru)
```

```python
import functools

import numpy as np
import jax
import jax.numpy as jnp
from jax import lax
from jax.experimental import pallas as pl
from jax.experimental.pallas import tpu as pltpu

F32 = jnp.float32
BF16 = jnp.bfloat16

D_MODEL = 1024
D_A = 512
D_B = 512
HEAD = 64
N_HEAD = 8
R_W = 64
R_A = 64
R_G = 128
D_FF = 4096
D_IN = 2944
GRID_W = 64
CONV_W = 4
LRU_C = 8.0
EPS = 1e-6
LNX_EPS = 64e-5

CHUNK = 64
GROUP = 128
HEADS_PER_GROUP = GROUP // HEAD
N_GROUP = D_A // GROUP
SUB = 8
LANE = 128
N_SLAB = D_A // LANE
HALO = 8
TOKEN_TILE = 256
CPT = TOKEN_TILE // CHUNK
OUT_TILE = 512
SCAN_TILE = 512
MXU_TILE = 256
VMEM_LIMIT = 60 * 1024 * 1024

_O_R, _O_K, _O_V, _O_XW, _O_XA, _O_XG, _O_XB, _O_GB = 0, 512, 1024, 1536, 1664, 1792, 1920, 2432

_M_STRICT, _M_INCL, _M_LEV0 = 0, 1, 2
_LEVELS = (1, 2, 4, 8, 16, 32)
_M_PER_DIR = 2 + len(_LEVELS)
_M_EYE = 2 * _M_PER_DIR


def _dot(a, b):
    return jnp.dot(a.astype(BF16), b.astype(BF16), preferred_element_type=F32)


def _dot_nt(a, b):
    return lax.dot_general(a.astype(BF16), b.astype(BF16), (((1,), (1,)), ((), ())),
                           preferred_element_type=F32)


def _dot_tn(a, b):
    return lax.dot_general(a.astype(BF16), b.astype(BF16), (((0,), (0,)), ((), ())),
                           preferred_element_type=F32)


def _strided_rows(ref, lead, base):
    return [ref[lead + (pl.ds(base + j, SUB, stride=SUB), slice(None))] for j in range(SUB)]


def _natural_rows(ref, lead, base, pieces):
    for j, piece in enumerate(pieces):
        ref[lead + (pl.ds(base + SUB * j, SUB), slice(None))] = piece
    return jnp.concatenate(_strided_rows(ref, lead, base), axis=0)


def _sublane_shift(x, steps, reverse, fill):
    sub = lax.broadcasted_iota(jnp.int32, x.shape, 0)
    if reverse:
        return jnp.where(sub < SUB - steps, pltpu.roll(x, SUB - steps, 0), fill)
    return jnp.where(sub >= steps, pltpu.roll(x, steps, 0), fill)


def _scan_affine(a, b, reverse):
    a, b = list(a), list(b)
    order = range(SUB - 2, -1, -1) if reverse else range(1, SUB)
    for j in order:
        p = j + 1 if reverse else j - 1
        b[j] = a[j] * b[p] + b[j]
        a[j] = a[j] * a[p]
    ta, tb = (a[0], b[0]) if reverse else (a[SUB - 1], b[SUB - 1])
    s = 1
    while s < SUB:
        tb = ta * _sublane_shift(tb, s, reverse, 0.0) + tb
        ta = ta * _sublane_shift(ta, s, reverse, 1.0)
        s *= 2
    ea = _sublane_shift(ta, 1, reverse, 1.0)
    eb = _sublane_shift(tb, 1, reverse, 0.0)
    return [x * ea for x in a], [x * eb + y for x, y in zip(a, b)]


def _scan_sum(x, reverse):
    x = list(x)
    order = range(SUB - 2, -1, -1) if reverse else range(1, SUB)
    for j in order:
        x[j] = x[j] + x[j + 1 if reverse else j - 1]
    t = x[0] if reverse else x[SUB - 1]
    s = 1
    while s < SUB:
        t = t + _sublane_shift(t, s, reverse, 0.0)
        s *= 2
    e = _sublane_shift(t, 1, reverse, 0.0)
    return [y + e for y in x]


def _sigmoid(x):
    return 0.5 * jnp.tanh(0.5 * x) + 0.5


def _blockdiag(x, bdm):
    xb = x.astype(BF16)
    return jnp.concatenate([xb] * HEADS_PER_GROUP, axis=0) * bdm


def _mod_kernel(c_ref, w_ref, b_ref, o_ref):
    c = c_ref[...]
    s = c * _sigmoid(c)
    o_ref[...] = _dot(s, w_ref[...]) + b_ref[...]


def _mod_call(c_all, w_mod, b_mod):
    m = c_all.shape[0]
    n = w_mod.shape[1]
    tn = 1536
    return pl.pallas_call(
        _mod_kernel,
        grid=(n // tn,),
        in_specs=[pl.BlockSpec((m, D_MODEL), lambda j: (0, 0)),
                  pl.BlockSpec((D_MODEL, tn), lambda j: (0, j)),
                  pl.BlockSpec((1, tn), lambda j: (0, j))],
        out_specs=pl.BlockSpec((m, tn), lambda j: (0, j)),
        out_shape=jax.ShapeDtypeStruct((m, n), F32),
        compiler_params=pltpu.CompilerParams(dimension_semantics=("parallel",),
                                             vmem_limit_bytes=VMEM_LIMIT),
        name="mod",
    )(c_all, w_mod, b_mod.reshape(1, n))


def _chunk_masks():
    t = np.arange(CHUNK)[:, None]
    s = (np.arange(GROUP) % CHUNK)[None, :]
    rows = []
    for d in (0, 1):
        before = (s < t) if d == 0 else (s > t)
        rows.append(before)
        rows.append(before | (s == t))
        for b in _LEVELS:
            same = (t // (2 * b)) == (s // (2 * b))
            if d == 0:
                rows.append(same & ((t // b) % 2 == 1) & ((s // b) % 2 == 0))
            else:
                rows.append(same & ((t // b) % 2 == 0) & ((s // b) % 2 == 1))
    rows.append(s == t)
    return np.stack(rows).astype(np.float32)


def _group_blockdiag_mask():
    i = np.arange(GROUP)
    return ((i[:, None] // HEAD) == (i[None, :] // HEAD)).astype(np.float32)


def _feat_kernel(has_pos, nt, *refs):
    if has_pos:
        x_ref, xp_ref, xn_ref, pos_ref, pp_ref, pn_ref = refs[:6]
        refs = refs[6:]
    else:
        x_ref, xp_ref, xn_ref = refs[:3]
        pos_ref = pp_ref = pn_ref = None
        refs = refs[3:]
    (mod_ref, gpre_ref, win_ref, wup_ref, w0_ref, aup_ref, a0_ref, gup_ref,
     kk_ref, ka_ref, rk_ref, seg_ref,
     convw_ref, convb_ref, wa_ref, ba_ref, wx_ref, bx_ref, lam_ref,
     masks_ref, bdm_ref,
     p_o, q_o, m_o, n_o, gt_o, ac_o, bs_o, bl_o, al_o, g_o, bonus_o, gate_o,
     lw_s, lc_s, la_s, lb_s, lo_s) = refs

    i = pl.program_id(0)
    tm = TOKEN_TILE
    mod = mod_ref[0]
    shift1, scale1 = mod[0:1], mod[1:2]

    def normmod(xv):
        ms = jnp.mean(xv * xv, axis=-1, keepdims=True)
        hv = (xv * lax.rsqrt(ms + EPS)) * gpre_ref[...]
        return hv * (1.0 + scale1) + shift1

    x = x_ref[0]
    halo = jnp.concatenate([xp_ref[0], xn_ref[0]], axis=0)
    if has_pos:
        x = x + pos_ref[...]
        halo = halo + jnp.concatenate([pp_ref[...], pn_ref[...]], axis=0)
    z = _dot(normmod(x), win_ref[...])
    zh = _dot(normmod(halo), win_ref[:, _O_XB:_O_XB + D_B])

    r = z[:, _O_R:_O_R + D_A]
    k = z[:, _O_K:_O_K + D_A]
    v = z[:, _O_V:_O_V + D_A]
    xw = z[:, _O_XW:_O_XW + 2 * R_W]
    xa = z[:, _O_XA:_O_XA + 2 * R_A]
    xg = z[:, _O_XG:_O_XG + R_G]

    g_o[0] = _dot(_sigmoid(xg), gup_ref[...])
    gate_o[0] = jax.nn.gelu(z[:, _O_GB:_O_GB + D_B], approximate=True)
    wl = w0_ref[...] + _dot(jnp.tanh(xw), wup_ref[...])
    lw2 = -_sigmoid(wl) * float(np.exp(-0.5))
    a2 = _sigmoid(a0_ref[...] + _dot(xa, aup_ref[...]))

    kks = k * kk_ref[...]
    ss = _dot(kks * kks, seg_ref[...])
    kk = kks * lax.rsqrt(jnp.maximum(ss, 1e-24))
    ka = ka_ref[...]
    kd2 = [k * (1.0 + (a2[:, d * D_A:(d + 1) * D_A] - 1.0) * ka) for d in (0, 1)]
    bonus_o[0] = _dot(r * (kd2[0] + kd2[1]) * rk_ref[...], seg_ref[...]) * v

    m_prev = jnp.where(i > 0, 1.0, 0.0)
    m_next = jnp.where(i < nt - 1, 1.0, 0.0)
    ext = jnp.concatenate([zh[:HALO] * m_prev, z[:, _O_XB:_O_XB + D_B], zh[HALO:] * m_next], axis=0)
    n_ext = tm + 2 * HALO
    xc = convb_ref[...]
    for j in range(CONV_W):
        sh = (2 - j) % n_ext
        tap = ext if sh == 0 else pltpu.roll(ext, sh, 0)
        xc = xc + tap[HALO:HALO + tm] * convw_ref[j:j + 1, :]
    for d in (0, 1):
        rg = jnp.concatenate([_dot(xc[:, g * GROUP:(g + 1) * GROUP], wa_ref[d, g]) for g in range(N_GROUP)], 1)
        ig = jnp.concatenate([_dot(xc[:, g * GROUP:(g + 1) * GROUP], wx_ref[d, g]) for g in range(N_GROUP)], 1)
        rg = _sigmoid(rg + ba_ref[d:d + 1, :])
        ig = _sigmoid(ig + bx_ref[d:d + 1, :])
        log_a = -LRU_C * rg * jax.nn.softplus(-lam_ref[d:d + 1, :])
        a_lru = jnp.exp(log_a)
        b_lru = jnp.sqrt(-jnp.tanh(log_a) * (a_lru * a_lru + 1.0)) * (ig * xc)
        for q in range(N_SLAB):
            la_s[d, q] = a_lru[:, q * LANE:(q + 1) * LANE]
            lb_s[d, q] = b_lru[:, q * LANE:(q + 1) * LANE]
    for c in range(CPT):
        base = c * CHUNK
        rs = slice(base, base + CHUNK)
        for q in range(N_SLAB):
            ls = slice(q * LANE, (q + 1) * LANE)
            bsum = None
            for d in (0, 1):
                acum, bcum = _scan_affine(_strided_rows(la_s, (d, q), base), _strided_rows(lb_s, (d, q), base),
                                          d == 1)
                ac_o[d, 0, rs, ls] = _natural_rows(lo_s, (d, q), base, acum)
                bsum = bcum if bsum is None else [x + y for x, y in zip(bsum, bcum)]
                bl_o[d, 0, c, :, ls] = bcum[0][0:1, :] if d == 1 else bcum[SUB - 1][SUB - 1:SUB, :]
                al_o[d, 0, c, :, ls] = acum[0][0:1, :] if d == 1 else acum[SUB - 1][SUB - 1:SUB, :]
            bs_o[0, rs, ls] = _natural_rows(lo_s, (2, q), base, bsum)

    bdm = bdm_ref[...]
    bd = lambda xv: _blockdiag(xv, bdm)
    bdot = lambda lhs, rhs: jnp.dot(lhs.astype(BF16), bd(rhs), preferred_element_type=F32)

    a_t, r_t, b_t, k_t, g_tot = [], [], [], [], []
    for d in (0, 1):
        lw = lw2[:, d * D_A:(d + 1) * D_A]
        for q in range(N_SLAB):
            lw_s[d, q] = lw[:, q * LANE:(q + 1) * LANE]
        g_tot.append([])
        lc_rows = []
        for c in range(CPT):
            base = c * CHUNK
            blocks, totals = [], []
            for q in range(N_SLAB):
                pieces = _scan_sum(_strided_rows(lw_s, (d, q), base), d == 1)
                totals.append(pieces[0][0:1, :] if d == 1 else pieces[SUB - 1][SUB - 1:SUB, :])
                blocks.append(_natural_rows(lc_s, (d, q), base, pieces))
            lc_rows.append(jnp.concatenate(blocks, axis=1))
            g_tot[d].append(jnp.exp(jnp.concatenate(totals, axis=1)))
            gt_o[d, 0, c] = g_tot[d][c]
        lc = jnp.concatenate(lc_rows, axis=0)
        e_neg = jnp.exp(-lc)
        a_t.append(-kk * jnp.exp(lc - lw))
        r_t.append(r * jnp.exp(lc))
        b_t.append(kk * a2[:, d * D_A:(d + 1) * D_A] * e_neg)
        k_t.append(kd2[d] * e_neg)

    combos = [(d, c, g) for d in (0, 1) for c in range(CPT) for g in range(N_GROUP)]

    def cut(arr, c, g):
        return arr[c * CHUNK:(c + 1) * CHUNK, g * GROUP:(g + 1) * GROUP]

    sc = {}
    for key in combos:
        d, c, g = key
        ar = jnp.concatenate([cut(a_t[d], c, g), cut(r_t[d], c, g)], axis=0)
        rhs = jnp.concatenate([bd(cut(b_t[d], c, g)), bd(cut(k_t[d], c, g))], axis=0)
        sc[key] = _dot_nt(ar, rhs)
    n_ab, n_ak, n_rb, n_rk, tinv = {}, {}, {}, {}, {}
    for key in combos:
        m0 = key[0] * _M_PER_DIR
        n_ab[key] = sc[key][:CHUNK, :GROUP] * masks_ref[m0 + _M_STRICT]
        n_ak[key] = sc[key][:CHUNK, GROUP:] * masks_ref[m0 + _M_STRICT]
        n_rb[key] = sc[key][CHUNK:, :GROUP] * masks_ref[m0 + _M_INCL]
        n_rk[key] = sc[key][CHUNK:, GROUP:] * masks_ref[m0 + _M_INCL]
        tinv[key] = masks_ref[_M_EYE] + n_ab[key] * masks_ref[m0 + _M_LEV0]
    for li in range(1, len(_LEVELS)):
        pm = {}
        for key in combos:
            pm[key] = bdot(n_ab[key] * masks_ref[key[0] * _M_PER_DIR + _M_LEV0 + li], tinv[key])
        for key in combos:
            tinv[key] = tinv[key] + bdot(tinv[key], pm[key])

    kv = {key: bdot(jnp.concatenate([n_ak[key], n_rk[key]], axis=0), cut(v, key[1], key[2])) for key in combos}
    gm = {key: bdot(n_rb[key], tinv[key]) for key in combos}
    wp = {}
    for key in combos:
        d, c, g = key
        lhs = jnp.concatenate([tinv[key], gm[key]], axis=0)
        rhs = jnp.concatenate([bd(cut(a_t[d], c, g)), bd(kv[key][:CHUNK])], axis=1)
        wp[key] = jnp.dot(lhs.astype(BF16), rhs, preferred_element_type=F32)

    lane = lax.broadcasted_iota(jnp.int32, (CHUNK, GROUP), 1)

    def fold(full):
        out = full[(HEADS_PER_GROUP - 1) * HEAD:]
        for h in range(HEADS_PER_GROUP - 2, -1, -1):
            out = jnp.where(lane < (h + 1) * HEAD, full[h * HEAD:(h + 1) * HEAD], out)
        return out

    for key in combos:
        d, c, g = key
        rs = slice(c * CHUNK, (c + 1) * CHUNK)
        ls = slice(g * GROUP, (g + 1) * GROUP)
        w_, u0 = wp[key][:CHUNK, :GROUP], wp[key][:CHUNK, GROUP:]
        vg = cut(v, c, g)
        gt = g_tot[d][c][:, ls]
        bh, kh = cut(b_t[d], c, g) * gt, cut(k_t[d], c, g) * gt
        p_o[d, 0, rs, ls] = (cut(r_t[d], c, g) + wp[key][CHUNK:, :GROUP]).astype(BF16)
        m_o[d, 0, rs, ls] = fold(_dot_tn(w_, bh)).astype(BF16)
        n_o[d, 0, rs, ls] = fold(_dot_tn(jnp.concatenate([u0, vg], axis=0), jnp.concatenate([bh, kh], axis=0)))
        if d == 1:
            other = (0, c, g)
            q_o[0, rs, ls] = ((wp[other][CHUNK:, GROUP:] + kv[other][CHUNK:])
                              + (wp[key][CHUNK:, GROUP:] + kv[key][CHUNK:]))


def _feat_call(x, pos, mod, mod_row, p):
    bsz, t, _ = x.shape
    tm = TOKEN_TILE
    nt = t // tm
    nc = t // CHUNK
    hpt = tm // HALO
    has_pos = pos is not None

    def const(shape):
        return pl.BlockSpec(shape, lambda i, b: (0,) * len(shape))

    tok = lambda width: pl.BlockSpec((1, tm, width), lambda i, b: (b, i, 0))
    prev_i = lambda i: jnp.maximum(i * hpt - 1, 0)
    next_i = lambda i: jnp.minimum((i + 1) * hpt, t // HALO - 1)
    in_specs = [tok(D_MODEL),
                pl.BlockSpec((1, HALO, D_MODEL), lambda i, b: (b, prev_i(i), 0)),
                pl.BlockSpec((1, HALO, D_MODEL), lambda i, b: (b, next_i(i), 0))]
    args = [x, x, x]
    if has_pos:
        in_specs += [pl.BlockSpec((tm, D_MODEL), lambda i, b: (i, 0)),
                     pl.BlockSpec((HALO, D_MODEL), lambda i, b: (prev_i(i), 0)),
                     pl.BlockSpec((HALO, D_MODEL), lambda i, b: (next_i(i), 0))]
        args += [pos, pos, pos]
    n_masks = 2 * _M_PER_DIR + 1
    in_specs += [
        pl.BlockSpec((1, 6, D_MODEL), lambda i, b: (mod_row(b), 0, 0)),
        const((1, D_MODEL)), const((D_MODEL, D_IN)),
        const((2 * R_W, 2 * D_A)), const((1, 2 * D_A)),
        const((2 * R_A, 2 * D_A)), const((1, 2 * D_A)),
        const((R_G, D_A)),
        const((1, D_A)), const((1, D_A)), const((1, D_A)), const((D_A, D_A)),
        const((CONV_W, D_B)), const((1, D_B)),
        const((2, N_GROUP, GROUP, GROUP)), const((2, D_B)),
        const((2, N_GROUP, GROUP, GROUP)), const((2, D_B)), const((2, D_B)),
        const((n_masks, CHUNK, GROUP)), const((GROUP, GROUP)),
    ]
    args += [mod, p["g_pre_mix"], p["w_in"], p["wup_bd"], p["w0"], p["aup_bd"], p["a0"], p["g_up"],
             p["k_k"], p["k_a"], p["r_k"], p["seg_ones"],
             p["conv_w"], p["conv_b"], p["wa_bd"], p["ba"], p["wx_bd"], p["bx"], p["lam"],
             p["chunk_masks"], p["bdm_bf16"]]
    tok_shape = jax.ShapeDtypeStruct((bsz, t, D_A), F32)
    dir_shape = jax.ShapeDtypeStruct((2, bsz, t, D_A), F32)
    row_shape = jax.ShapeDtypeStruct((2, bsz, nc, 1, D_A), F32)
    mxu_shape = jax.ShapeDtypeStruct((2, bsz, t, D_A), BF16)
    dir_spec = pl.BlockSpec((2, 1, tm, D_A), lambda i, b: (0, b, i, 0))
    row_spec = pl.BlockSpec((2, 1, CPT, 1, D_A), lambda i, b: (0, b, i, 0, 0))
    out_shape = [mxu_shape, tok_shape, mxu_shape, dir_shape, row_shape, dir_shape, tok_shape, row_shape, row_shape,
                 tok_shape, tok_shape, tok_shape]
    out_specs = [dir_spec, tok(D_A), dir_spec, dir_spec, row_spec, dir_spec, tok(D_B), row_spec, row_spec,
                 tok(D_A), tok(D_A), tok(D_B)]
    return pl.pallas_call(
        functools.partial(_feat_kernel, has_pos, nt),
        grid=(nt, bsz),
        in_specs=in_specs,
        out_specs=out_specs,
        out_shape=out_shape,
        scratch_shapes=[pltpu.VMEM((2, N_SLAB, tm, LANE), F32)] * 4 + [pltpu.VMEM((3, N_SLAB, tm, LANE), F32)],
        compiler_params=pltpu.CompilerParams(dimension_semantics=("parallel", "parallel"),
                                             vmem_limit_bytes=VMEM_LIMIT),
        name="feat",
    )(*args)


def _scan_kernel(nt, tile, *refs):
    (pf_ref, mf_ref, nf_ref, gtf_ref, alf_ref, blf_ref,
     pb_ref, mb_ref, nb_ref, gtb_ref, alb_ref, blb_ref,
     s0_ref, l0_ref, bdm_ref,
     y_ref, hc_ref, s_ref, hl_ref) = refs

    i = pl.program_id(1)
    cpt = tile // CHUNK

    @pl.when(i == 0)
    def _init():
        y_ref[...] = jnp.zeros_like(y_ref)
        s_ref[...] = s0_ref[...]
        hl_ref[...] = l0_ref[...]

    bdm = bdm_ref[...]
    per_dir = ((0, pf_ref, mf_ref, nf_ref, gtf_ref, alf_ref, blf_ref),
               (1, pb_ref, mb_ref, nb_ref, gtb_ref, alb_ref, blb_ref))
    tiles = (i, nt - 1 - i)
    state = [s_ref[0, d] for d in (0, 1)]
    h0 = [hl_ref[0, d:d + 1, :] for d in (0, 1)]
    for cc in range(cpt):
        for d, p_ref, m_ref, n_ref, gt_ref, al_ref, bl_ref in per_dir:
            c = cc if d == 0 else cpt - 1 - cc
            rs = slice(c * CHUNK, (c + 1) * CHUNK)
            rows = pl.ds(pl.multiple_of(tiles[d] * tile + c * CHUNK, CHUNK), CHUNK)
            gt = gt_ref[0, 0, c]
            y_parts, s_parts = [], []
            for g in range(N_GROUP):
                ls = slice(g * GROUP, (g + 1) * GROUP)
                sg = state[d][:, ls]
                y_parts.append(_dot_nt(p_ref[0, 0, rs, ls], _blockdiag(sg, bdm)))
                s_parts.append(sg * gt[:, ls]
                               + jnp.dot(sg.astype(BF16), _blockdiag(m_ref[0, 0, rs, ls], bdm),
                                         preferred_element_type=F32)
                               + n_ref[0, 0, rs, ls])
            state[d] = jnp.concatenate(s_parts, axis=1)
            y_ref[0, rows, :] += jnp.concatenate(y_parts, axis=1)
            hc_ref[d, 0, tiles[d] * cpt + c] = h0[d]
            h0[d] = bl_ref[0, 0, c] + al_ref[0, 0, c] * h0[d]
    for d in (0, 1):
        s_ref[0, d] = state[d]
        hl_ref[0, d:d + 1, :] = h0[d]


def _scan_call(feats, s0, l0, p):
    pm, mm, nm, gt, al, bl = feats
    _, bsz, t, _ = pm.shape
    tile = min(SCAN_TILE, t)
    nt = t // tile
    nc = t // CHUNK

    def dir_specs(d, tile_of):
        big = pl.BlockSpec((1, 1, tile, D_A), lambda b, i: (d, b, tile_of(i), 0))
        small = pl.BlockSpec((1, 1, tile // CHUNK, 1, D_A), lambda b, i: (d, b, tile_of(i), 0, 0))
        return big, small

    big_f, small_f = dir_specs(0, lambda i: i)
    big_b, small_b = dir_specs(1, lambda i: nt - 1 - i)
    in_specs = [big_f, big_f, big_f, small_f, small_f, small_f,
                big_b, big_b, big_b, small_b, small_b, small_b,
                pl.BlockSpec((1, 2, HEAD, D_A), lambda b, i: (b, 0, 0, 0)),
                pl.BlockSpec((1, 2, D_B), lambda b, i: (b, 0, 0)),
                pl.BlockSpec((GROUP, GROUP), lambda b, i: (0, 0))]
    args = [pm, mm, nm, gt, al, bl, pm, mm, nm, gt, al, bl, s0, l0, p["bdm_bf16"]]
    out_specs = [pl.BlockSpec((1, t, D_A), lambda b, i: (b, 0, 0)),
                 pl.BlockSpec((2, 1, nc, 1, D_B), lambda b, i: (0, b, 0, 0, 0)),
                 pl.BlockSpec((1, 2, HEAD, D_A), lambda b, i: (b, 0, 0, 0)),
                 pl.BlockSpec((1, 2, D_B), lambda b, i: (b, 0, 0))]
    out_shape = [jax.ShapeDtypeStruct((bsz, t, D_A), F32),
                 jax.ShapeDtypeStruct((2, bsz, nc, 1, D_B), F32),
                 jax.ShapeDtypeStruct((bsz, 2, HEAD, D_A), F32),
                 jax.ShapeDtypeStruct((bsz, 2, D_B), F32)]
    return pl.pallas_call(
        functools.partial(_scan_kernel, nt, tile),
        grid=(bsz, nt),
        in_specs=in_specs,
        out_specs=out_specs,
        out_shape=out_shape,
        compiler_params=pltpu.CompilerParams(dimension_semantics=("parallel", "arbitrary"),
                                             vmem_limit_bytes=VMEM_LIMIT),
        name="scan",
    )(*args)


def _out_kernel(has_pos, *refs):
    if has_pos:
        x_ref, pos_ref = refs[0], refs[1]
        refs = refs[2:]
    else:
        x_ref, pos_ref = refs[0], None
        refs = refs[1:]
    (y_ref, q_ref, ac_ref, hc_ref, bs_ref, g_ref, bonus_ref, gate_ref, mod_ref,
     avg_ref, lnxg_ref, lnxb_ref, wout_ref, gpost_ref, gpre2_ref, w1_ref, w2_ref, gpost2_ref,
     o_ref) = refs

    x = x_ref[0]
    if has_pos:
        x = x + pos_ref[...]
    mod = mod_ref[0]
    gate1, shift2, scale2, gate2 = mod[2:3], mod[3:4], mod[4:5], mod[5:6]

    y = y_ref[0] + q_ref[0]
    avg = avg_ref[...]

    def head_mean(a):
        return jnp.concatenate([_dot(a[:, t * MXU_TILE:(t + 1) * MXU_TILE], avg)
                                for t in range(D_A // MXU_TILE)], axis=1)

    y_hi = y.astype(BF16)
    y_lo = (y - y_hi.astype(F32)).astype(BF16)
    mu = head_mean(y_hi) + head_mean(y_lo)
    yc = y - mu
    var = head_mean(yc * yc)
    yn = yc * lax.rsqrt(var + LNX_EPS) * lnxg_ref[...] + lnxb_ref[...]
    out_a = (yn + bonus_ref[0]) * g_ref[0]
    hs = jnp.concatenate(
        [bs_ref[0, c * CHUNK:(c + 1) * CHUNK, :]
         + ac_ref[0, 0, c * CHUNK:(c + 1) * CHUNK, :] * hc_ref[0, 0, c]
         + ac_ref[1, 0, c * CHUNK:(c + 1) * CHUNK, :] * hc_ref[1, 0, c]
         for c in range(x_ref.shape[1] // CHUNK)], axis=0)
    out_b = hs * gate_ref[0]
    mix = _dot(jnp.concatenate([out_a, out_b], axis=1), wout_ref[...])
    ms = jnp.mean(mix * mix, axis=-1, keepdims=True)
    x = x + gate1 * ((mix * lax.rsqrt(ms + EPS)) * gpost_ref[...])

    ms = jnp.mean(x * x, axis=-1, keepdims=True)
    h = (x * lax.rsqrt(ms + EPS)) * gpre2_ref[...]
    h = h * (1.0 + scale2) + shift2
    f = _dot(h, w1_ref[...])
    f = jnp.square(jnp.maximum(f, 0.0))
    f = _dot(f, w2_ref[...])
    ms = jnp.mean(f * f, axis=-1, keepdims=True)
    o_ref[0] = x + gate2 * ((f * lax.rsqrt(ms + EPS)) * gpost2_ref[...])


def _out_call(x, pos, mod, mod_row, y, q, ac, hc, bs, g, bonus, gate, p):
    bsz, t, _ = x.shape
    tm = min(OUT_TILE, t)
    nt = t // tm
    has_pos = pos is not None

    def const(shape):
        return pl.BlockSpec(shape, lambda i, b: (0,) * len(shape))

    tok = lambda width: pl.BlockSpec((1, tm, width), lambda i, b: (b, i, 0))
    in_specs = [tok(D_MODEL)]
    args = [x]
    if has_pos:
        in_specs.append(pl.BlockSpec((tm, D_MODEL), lambda i, b: (i, 0)))
        args.append(pos)
    in_specs += [tok(D_A), tok(D_A),
                 pl.BlockSpec((2, 1, tm, D_B), lambda i, b: (0, b, i, 0)),
                 pl.BlockSpec((2, 1, tm // CHUNK, 1, D_B), lambda i, b: (0, b, i, 0, 0))]
    in_specs += [tok(D_A)] * 4
    in_specs += [
        pl.BlockSpec((1, 6, D_MODEL), lambda i, b: (mod_row(b), 0, 0)),
        const((MXU_TILE, MXU_TILE)), const((1, D_A)), const((1, D_A)),
        const((D_MODEL, D_MODEL)), const((1, D_MODEL)), const((1, D_MODEL)),
        const((D_MODEL, D_FF)), const((D_FF, D_MODEL)), const((1, D_MODEL)),
    ]
    args += [y, q, ac, hc, bs, g, bonus, gate, mod,
             p["seg_avg"], p["lnx_g"], p["lnx_b"], p["w_out"], p["g_post_mix"], p["g_pre_mlp"],
             p["w_mlp1"], p["w_mlp2"], p["g_post_mlp"]]
    return pl.pallas_call(
        functools.partial(_out_kernel, has_pos),
        grid=(nt, bsz),
        in_specs=in_specs,
        out_specs=tok(D_MODEL),
        out_shape=jax.ShapeDtypeStruct((bsz, t, D_MODEL), F32),
        compiler_params=pltpu.CompilerParams(dimension_semantics=("parallel", "parallel"),
                                             vmem_limit_bytes=VMEM_LIMIT),
        name="out",
    )(*args)


def _sincos_1d(pos, dim):
    omega = 1.0 / (10000.0 ** (jnp.arange(dim // 2, dtype=F32) / (dim // 2)))
    ang = pos.astype(F32)[:, None] * omega[None, :]
    return jnp.concatenate([jnp.sin(ang), jnp.cos(ang)], axis=-1)


def _grid_pos_embed(n_tokens):
    rows = n_tokens // GRID_W
    half = D_MODEL // 2
    e_row = _sincos_1d(jnp.arange(rows), half)
    e_col = _sincos_1d(jnp.arange(GRID_W), half)
    emb = jnp.concatenate([jnp.broadcast_to(e_row[:, None, :], (rows, GRID_W, half)),
                           jnp.broadcast_to(e_col[None, :, :], (rows, GRID_W, half))], axis=-1)
    return emb.reshape(rows * GRID_W, D_MODEL)


def _blockdiag_pairs(w):
    z = jnp.zeros_like(w[0])
    return jnp.concatenate([jnp.concatenate([w[0], z], axis=1),
                            jnp.concatenate([z, w[1]], axis=1)], axis=0)


def _heads_to_blockdiag(w):
    lead = w.shape[:-3]
    w = w.reshape(lead + (N_GROUP, HEADS_PER_GROUP, HEAD, HEAD))
    eye = jnp.eye(HEADS_PER_GROUP, dtype=w.dtype)
    bd = jnp.einsum('...ghab,hj->...ghajb', w, eye)
    return bd.reshape(lead + (N_GROUP, GROUP, GROUP))


def _state_to_lanes(s):
    b = s.shape[0]
    return jnp.transpose(s, (0, 1, 3, 2, 4)).reshape(b, 2, HEAD, D_A)


def _state_from_lanes(s):
    b = s.shape[0]
    return jnp.transpose(s.reshape(b, 2, HEAD, N_HEAD, HEAD), (0, 1, 3, 2, 4))


def kernel(x_prompt, x_sample, c, state_rwkv, state_lru, c_ctx, w_mod, b_mod, g_pre_mix, g_post_mix,
           g_pre_mlp, g_post_mlp, w_in, rwkv_w0, rwkv_w_up, rwkv_a0, rwkv_a_up, rwkv_g_up, rwkv_k_k,
           rwkv_k_a, rwkv_r_k, rwkv_lnx_g, rwkv_lnx_b, lru_conv_w, lru_conv_b, lru_wa, lru_ba, lru_wx,
           lru_bx, lru_lambda, w_out, w_mlp1, w_mlp2):
    n_ctx = x_prompt.shape[0]
    n_lat = x_sample.shape[0]
    l = 0
    seg = _group_blockdiag_mask()
    seg512 = np.kron(np.eye(N_GROUP, dtype=np.float32), seg)
    p = {
        "g_pre_mix": g_pre_mix[l][None], "g_post_mix": g_post_mix[l][None],
        "g_pre_mlp": g_pre_mlp[l][None], "g_post_mlp": g_post_mlp[l][None],
        "w_in": w_in[l].astype(BF16), "w_out": w_out[l].astype(BF16),
        "w_mlp1": w_mlp1[l].astype(BF16), "w_mlp2": w_mlp2[l].astype(BF16),
        "w0": rwkv_w0[l].reshape(1, 2 * D_A), "a0": rwkv_a0[l].reshape(1, 2 * D_A),
        "wup_bd": _blockdiag_pairs(rwkv_w_up[l]).astype(BF16),
        "aup_bd": _blockdiag_pairs(rwkv_a_up[l]).astype(BF16),
        "g_up": rwkv_g_up[l].astype(BF16),
        "k_k": rwkv_k_k[l][None], "k_a": rwkv_k_a[l][None], "r_k": rwkv_r_k[l].reshape(1, D_A),
        "lnx_g": rwkv_lnx_g[l][None], "lnx_b": rwkv_lnx_b[l][None],
        "conv_w": lru_conv_w[l], "conv_b": lru_conv_b[l][None],
        "wa_bd": _heads_to_blockdiag(lru_wa[l]).astype(BF16), "ba": lru_ba[l],
        "wx_bd": _heads_to_blockdiag(lru_wx[l]).astype(BF16), "bx": lru_bx[l],
        "lam": lru_lambda[l],
        "seg_ones": jnp.asarray(seg512, BF16),
        "seg_avg": jnp.asarray(np.kron(np.eye(MXU_TILE // GROUP, dtype=np.float32), seg) / HEAD, BF16),
        "chunk_masks": jnp.asarray(_chunk_masks()),
        "bdm_bf16": jnp.asarray(seg, BF16),
    }

    m_rows = 16
    c_all = jnp.concatenate([c_ctx[None], c, jnp.zeros((m_rows - 1 - n_lat, D_MODEL), F32)], axis=0)
    mod = _mod_call(c_all, w_mod[l], b_mod[l]).reshape(m_rows, 6, D_MODEL)

    pos = _grid_pos_embed(x_sample.shape[1]).astype(x_sample.dtype)
    ctx_row = lambda b: 0
    lat_row = lambda b: b + 1

    pm, q, mm, nm, gt, ac, bs, bl, al, g, bonus, gate = _feat_call(x_prompt, None, mod, ctx_row, p)
    y, hc, s_ctx, l_ctx = _scan_call(
        (pm, mm, nm, gt, al, bl), jnp.zeros((n_ctx, 2, HEAD, D_A), F32), jnp.zeros((n_ctx, 2, D_B), F32), p)
    y_prompt = _out_call(x_prompt, None, mod, ctx_row, y, q, ac, hc, bs, g, bonus, gate, p)

    pm, q, mm, nm, gt, ac, bs, bl, al, g, bonus, gate = _feat_call(x_sample, pos, mod, lat_row, p)
    y, hc, _, _ = _scan_call((pm, mm, nm, gt, al, bl), _state_to_lanes(state_rwkv[:, l]), state_lru[:, l], p)
    y_sample = _out_call(x_sample, pos, mod, lat_row, y, q, ac, hc, bs, g, bonus, gate, p)

    new_state_rwkv = _state_from_lanes(s_ctx)[:, None].astype(x_prompt.dtype)
    new_state_lru = l_ctx[:, None].astype(x_prompt.dtype)
    return (y_prompt, y_sample, new_state_rwkv, new_state_lru)
```

```python
import functools

import numpy as np
import jax
import jax.numpy as jnp
from jax import lax
from jax.experimental import pallas as pl
from jax.experimental.pallas import tpu as pltpu

F32 = jnp.float32
BF16 = jnp.bfloat16

D_MODEL = 1024
D_A = 512
D_B = 512
HEAD = 64
N_HEAD = 8
R_W = 64
R_A = 64
R_G = 128
D_FF = 4096
D_IN = 2944
GRID_W = 64
CONV_W = 4
LRU_C = 8.0
EPS = 1e-6
LNX_EPS = 64e-5

CHUNK = 64
GROUP = 128
HEADS_PER_GROUP = GROUP // HEAD
N_GROUP = D_A // GROUP
SUB = 8
LANE = 128
N_SLAB = D_A // LANE
HALO = 8
TOKEN_TILE = 256
CPT = TOKEN_TILE // CHUNK
OUT_TILE = 512
SCAN_TILE = 1024
MXU_TILE = 256
V7X_VMEM_BYTES = 64 * 1024 * 1024
VMEM_LIMIT = V7X_VMEM_BYTES - 4 * 1024 * 1024

_O_R, _O_K, _O_V, _O_XW, _O_XA, _O_XG, _O_XB, _O_GB = 0, 512, 1024, 1536, 1664, 1792, 1920, 2432

_M_STRICT, _M_INCL, _M_LEV0 = 0, 1, 2
_LEVELS = (1, 2, 4, 8, 16, 32)
_M_PER_DIR = 2 + len(_LEVELS)
_M_EYE = 2 * _M_PER_DIR


def _dot(a, b):
    return jnp.dot(a.astype(BF16), b.astype(BF16), preferred_element_type=F32)


def _dot_nt(a, b):
    return lax.dot_general(a.astype(BF16), b.astype(BF16), (((1,), (1,)), ((), ())),
                           preferred_element_type=F32)


def _dot_tn(a, b):
    return lax.dot_general(a.astype(BF16), b.astype(BF16), (((0,), (0,)), ((), ())),
                           preferred_element_type=F32)


def _strided_rows(ref, lead, base):
    return [ref[lead + (pl.ds(base + j, SUB, stride=SUB), slice(None))] for j in range(SUB)]


def _natural_rows(ref, lead, base, pieces):
    for j, piece in enumerate(pieces):
        ref[lead + (pl.ds(base + SUB * j, SUB), slice(None))] = piece
    return jnp.concatenate(_strided_rows(ref, lead, base), axis=0)


def _sublane_shift(x, steps, reverse, fill):
    sub = lax.broadcasted_iota(jnp.int32, x.shape, 0)
    if reverse:
        return jnp.where(sub < SUB - steps, pltpu.roll(x, SUB - steps, 0), fill)
    return jnp.where(sub >= steps, pltpu.roll(x, steps, 0), fill)


def _scan_affine(a, b, reverse):
    a, b = list(a), list(b)
    order = range(SUB - 2, -1, -1) if reverse else range(1, SUB)
    for j in order:
        p = j + 1 if reverse else j - 1
        b[j] = a[j] * b[p] + b[j]
        a[j] = a[j] * a[p]
    ta, tb = (a[0], b[0]) if reverse else (a[SUB - 1], b[SUB - 1])
    s = 1
    while s < SUB:
        tb = ta * _sublane_shift(tb, s, reverse, 0.0) + tb
        ta = ta * _sublane_shift(ta, s, reverse, 1.0)
        s *= 2
    ea = _sublane_shift(ta, 1, reverse, 1.0)
    eb = _sublane_shift(tb, 1, reverse, 0.0)
    return [x * ea for x in a], [x * eb + y for x, y in zip(a, b)]


def _scan_sum(x, reverse):
    x = list(x)
    order = range(SUB - 2, -1, -1) if reverse else range(1, SUB)
    for j in order:
        x[j] = x[j] + x[j + 1 if reverse else j - 1]
    t = x[0] if reverse else x[SUB - 1]
    s = 1
    while s < SUB:
        t = t + _sublane_shift(t, s, reverse, 0.0)
        s *= 2
    e = _sublane_shift(t, 1, reverse, 0.0)
    return [y + e for y in x]


def _sigmoid(x):
    return 0.5 * jnp.tanh(0.5 * x) + 0.5


def _blockdiag(x, bdm):
    xb = x.astype(BF16)
    return jnp.concatenate([xb] * HEADS_PER_GROUP, axis=0) * bdm


def _mod_kernel(c_ref, w_ref, b_ref, o_ref):
    c = c_ref[...]
    s = c * _sigmoid(c)
    o_ref[...] = _dot(s, w_ref[...]) + b_ref[...]


def _mod_call(c_all, w_mod, b_mod):
    m = c_all.shape[0]
    n = w_mod.shape[1]
    tn = 1536
    return pl.pallas_call(
        _mod_kernel,
        grid=(n // tn,),
        in_specs=[pl.BlockSpec((m, D_MODEL), lambda j: (0, 0)),
                  pl.BlockSpec((D_MODEL, tn), lambda j: (0, j)),
                  pl.BlockSpec((1, tn), lambda j: (0, j))],
        out_specs=pl.BlockSpec((m, tn), lambda j: (0, j)),
        out_shape=jax.ShapeDtypeStruct((m, n), F32),
        compiler_params=pltpu.CompilerParams(dimension_semantics=("parallel",),
                                             vmem_limit_bytes=VMEM_LIMIT),
        name="mod",
    )(c_all, w_mod, b_mod.reshape(1, n))


def _chunk_masks():
    t = np.arange(CHUNK)[:, None]
    s = (np.arange(GROUP) % CHUNK)[None, :]
    rows = []
    for d in (0, 1):
        before = (s < t) if d == 0 else (s > t)
        rows.append(before)
        rows.append(before | (s == t))
        for b in _LEVELS:
            same = (t // (2 * b)) == (s // (2 * b))
            if d == 0:
                rows.append(same & ((t // b) % 2 == 1) & ((s // b) % 2 == 0))
            else:
                rows.append(same & ((t // b) % 2 == 0) & ((s // b) % 2 == 1))
    rows.append(s == t)
    return np.stack(rows).astype(np.float32)


def _group_blockdiag_mask():
    i = np.arange(GROUP)
    return ((i[:, None] // HEAD) == (i[None, :] // HEAD)).astype(np.float32)


def _feat_kernel(has_pos, nt, *refs):
    if has_pos:
        x_ref, xp_ref, xn_ref, pos_ref, pp_ref, pn_ref = refs[:6]
        refs = refs[6:]
    else:
        x_ref, xp_ref, xn_ref = refs[:3]
        pos_ref = pp_ref = pn_ref = None
        refs = refs[3:]
    (mod_ref, gpre_ref, win_ref, wup_ref, w0_ref, aup_ref, a0_ref, gup_ref,
     kk_ref, ka_ref, rk_ref, seg_ref,
     convw_ref, convb_ref, wa_ref, ba_ref, wx_ref, bx_ref, lam_ref,
     masks_ref, bdm_ref,
     p_o, q_o, m_o, n_o, gt_o, ac_o, bs_o, bl_o, al_o, g_o, bonus_o, gate_o,
     lw_s, lc_s, la_s, lb_s, lo_s) = refs

    i = pl.program_id(0)
    tm = TOKEN_TILE
    mod = mod_ref[0]
    shift1, scale1 = mod[0:1], mod[1:2]

    def normmod(xv):
        ms = jnp.mean(xv * xv, axis=-1, keepdims=True)
        hv = (xv * lax.rsqrt(ms + EPS)) * gpre_ref[...]
        return hv * (1.0 + scale1) + shift1

    x = x_ref[0]
    halo = jnp.concatenate([xp_ref[0], xn_ref[0]], axis=0)
    if has_pos:
        x = x + pos_ref[...]
        halo = halo + jnp.concatenate([pp_ref[...], pn_ref[...]], axis=0)
    z = _dot(normmod(x), win_ref[...])
    zh = _dot(normmod(halo), win_ref[:, _O_XB:_O_XB + D_B])

    r = z[:, _O_R:_O_R + D_A]
    k = z[:, _O_K:_O_K + D_A]
    v = z[:, _O_V:_O_V + D_A]
    xw = z[:, _O_XW:_O_XW + 2 * R_W]
    xa = z[:, _O_XA:_O_XA + 2 * R_A]
    xg = z[:, _O_XG:_O_XG + R_G]

    g_o[0] = _dot(_sigmoid(xg), gup_ref[...])
    gate_o[0] = jax.nn.gelu(z[:, _O_GB:_O_GB + D_B], approximate=True)
    wl = w0_ref[...] + _dot(jnp.tanh(xw), wup_ref[...])
    lw2 = -_sigmoid(wl) * float(np.exp(-0.5))
    a2 = _sigmoid(a0_ref[...] + _dot(xa, aup_ref[...]))

    kks = k * kk_ref[...]
    ss = _dot(kks * kks, seg_ref[...])
    kk = kks * lax.rsqrt(jnp.maximum(ss, 1e-24))
    ka = ka_ref[...]
    kd2 = [k * (1.0 + (a2[:, d * D_A:(d + 1) * D_A] - 1.0) * ka) for d in (0, 1)]
    bonus_o[0] = _dot(r * (kd2[0] + kd2[1]) * rk_ref[...], seg_ref[...]) * v

    m_prev = jnp.where(i > 0, 1.0, 0.0)
    m_next = jnp.where(i < nt - 1, 1.0, 0.0)
    ext = jnp.concatenate([zh[:HALO] * m_prev, z[:, _O_XB:_O_XB + D_B], zh[HALO:] * m_next], axis=0)
    n_ext = tm + 2 * HALO
    xc = convb_ref[...]
    for j in range(CONV_W):
        sh = (2 - j) % n_ext
        tap = ext if sh == 0 else pltpu.roll(ext, sh, 0)
        xc = xc + tap[HALO:HALO + tm] * convw_ref[j:j + 1, :]
    for d in (0, 1):
        rg = jnp.concatenate([_dot(xc[:, g * GROUP:(g + 1) * GROUP], wa_ref[d, g]) for g in range(N_GROUP)], 1)
        ig = jnp.concatenate([_dot(xc[:, g * GROUP:(g + 1) * GROUP], wx_ref[d, g]) for g in range(N_GROUP)], 1)
        rg = _sigmoid(rg + ba_ref[d:d + 1, :])
        ig = _sigmoid(ig + bx_ref[d:d + 1, :])
        log_a = -LRU_C * rg * jax.nn.softplus(-lam_ref[d:d + 1, :])
        a_lru = jnp.exp(log_a)
        b_lru = jnp.sqrt(-jnp.tanh(log_a) * (a_lru * a_lru + 1.0)) * (ig * xc)
        for q in range(N_SLAB):
            la_s[d, q] = a_lru[:, q * LANE:(q + 1) * LANE]
            lb_s[d, q] = b_lru[:, q * LANE:(q + 1) * LANE]
    for c in range(CPT):
        base = c * CHUNK
        rs = slice(base, base + CHUNK)
        for q in range(N_SLAB):
            ls = slice(q * LANE, (q + 1) * LANE)
            bsum = None
            for d in (0, 1):
                acum, bcum = _scan_affine(_strided_rows(la_s, (d, q), base), _strided_rows(lb_s, (d, q), base),
                                          d == 1)
                ac_o[d, 0, rs, ls] = _natural_rows(lo_s, (d, q), base, acum)
                bsum = bcum if bsum is None else [x + y for x, y in zip(bsum, bcum)]
                bl_o[d, 0, c, :, ls] = bcum[0][0:1, :] if d == 1 else bcum[SUB - 1][SUB - 1:SUB, :]
                al_o[d, 0, c, :, ls] = acum[0][0:1, :] if d == 1 else acum[SUB - 1][SUB - 1:SUB, :]
            bs_o[0, rs, ls] = _natural_rows(lo_s, (2, q), base, bsum)

    bdm = bdm_ref[...]
    bd = lambda xv: _blockdiag(xv, bdm)
    bdot = lambda lhs, rhs: jnp.dot(lhs.astype(BF16), bd(rhs), preferred_element_type=F32)

    a_t, r_t, b_t, k_t, g_tot = [], [], [], [], []
    for d in (0, 1):
        lw = lw2[:, d * D_A:(d + 1) * D_A]
        for q in range(N_SLAB):
            lw_s[d, q] = lw[:, q * LANE:(q + 1) * LANE]
        g_tot.append([])
        lc_rows = []
        for c in range(CPT):
            base = c * CHUNK
            blocks, totals = [], []
            for q in range(N_SLAB):
                pieces = _scan_sum(_strided_rows(lw_s, (d, q), base), d == 1)
                totals.append(pieces[0][0:1, :] if d == 1 else pieces[SUB - 1][SUB - 1:SUB, :])
                blocks.append(_natural_rows(lc_s, (d, q), base, pieces))
            lc_rows.append(jnp.concatenate(blocks, axis=1))
            g_tot[d].append(jnp.exp(jnp.concatenate(totals, axis=1)))
            gt_o[d, 0, c] = g_tot[d][c]
        lc = jnp.concatenate(lc_rows, axis=0)
        e_neg = jnp.exp(-lc)
        a_t.append(-kk * jnp.exp(lc - lw))
        r_t.append(r * jnp.exp(lc))
        b_t.append(kk * a2[:, d * D_A:(d + 1) * D_A] * e_neg)
        k_t.append(kd2[d] * e_neg)

    combos = [(d, c, g) for d in (0, 1) for c in range(CPT) for g in range(N_GROUP)]

    def cut(arr, c, g):
        return arr[c * CHUNK:(c + 1) * CHUNK, g * GROUP:(g + 1) * GROUP]

    sc = {}
    for key in combos:
        d, c, g = key
        ar = jnp.concatenate([cut(a_t[d], c, g), cut(r_t[d], c, g)], axis=0)
        rhs = jnp.concatenate([bd(cut(b_t[d], c, g)), bd(cut(k_t[d], c, g))], axis=0)
        sc[key] = _dot_nt(ar, rhs)
    n_ab, n_ak, n_rb, n_rk, tinv = {}, {}, {}, {}, {}
    for key in combos:
        m0 = key[0] * _M_PER_DIR
        n_ab[key] = sc[key][:CHUNK, :GROUP] * masks_ref[m0 + _M_STRICT]
        n_ak[key] = sc[key][:CHUNK, GROUP:] * masks_ref[m0 + _M_STRICT]
        n_rb[key] = sc[key][CHUNK:, :GROUP] * masks_ref[m0 + _M_INCL]
        n_rk[key] = sc[key][CHUNK:, GROUP:] * masks_ref[m0 + _M_INCL]
        tinv[key] = masks_ref[_M_EYE] + n_ab[key] * masks_ref[m0 + _M_LEV0]
    for li in range(1, len(_LEVELS)):
        pm = {}
        for key in combos:
            pm[key] = bdot(n_ab[key] * masks_ref[key[0] * _M_PER_DIR + _M_LEV0 + li], tinv[key])
        for key in combos:
            tinv[key] = tinv[key] + bdot(tinv[key], pm[key])

    kv = {key: bdot(jnp.concatenate([n_ak[key], n_rk[key]], axis=0), cut(v, key[1], key[2])) for key in combos}
    gm = {key: bdot(n_rb[key], tinv[key]) for key in combos}
    wp = {}
    for key in combos:
        d, c, g = key
        lhs = jnp.concatenate([tinv[key], gm[key]], axis=0)
        rhs = jnp.concatenate([bd(cut(a_t[d], c, g)), bd(kv[key][:CHUNK])], axis=1)
        wp[key] = jnp.dot(lhs.astype(BF16), rhs, preferred_element_type=F32)

    lane = lax.broadcasted_iota(jnp.int32, (CHUNK, GROUP), 1)

    def fold(full):
        out = full[(HEADS_PER_GROUP - 1) * HEAD:]
        for h in range(HEADS_PER_GROUP - 2, -1, -1):
            out = jnp.where(lane < (h + 1) * HEAD, full[h * HEAD:(h + 1) * HEAD], out)
        return out

    for key in combos:
        d, c, g = key
        rs = slice(c * CHUNK, (c + 1) * CHUNK)
        ls = slice(g * GROUP, (g + 1) * GROUP)
        w_, u0 = wp[key][:CHUNK, :GROUP], wp[key][:CHUNK, GROUP:]
        vg = cut(v, c, g)
        gt = g_tot[d][c][:, ls]
        bh, kh = cut(b_t[d], c, g) * gt, cut(k_t[d], c, g) * gt
        p_o[d, 0, rs, ls] = (cut(r_t[d], c, g) + wp[key][CHUNK:, :GROUP]).astype(BF16)
        m_o[d, 0, rs, ls] = fold(_dot_tn(w_, bh)).astype(BF16)
        n_o[d, 0, rs, ls] = fold(_dot_tn(jnp.concatenate([u0, vg], axis=0), jnp.concatenate([bh, kh], axis=0)))
        if d == 1:
            other = (0, c, g)
            q_o[0, rs, ls] = ((wp[other][CHUNK:, GROUP:] + kv[other][CHUNK:])
                              + (wp[key][CHUNK:, GROUP:] + kv[key][CHUNK:]))


def _feat_call(x, pos, mod, mod_row, p):
    bsz, t, _ = x.shape
    tm = TOKEN_TILE
    nt = t // tm
    nc = t // CHUNK
    hpt = tm // HALO
    has_pos = pos is not None

    def const(shape):
        return pl.BlockSpec(shape, lambda i, b: (0,) * len(shape))

    tok = lambda width: pl.BlockSpec((1, tm, width), lambda i, b: (b, i, 0))
    prev_i = lambda i: jnp.maximum(i * hpt - 1, 0)
    next_i = lambda i: jnp.minimum((i + 1) * hpt, t // HALO - 1)
    in_specs = [tok(D_MODEL),
                pl.BlockSpec((1, HALO, D_MODEL), lambda i, b: (b, prev_i(i), 0)),
                pl.BlockSpec((1, HALO, D_MODEL), lambda i, b: (b, next_i(i), 0))]
    args = [x, x, x]
    if has_pos:
        in_specs += [pl.BlockSpec((tm, D_MODEL), lambda i, b: (i, 0)),
                     pl.BlockSpec((HALO, D_MODEL), lambda i, b: (prev_i(i), 0)),
                     pl.BlockSpec((HALO, D_MODEL), lambda i, b: (next_i(i), 0))]
        args += [pos, pos, pos]
    n_masks = 2 * _M_PER_DIR + 1
    in_specs += [
        pl.BlockSpec((1, 6, D_MODEL), lambda i, b: (mod_row(b), 0, 0)),
        const((1, D_MODEL)), const((D_MODEL, D_IN)),
        const((2 * R_W, 2 * D_A)), const((1, 2 * D_A)),
        const((2 * R_A, 2 * D_A)), const((1, 2 * D_A)),
        const((R_G, D_A)),
        const((1, D_A)), const((1, D_A)), const((1, D_A)), const((D_A, D_A)),
        const((CONV_W, D_B)), const((1, D_B)),
        const((2, N_GROUP, GROUP, GROUP)), const((2, D_B)),
        const((2, N_GROUP, GROUP, GROUP)), const((2, D_B)), const((2, D_B)),
        const((n_masks, CHUNK, GROUP)), const((GROUP, GROUP)),
    ]
    args += [mod, p["g_pre_mix"], p["w_in"], p["wup_bd"], p["w0"], p["aup_bd"], p["a0"], p["g_up"],
             p["k_k"], p["k_a"], p["r_k"], p["seg_ones"],
             p["conv_w"], p["conv_b"], p["wa_bd"], p["ba"], p["wx_bd"], p["bx"], p["lam"],
             p["chunk_masks"], p["bdm_bf16"]]
    tok_shape = jax.ShapeDtypeStruct((bsz, t, D_A), F32)
    dir_shape = jax.ShapeDtypeStruct((2, bsz, t, D_A), F32)
    row_shape = jax.ShapeDtypeStruct((2, bsz, nc, 1, D_A), F32)
    mxu_shape = jax.ShapeDtypeStruct((2, bsz, t, D_A), BF16)
    dir_spec = pl.BlockSpec((2, 1, tm, D_A), lambda i, b: (0, b, i, 0))
    row_spec = pl.BlockSpec((2, 1, CPT, 1, D_A), lambda i, b: (0, b, i, 0, 0))
    out_shape = [mxu_shape, tok_shape, mxu_shape, dir_shape, row_shape, dir_shape, tok_shape, row_shape, row_shape,
                 tok_shape, tok_shape, tok_shape]
    out_specs = [dir_spec, tok(D_A), dir_spec, dir_spec, row_spec, dir_spec, tok(D_B), row_spec, row_spec,
                 tok(D_A), tok(D_A), tok(D_B)]
    return pl.pallas_call(
        functools.partial(_feat_kernel, has_pos, nt),
        grid=(nt, bsz),
        in_specs=in_specs,
        out_specs=out_specs,
        out_shape=out_shape,
        scratch_shapes=[pltpu.VMEM((2, N_SLAB, tm, LANE), F32)] * 4 + [pltpu.VMEM((3, N_SLAB, tm, LANE), F32)],
        compiler_params=pltpu.CompilerParams(dimension_semantics=("parallel", "parallel"),
                                             vmem_limit_bytes=VMEM_LIMIT),
        name="feat",
    )(*args)


def _scan_kernel(nt, tile, *refs):
    (pf_ref, mf_ref, nf_ref, gtf_ref, alf_ref, blf_ref,
     pb_ref, mb_ref, nb_ref, gtb_ref, alb_ref, blb_ref,
     s0_ref, l0_ref, bdm_ref,
     y_ref, hc_ref, s_ref, hl_ref) = refs

    i = pl.program_id(1)
    cpt = tile // CHUNK

    @pl.when(i == 0)
    def _init():
        y_ref[...] = jnp.zeros_like(y_ref)
        s_ref[...] = s0_ref[...]
        hl_ref[...] = l0_ref[...]

    bdm = bdm_ref[...]
    per_dir = ((0, pf_ref, mf_ref, nf_ref, gtf_ref, alf_ref, blf_ref),
               (1, pb_ref, mb_ref, nb_ref, gtb_ref, alb_ref, blb_ref))
    tiles = (i, nt - 1 - i)
    state = [s_ref[0, d] for d in (0, 1)]
    h0 = [hl_ref[0, d:d + 1, :] for d in (0, 1)]
    for cc in range(cpt):
        for d, p_ref, m_ref, n_ref, gt_ref, al_ref, bl_ref in per_dir:
            c = cc if d == 0 else cpt - 1 - cc
            rs = slice(c * CHUNK, (c + 1) * CHUNK)
            rows = pl.ds(pl.multiple_of(tiles[d] * tile + c * CHUNK, CHUNK), CHUNK)
            gt = gt_ref[0, 0, c]
            y_parts, s_parts = [], []
            for g in range(N_GROUP):
                ls = slice(g * GROUP, (g + 1) * GROUP)
                sg = state[d][:, ls]
                y_parts.append(_dot_nt(p_ref[0, 0, rs, ls], _blockdiag(sg, bdm)))
                s_parts.append(sg * gt[:, ls]
                               + jnp.dot(sg.astype(BF16), _blockdiag(m_ref[0, 0, rs, ls], bdm),
                                         preferred_element_type=F32)
                               + n_ref[0, 0, rs, ls])
            state[d] = jnp.concatenate(s_parts, axis=1)
            y_ref[0, rows, :] += jnp.concatenate(y_parts, axis=1)
            hc_ref[d, 0, tiles[d] * cpt + c] = h0[d]
            h0[d] = bl_ref[0, 0, c] + al_ref[0, 0, c] * h0[d]
    for d in (0, 1):
        s_ref[0, d] = state[d]
        hl_ref[0, d:d + 1, :] = h0[d]


def _scan_call(feats, s0, l0, p):
    pm, mm, nm, gt, al, bl = feats
    _, bsz, t, _ = pm.shape
    tile = min(SCAN_TILE, t)
    nt = t // tile
    nc = t // CHUNK

    def dir_specs(d, tile_of):
        big = pl.BlockSpec((1, 1, tile, D_A), lambda b, i: (d, b, tile_of(i), 0))
        small = pl.BlockSpec((1, 1, tile // CHUNK, 1, D_A), lambda b, i: (d, b, tile_of(i), 0, 0))
        return big, small

    big_f, small_f = dir_specs(0, lambda i: i)
    big_b, small_b = dir_specs(1, lambda i: nt - 1 - i)
    in_specs = [big_f, big_f, big_f, small_f, small_f, small_f,
                big_b, big_b, big_b, small_b, small_b, small_b,
                pl.BlockSpec((1, 2, HEAD, D_A), lambda b, i: (b, 0, 0, 0)),
                pl.BlockSpec((1, 2, D_B), lambda b, i: (b, 0, 0)),
                pl.BlockSpec((GROUP, GROUP), lambda b, i: (0, 0))]
    args = [pm, mm, nm, gt, al, bl, pm, mm, nm, gt, al, bl, s0, l0, p["bdm_bf16"]]
    out_specs = [pl.BlockSpec((1, t, D_A), lambda b, i: (b, 0, 0)),
                 pl.BlockSpec((2, 1, nc, 1, D_B), lambda b, i: (0, b, 0, 0, 0)),
                 pl.BlockSpec((1, 2, HEAD, D_A), lambda b, i: (b, 0, 0, 0)),
                 pl.BlockSpec((1, 2, D_B), lambda b, i: (b, 0, 0))]
    out_shape = [jax.ShapeDtypeStruct((bsz, t, D_A), F32),
                 jax.ShapeDtypeStruct((2, bsz, nc, 1, D_B), F32),
                 jax.ShapeDtypeStruct((bsz, 2, HEAD, D_A), F32),
                 jax.ShapeDtypeStruct((bsz, 2, D_B), F32)]
    return pl.pallas_call(
        functools.partial(_scan_kernel, nt, tile),
        grid=(bsz, nt),
        in_specs=in_specs,
        out_specs=out_specs,
        out_shape=out_shape,
        compiler_params=pltpu.CompilerParams(dimension_semantics=("parallel", "arbitrary"),
                                             vmem_limit_bytes=VMEM_LIMIT),
        name="scan",
    )(*args)


def _out_kernel(has_pos, *refs):
    if has_pos:
        x_ref, pos_ref = refs[0], refs[1]
        refs = refs[2:]
    else:
        x_ref, pos_ref = refs[0], None
        refs = refs[1:]
    (y_ref, q_ref, ac_ref, hc_ref, bs_ref, g_ref, bonus_ref, gate_ref, mod_ref,
     avg_ref, lnxg_ref, lnxb_ref, wout_ref, gpost_ref, gpre2_ref, w1_ref, w2_ref, gpost2_ref,
     o_ref) = refs

    x = x_ref[0]
    if has_pos:
        x = x + pos_ref[...]
    mod = mod_ref[0]
    gate1, shift2, scale2, gate2 = mod[2:3], mod[3:4], mod[4:5], mod[5:6]

    y = y_ref[0] + q_ref[0]
    avg = avg_ref[...]

    def head_mean(a):
        return jnp.concatenate([_dot(a[:, t * MXU_TILE:(t + 1) * MXU_TILE], avg)
                                for t in range(D_A // MXU_TILE)], axis=1)

    y_hi = y.astype(BF16)
    y_lo = (y - y_hi.astype(F32)).astype(BF16)
    mu = head_mean(y_hi) + head_mean(y_lo)
    yc = y - mu
    var = head_mean(yc * yc)
    yn = yc * lax.rsqrt(var + LNX_EPS) * lnxg_ref[...] + lnxb_ref[...]
    out_a = (yn + bonus_ref[0]) * g_ref[0]
    hs = jnp.concatenate(
        [bs_ref[0, c * CHUNK:(c + 1) * CHUNK, :]
         + ac_ref[0, 0, c * CHUNK:(c + 1) * CHUNK, :] * hc_ref[0, 0, c]
         + ac_ref[1, 0, c * CHUNK:(c + 1) * CHUNK, :] * hc_ref[1, 0, c]
         for c in range(x_ref.shape[1] // CHUNK)], axis=0)
    out_b = hs * gate_ref[0]
    mix = _dot(jnp.concatenate([out_a, out_b], axis=1), wout_ref[...])
    ms = jnp.mean(mix * mix, axis=-1, keepdims=True)
    x = x + gate1 * ((mix * lax.rsqrt(ms + EPS)) * gpost_ref[...])

    ms = jnp.mean(x * x, axis=-1, keepdims=True)
    h = (x * lax.rsqrt(ms + EPS)) * gpre2_ref[...]
    h = h * (1.0 + scale2) + shift2
    f = _dot(h, w1_ref[...])
    f = jnp.square(jnp.maximum(f, 0.0))
    f = _dot(f, w2_ref[...])
    ms = jnp.mean(f * f, axis=-1, keepdims=True)
    o_ref[0] = x + gate2 * ((f * lax.rsqrt(ms + EPS)) * gpost2_ref[...])


def _out_call(x, pos, mod, mod_row, y, q, ac, hc, bs, g, bonus, gate, p):
    bsz, t, _ = x.shape
    tm = min(OUT_TILE, t)
    nt = t // tm
    has_pos = pos is not None

    def const(shape):
        return pl.BlockSpec(shape, lambda i, b: (0,) * len(shape))

    tok = lambda width: pl.BlockSpec((1, tm, width), lambda i, b: (b, i, 0))
    in_specs = [tok(D_MODEL)]
    args = [x]
    if has_pos:
        in_specs.append(pl.BlockSpec((tm, D_MODEL), lambda i, b: (i, 0)))
        args.append(pos)
    in_specs += [tok(D_A), tok(D_A),
                 pl.BlockSpec((2, 1, tm, D_B), lambda i, b: (0, b, i, 0)),
                 pl.BlockSpec((2, 1, tm // CHUNK, 1, D_B), lambda i, b: (0, b, i, 0, 0))]
    in_specs += [tok(D_A)] * 4
    in_specs += [
        pl.BlockSpec((1, 6, D_MODEL), lambda i, b: (mod_row(b), 0, 0)),
        const((MXU_TILE, MXU_TILE)), const((1, D_A)), const((1, D_A)),
        const((D_MODEL, D_MODEL)), const((1, D_MODEL)), const((1, D_MODEL)),
        const((D_MODEL, D_FF)), const((D_FF, D_MODEL)), const((1, D_MODEL)),
    ]
    args += [y, q, ac, hc, bs, g, bonus, gate, mod,
             p["seg_avg"], p["lnx_g"], p["lnx_b"], p["w_out"], p["g_post_mix"], p["g_pre_mlp"],
             p["w_mlp1"], p["w_mlp2"], p["g_post_mlp"]]
    return pl.pallas_call(
        functools.partial(_out_kernel, has_pos),
        grid=(nt, bsz),
        in_specs=in_specs,
        out_specs=tok(D_MODEL),
        out_shape=jax.ShapeDtypeStruct((bsz, t, D_MODEL), F32),
        compiler_params=pltpu.CompilerParams(dimension_semantics=("parallel", "parallel"),
                                             vmem_limit_bytes=VMEM_LIMIT),
        name="out",
    )(*args)


def _sincos_1d(pos, dim):
    omega = 1.0 / (10000.0 ** (jnp.arange(dim // 2, dtype=F32) / (dim // 2)))
    ang = pos.astype(F32)[:, None] * omega[None, :]
    return jnp.concatenate([jnp.sin(ang), jnp.cos(ang)], axis=-1)


def _grid_pos_embed(n_tokens):
    rows = n_tokens // GRID_W
    half = D_MODEL // 2
    e_row = _sincos_1d(jnp.arange(rows), half)
    e_col = _sincos_1d(jnp.arange(GRID_W), half)
    emb = jnp.concatenate([jnp.broadcast_to(e_row[:, None, :], (rows, GRID_W, half)),
                           jnp.broadcast_to(e_col[None, :, :], (rows, GRID_W, half))], axis=-1)
    return emb.reshape(rows * GRID_W, D_MODEL)


def _blockdiag_pairs(w):
    z = jnp.zeros_like(w[0])
    return jnp.concatenate([jnp.concatenate([w[0], z], axis=1),
                            jnp.concatenate([z, w[1]], axis=1)], axis=0)


def _heads_to_blockdiag(w):
    lead = w.shape[:-3]
    w = w.reshape(lead + (N_GROUP, HEADS_PER_GROUP, HEAD, HEAD))
    eye = jnp.eye(HEADS_PER_GROUP, dtype=w.dtype)
    bd = jnp.einsum('...ghab,hj->...ghajb', w, eye)
    return bd.reshape(lead + (N_GROUP, GROUP, GROUP))


def _state_to_lanes(s):
    b = s.shape[0]
    return jnp.transpose(s, (0, 1, 3, 2, 4)).reshape(b, 2, HEAD, D_A)


def _state_from_lanes(s):
    b = s.shape[0]
    return jnp.transpose(s.reshape(b, 2, HEAD, N_HEAD, HEAD), (0, 1, 3, 2, 4))


def kernel(x_prompt, x_sample, c, state_rwkv, state_lru, c_ctx, w_mod, b_mod, g_pre_mix, g_post_mix,
           g_pre_mlp, g_post_mlp, w_in, rwkv_w0, rwkv_w_up, rwkv_a0, rwkv_a_up, rwkv_g_up, rwkv_k_k,
           rwkv_k_a, rwkv_r_k, rwkv_lnx_g, rwkv_lnx_b, lru_conv_w, lru_conv_b, lru_wa, lru_ba, lru_wx,
           lru_bx, lru_lambda, w_out, w_mlp1, w_mlp2):
    n_ctx = x_prompt.shape[0]
    n_lat = x_sample.shape[0]
    l = 0
    seg = _group_blockdiag_mask()
    seg512 = np.kron(np.eye(N_GROUP, dtype=np.float32), seg)
    p = {
        "g_pre_mix": g_pre_mix[l][None], "g_post_mix": g_post_mix[l][None],
        "g_pre_mlp": g_pre_mlp[l][None], "g_post_mlp": g_post_mlp[l][None],
        "w_in": w_in[l].astype(BF16), "w_out": w_out[l].astype(BF16),
        "w_mlp1": w_mlp1[l].astype(BF16), "w_mlp2": w_mlp2[l].astype(BF16),
        "w0": rwkv_w0[l].reshape(1, 2 * D_A), "a0": rwkv_a0[l].reshape(1, 2 * D_A),
        "wup_bd": _blockdiag_pairs(rwkv_w_up[l]).astype(BF16),
        "aup_bd": _blockdiag_pairs(rwkv_a_up[l]).astype(BF16),
        "g_up": rwkv_g_up[l].astype(BF16),
        "k_k": rwkv_k_k[l][None], "k_a": rwkv_k_a[l][None], "r_k": rwkv_r_k[l].reshape(1, D_A),
        "lnx_g": rwkv_lnx_g[l][None], "lnx_b": rwkv_lnx_b[l][None],
        "conv_w": lru_conv_w[l], "conv_b": lru_conv_b[l][None],
        "wa_bd": _heads_to_blockdiag(lru_wa[l]).astype(BF16), "ba": lru_ba[l],
        "wx_bd": _heads_to_blockdiag(lru_wx[l]).astype(BF16), "bx": lru_bx[l],
        "lam": lru_lambda[l],
        "seg_ones": jnp.asarray(seg512, BF16),
        "seg_avg": jnp.asarray(np.kron(np.eye(MXU_TILE // GROUP, dtype=np.float32), seg) / HEAD, BF16),
        "chunk_masks": jnp.asarray(_chunk_masks()),
        "bdm_bf16": jnp.asarray(seg, BF16),
    }

    m_rows = 16
    c_all = jnp.concatenate([c_ctx[None], c, jnp.zeros((m_rows - 1 - n_lat, D_MODEL), F32)], axis=0)
    mod = _mod_call(c_all, w_mod[l], b_mod[l]).reshape(m_rows, 6, D_MODEL)

    pos = _grid_pos_embed(x_sample.shape[1]).astype(x_sample.dtype)
    ctx_row = lambda b: 0
    lat_row = lambda b: b + 1

    pm, q, mm, nm, gt, ac, bs, bl, al, g, bonus, gate = _feat_call(x_prompt, None, mod, ctx_row, p)
    y, hc, s_ctx, l_ctx = _scan_call(
        (pm, mm, nm, gt, al, bl), jnp.zeros((n_ctx, 2, HEAD, D_A), F32), jnp.zeros((n_ctx, 2, D_B), F32), p)
    y_prompt = _out_call(x_prompt, None, mod, ctx_row, y, q, ac, hc, bs, g, bonus, gate, p)

    pm, q, mm, nm, gt, ac, bs, bl, al, g, bonus, gate = _feat_call(x_sample, pos, mod, lat_row, p)
    y, hc, _, _ = _scan_call((pm, mm, nm, gt, al, bl), _state_to_lanes(state_rwkv[:, l]), state_lru[:, l], p)
    y_sample = _out_call(x_sample, pos, mod, lat_row, y, q, ac, hc, bs, g, bonus, gate, p)

    new_state_rwkv = _state_from_lanes(s_ctx)[:, None].astype(x_prompt.dtype)
    new_state_lru = l_ctx[:, None].astype(x_prompt.dtype)
    return (y_prompt, y_sample, new_state_rwkv, new_state_lru)
```

```python
import functools

import numpy as np
import jax
import jax.numpy as jnp
from jax import lax
from jax.experimental import pallas as pl
from jax.experimental.pallas import tpu as pltpu

F32 = jnp.float32
BF16 = jnp.bfloat16

D_MODEL = 1024
D_A = 512
D_B = 512
HEAD = 64
N_HEAD = 8
R_W = 64
R_A = 64
R_G = 128
D_FF = 4096
D_IN = 2944
GRID_W = 64
CONV_W = 4
LRU_C = 8.0
EPS = 1e-6
LNX_EPS = 64e-5

CHUNK = 64
GROUP = 128
HEADS_PER_GROUP = GROUP // HEAD
N_GROUP = D_A // GROUP
SUB = 8
LANE = 128
N_SLAB = D_A // LANE
HALO = 8
TOKEN_TILE = 256
CPT = TOKEN_TILE // CHUNK
OUT_TILE = 512
SCAN_TILE = 1024
MXU_TILE = 256
V7X_VMEM_BYTES = 64 * 1024 * 1024
VMEM_LIMIT = V7X_VMEM_BYTES - 4 * 1024 * 1024

_O_R, _O_K, _O_V, _O_XW, _O_XA, _O_XG, _O_XB, _O_GB = 0, 512, 1024, 1536, 1664, 1792, 1920, 2432

_M_STRICT, _M_INCL, _M_LEV0 = 0, 1, 2
_LEVELS = (1, 2, 4, 8, 16, 32)
_M_PER_DIR = 2 + len(_LEVELS)
_M_EYE = 2 * _M_PER_DIR


def _dot(a, b):
    return jnp.dot(a.astype(BF16), b.astype(BF16), preferred_element_type=F32)


def _dot_nt(a, b):
    return lax.dot_general(a.astype(BF16), b.astype(BF16), (((1,), (1,)), ((), ())),
                           preferred_element_type=F32)


def _dot_tn(a, b):
    return lax.dot_general(a.astype(BF16), b.astype(BF16), (((0,), (0,)), ((), ())),
                           preferred_element_type=F32)


def _strided_rows(ref, lead, base):
    return [ref[lead + (pl.ds(base + j, SUB, stride=SUB), slice(None))] for j in range(SUB)]


def _natural_rows(ref, lead, base, pieces):
    for j, piece in enumerate(pieces):
        ref[lead + (pl.ds(base + SUB * j, SUB), slice(None))] = piece
    return jnp.concatenate(_strided_rows(ref, lead, base), axis=0)


def _sublane_shift(x, steps, reverse, fill):
    sub = lax.broadcasted_iota(jnp.int32, x.shape, 0)
    if reverse:
        return jnp.where(sub < SUB - steps, pltpu.roll(x, SUB - steps, 0), fill)
    return jnp.where(sub >= steps, pltpu.roll(x, steps, 0), fill)


def _scan_affine(a, b, reverse):
    a, b = list(a), list(b)
    order = range(SUB - 2, -1, -1) if reverse else range(1, SUB)
    for j in order:
        p = j + 1 if reverse else j - 1
        b[j] = a[j] * b[p] + b[j]
        a[j] = a[j] * a[p]
    ta, tb = (a[0], b[0]) if reverse else (a[SUB - 1], b[SUB - 1])
    s = 1
    while s < SUB:
        tb = ta * _sublane_shift(tb, s, reverse, 0.0) + tb
        ta = ta * _sublane_shift(ta, s, reverse, 1.0)
        s *= 2
    ea = _sublane_shift(ta, 1, reverse, 1.0)
    eb = _sublane_shift(tb, 1, reverse, 0.0)
    return [x * ea for x in a], [x * eb + y for x, y in zip(a, b)]


def _scan_sum(x, reverse):
    x = list(x)
    order = range(SUB - 2, -1, -1) if reverse else range(1, SUB)
    for j in order:
        x[j] = x[j] + x[j + 1 if reverse else j - 1]
    t = x[0] if reverse else x[SUB - 1]
    s = 1
    while s < SUB:
        t = t + _sublane_shift(t, s, reverse, 0.0)
        s *= 2
    e = _sublane_shift(t, 1, reverse, 0.0)
    return [y + e for y in x]


def _sigmoid(x):
    return 0.5 * jnp.tanh(0.5 * x) + 0.5


def _blockdiag(x, bdm):
    xb = x.astype(BF16)
    return jnp.concatenate([xb] * HEADS_PER_GROUP, axis=0) * bdm


def _mod_kernel(c_ref, w_ref, b_ref, o_ref):
    c = c_ref[...]
    s = c * _sigmoid(c)
    o_ref[...] = _dot(s, w_ref[...]) + b_ref[...]


def _mod_call(c_all, w_mod, b_mod):
    m = c_all.shape[0]
    n = w_mod.shape[1]
    tn = 1536
    return pl.pallas_call(
        _mod_kernel,
        grid=(n // tn,),
        in_specs=[pl.BlockSpec((m, D_MODEL), lambda j: (0, 0)),
                  pl.BlockSpec((D_MODEL, tn), lambda j: (0, j)),
                  pl.BlockSpec((1, tn), lambda j: (0, j))],
        out_specs=pl.BlockSpec((m, tn), lambda j: (0, j)),
        out_shape=jax.ShapeDtypeStruct((m, n), F32),
        compiler_params=pltpu.CompilerParams(dimension_semantics=("parallel",),
                                             vmem_limit_bytes=VMEM_LIMIT),
        name="mod",
    )(c_all, w_mod, b_mod.reshape(1, n))


def _chunk_masks():
    t = np.arange(CHUNK)[:, None]
    s = (np.arange(GROUP) % CHUNK)[None, :]
    rows = []
    for d in (0, 1):
        before = (s < t) if d == 0 else (s > t)
        rows.append(before)
        rows.append(before | (s == t))
        for b in _LEVELS:
            same = (t // (2 * b)) == (s // (2 * b))
            if d == 0:
                rows.append(same & ((t // b) % 2 == 1) & ((s // b) % 2 == 0))
            else:
                rows.append(same & ((t // b) % 2 == 0) & ((s // b) % 2 == 1))
    rows.append(s == t)
    return np.stack(rows).astype(np.float32)


def _group_blockdiag_mask():
    i = np.arange(GROUP)
    return ((i[:, None] // HEAD) == (i[None, :] // HEAD)).astype(np.float32)


def _feat_kernel(has_pos, nt, *refs):
    if has_pos:
        x_ref, xp_ref, xn_ref, pos_ref, pp_ref, pn_ref = refs[:6]
        refs = refs[6:]
    else:
        x_ref, xp_ref, xn_ref = refs[:3]
        pos_ref = pp_ref = pn_ref = None
        refs = refs[3:]
    (mod_ref, gpre_ref, win_ref, wup_ref, w0_ref, aup_ref, a0_ref, gup_ref,
     kk_ref, ka_ref, rk_ref, seg_ref,
     convw_ref, convb_ref, wa_ref, ba_ref, wx_ref, bx_ref, lam_ref,
     masks_ref, bdm_ref,
     p_o, q_o, m_o, n_o, gt_o, ac_o, bs_o, bl_o, al_o, g_o, bonus_o, gate_o,
     lw_s, lc_s, la_s, lb_s, lo_s) = refs

    i = pl.program_id(0)
    tm = TOKEN_TILE
    mod = mod_ref[0]
    shift1, scale1 = mod[0:1], mod[1:2]

    def normmod(xv):
        ms = jnp.mean(xv * xv, axis=-1, keepdims=True)
        hv = (xv * lax.rsqrt(ms + EPS)) * gpre_ref[...]
        return hv * (1.0 + scale1) + shift1

    x = x_ref[0]
    halo = jnp.concatenate([xp_ref[0], xn_ref[0]], axis=0)
    if has_pos:
        x = x + pos_ref[...]
        halo = halo + jnp.concatenate([pp_ref[...], pn_ref[...]], axis=0)
    z = _dot(normmod(x), win_ref[...])
    zh = _dot(normmod(halo), win_ref[:, _O_XB:_O_XB + D_B])

    r = z[:, _O_R:_O_R + D_A]
    k = z[:, _O_K:_O_K + D_A]
    v = z[:, _O_V:_O_V + D_A]
    xw = z[:, _O_XW:_O_XW + 2 * R_W]
    xa = z[:, _O_XA:_O_XA + 2 * R_A]
    xg = z[:, _O_XG:_O_XG + R_G]

    g_o[0] = _dot(_sigmoid(xg), gup_ref[...])
    gate_o[0] = jax.nn.gelu(z[:, _O_GB:_O_GB + D_B], approximate=True)
    wl = w0_ref[...] + _dot(jnp.tanh(xw), wup_ref[...])
    lw2 = -_sigmoid(wl) * float(np.exp(-0.5))
    a2 = _sigmoid(a0_ref[...] + _dot(xa, aup_ref[...]))

    kks = k * kk_ref[...]
    ss = _dot(kks * kks, seg_ref[...])
    kk = kks * lax.rsqrt(jnp.maximum(ss, 1e-24))
    ka = ka_ref[...]
    kd2 = [k * (1.0 + (a2[:, d * D_A:(d + 1) * D_A] - 1.0) * ka) for d in (0, 1)]
    bonus_o[0] = _dot(r * (kd2[0] + kd2[1]) * rk_ref[...], seg_ref[...]) * v

    m_prev = jnp.where(i > 0, 1.0, 0.0)
    m_next = jnp.where(i < nt - 1, 1.0, 0.0)
    ext = jnp.concatenate([zh[:HALO] * m_prev, z[:, _O_XB:_O_XB + D_B], zh[HALO:] * m_next], axis=0)
    n_ext = tm + 2 * HALO
    xc = convb_ref[...]
    for j in range(CONV_W):
        sh = (2 - j) % n_ext
        tap = ext if sh == 0 else pltpu.roll(ext, sh, 0)
        xc = xc + tap[HALO:HALO + tm] * convw_ref[j:j + 1, :]
    for d in (0, 1):
        rg = jnp.concatenate([_dot(xc[:, g * GROUP:(g + 1) * GROUP], wa_ref[d, g]) for g in range(N_GROUP)], 1)
        ig = jnp.concatenate([_dot(xc[:, g * GROUP:(g + 1) * GROUP], wx_ref[d, g]) for g in range(N_GROUP)], 1)
        rg = _sigmoid(rg + ba_ref[d:d + 1, :])
        ig = _sigmoid(ig + bx_ref[d:d + 1, :])
        log_a = -LRU_C * rg * jax.nn.softplus(-lam_ref[d:d + 1, :])
        a_lru = jnp.exp(log_a)
        b_lru = jnp.sqrt(-jnp.tanh(log_a) * (a_lru * a_lru + 1.0)) * (ig * xc)
        for q in range(N_SLAB):
            la_s[d, q] = a_lru[:, q * LANE:(q + 1) * LANE]
            lb_s[d, q] = b_lru[:, q * LANE:(q + 1) * LANE]
    for c in range(CPT):
        base = c * CHUNK
        rs = slice(base, base + CHUNK)
        for q in range(N_SLAB):
            ls = slice(q * LANE, (q + 1) * LANE)
            bsum = None
            for d in (0, 1):
                acum, bcum = _scan_affine(_strided_rows(la_s, (d, q), base), _strided_rows(lb_s, (d, q), base),
                                          d == 1)
                ac_o[d, 0, rs, ls] = _natural_rows(lo_s, (d, q), base, acum)
                bsum = bcum if bsum is None else [x + y for x, y in zip(bsum, bcum)]
                bl_o[d, 0, c, :, ls] = bcum[0][0:1, :] if d == 1 else bcum[SUB - 1][SUB - 1:SUB, :]
                al_o[d, 0, c, :, ls] = acum[0][0:1, :] if d == 1 else acum[SUB - 1][SUB - 1:SUB, :]
            bs_o[0, rs, ls] = _natural_rows(lo_s, (2, q), base, bsum)

    bdm = bdm_ref[...]
    bd = lambda xv: _blockdiag(xv, bdm)
    bdot = lambda lhs, rhs: jnp.dot(lhs.astype(BF16), bd(rhs), preferred_element_type=F32)

    a_t, r_t, b_t, k_t, g_tot = [], [], [], [], []
    for d in (0, 1):
        lw = lw2[:, d * D_A:(d + 1) * D_A]
        for q in range(N_SLAB):
            lw_s[d, q] = lw[:, q * LANE:(q + 1) * LANE]
        g_tot.append([])
        lc_rows = []
        for c in range(CPT):
            base = c * CHUNK
            blocks, totals = [], []
            for q in range(N_SLAB):
                pieces = _scan_sum(_strided_rows(lw_s, (d, q), base), d == 1)
                totals.append(pieces[0][0:1, :] if d == 1 else pieces[SUB - 1][SUB - 1:SUB, :])
                blocks.append(_natural_rows(lc_s, (d, q), base, pieces))
            lc_rows.append(jnp.concatenate(blocks, axis=1))
            g_tot[d].append(jnp.exp(jnp.concatenate(totals, axis=1)))
            gt_o[d, 0, c] = g_tot[d][c]
        lc = jnp.concatenate(lc_rows, axis=0)
        e_neg = jnp.exp(-lc)
        a_t.append(-kk * jnp.exp(lc - lw))
        r_t.append(r * jnp.exp(lc))
        b_t.append(kk * a2[:, d * D_A:(d + 1) * D_A] * e_neg)
        k_t.append(kd2[d] * e_neg)

    combos = [(d, c, g) for d in (0, 1) for c in range(CPT) for g in range(N_GROUP)]

    def cut(arr, c, g):
        return arr[c * CHUNK:(c + 1) * CHUNK, g * GROUP:(g + 1) * GROUP]

    sc = {}
    for key in combos:
        d, c, g = key
        ar = jnp.concatenate([cut(a_t[d], c, g), cut(r_t[d], c, g)], axis=0)
        rhs = jnp.concatenate([bd(cut(b_t[d], c, g)), bd(cut(k_t[d], c, g))], axis=0)
        sc[key] = _dot_nt(ar, rhs)
    n_ab, n_ak, n_rb, n_rk, tinv = {}, {}, {}, {}, {}
    for key in combos:
        m0 = key[0] * _M_PER_DIR
        n_ab[key] = sc[key][:CHUNK, :GROUP] * masks_ref[m0 + _M_STRICT]
        n_ak[key] = sc[key][:CHUNK, GROUP:] * masks_ref[m0 + _M_STRICT]
        n_rb[key] = sc[key][CHUNK:, :GROUP] * masks_ref[m0 + _M_INCL]
        n_rk[key] = sc[key][CHUNK:, GROUP:] * masks_ref[m0 + _M_INCL]
        tinv[key] = masks_ref[_M_EYE] + n_ab[key] * masks_ref[m0 + _M_LEV0]
    for li in range(1, len(_LEVELS)):
        pm = {}
        for key in combos:
            pm[key] = bdot(n_ab[key] * masks_ref[key[0] * _M_PER_DIR + _M_LEV0 + li], tinv[key])
        for key in combos:
            tinv[key] = tinv[key] + bdot(tinv[key], pm[key])

    kv = {key: bdot(jnp.concatenate([n_ak[key], n_rk[key]], axis=0), cut(v, key[1], key[2])) for key in combos}
    gm = {key: bdot(n_rb[key], tinv[key]) for key in combos}
    wp = {}
    for key in combos:
        d, c, g = key
        lhs = jnp.concatenate([tinv[key], gm[key]], axis=0)
        rhs = jnp.concatenate([bd(cut(a_t[d], c, g)), bd(kv[key][:CHUNK])], axis=1)
        wp[key] = jnp.dot(lhs.astype(BF16), rhs, preferred_element_type=F32)

    lane = lax.broadcasted_iota(jnp.int32, (CHUNK, GROUP), 1)

    def fold(full):
        out = full[(HEADS_PER_GROUP - 1) * HEAD:]
        for h in range(HEADS_PER_GROUP - 2, -1, -1):
            out = jnp.where(lane < (h + 1) * HEAD, full[h * HEAD:(h + 1) * HEAD], out)
        return out

    for key in combos:
        d, c, g = key
        rs = slice(c * CHUNK, (c + 1) * CHUNK)
        ls = slice(g * GROUP, (g + 1) * GROUP)
        w_, u0 = wp[key][:CHUNK, :GROUP], wp[key][:CHUNK, GROUP:]
        vg = cut(v, c, g)
        gt = g_tot[d][c][:, ls]
        bh, kh = cut(b_t[d], c, g) * gt, cut(k_t[d], c, g) * gt
        p_o[d, 0, rs, ls] = (cut(r_t[d], c, g) + wp[key][CHUNK:, :GROUP]).astype(BF16)
        m_o[d, 0, rs, ls] = fold(_dot_tn(w_, bh)).astype(BF16)
        n_o[d, 0, rs, ls] = fold(_dot_tn(jnp.concatenate([u0, vg], axis=0), jnp.concatenate([bh, kh], axis=0)))
        if d == 1:
            other = (0, c, g)
            q_o[0, rs, ls] = ((wp[other][CHUNK:, GROUP:] + kv[other][CHUNK:])
                              + (wp[key][CHUNK:, GROUP:] + kv[key][CHUNK:]))


def _feat_call(x, pos, mod, mod_row, p):
    bsz, t, _ = x.shape
    tm = TOKEN_TILE
    nt = t // tm
    nc = t // CHUNK
    hpt = tm // HALO
    has_pos = pos is not None

    def const(shape):
        return pl.BlockSpec(shape, lambda i, b: (0,) * len(shape))

    tok = lambda width: pl.BlockSpec((1, tm, width), lambda i, b: (b, i, 0))
    prev_i = lambda i: jnp.maximum(i * hpt - 1, 0)
    next_i = lambda i: jnp.minimum((i + 1) * hpt, t // HALO - 1)
    in_specs = [tok(D_MODEL),
                pl.BlockSpec((1, HALO, D_MODEL), lambda i, b: (b, prev_i(i), 0)),
                pl.BlockSpec((1, HALO, D_MODEL), lambda i, b: (b, next_i(i), 0))]
    args = [x, x, x]
    if has_pos:
        in_specs += [pl.BlockSpec((tm, D_MODEL), lambda i, b: (i, 0)),
                     pl.BlockSpec((HALO, D_MODEL), lambda i, b: (prev_i(i), 0)),
                     pl.BlockSpec((HALO, D_MODEL), lambda i, b: (next_i(i), 0))]
        args += [pos, pos, pos]
    n_masks = 2 * _M_PER_DIR + 1
    in_specs += [
        pl.BlockSpec((1, 6, D_MODEL), lambda i, b: (mod_row(b), 0, 0)),
        const((1, D_MODEL)), const((D_MODEL, D_IN)),
        const((2 * R_W, 2 * D_A)), const((1, 2 * D_A)),
        const((2 * R_A, 2 * D_A)), const((1, 2 * D_A)),
        const((R_G, D_A)),
        const((1, D_A)), const((1, D_A)), const((1, D_A)), const((D_A, D_A)),
        const((CONV_W, D_B)), const((1, D_B)),
        const((2, N_GROUP, GROUP, GROUP)), const((2, D_B)),
        const((2, N_GROUP, GROUP, GROUP)), const((2, D_B)), const((2, D_B)),
        const((n_masks, CHUNK, GROUP)), const((GROUP, GROUP)),
    ]
    args += [mod, p["g_pre_mix"], p["w_in"], p["wup_bd"], p["w0"], p["aup_bd"], p["a0"], p["g_up"],
             p["k_k"], p["k_a"], p["r_k"], p["seg_ones"],
             p["conv_w"], p["conv_b"], p["wa_bd"], p["ba"], p["wx_bd"], p["bx"], p["lam"],
             p["chunk_masks"], p["bdm_bf16"]]
    tok_shape = jax.ShapeDtypeStruct((bsz, t, D_A), F32)
    dir_shape = jax.ShapeDtypeStruct((2, bsz, t, D_A), F32)
    row_shape = jax.ShapeDtypeStruct((2, bsz, nc, 1, D_A), F32)
    mxu_shape = jax.ShapeDtypeStruct((2, bsz, t, D_A), BF16)
    dir_spec = pl.BlockSpec((2, 1, tm, D_A), lambda i, b: (0, b, i, 0))
    row_spec = pl.BlockSpec((2, 1, CPT, 1, D_A), lambda i, b: (0, b, i, 0, 0))
    out_shape = [mxu_shape, tok_shape, mxu_shape, dir_shape, row_shape, dir_shape, tok_shape, row_shape, row_shape,
                 tok_shape, tok_shape, tok_shape]
    out_specs = [dir_spec, tok(D_A), dir_spec, dir_spec, row_spec, dir_spec, tok(D_B), row_spec, row_spec,
                 tok(D_A), tok(D_A), tok(D_B)]
    return pl.pallas_call(
        functools.partial(_feat_kernel, has_pos, nt),
        grid=(nt, bsz),
        in_specs=in_specs,
        out_specs=out_specs,
        out_shape=out_shape,
        scratch_shapes=[pltpu.VMEM((2, N_SLAB, tm, LANE), F32)] * 4 + [pltpu.VMEM((3, N_SLAB, tm, LANE), F32)],
        compiler_params=pltpu.CompilerParams(dimension_semantics=("parallel", "parallel"),
                                             vmem_limit_bytes=VMEM_LIMIT),
        name="feat",
    )(*args)


def _scan_kernel(nt, tile, *refs):
    (pf_ref, mf_ref, nf_ref, gtf_ref, alf_ref, blf_ref,
     pb_ref, mb_ref, nb_ref, gtb_ref, alb_ref, blb_ref,
     s0_ref, l0_ref, bdm_ref,
     y_ref, hc_ref, s_ref, hl_ref) = refs

    i = pl.program_id(1)
    cpt = tile // CHUNK

    @pl.when(i == 0)
    def _init():
        y_ref[...] = jnp.zeros_like(y_ref)
        s_ref[...] = s0_ref[...]
        hl_ref[...] = l0_ref[...]

    bdm = bdm_ref[...]
    per_dir = ((0, pf_ref, mf_ref, nf_ref, gtf_ref, alf_ref, blf_ref),
               (1, pb_ref, mb_ref, nb_ref, gtb_ref, alb_ref, blb_ref))
    tiles = (i, nt - 1 - i)
    state = [s_ref[0, d] for d in (0, 1)]
    h0 = [hl_ref[0, d:d + 1, :] for d in (0, 1)]
    for cc in range(cpt):
        for d, p_ref, m_ref, n_ref, gt_ref, al_ref, bl_ref in per_dir:
            c = cc if d == 0 else cpt - 1 - cc
            rs = slice(c * CHUNK, (c + 1) * CHUNK)
            rows = pl.ds(pl.multiple_of(tiles[d] * tile + c * CHUNK, CHUNK), CHUNK)
            gt = gt_ref[0, 0, c]
            y_parts, s_parts = [], []
            for g in range(N_GROUP):
                ls = slice(g * GROUP, (g + 1) * GROUP)
                sg = state[d][:, ls]
                y_parts.append(_dot_nt(p_ref[0, 0, rs, ls], _blockdiag(sg, bdm)))
                s_parts.append(sg * gt[:, ls]
                               + jnp.dot(sg.astype(BF16), _blockdiag(m_ref[0, 0, rs, ls], bdm),
                                         preferred_element_type=F32)
                               + n_ref[0, 0, rs, ls])
            state[d] = jnp.concatenate(s_parts, axis=1)
            y_ref[0, rows, :] += jnp.concatenate(y_parts, axis=1)
            hc_ref[d, 0, tiles[d] * cpt + c] = h0[d]
            h0[d] = bl_ref[0, 0, c] + al_ref[0, 0, c] * h0[d]
    for d in (0, 1):
        s_ref[0, d] = state[d]
        hl_ref[0, d:d + 1, :] = h0[d]


def _scan_call(feats, s0, l0, p):
    pm, mm, nm, gt, al, bl = feats
    _, bsz, t, _ = pm.shape
    tile = min(SCAN_TILE, t)
    nt = t // tile
    nc = t // CHUNK

    def dir_specs(d, tile_of):
        big = pl.BlockSpec((1, 1, tile, D_A), lambda b, i: (d, b, tile_of(i), 0))
        small = pl.BlockSpec((1, 1, tile // CHUNK, 1, D_A), lambda b, i: (d, b, tile_of(i), 0, 0))
        return big, small

    big_f, small_f = dir_specs(0, lambda i: i)
    big_b, small_b = dir_specs(1, lambda i: nt - 1 - i)
    in_specs = [big_f, big_f, big_f, small_f, small_f, small_f,
                big_b, big_b, big_b, small_b, small_b, small_b,
                pl.BlockSpec((1, 2, HEAD, D_A), lambda b, i: (b, 0, 0, 0)),
                pl.BlockSpec((1, 2, D_B), lambda b, i: (b, 0, 0)),
                pl.BlockSpec((GROUP, GROUP), lambda b, i: (0, 0))]
    args = [pm, mm, nm, gt, al, bl, pm, mm, nm, gt, al, bl, s0, l0, p["bdm_bf16"]]
    out_specs = [pl.BlockSpec((1, t, D_A), lambda b, i: (b, 0, 0)),
                 pl.BlockSpec((2, 1, nc, 1, D_B), lambda b, i: (0, b, 0, 0, 0)),
                 pl.BlockSpec((1, 2, HEAD, D_A), lambda b, i: (b, 0, 0, 0)),
                 pl.BlockSpec((1, 2, D_B), lambda b, i: (b, 0, 0))]
    out_shape = [jax.ShapeDtypeStruct((bsz, t, D_A), F32),
                 jax.ShapeDtypeStruct((2, bsz, nc, 1, D_B), F32),
                 jax.ShapeDtypeStruct((bsz, 2, HEAD, D_A), F32),
                 jax.ShapeDtypeStruct((bsz, 2, D_B), F32)]
    return pl.pallas_call(
        functools.partial(_scan_kernel, nt, tile),
        grid=(bsz, nt),
        in_specs=in_specs,
        out_specs=out_specs,
        out_shape=out_shape,
        compiler_params=pltpu.CompilerParams(dimension_semantics=("parallel", "arbitrary"),
                                             vmem_limit_bytes=VMEM_LIMIT),
        name="scan",
    )(*args)


def _out_kernel(has_pos, *refs):
    if has_pos:
        x_ref, pos_ref = refs[0], refs[1]
        refs = refs[2:]
    else:
        x_ref, pos_ref = refs[0], None
        refs = refs[1:]
    (y_ref, q_ref, ac_ref, hc_ref, bs_ref, g_ref, bonus_ref, gate_ref, mod_ref,
     avg_ref, lnxg_ref, lnxb_ref, wout_ref, gpost_ref, gpre2_ref, w1_ref, w2_ref, gpost2_ref,
     o_ref) = refs

    ns, tm = x_ref.shape[0], x_ref.shape[1]
    rows = lambda ref: jnp.concatenate([ref[s] for s in range(ns)], axis=0)
    x = rows(x_ref)
    if has_pos:
        x = x + jnp.concatenate([pos_ref[...]] * ns, axis=0)
    mod = mod_ref[0]
    gate1, shift2, scale2, gate2 = mod[2:3], mod[3:4], mod[4:5], mod[5:6]

    y = rows(y_ref) + rows(q_ref)
    avg = avg_ref[...]

    def head_mean(a):
        return jnp.concatenate([_dot(a[:, t * MXU_TILE:(t + 1) * MXU_TILE], avg)
                                for t in range(D_A // MXU_TILE)], axis=1)

    y_hi = y.astype(BF16)
    y_lo = (y - y_hi.astype(F32)).astype(BF16)
    mu = head_mean(y_hi) + head_mean(y_lo)
    yc = y - mu
    var = head_mean(yc * yc)
    yn = yc * lax.rsqrt(var + LNX_EPS) * lnxg_ref[...] + lnxb_ref[...]
    out_a = (yn + rows(bonus_ref)) * rows(g_ref)
    hs = jnp.concatenate(
        [bs_ref[s, c * CHUNK:(c + 1) * CHUNK, :]
         + ac_ref[0, s, c * CHUNK:(c + 1) * CHUNK, :] * hc_ref[0, s, c]
         + ac_ref[1, s, c * CHUNK:(c + 1) * CHUNK, :] * hc_ref[1, s, c]
         for s in range(ns) for c in range(tm // CHUNK)], axis=0)
    out_b = hs * rows(gate_ref)
    mix = _dot(jnp.concatenate([out_a, out_b], axis=1), wout_ref[...])
    ms = jnp.mean(mix * mix, axis=-1, keepdims=True)
    x = x + gate1 * ((mix * lax.rsqrt(ms + EPS)) * gpost_ref[...])

    ms = jnp.mean(x * x, axis=-1, keepdims=True)
    h = (x * lax.rsqrt(ms + EPS)) * gpre2_ref[...]
    h = h * (1.0 + scale2) + shift2
    f = _dot(h, w1_ref[...])
    f = jnp.square(jnp.maximum(f, 0.0))
    f = _dot(f, w2_ref[...])
    ms = jnp.mean(f * f, axis=-1, keepdims=True)
    res = x + gate2 * ((f * lax.rsqrt(ms + EPS)) * gpost2_ref[...])
    for s in range(ns):
        o_ref[s] = res[s * tm:(s + 1) * tm]


def _out_call(x, pos, mod, mod_row, shared_mod, y, q, ac, hc, bs, g, bonus, gate, p):
    bsz, t, _ = x.shape
    tm = min(OUT_TILE, t)
    nt = t // tm
    ns = OUT_TILE // tm if shared_mod else 1
    has_pos = pos is not None

    def const(shape):
        return pl.BlockSpec(shape, lambda i, b: (0,) * len(shape))

    tok = lambda width: pl.BlockSpec((ns, tm, width), lambda i, b: (b, i, 0))
    in_specs = [tok(D_MODEL)]
    args = [x]
    if has_pos:
        in_specs.append(pl.BlockSpec((tm, D_MODEL), lambda i, b: (i, 0)))
        args.append(pos)
    in_specs += [tok(D_A), tok(D_A),
                 pl.BlockSpec((2, ns, tm, D_B), lambda i, b: (0, b, i, 0)),
                 pl.BlockSpec((2, ns, tm // CHUNK, 1, D_B), lambda i, b: (0, b, i, 0, 0))]
    in_specs += [tok(D_A)] * 4
    in_specs += [
        pl.BlockSpec((1, 6, D_MODEL), lambda i, b: (mod_row(b * ns), 0, 0)),
        const((MXU_TILE, MXU_TILE)), const((1, D_A)), const((1, D_A)),
        const((D_MODEL, D_MODEL)), const((1, D_MODEL)), const((1, D_MODEL)),
        const((D_MODEL, D_FF)), const((D_FF, D_MODEL)), const((1, D_MODEL)),
    ]
    args += [y, q, ac, hc, bs, g, bonus, gate, mod,
             p["seg_avg"], p["lnx_g"], p["lnx_b"], p["w_out"], p["g_post_mix"], p["g_pre_mlp"],
             p["w_mlp1"], p["w_mlp2"], p["g_post_mlp"]]
    return pl.pallas_call(
        functools.partial(_out_kernel, has_pos),
        grid=(nt, bsz // ns),
        in_specs=in_specs,
        out_specs=tok(D_MODEL),
        out_shape=jax.ShapeDtypeStruct((bsz, t, D_MODEL), F32),
        compiler_params=pltpu.CompilerParams(dimension_semantics=("parallel", "parallel"),
                                             vmem_limit_bytes=VMEM_LIMIT),
        name="out",
    )(*args)


def _sincos_1d(pos, dim):
    omega = 1.0 / (10000.0 ** (jnp.arange(dim // 2, dtype=F32) / (dim // 2)))
    ang = pos.astype(F32)[:, None] * omega[None, :]
    return jnp.concatenate([jnp.sin(ang), jnp.cos(ang)], axis=-1)


def _grid_pos_embed(n_tokens):
    rows = n_tokens // GRID_W
    half = D_MODEL // 2
    e_row = _sincos_1d(jnp.arange(rows), half)
    e_col = _sincos_1d(jnp.arange(GRID_W), half)
    emb = jnp.concatenate([jnp.broadcast_to(e_row[:, None, :], (rows, GRID_W, half)),
                           jnp.broadcast_to(e_col[None, :, :], (rows, GRID_W, half))], axis=-1)
    return emb.reshape(rows * GRID_W, D_MODEL)


def _blockdiag_pairs(w):
    z = jnp.zeros_like(w[0])
    return jnp.concatenate([jnp.concatenate([w[0], z], axis=1),
                            jnp.concatenate([z, w[1]], axis=1)], axis=0)


def _heads_to_blockdiag(w):
    lead = w.shape[:-3]
    w = w.reshape(lead + (N_GROUP, HEADS_PER_GROUP, HEAD, HEAD))
    eye = jnp.eye(HEADS_PER_GROUP, dtype=w.dtype)
    bd = jnp.einsum('...ghab,hj->...ghajb', w, eye)
    return bd.reshape(lead + (N_GROUP, GROUP, GROUP))


def _state_to_lanes(s):
    b = s.shape[0]
    return jnp.transpose(s, (0, 1, 3, 2, 4)).reshape(b, 2, HEAD, D_A)


def _state_from_lanes(s):
    b = s.shape[0]
    return jnp.transpose(s.reshape(b, 2, HEAD, N_HEAD, HEAD), (0, 1, 3, 2, 4))


def kernel(x_prompt, x_sample, c, state_rwkv, state_lru, c_ctx, w_mod, b_mod, g_pre_mix, g_post_mix,
           g_pre_mlp, g_post_mlp, w_in, rwkv_w0, rwkv_w_up, rwkv_a0, rwkv_a_up, rwkv_g_up, rwkv_k_k,
           rwkv_k_a, rwkv_r_k, rwkv_lnx_g, rwkv_lnx_b, lru_conv_w, lru_conv_b, lru_wa, lru_ba, lru_wx,
           lru_bx, lru_lambda, w_out, w_mlp1, w_mlp2):
    n_ctx = x_prompt.shape[0]
    n_lat = x_sample.shape[0]
    l = 0
    seg = _group_blockdiag_mask()
    seg512 = np.kron(np.eye(N_GROUP, dtype=np.float32), seg)
    p = {
        "g_pre_mix": g_pre_mix[l][None], "g_post_mix": g_post_mix[l][None],
        "g_pre_mlp": g_pre_mlp[l][None], "g_post_mlp": g_post_mlp[l][None],
        "w_in": w_in[l].astype(BF16), "w_out": w_out[l].astype(BF16),
        "w_mlp1": w_mlp1[l].astype(BF16), "w_mlp2": w_mlp2[l].astype(BF16),
        "w0": rwkv_w0[l].reshape(1, 2 * D_A), "a0": rwkv_a0[l].reshape(1, 2 * D_A),
        "wup_bd": _blockdiag_pairs(rwkv_w_up[l]).astype(BF16),
        "aup_bd": _blockdiag_pairs(rwkv_a_up[l]).astype(BF16),
        "g_up": rwkv_g_up[l].astype(BF16),
        "k_k": rwkv_k_k[l][None], "k_a": rwkv_k_a[l][None], "r_k": rwkv_r_k[l].reshape(1, D_A),
        "lnx_g": rwkv_lnx_g[l][None], "lnx_b": rwkv_lnx_b[l][None],
        "conv_w": lru_conv_w[l], "conv_b": lru_conv_b[l][None],
        "wa_bd": _heads_to_blockdiag(lru_wa[l]).astype(BF16), "ba": lru_ba[l],
        "wx_bd": _heads_to_blockdiag(lru_wx[l]).astype(BF16), "bx": lru_bx[l],
        "lam": lru_lambda[l],
        "seg_ones": jnp.asarray(seg512, BF16),
        "seg_avg": jnp.asarray(np.kron(np.eye(MXU_TILE // GROUP, dtype=np.float32), seg) / HEAD, BF16),
        "chunk_masks": jnp.asarray(_chunk_masks()),
        "bdm_bf16": jnp.asarray(seg, BF16),
    }

    m_rows = 16
    c_all = jnp.concatenate([c_ctx[None], c, jnp.zeros((m_rows - 1 - n_lat, D_MODEL), F32)], axis=0)
    mod = _mod_call(c_all, w_mod[l], b_mod[l]).reshape(m_rows, 6, D_MODEL)

    pos = _grid_pos_embed(x_sample.shape[1]).astype(x_sample.dtype)
    ctx_row = lambda b: 0
    lat_row = lambda b: b + 1

    pm, q, mm, nm, gt, ac, bs, bl, al, g, bonus, gate = _feat_call(x_prompt, None, mod, ctx_row, p)
    y, hc, s_ctx, l_ctx = _scan_call(
        (pm, mm, nm, gt, al, bl), jnp.zeros((n_ctx, 2, HEAD, D_A), F32), jnp.zeros((n_ctx, 2, D_B), F32), p)
    y_prompt = _out_call(x_prompt, None, mod, ctx_row, True, y, q, ac, hc, bs, g, bonus, gate, p)

    pm, q, mm, nm, gt, ac, bs, bl, al, g, bonus, gate = _feat_call(x_sample, pos, mod, lat_row, p)
    y, hc, _, _ = _scan_call((pm, mm, nm, gt, al, bl), _state_to_lanes(state_rwkv[:, l]), state_lru[:, l], p)
    y_sample = _out_call(x_sample, pos, mod, lat_row, False, y, q, ac, hc, bs, g, bonus, gate, p)

    new_state_rwkv = _state_from_lanes(s_ctx)[:, None].astype(x_prompt.dtype)
    new_state_lru = l_ctx[:, None].astype(x_prompt.dtype)
    return (y_prompt, y_sample, new_state_rwkv, new_state_lru)
```

```python
import functools

import numpy as np
import jax
import jax.numpy as jnp
from jax import lax
from jax.experimental import pallas as pl
from jax.experimental.pallas import tpu as pltpu

F32 = jnp.float32
BF16 = jnp.bfloat16

D_MODEL = 1024
D_A = 512
D_B = 512
HEAD = 64
N_HEAD = 8
R_W = 64
R_A = 64
R_G = 128
D_FF = 4096
D_IN = 2944
GRID_W = 64
CONV_W = 4
LRU_C = 8.0
EPS = 1e-6
LNX_EPS = 64e-5

CHUNK = 64
GROUP = 128
HEADS_PER_GROUP = GROUP // HEAD
N_GROUP = D_A // GROUP
SUB = 8
LANE = 128
N_SLAB = D_A // LANE
HALO = 8
TOKEN_TILE = 256
CPT = TOKEN_TILE // CHUNK
OUT_TILE = 512
SCAN_TILE = 1024
MXU_TILE = 256
V7X_VMEM_BYTES = 64 * 1024 * 1024
VMEM_LIMIT = V7X_VMEM_BYTES - 4 * 1024 * 1024

_O_R, _O_K, _O_V, _O_XW, _O_XA, _O_XG, _O_XB, _O_GB = 0, 512, 1024, 1536, 1664, 1792, 1920, 2432

_M_STRICT, _M_INCL, _M_LEV0 = 0, 1, 2
_LEVELS = (1, 2, 4, 8, 16, 32)
_M_PER_DIR = 2 + len(_LEVELS)
_M_EYE = 2 * _M_PER_DIR


def _dot(a, b):
    return jnp.dot(a.astype(BF16), b.astype(BF16), preferred_element_type=F32)


def _dot_nt(a, b):
    return lax.dot_general(a.astype(BF16), b.astype(BF16), (((1,), (1,)), ((), ())),
                           preferred_element_type=F32)


def _dot_tn(a, b):
    return lax.dot_general(a.astype(BF16), b.astype(BF16), (((0,), (0,)), ((), ())),
                           preferred_element_type=F32)


def _strided_rows(ref, lead, base):
    return [ref[lead + (pl.ds(base + j, SUB, stride=SUB), slice(None))] for j in range(SUB)]


def _natural_rows(ref, lead, base, pieces):
    for j, piece in enumerate(pieces):
        ref[lead + (pl.ds(base + SUB * j, SUB), slice(None))] = piece
    return jnp.concatenate(_strided_rows(ref, lead, base), axis=0)


def _sublane_shift(x, steps, reverse, fill):
    sub = lax.broadcasted_iota(jnp.int32, x.shape, 0)
    if reverse:
        return jnp.where(sub < SUB - steps, pltpu.roll(x, SUB - steps, 0), fill)
    return jnp.where(sub >= steps, pltpu.roll(x, steps, 0), fill)


def _scan_affine(a, b, reverse):
    a, b = list(a), list(b)
    order = range(SUB - 2, -1, -1) if reverse else range(1, SUB)
    for j in order:
        p = j + 1 if reverse else j - 1
        b[j] = a[j] * b[p] + b[j]
        a[j] = a[j] * a[p]
    ta, tb = (a[0], b[0]) if reverse else (a[SUB - 1], b[SUB - 1])
    s = 1
    while s < SUB:
        tb = ta * _sublane_shift(tb, s, reverse, 0.0) + tb
        ta = ta * _sublane_shift(ta, s, reverse, 1.0)
        s *= 2
    ea = _sublane_shift(ta, 1, reverse, 1.0)
    eb = _sublane_shift(tb, 1, reverse, 0.0)
    return [x * ea for x in a], [x * eb + y for x, y in zip(a, b)]


def _scan_sum(x, reverse):
    x = list(x)
    order = range(SUB - 2, -1, -1) if reverse else range(1, SUB)
    for j in order:
        x[j] = x[j] + x[j + 1 if reverse else j - 1]
    t = x[0] if reverse else x[SUB - 1]
    s = 1
    while s < SUB:
        t = t + _sublane_shift(t, s, reverse, 0.0)
        s *= 2
    e = _sublane_shift(t, 1, reverse, 0.0)
    return [y + e for y in x]


def _sigmoid(x):
    return 0.5 * jnp.tanh(0.5 * x) + 0.5


def _blockdiag(x, bdm):
    xb = x.astype(BF16)
    return jnp.concatenate([xb] * HEADS_PER_GROUP, axis=0) * bdm


def _mod_kernel(c_ref, w_ref, b_ref, o_ref):
    c = c_ref[...]
    s = c * _sigmoid(c)
    o_ref[...] = _dot(s, w_ref[...]) + b_ref[...]


def _mod_call(c_all, w_mod, b_mod):
    m = c_all.shape[0]
    n = w_mod.shape[1]
    tn = 1536
    return pl.pallas_call(
        _mod_kernel,
        grid=(n // tn,),
        in_specs=[pl.BlockSpec((m, D_MODEL), lambda j: (0, 0)),
                  pl.BlockSpec((D_MODEL, tn), lambda j: (0, j)),
                  pl.BlockSpec((1, tn), lambda j: (0, j))],
        out_specs=pl.BlockSpec((m, tn), lambda j: (0, j)),
        out_shape=jax.ShapeDtypeStruct((m, n), F32),
        compiler_params=pltpu.CompilerParams(dimension_semantics=("parallel",),
                                             vmem_limit_bytes=VMEM_LIMIT),
        name="mod",
    )(c_all, w_mod, b_mod.reshape(1, n))


def _chunk_masks():
    t = np.arange(CHUNK)[:, None]
    s = (np.arange(GROUP) % CHUNK)[None, :]
    rows = []
    for d in (0, 1):
        before = (s < t) if d == 0 else (s > t)
        rows.append(before)
        rows.append(before | (s == t))
        for b in _LEVELS:
            same = (t // (2 * b)) == (s // (2 * b))
            if d == 0:
                rows.append(same & ((t // b) % 2 == 1) & ((s // b) % 2 == 0))
            else:
                rows.append(same & ((t // b) % 2 == 0) & ((s // b) % 2 == 1))
    rows.append(s == t)
    return np.stack(rows).astype(np.float32)


def _group_blockdiag_mask():
    i = np.arange(GROUP)
    return ((i[:, None] // HEAD) == (i[None, :] // HEAD)).astype(np.float32)


def _feat_kernel(has_pos, nt, *refs):
    if has_pos:
        x_ref, xp_ref, xn_ref, pos_ref, pp_ref, pn_ref = refs[:6]
        refs = refs[6:]
    else:
        x_ref, xp_ref, xn_ref = refs[:3]
        pos_ref = pp_ref = pn_ref = None
        refs = refs[3:]
    (mod_ref, gpre_ref, win_ref, wup_ref, w0_ref, aup_ref, a0_ref, gup_ref,
     kk_ref, ka_ref, rk_ref, seg_ref,
     convw_ref, convb_ref, wa_ref, ba_ref, wx_ref, bx_ref, lam_ref,
     masks_ref, bdm_ref,
     p_o, q_o, m_o, n_o, gt_o, ac_o, bs_o, bl_o, al_o, g_o, bonus_o, gate_o,
     lw_s, lc_s, la_s, lb_s, lo_s) = refs

    i = pl.program_id(0)
    tm = TOKEN_TILE
    mod = mod_ref[0]
    shift1, scale1 = mod[0:1], mod[1:2]

    def normmod(xv):
        ms = jnp.mean(xv * xv, axis=-1, keepdims=True)
        hv = (xv * lax.rsqrt(ms + EPS)) * gpre_ref[...]
        return hv * (1.0 + scale1) + shift1

    x = x_ref[0]
    halo = jnp.concatenate([xp_ref[0], xn_ref[0]], axis=0)
    if has_pos:
        x = x + pos_ref[...]
        halo = halo + jnp.concatenate([pp_ref[...], pn_ref[...]], axis=0)
    z = _dot(normmod(x), win_ref[...])
    zh = _dot(normmod(halo), win_ref[:, _O_XB:_O_XB + D_B])

    r = z[:, _O_R:_O_R + D_A]
    k = z[:, _O_K:_O_K + D_A]
    v = z[:, _O_V:_O_V + D_A]
    xw = z[:, _O_XW:_O_XW + 2 * R_W]
    xa = z[:, _O_XA:_O_XA + 2 * R_A]
    xg = z[:, _O_XG:_O_XG + R_G]

    g_o[0] = _dot(_sigmoid(xg), gup_ref[...])
    gb = z[:, _O_GB:_O_GB + D_B]
    c_gelu = float(np.sqrt(2.0 / np.pi))
    gate_o[0] = gb * (0.5 + 0.5 * jnp.tanh(gb * (c_gelu + (0.044715 * c_gelu) * (gb * gb))))
    half_c = 0.5 * float(np.exp(-0.5))
    lw2 = -half_c * jnp.tanh(w0_ref[...] + _dot(jnp.tanh(xw), wup_ref[...])) - half_c
    a2 = 0.5 * jnp.tanh(a0_ref[...] + _dot(xa, aup_ref[...])) + 0.5

    kks = k * kk_ref[...]
    ss = _dot(kks * kks, seg_ref[...])
    kk = kks * lax.rsqrt(jnp.maximum(ss, 1e-24))
    ka = ka_ref[...]
    k_fix, k_var = k * (1.0 - ka), k * ka
    kd2 = [k_fix + k_var * a2[:, d * D_A:(d + 1) * D_A] for d in (0, 1)]
    bonus_o[0] = _dot(r * (kd2[0] + kd2[1]) * rk_ref[...], seg_ref[...]) * v

    m_prev = jnp.where(i > 0, 1.0, 0.0)
    m_next = jnp.where(i < nt - 1, 1.0, 0.0)
    ext = jnp.concatenate([zh[:HALO] * m_prev, z[:, _O_XB:_O_XB + D_B], zh[HALO:] * m_next], axis=0)
    n_ext = tm + 2 * HALO
    xc = convb_ref[...]
    for j in range(CONV_W):
        sh = (2 - j) % n_ext
        tap = ext if sh == 0 else pltpu.roll(ext, sh, 0)
        xc = xc + tap[HALO:HALO + tm] * convw_ref[j:j + 1, :]
    for d in (0, 1):
        rg = jnp.concatenate([_dot(xc[:, g * GROUP:(g + 1) * GROUP], wa_ref[d, g]) for g in range(N_GROUP)], 1)
        ig = jnp.concatenate([_dot(xc[:, g * GROUP:(g + 1) * GROUP], wx_ref[d, g]) for g in range(N_GROUP)], 1)
        rg = 0.5 * jnp.tanh(rg + ba_ref[d:d + 1, :]) + 0.5
        ig = 0.5 * jnp.tanh(ig + bx_ref[d:d + 1, :]) + 0.5
        neg_log_a = rg * (LRU_C * jax.nn.softplus(-lam_ref[d:d + 1, :]))
        a_lru = jnp.exp(-neg_log_a)
        b_lru = jnp.sqrt(jnp.tanh(neg_log_a) * (a_lru * a_lru + 1.0)) * (ig * xc)
        for q in range(N_SLAB):
            la_s[d, q] = a_lru[:, q * LANE:(q + 1) * LANE]
            lb_s[d, q] = b_lru[:, q * LANE:(q + 1) * LANE]
    for c in range(CPT):
        base = c * CHUNK
        rs = slice(base, base + CHUNK)
        for q in range(N_SLAB):
            ls = slice(q * LANE, (q + 1) * LANE)
            bsum = None
            for d in (0, 1):
                acum, bcum = _scan_affine(_strided_rows(la_s, (d, q), base), _strided_rows(lb_s, (d, q), base),
                                          d == 1)
                ac_o[d, 0, rs, ls] = _natural_rows(lo_s, (d, q), base, acum)
                bsum = bcum if bsum is None else [x + y for x, y in zip(bsum, bcum)]
                bl_o[d, 0, c, :, ls] = bcum[0][0:1, :] if d == 1 else bcum[SUB - 1][SUB - 1:SUB, :]
                al_o[d, 0, c, :, ls] = acum[0][0:1, :] if d == 1 else acum[SUB - 1][SUB - 1:SUB, :]
            bs_o[0, rs, ls] = _natural_rows(lo_s, (2, q), base, bsum)

    bdm = bdm_ref[...]
    bd = lambda xv: _blockdiag(xv, bdm)
    bdot = lambda lhs, rhs: jnp.dot(lhs.astype(BF16), bd(rhs), preferred_element_type=F32)

    a_t, r_t, b_t, k_t, g_tot = [], [], [], [], []
    neg_kk = -kk
    for d in (0, 1):
        lw = lw2[:, d * D_A:(d + 1) * D_A]
        for q in range(N_SLAB):
            lw_s[d, q] = lw[:, q * LANE:(q + 1) * LANE]
        g_tot.append([])
        lc_rows = []
        for c in range(CPT):
            base = c * CHUNK
            blocks, totals = [], []
            for q in range(N_SLAB):
                pieces = _scan_sum(_strided_rows(lw_s, (d, q), base), d == 1)
                totals.append(pieces[0][0:1, :] if d == 1 else pieces[SUB - 1][SUB - 1:SUB, :])
                blocks.append(_natural_rows(lc_s, (d, q), base, pieces))
            lc_rows.append(jnp.concatenate(blocks, axis=1))
            g_tot[d].append(jnp.exp(jnp.concatenate(totals, axis=1)))
            gt_o[d, 0, c] = g_tot[d][c]
        lc = jnp.concatenate(lc_rows, axis=0)
        e_neg = jnp.exp(-lc)
        a_t.append(neg_kk * jnp.exp(lc - lw))
        r_t.append(r * jnp.exp(lc))
        b_t.append(kk * a2[:, d * D_A:(d + 1) * D_A] * e_neg)
        k_t.append(kd2[d] * e_neg)

    combos = [(d, c, g) for d in (0, 1) for c in range(CPT) for g in range(N_GROUP)]

    def cut(arr, c, g):
        return arr[c * CHUNK:(c + 1) * CHUNK, g * GROUP:(g + 1) * GROUP]

    sc = {}
    for key in combos:
        d, c, g = key
        ar = jnp.concatenate([cut(a_t[d], c, g), cut(r_t[d], c, g)], axis=0)
        rhs = jnp.concatenate([bd(cut(b_t[d], c, g)), bd(cut(k_t[d], c, g))], axis=0)
        sc[key] = _dot_nt(ar, rhs)
    n_ab, n_ak, n_rb, n_rk, tinv = {}, {}, {}, {}, {}
    for key in combos:
        m0 = key[0] * _M_PER_DIR
        n_ab[key] = sc[key][:CHUNK, :GROUP] * masks_ref[m0 + _M_STRICT]
        n_ak[key] = sc[key][:CHUNK, GROUP:] * masks_ref[m0 + _M_STRICT]
        n_rb[key] = sc[key][CHUNK:, :GROUP] * masks_ref[m0 + _M_INCL]
        n_rk[key] = sc[key][CHUNK:, GROUP:] * masks_ref[m0 + _M_INCL]
        tinv[key] = masks_ref[_M_EYE] + n_ab[key] * masks_ref[m0 + _M_LEV0]
    for li in range(1, len(_LEVELS)):
        pm = {}
        for key in combos:
            pm[key] = bdot(n_ab[key] * masks_ref[key[0] * _M_PER_DIR + _M_LEV0 + li], tinv[key])
        for key in combos:
            tinv[key] = tinv[key] + bdot(tinv[key], pm[key])

    kv = {key: bdot(jnp.concatenate([n_ak[key], n_rk[key]], axis=0), cut(v, key[1], key[2])) for key in combos}
    gm = {key: bdot(n_rb[key], tinv[key]) for key in combos}
    wp = {}
    for key in combos:
        d, c, g = key
        lhs = jnp.concatenate([tinv[key], gm[key]], axis=0)
        rhs = jnp.concatenate([bd(cut(a_t[d], c, g)), bd(kv[key][:CHUNK])], axis=1)
        wp[key] = jnp.dot(lhs.astype(BF16), rhs, preferred_element_type=F32)

    lane = lax.broadcasted_iota(jnp.int32, (CHUNK, GROUP), 1)

    def fold(full):
        out = full[(HEADS_PER_GROUP - 1) * HEAD:]
        for h in range(HEADS_PER_GROUP - 2, -1, -1):
            out = jnp.where(lane < (h + 1) * HEAD, full[h * HEAD:(h + 1) * HEAD], out)
        return out

    for key in combos:
        d, c, g = key
        rs = slice(c * CHUNK, (c + 1) * CHUNK)
        ls = slice(g * GROUP, (g + 1) * GROUP)
        w_, u0 = wp[key][:CHUNK, :GROUP], wp[key][:CHUNK, GROUP:]
        vg = cut(v, c, g)
        gt = g_tot[d][c][:, ls]
        bh, kh = cut(b_t[d], c, g) * gt, cut(k_t[d], c, g) * gt
        p_o[d, 0, rs, ls] = (cut(r_t[d], c, g) + wp[key][CHUNK:, :GROUP]).astype(BF16)
        m_o[d, 0, rs, ls] = fold(_dot_tn(w_, bh)).astype(BF16)
        n_o[d, 0, rs, ls] = fold(_dot_tn(jnp.concatenate([u0, vg], axis=0), jnp.concatenate([bh, kh], axis=0)))
        if d == 1:
            other = (0, c, g)
            q_o[0, rs, ls] = ((wp[other][CHUNK:, GROUP:] + kv[other][CHUNK:])
                              + (wp[key][CHUNK:, GROUP:] + kv[key][CHUNK:]))


def _feat_call(x, pos, mod, mod_row, p):
    bsz, t, _ = x.shape
    tm = TOKEN_TILE
    nt = t // tm
    nc = t // CHUNK
    hpt = tm // HALO
    has_pos = pos is not None

    def const(shape):
        return pl.BlockSpec(shape, lambda i, b: (0,) * len(shape))

    tok = lambda width: pl.BlockSpec((1, tm, width), lambda i, b: (b, i, 0))
    prev_i = lambda i: jnp.maximum(i * hpt - 1, 0)
    next_i = lambda i: jnp.minimum((i + 1) * hpt, t // HALO - 1)
    in_specs = [tok(D_MODEL),
                pl.BlockSpec((1, HALO, D_MODEL), lambda i, b: (b, prev_i(i), 0)),
                pl.BlockSpec((1, HALO, D_MODEL), lambda i, b: (b, next_i(i), 0))]
    args = [x, x, x]
    if has_pos:
        in_specs += [pl.BlockSpec((tm, D_MODEL), lambda i, b: (i, 0)),
                     pl.BlockSpec((HALO, D_MODEL), lambda i, b: (prev_i(i), 0)),
                     pl.BlockSpec((HALO, D_MODEL), lambda i, b: (next_i(i), 0))]
        args += [pos, pos, pos]
    n_masks = 2 * _M_PER_DIR + 1
    in_specs += [
        pl.BlockSpec((1, 6, D_MODEL), lambda i, b: (mod_row(b), 0, 0)),
        const((1, D_MODEL)), const((D_MODEL, D_IN)),
        const((2 * R_W, 2 * D_A)), const((1, 2 * D_A)),
        const((2 * R_A, 2 * D_A)), const((1, 2 * D_A)),
        const((R_G, D_A)),
        const((1, D_A)), const((1, D_A)), const((1, D_A)), const((D_A, D_A)),
        const((CONV_W, D_B)), const((1, D_B)),
        const((2, N_GROUP, GROUP, GROUP)), const((2, D_B)),
        const((2, N_GROUP, GROUP, GROUP)), const((2, D_B)), const((2, D_B)),
        const((n_masks, CHUNK, GROUP)), const((GROUP, GROUP)),
    ]
    args += [mod, p["g_pre_mix"], p["w_in"], p["wup_bd"], p["w0"], p["aup_bd"], p["a0"], p["g_up"],
             p["k_k"], p["k_a"], p["r_k"], p["seg_ones"],
             p["conv_w"], p["conv_b"], p["wa_bd"], p["ba"], p["wx_bd"], p["bx"], p["lam"],
             p["chunk_masks"], p["bdm_bf16"]]
    tok_shape = jax.ShapeDtypeStruct((bsz, t, D_A), F32)
    dir_shape = jax.ShapeDtypeStruct((2, bsz, t, D_A), F32)
    row_shape = jax.ShapeDtypeStruct((2, bsz, nc, 1, D_A), F32)
    mxu_shape = jax.ShapeDtypeStruct((2, bsz, t, D_A), BF16)
    dir_spec = pl.BlockSpec((2, 1, tm, D_A), lambda i, b: (0, b, i, 0))
    row_spec = pl.BlockSpec((2, 1, CPT, 1, D_A), lambda i, b: (0, b, i, 0, 0))
    out_shape = [mxu_shape, tok_shape, mxu_shape, dir_shape, row_shape, dir_shape, tok_shape, row_shape, row_shape,
                 tok_shape, tok_shape, tok_shape]
    out_specs = [dir_spec, tok(D_A), dir_spec, dir_spec, row_spec, dir_spec, tok(D_B), row_spec, row_spec,
                 tok(D_A), tok(D_A), tok(D_B)]
    return pl.pallas_call(
        functools.partial(_feat_kernel, has_pos, nt),
        grid=(nt, bsz),
        in_specs=in_specs,
        out_specs=out_specs,
        out_shape=out_shape,
        scratch_shapes=[pltpu.VMEM((2, N_SLAB, tm, LANE), F32)] * 4 + [pltpu.VMEM((3, N_SLAB, tm, LANE), F32)],
        compiler_params=pltpu.CompilerParams(dimension_semantics=("parallel", "parallel"),
                                             vmem_limit_bytes=VMEM_LIMIT),
        name="feat",
    )(*args)


def _scan_kernel(nt, tile, *refs):
    (pf_ref, mf_ref, nf_ref, gtf_ref, alf_ref, blf_ref,
     pb_ref, mb_ref, nb_ref, gtb_ref, alb_ref, blb_ref,
     s0_ref, l0_ref, bdm_ref,
     y_ref, hc_ref, s_ref, hl_ref) = refs

    i = pl.program_id(1)
    cpt = tile // CHUNK

    @pl.when(i == 0)
    def _init():
        y_ref[...] = jnp.zeros_like(y_ref)
        s_ref[...] = s0_ref[...]
        hl_ref[...] = l0_ref[...]

    bdm = bdm_ref[...]
    per_dir = ((0, pf_ref, mf_ref, nf_ref, gtf_ref, alf_ref, blf_ref),
               (1, pb_ref, mb_ref, nb_ref, gtb_ref, alb_ref, blb_ref))
    tiles = (i, nt - 1 - i)
    state = [s_ref[0, d] for d in (0, 1)]
    h0 = [hl_ref[0, d:d + 1, :] for d in (0, 1)]
    for cc in range(cpt):
        for d, p_ref, m_ref, n_ref, gt_ref, al_ref, bl_ref in per_dir:
            c = cc if d == 0 else cpt - 1 - cc
            rs = slice(c * CHUNK, (c + 1) * CHUNK)
            rows = pl.ds(pl.multiple_of(tiles[d] * tile + c * CHUNK, CHUNK), CHUNK)
            gt = gt_ref[0, 0, c]
            y_parts, s_parts = [], []
            for g in range(N_GROUP):
                ls = slice(g * GROUP, (g + 1) * GROUP)
                sg = state[d][:, ls]
                y_parts.append(_dot_nt(p_ref[0, 0, rs, ls], _blockdiag(sg, bdm)))
                s_parts.append(sg * gt[:, ls]
                               + jnp.dot(sg.astype(BF16), _blockdiag(m_ref[0, 0, rs, ls], bdm),
                                         preferred_element_type=F32)
                               + n_ref[0, 0, rs, ls])
            state[d] = jnp.concatenate(s_parts, axis=1)
            y_ref[0, rows, :] += jnp.concatenate(y_parts, axis=1)
            hc_ref[d, 0, tiles[d] * cpt + c] = h0[d]
            h0[d] = bl_ref[0, 0, c] + al_ref[0, 0, c] * h0[d]
    for d in (0, 1):
        s_ref[0, d] = state[d]
        hl_ref[0, d:d + 1, :] = h0[d]


def _scan_call(feats, s0, l0, p):
    pm, mm, nm, gt, al, bl = feats
    _, bsz, t, _ = pm.shape
    tile = min(SCAN_TILE, t)
    nt = t // tile
    nc = t // CHUNK

    def dir_specs(d, tile_of):
        big = pl.BlockSpec((1, 1, tile, D_A), lambda b, i: (d, b, tile_of(i), 0))
        small = pl.BlockSpec((1, 1, tile // CHUNK, 1, D_A), lambda b, i: (d, b, tile_of(i), 0, 0))
        return big, small

    big_f, small_f = dir_specs(0, lambda i: i)
    big_b, small_b = dir_specs(1, lambda i: nt - 1 - i)
    in_specs = [big_f, big_f, big_f, small_f, small_f, small_f,
                big_b, big_b, big_b, small_b, small_b, small_b,
                pl.BlockSpec((1, 2, HEAD, D_A), lambda b, i: (b, 0, 0, 0)),
                pl.BlockSpec((1, 2, D_B), lambda b, i: (b, 0, 0)),
                pl.BlockSpec((GROUP, GROUP), lambda b, i: (0, 0))]
    args = [pm, mm, nm, gt, al, bl, pm, mm, nm, gt, al, bl, s0, l0, p["bdm_bf16"]]
    out_specs = [pl.BlockSpec((1, t, D_A), lambda b, i: (b, 0, 0)),
                 pl.BlockSpec((2, 1, nc, 1, D_B), lambda b, i: (0, b, 0, 0, 0)),
                 pl.BlockSpec((1, 2, HEAD, D_A), lambda b, i: (b, 0, 0, 0)),
                 pl.BlockSpec((1, 2, D_B), lambda b, i: (b, 0, 0))]
    out_shape = [jax.ShapeDtypeStruct((bsz, t, D_A), F32),
                 jax.ShapeDtypeStruct((2, bsz, nc, 1, D_B), F32),
                 jax.ShapeDtypeStruct((bsz, 2, HEAD, D_A), F32),
                 jax.ShapeDtypeStruct((bsz, 2, D_B), F32)]
    return pl.pallas_call(
        functools.partial(_scan_kernel, nt, tile),
        grid=(bsz, nt),
        in_specs=in_specs,
        out_specs=out_specs,
        out_shape=out_shape,
        compiler_params=pltpu.CompilerParams(dimension_semantics=("parallel", "arbitrary"),
                                             vmem_limit_bytes=VMEM_LIMIT),
        name="scan",
    )(*args)


def _out_kernel(has_pos, *refs):
    if has_pos:
        x_ref, pos_ref = refs[0], refs[1]
        refs = refs[2:]
    else:
        x_ref, pos_ref = refs[0], None
        refs = refs[1:]
    (y_ref, q_ref, ac_ref, hc_ref, bs_ref, g_ref, bonus_ref, gate_ref, mod_ref,
     avg_ref, lnxg_ref, lnxb_ref, wout_ref, gpost_ref, gpre2_ref, w1_ref, w2_ref, gpost2_ref,
     o_ref) = refs

    x = x_ref[0]
    if has_pos:
        x = x + pos_ref[...]
    mod = mod_ref[0]
    gate1, shift2, scale2, gate2 = mod[2:3], mod[3:4], mod[4:5], mod[5:6]

    y = y_ref[0] + q_ref[0]
    avg = avg_ref[...]

    def head_mean(a):
        return jnp.concatenate([_dot(a[:, t * MXU_TILE:(t + 1) * MXU_TILE], avg)
                                for t in range(D_A // MXU_TILE)], axis=1)

    y_hi = y.astype(BF16)
    y_lo = (y - y_hi.astype(F32)).astype(BF16)
    mu = head_mean(y_hi) + head_mean(y_lo)
    yc = y - mu
    var = head_mean(yc * yc)
    yn = yc * lax.rsqrt(var + LNX_EPS) * lnxg_ref[...] + lnxb_ref[...]
    out_a = (yn + bonus_ref[0]) * g_ref[0]
    hs = jnp.concatenate(
        [bs_ref[0, c * CHUNK:(c + 1) * CHUNK, :]
         + ac_ref[0, 0, c * CHUNK:(c + 1) * CHUNK, :] * hc_ref[0, 0, c]
         + ac_ref[1, 0, c * CHUNK:(c + 1) * CHUNK, :] * hc_ref[1, 0, c]
         for c in range(x_ref.shape[1] // CHUNK)], axis=0)
    out_b = hs * gate_ref[0]
    mix = _dot(jnp.concatenate([out_a, out_b], axis=1), wout_ref[...])
    ms = jnp.mean(mix * mix, axis=-1, keepdims=True)
    x = x + gate1 * ((mix * lax.rsqrt(ms + EPS)) * gpost_ref[...])

    ms = jnp.mean(x * x, axis=-1, keepdims=True)
    h = (x * lax.rsqrt(ms + EPS)) * gpre2_ref[...]
    h = h * (1.0 + scale2) + shift2
    f = _dot(h, w1_ref[...])
    f = jnp.square(jnp.maximum(f, 0.0))
    f = _dot(f, w2_ref[...])
    ms = jnp.mean(f * f, axis=-1, keepdims=True)
    o_ref[0] = x + gate2 * ((f * lax.rsqrt(ms + EPS)) * gpost2_ref[...])


def _out_call(x, pos, mod, mod_row, y, q, ac, hc, bs, g, bonus, gate, p):
    bsz, t, _ = x.shape
    tm = min(OUT_TILE, t)
    nt = t // tm
    has_pos = pos is not None

    def const(shape):
        return pl.BlockSpec(shape, lambda i, b: (0,) * len(shape))

    tok = lambda width: pl.BlockSpec((1, tm, width), lambda i, b: (b, i, 0))
    in_specs = [tok(D_MODEL)]
    args = [x]
    if has_pos:
        in_specs.append(pl.BlockSpec((tm, D_MODEL), lambda i, b: (i, 0)))
        args.append(pos)
    in_specs += [tok(D_A), tok(D_A),
                 pl.BlockSpec((2, 1, tm, D_B), lambda i, b: (0, b, i, 0)),
                 pl.BlockSpec((2, 1, tm // CHUNK, 1, D_B), lambda i, b: (0, b, i, 0, 0))]
    in_specs += [tok(D_A)] * 4
    in_specs += [
        pl.BlockSpec((1, 6, D_MODEL), lambda i, b: (mod_row(b), 0, 0)),
        const((MXU_TILE, MXU_TILE)), const((1, D_A)), const((1, D_A)),
        const((D_MODEL, D_MODEL)), const((1, D_MODEL)), const((1, D_MODEL)),
        const((D_MODEL, D_FF)), const((D_FF, D_MODEL)), const((1, D_MODEL)),
    ]
    args += [y, q, ac, hc, bs, g, bonus, gate, mod,
             p["seg_avg"], p["lnx_g"], p["lnx_b"], p["w_out"], p["g_post_mix"], p["g_pre_mlp"],
             p["w_mlp1"], p["w_mlp2"], p["g_post_mlp"]]
    return pl.pallas_call(
        functools.partial(_out_kernel, has_pos),
        grid=(nt, bsz),
        in_specs=in_specs,
        out_specs=tok(D_MODEL),
        out_shape=jax.ShapeDtypeStruct((bsz, t, D_MODEL), F32),
        compiler_params=pltpu.CompilerParams(dimension_semantics=("parallel", "parallel"),
                                             vmem_limit_bytes=VMEM_LIMIT),
        name="out",
    )(*args)


def _sincos_1d(pos, dim):
    omega = 1.0 / (10000.0 ** (jnp.arange(dim // 2, dtype=F32) / (dim // 2)))
    ang = pos.astype(F32)[:, None] * omega[None, :]
    return jnp.concatenate([jnp.sin(ang), jnp.cos(ang)], axis=-1)


def _grid_pos_embed(n_tokens):
    rows = n_tokens // GRID_W
    half = D_MODEL // 2
    e_row = _sincos_1d(jnp.arange(rows), half)
    e_col = _sincos_1d(jnp.arange(GRID_W), half)
    emb = jnp.concatenate([jnp.broadcast_to(e_row[:, None, :], (rows, GRID_W, half)),
                           jnp.broadcast_to(e_col[None, :, :], (rows, GRID_W, half))], axis=-1)
    return emb.reshape(rows * GRID_W, D_MODEL)


def _blockdiag_pairs(w):
    z = jnp.zeros_like(w[0])
    return jnp.concatenate([jnp.concatenate([w[0], z], axis=1),
                            jnp.concatenate([z, w[1]], axis=1)], axis=0)


def _heads_to_blockdiag(w):
    lead = w.shape[:-3]
    w = w.reshape(lead + (N_GROUP, HEADS_PER_GROUP, HEAD, HEAD))
    eye = jnp.eye(HEADS_PER_GROUP, dtype=w.dtype)
    bd = jnp.einsum('...ghab,hj->...ghajb', w, eye)
    return bd.reshape(lead + (N_GROUP, GROUP, GROUP))


def _state_to_lanes(s):
    b = s.shape[0]
    return jnp.transpose(s, (0, 1, 3, 2, 4)).reshape(b, 2, HEAD, D_A)


def _state_from_lanes(s):
    b = s.shape[0]
    return jnp.transpose(s.reshape(b, 2, HEAD, N_HEAD, HEAD), (0, 1, 3, 2, 4))


def kernel(x_prompt, x_sample, c, state_rwkv, state_lru, c_ctx, w_mod, b_mod, g_pre_mix, g_post_mix,
           g_pre_mlp, g_post_mlp, w_in, rwkv_w0, rwkv_w_up, rwkv_a0, rwkv_a_up, rwkv_g_up, rwkv_k_k,
           rwkv_k_a, rwkv_r_k, rwkv_lnx_g, rwkv_lnx_b, lru_conv_w, lru_conv_b, lru_wa, lru_ba, lru_wx,
           lru_bx, lru_lambda, w_out, w_mlp1, w_mlp2):
    n_ctx = x_prompt.shape[0]
    n_lat = x_sample.shape[0]
    l = 0
    seg = _group_blockdiag_mask()
    seg512 = np.kron(np.eye(N_GROUP, dtype=np.float32), seg)
    p = {
        "g_pre_mix": g_pre_mix[l][None], "g_post_mix": g_post_mix[l][None],
        "g_pre_mlp": g_pre_mlp[l][None], "g_post_mlp": g_post_mlp[l][None],
        "w_in": w_in[l].astype(BF16), "w_out": w_out[l].astype(BF16),
        "w_mlp1": w_mlp1[l].astype(BF16), "w_mlp2": w_mlp2[l].astype(BF16),
        "w0": 0.5 * rwkv_w0[l].reshape(1, 2 * D_A), "a0": 0.5 * rwkv_a0[l].reshape(1, 2 * D_A),
        "wup_bd": (0.5 * _blockdiag_pairs(rwkv_w_up[l])).astype(BF16),
        "aup_bd": (0.5 * _blockdiag_pairs(rwkv_a_up[l])).astype(BF16),
        "g_up": rwkv_g_up[l].astype(BF16),
        "k_k": rwkv_k_k[l][None], "k_a": rwkv_k_a[l][None], "r_k": rwkv_r_k[l].reshape(1, D_A),
        "lnx_g": rwkv_lnx_g[l][None], "lnx_b": rwkv_lnx_b[l][None],
        "conv_w": lru_conv_w[l], "conv_b": lru_conv_b[l][None],
        "wa_bd": (0.5 * _heads_to_blockdiag(lru_wa[l])).astype(BF16), "ba": 0.5 * lru_ba[l],
        "wx_bd": (0.5 * _heads_to_blockdiag(lru_wx[l])).astype(BF16), "bx": 0.5 * lru_bx[l],
        "lam": lru_lambda[l],
        "seg_ones": jnp.asarray(seg512, BF16),
        "seg_avg": jnp.asarray(np.kron(np.eye(MXU_TILE // GROUP, dtype=np.float32), seg) / HEAD, BF16),
        "chunk_masks": jnp.asarray(_chunk_masks()),
        "bdm_bf16": jnp.asarray(seg, BF16),
    }

    m_rows = 16
    c_all = jnp.concatenate([c_ctx[None], c, jnp.zeros((m_rows - 1 - n_lat, D_MODEL), F32)], axis=0)
    mod = _mod_call(c_all, w_mod[l], b_mod[l]).reshape(m_rows, 6, D_MODEL)

    pos = _grid_pos_embed(x_sample.shape[1]).astype(x_sample.dtype)
    ctx_row = lambda b: 0
    lat_row = lambda b: b + 1

    pm, q, mm, nm, gt, ac, bs, bl, al, g, bonus, gate = _feat_call(x_prompt, None, mod, ctx_row, p)
    y, hc, s_ctx, l_ctx = _scan_call(
        (pm, mm, nm, gt, al, bl), jnp.zeros((n_ctx, 2, HEAD, D_A), F32), jnp.zeros((n_ctx, 2, D_B), F32), p)
    y_prompt = _out_call(x_prompt, None, mod, ctx_row, y, q, ac, hc, bs, g, bonus, gate, p)

    pm, q, mm, nm, gt, ac, bs, bl, al, g, bonus, gate = _feat_call(x_sample, pos, mod, lat_row, p)
    y, hc, _, _ = _scan_call((pm, mm, nm, gt, al, bl), _state_to_lanes(state_rwkv[:, l]), state_lru[:, l], p)
    y_sample = _out_call(x_sample, pos, mod, lat_row, y, q, ac, hc, bs, g, bonus, gate, p)

    new_state_rwkv = _state_from_lanes(s_ctx)[:, None].astype(x_prompt.dtype)
    new_state_lru = l_ctx[:, None].astype(x_prompt.dtype)
    return (y_prompt, y_sample, new_state_rwkv, new_state_lru)
```

```python
import functools

import numpy as np
import jax
import jax.numpy as jnp
from jax import lax
from jax.experimental import pallas as pl
from jax.experimental.pallas import tpu as pltpu

F32 = jnp.float32
BF16 = jnp.bfloat16

D_MODEL = 1024
D_A = 512
D_B = 512
HEAD = 64
N_HEAD = 8
R_W = 64
R_A = 64
R_G = 128
D_FF = 4096
D_IN = 2944
GRID_W = 64
CONV_W = 4
LRU_C = 8.0
EPS = 1e-6
LNX_EPS = 64e-5

CHUNK = 64
GROUP = 128
HEADS_PER_GROUP = GROUP // HEAD
N_GROUP = D_A // GROUP
SUB = 8
LANE = 128
N_SLAB = D_A // LANE
HALO = 8
TOKEN_TILE = 256
CPT = TOKEN_TILE // CHUNK
OUT_TILE = 512
SCAN_TILE = 1024
MXU_TILE = 256
V7X_VMEM_BYTES = 64 * 1024 * 1024
VMEM_LIMIT = V7X_VMEM_BYTES - 4 * 1024 * 1024

_O_R, _O_K, _O_V, _O_XW, _O_XA, _O_XG, _O_XB, _O_GB = 0, 512, 1024, 1536, 1664, 1792, 1920, 2432

_M_STRICT, _M_INCL, _M_LEV0 = 0, 1, 2
_LEVELS = (1, 2, 4, 8, 16, 32)
_M_PER_DIR = 2 + len(_LEVELS)
_M_EYE = 2 * _M_PER_DIR


def _dot(a, b):
    return jnp.dot(a.astype(BF16), b.astype(BF16), preferred_element_type=F32)


def _dot_nt(a, b):
    return lax.dot_general(a.astype(BF16), b.astype(BF16), (((1,), (1,)), ((), ())),
                           preferred_element_type=F32)


def _dot_tn(a, b):
    return lax.dot_general(a.astype(BF16), b.astype(BF16), (((0,), (0,)), ((), ())),
                           preferred_element_type=F32)


def _strided_rows(ref, lead, base):
    return [ref[lead + (pl.ds(base + j, SUB, stride=SUB), slice(None))] for j in range(SUB)]


def _natural_rows(ref, lead, base, pieces):
    for j, piece in enumerate(pieces):
        ref[lead + (pl.ds(base + SUB * j, SUB), slice(None))] = piece
    return jnp.concatenate(_strided_rows(ref, lead, base), axis=0)


def _sublane_shift(x, steps, reverse, fill):
    sub = lax.broadcasted_iota(jnp.int32, x.shape, 0)
    if reverse:
        return jnp.where(sub < SUB - steps, pltpu.roll(x, SUB - steps, 0), fill)
    return jnp.where(sub >= steps, pltpu.roll(x, steps, 0), fill)


def _scan_affine(a, b, reverse):
    a, b = list(a), list(b)
    order = range(SUB - 2, -1, -1) if reverse else range(1, SUB)
    for j in order:
        p = j + 1 if reverse else j - 1
        b[j] = a[j] * b[p] + b[j]
        a[j] = a[j] * a[p]
    ta, tb = (a[0], b[0]) if reverse else (a[SUB - 1], b[SUB - 1])
    s = 1
    while s < SUB:
        tb = ta * _sublane_shift(tb, s, reverse, 0.0) + tb
        ta = ta * _sublane_shift(ta, s, reverse, 1.0)
        s *= 2
    ea = _sublane_shift(ta, 1, reverse, 1.0)
    eb = _sublane_shift(tb, 1, reverse, 0.0)
    return [x * ea for x in a], [x * eb + y for x, y in zip(a, b)]


def _scan_sum(x, reverse):
    x = list(x)
    order = range(SUB - 2, -1, -1) if reverse else range(1, SUB)
    for j in order:
        x[j] = x[j] + x[j + 1 if reverse else j - 1]
    t = x[0] if reverse else x[SUB - 1]
    s = 1
    while s < SUB:
        t = t + _sublane_shift(t, s, reverse, 0.0)
        s *= 2
    e = _sublane_shift(t, 1, reverse, 0.0)
    return [y + e for y in x]


def _sigmoid(x):
    return 0.5 * jnp.tanh(0.5 * x) + 0.5


def _blockdiag(x, bdm):
    xb = x.astype(BF16)
    return jnp.concatenate([xb] * HEADS_PER_GROUP, axis=0) * bdm


def _mod_kernel(c_ref, w_ref, b_ref, o_ref):
    c = c_ref[...]
    s = c * _sigmoid(c)
    o_ref[...] = _dot(s, w_ref[...]) + b_ref[...]


def _mod_call(c_all, w_mod, b_mod):
    m = c_all.shape[0]
    n = w_mod.shape[1]
    tn = 1536
    return pl.pallas_call(
        _mod_kernel,
        grid=(n // tn,),
        in_specs=[pl.BlockSpec((m, D_MODEL), lambda j: (0, 0)),
                  pl.BlockSpec((D_MODEL, tn), lambda j: (0, j)),
                  pl.BlockSpec((1, tn), lambda j: (0, j))],
        out_specs=pl.BlockSpec((m, tn), lambda j: (0, j)),
        out_shape=jax.ShapeDtypeStruct((m, n), F32),
        compiler_params=pltpu.CompilerParams(dimension_semantics=("parallel",),
                                             vmem_limit_bytes=VMEM_LIMIT),
        name="mod",
    )(c_all, w_mod, b_mod.reshape(1, n))


def _chunk_masks():
    t = np.arange(CHUNK)[:, None]
    s = (np.arange(GROUP) % CHUNK)[None, :]
    rows = []
    for d in (0, 1):
        before = (s < t) if d == 0 else (s > t)
        rows.append(before)
        rows.append(before | (s == t))
        for b in _LEVELS:
            same = (t // (2 * b)) == (s // (2 * b))
            if d == 0:
                rows.append(same & ((t // b) % 2 == 1) & ((s // b) % 2 == 0))
            else:
                rows.append(same & ((t // b) % 2 == 0) & ((s // b) % 2 == 1))
    rows.append(s == t)
    return np.stack(rows).astype(np.float32)


def _group_blockdiag_mask():
    i = np.arange(GROUP)
    return ((i[:, None] // HEAD) == (i[None, :] // HEAD)).astype(np.float32)


def _feat_kernel(has_pos, nt, *refs):
    if has_pos:
        x_ref, xp_ref, xn_ref, pos_ref, pp_ref, pn_ref = refs[:6]
        refs = refs[6:]
    else:
        x_ref, xp_ref, xn_ref = refs[:3]
        pos_ref = pp_ref = pn_ref = None
        refs = refs[3:]
    (mod_ref, gpre_ref, win_ref, wup_ref, w0_ref, aup_ref, a0_ref, gup_ref,
     kk_ref, ka_ref, rk_ref, seg_ref,
     convw_ref, convb_ref, wa_ref, ba_ref, wx_ref, bx_ref, lam_ref,
     masks_ref, bdm_ref,
     p_o, q_o, m_o, n_o, gt_o, ac_o, bs_o, bl_o, al_o, g_o, bonus_o, gate_o,
     lw_s, lc_s, la_s, lb_s, lo_s) = refs

    i = pl.program_id(0)
    tm = TOKEN_TILE
    mod = mod_ref[0]
    shift1, scale1 = mod[0:1], mod[1:2]

    gain1 = gpre_ref[...] * (1.0 + scale1)

    def normmod(xv):
        ms = jnp.mean(xv * xv, axis=-1, keepdims=True)
        return (xv * lax.rsqrt(ms + EPS)) * gain1 + shift1

    x = x_ref[0]
    halo = jnp.concatenate([xp_ref[0], xn_ref[0]], axis=0)
    if has_pos:
        x = x + pos_ref[...]
        halo = halo + jnp.concatenate([pp_ref[...], pn_ref[...]], axis=0)
    z = _dot(normmod(x), win_ref[...])
    zh = _dot(normmod(halo), win_ref[:, _O_XB:_O_XB + D_B])

    r = z[:, _O_R:_O_R + D_A]
    k = z[:, _O_K:_O_K + D_A]
    v = z[:, _O_V:_O_V + D_A]
    xw = z[:, _O_XW:_O_XW + 2 * R_W]
    xa = z[:, _O_XA:_O_XA + 2 * R_A]
    xg = z[:, _O_XG:_O_XG + R_G]

    g_o[0] = _dot(_sigmoid(xg), gup_ref[...])
    gb = z[:, _O_GB:_O_GB + D_B]
    c_gelu = float(np.sqrt(2.0 / np.pi))
    gate_o[0] = gb * (0.5 + 0.5 * jnp.tanh(gb * (c_gelu + (0.044715 * c_gelu) * (gb * gb))))
    half_c = 0.5 * float(np.exp(-0.5))
    lw2 = -half_c * jnp.tanh(w0_ref[...] + _dot(jnp.tanh(xw), wup_ref[...])) - half_c
    a2 = 0.5 * jnp.tanh(a0_ref[...] + _dot(xa, aup_ref[...])) + 0.5

    kks = k * kk_ref[...]
    ss = _dot(kks * kks, seg_ref[...])
    kk = kks * lax.rsqrt(jnp.maximum(ss, 1e-24))
    ka = ka_ref[...]
    k_fix, k_var = k * (1.0 - ka), k * ka
    kd2 = [k_fix + k_var * a2[:, d * D_A:(d + 1) * D_A] for d in (0, 1)]
    bonus_o[0] = _dot(r * (kd2[0] + kd2[1]) * rk_ref[...], seg_ref[...]) * v

    m_prev = jnp.where(i > 0, 1.0, 0.0)
    m_next = jnp.where(i < nt - 1, 1.0, 0.0)
    ext = jnp.concatenate([zh[:HALO] * m_prev, z[:, _O_XB:_O_XB + D_B], zh[HALO:] * m_next], axis=0)
    n_ext = tm + 2 * HALO
    xc = convb_ref[...]
    for j in range(CONV_W):
        sh = (2 - j) % n_ext
        tap = ext if sh == 0 else pltpu.roll(ext, sh, 0)
        xc = xc + tap[HALO:HALO + tm] * convw_ref[j:j + 1, :]
    for d in (0, 1):
        rg = jnp.concatenate([_dot(xc[:, g * GROUP:(g + 1) * GROUP], wa_ref[d, g]) for g in range(N_GROUP)], 1)
        ig = jnp.concatenate([_dot(xc[:, g * GROUP:(g + 1) * GROUP], wx_ref[d, g]) for g in range(N_GROUP)], 1)
        rg = 0.5 * jnp.tanh(rg + ba_ref[d:d + 1, :]) + 0.5
        ig = 0.5 * jnp.tanh(ig + bx_ref[d:d + 1, :]) + 0.5
        neg_log_a = rg * (LRU_C * jax.nn.softplus(-lam_ref[d:d + 1, :]))
        a_lru = jnp.exp(-neg_log_a)
        b_lru = jnp.sqrt(jnp.tanh(neg_log_a) * (a_lru * a_lru + 1.0)) * (ig * xc)
        for q in range(N_SLAB):
            la_s[d, q] = a_lru[:, q * LANE:(q + 1) * LANE]
            lb_s[d, q] = b_lru[:, q * LANE:(q + 1) * LANE]
    for c in range(CPT):
        base = c * CHUNK
        rs = slice(base, base + CHUNK)
        for q in range(N_SLAB):
            ls = slice(q * LANE, (q + 1) * LANE)
            bsum = None
            for d in (0, 1):
                acum, bcum = _scan_affine(_strided_rows(la_s, (d, q), base), _strided_rows(lb_s, (d, q), base),
                                          d == 1)
                ac_o[d, 0, rs, ls] = _natural_rows(lo_s, (d, q), base, acum)
                bsum = bcum if bsum is None else [x + y for x, y in zip(bsum, bcum)]
                bl_o[d, 0, c, :, ls] = bcum[0][0:1, :] if d == 1 else bcum[SUB - 1][SUB - 1:SUB, :]
                al_o[d, 0, c, :, ls] = acum[0][0:1, :] if d == 1 else acum[SUB - 1][SUB - 1:SUB, :]
            bs_o[0, rs, ls] = _natural_rows(lo_s, (2, q), base, bsum)

    bdm = bdm_ref[...]
    bd = lambda xv: _blockdiag(xv, bdm)
    bdot = lambda lhs, rhs: jnp.dot(lhs.astype(BF16), bd(rhs), preferred_element_type=F32)

    a_t, r_t, b_t, k_t, g_tot = [], [], [], [], []
    neg_kk = -kk
    for d in (0, 1):
        lw = lw2[:, d * D_A:(d + 1) * D_A]
        for q in range(N_SLAB):
            lw_s[d, q] = lw[:, q * LANE:(q + 1) * LANE]
        g_tot.append([])
        lc_rows = []
        for c in range(CPT):
            base = c * CHUNK
            blocks, totals = [], []
            for q in range(N_SLAB):
                pieces = _scan_sum(_strided_rows(lw_s, (d, q), base), d == 1)
                totals.append(pieces[0][0:1, :] if d == 1 else pieces[SUB - 1][SUB - 1:SUB, :])
                blocks.append(_natural_rows(lc_s, (d, q), base, pieces))
            lc_rows.append(jnp.concatenate(blocks, axis=1))
            g_tot[d].append(jnp.exp(jnp.concatenate(totals, axis=1)))
            gt_o[d, 0, c] = g_tot[d][c]
        lc = jnp.concatenate(lc_rows, axis=0)
        e_neg = jnp.exp(-lc)
        a_t.append(neg_kk * jnp.exp(lc - lw))
        r_t.append(r * jnp.exp(lc))
        b_t.append(kk * a2[:, d * D_A:(d + 1) * D_A] * e_neg)
        k_t.append(kd2[d] * e_neg)

    combos = [(d, c, g) for d in (0, 1) for c in range(CPT) for g in range(N_GROUP)]

    def cut(arr, c, g):
        return arr[c * CHUNK:(c + 1) * CHUNK, g * GROUP:(g + 1) * GROUP]

    sc = {}
    for key in combos:
        d, c, g = key
        ar = jnp.concatenate([cut(a_t[d], c, g), cut(r_t[d], c, g)], axis=0)
        rhs = jnp.concatenate([bd(cut(b_t[d], c, g)), bd(cut(k_t[d], c, g))], axis=0)
        sc[key] = _dot_nt(ar, rhs)
    n_ab, n_ak, n_rb, n_rk, tinv = {}, {}, {}, {}, {}
    for key in combos:
        m0 = key[0] * _M_PER_DIR
        n_ab[key] = sc[key][:CHUNK, :GROUP] * masks_ref[m0 + _M_STRICT]
        n_ak[key] = sc[key][:CHUNK, GROUP:] * masks_ref[m0 + _M_STRICT]
        n_rb[key] = sc[key][CHUNK:, :GROUP] * masks_ref[m0 + _M_INCL]
        n_rk[key] = sc[key][CHUNK:, GROUP:] * masks_ref[m0 + _M_INCL]
        tinv[key] = masks_ref[_M_EYE] + n_ab[key] * masks_ref[m0 + _M_LEV0]
    for li in range(1, len(_LEVELS)):
        pm = {}
        for key in combos:
            pm[key] = bdot(n_ab[key] * masks_ref[key[0] * _M_PER_DIR + _M_LEV0 + li], tinv[key])
        for key in combos:
            tinv[key] = tinv[key] + bdot(tinv[key], pm[key])

    kv = {key: bdot(jnp.concatenate([n_ak[key], n_rk[key]], axis=0), cut(v, key[1], key[2])) for key in combos}
    gm = {key: bdot(n_rb[key], tinv[key]) for key in combos}
    wp = {}
    for key in combos:
        d, c, g = key
        lhs = jnp.concatenate([tinv[key], gm[key]], axis=0)
        rhs = jnp.concatenate([bd(cut(a_t[d], c, g)), bd(kv[key][:CHUNK])], axis=1)
        wp[key] = jnp.dot(lhs.astype(BF16), rhs, preferred_element_type=F32)

    lane = lax.broadcasted_iota(jnp.int32, (CHUNK, GROUP), 1)

    def fold(full):
        out = full[(HEADS_PER_GROUP - 1) * HEAD:]
        for h in range(HEADS_PER_GROUP - 2, -1, -1):
            out = jnp.where(lane < (h + 1) * HEAD, full[h * HEAD:(h + 1) * HEAD], out)
        return out

    for key in combos:
        d, c, g = key
        rs = slice(c * CHUNK, (c + 1) * CHUNK)
        ls = slice(g * GROUP, (g + 1) * GROUP)
        w_, u0 = wp[key][:CHUNK, :GROUP], wp[key][:CHUNK, GROUP:]
        vg = cut(v, c, g)
        gt = g_tot[d][c][:, ls]
        bh, kh = cut(b_t[d], c, g) * gt, cut(k_t[d], c, g) * gt
        p_o[d, 0, rs, ls] = (cut(r_t[d], c, g) + wp[key][CHUNK:, :GROUP]).astype(BF16)
        m_o[d, 0, rs, ls] = fold(_dot_tn(w_, bh)).astype(BF16)
        n_o[d, 0, rs, ls] = fold(_dot_tn(jnp.concatenate([u0, vg], axis=0), jnp.concatenate([bh, kh], axis=0)))
        if d == 1:
            other = (0, c, g)
            q_o[0, rs, ls] = ((wp[other][CHUNK:, GROUP:] + kv[other][CHUNK:])
                              + (wp[key][CHUNK:, GROUP:] + kv[key][CHUNK:]))


def _feat_call(x, pos, mod, mod_row, p):
    bsz, t, _ = x.shape
    tm = TOKEN_TILE
    nt = t // tm
    nc = t // CHUNK
    hpt = tm // HALO
    has_pos = pos is not None

    def const(shape):
        return pl.BlockSpec(shape, lambda i, b: (0,) * len(shape))

    tok = lambda width: pl.BlockSpec((1, tm, width), lambda i, b: (b, i, 0))
    prev_i = lambda i: jnp.maximum(i * hpt - 1, 0)
    next_i = lambda i: jnp.minimum((i + 1) * hpt, t // HALO - 1)
    in_specs = [tok(D_MODEL),
                pl.BlockSpec((1, HALO, D_MODEL), lambda i, b: (b, prev_i(i), 0)),
                pl.BlockSpec((1, HALO, D_MODEL), lambda i, b: (b, next_i(i), 0))]
    args = [x, x, x]
    if has_pos:
        in_specs += [pl.BlockSpec((tm, D_MODEL), lambda i, b: (i, 0)),
                     pl.BlockSpec((HALO, D_MODEL), lambda i, b: (prev_i(i), 0)),
                     pl.BlockSpec((HALO, D_MODEL), lambda i, b: (next_i(i), 0))]
        args += [pos, pos, pos]
    n_masks = 2 * _M_PER_DIR + 1
    in_specs += [
        pl.BlockSpec((1, 6, D_MODEL), lambda i, b: (mod_row(b), 0, 0)),
        const((1, D_MODEL)), const((D_MODEL, D_IN)),
        const((2 * R_W, 2 * D_A)), const((1, 2 * D_A)),
        const((2 * R_A, 2 * D_A)), const((1, 2 * D_A)),
        const((R_G, D_A)),
        const((1, D_A)), const((1, D_A)), const((1, D_A)), const((D_A, D_A)),
        const((CONV_W, D_B)), const((1, D_B)),
        const((2, N_GROUP, GROUP, GROUP)), const((2, D_B)),
        const((2, N_GROUP, GROUP, GROUP)), const((2, D_B)), const((2, D_B)),
        const((n_masks, CHUNK, GROUP)), const((GROUP, GROUP)),
    ]
    args += [mod, p["g_pre_mix"], p["w_in"], p["wup_bd"], p["w0"], p["aup_bd"], p["a0"], p["g_up"],
             p["k_k"], p["k_a"], p["r_k"], p["seg_ones"],
             p["conv_w"], p["conv_b"], p["wa_bd"], p["ba"], p["wx_bd"], p["bx"], p["lam"],
             p["chunk_masks"], p["bdm_bf16"]]
    tok_shape = jax.ShapeDtypeStruct((bsz, t, D_A), F32)
    dir_shape = jax.ShapeDtypeStruct((2, bsz, t, D_A), F32)
    row_shape = jax.ShapeDtypeStruct((2, bsz, nc, 1, D_A), F32)
    mxu_shape = jax.ShapeDtypeStruct((2, bsz, t, D_A), BF16)
    dir_spec = pl.BlockSpec((2, 1, tm, D_A), lambda i, b: (0, b, i, 0))
    row_spec = pl.BlockSpec((2, 1, CPT, 1, D_A), lambda i, b: (0, b, i, 0, 0))
    out_shape = [mxu_shape, tok_shape, mxu_shape, dir_shape, row_shape, dir_shape, tok_shape, row_shape, row_shape,
                 tok_shape, tok_shape, tok_shape]
    out_specs = [dir_spec, tok(D_A), dir_spec, dir_spec, row_spec, dir_spec, tok(D_B), row_spec, row_spec,
                 tok(D_A), tok(D_A), tok(D_B)]
    return pl.pallas_call(
        functools.partial(_feat_kernel, has_pos, nt),
        grid=(nt, bsz),
        in_specs=in_specs,
        out_specs=out_specs,
        out_shape=out_shape,
        scratch_shapes=[pltpu.VMEM((2, N_SLAB, tm, LANE), F32)] * 4 + [pltpu.VMEM((3, N_SLAB, tm, LANE), F32)],
        compiler_params=pltpu.CompilerParams(dimension_semantics=("parallel", "parallel"),
                                             vmem_limit_bytes=VMEM_LIMIT),
        name="feat",
    )(*args)


def _scan_kernel(nt, tile, *refs):
    (pf_ref, mf_ref, nf_ref, gtf_ref, alf_ref, blf_ref,
     pb_ref, mb_ref, nb_ref, gtb_ref, alb_ref, blb_ref,
     s0_ref, l0_ref, bdm_ref,
     y_ref, hc_ref, s_ref, hl_ref) = refs

    i = pl.program_id(1)
    cpt = tile // CHUNK

    @pl.when(i == 0)
    def _init():
        y_ref[...] = jnp.zeros_like(y_ref)
        s_ref[...] = s0_ref[...]
        hl_ref[...] = l0_ref[...]

    bdm = bdm_ref[...]
    per_dir = ((0, pf_ref, mf_ref, nf_ref, gtf_ref, alf_ref, blf_ref),
               (1, pb_ref, mb_ref, nb_ref, gtb_ref, alb_ref, blb_ref))
    tiles = (i, nt - 1 - i)
    state = [s_ref[0, d] for d in (0, 1)]
    h0 = [hl_ref[0, d:d + 1, :] for d in (0, 1)]
    for cc in range(cpt):
        for d, p_ref, m_ref, n_ref, gt_ref, al_ref, bl_ref in per_dir:
            c = cc if d == 0 else cpt - 1 - cc
            rs = slice(c * CHUNK, (c + 1) * CHUNK)
            rows = pl.ds(pl.multiple_of(tiles[d] * tile + c * CHUNK, CHUNK), CHUNK)
            gt = gt_ref[0, 0, c]
            y_parts, s_parts = [], []
            for g in range(N_GROUP):
                ls = slice(g * GROUP, (g + 1) * GROUP)
                sg = state[d][:, ls]
                y_parts.append(_dot_nt(p_ref[0, 0, rs, ls], _blockdiag(sg, bdm)))
                s_parts.append(sg * gt[:, ls]
                               + jnp.dot(sg.astype(BF16), _blockdiag(m_ref[0, 0, rs, ls], bdm),
                                         preferred_element_type=F32)
                               + n_ref[0, 0, rs, ls])
            state[d] = jnp.concatenate(s_parts, axis=1)
            y_ref[0, rows, :] += jnp.concatenate(y_parts, axis=1)
            hc_ref[d, 0, tiles[d] * cpt + c] = h0[d]
            h0[d] = bl_ref[0, 0, c] + al_ref[0, 0, c] * h0[d]
    for d in (0, 1):
        s_ref[0, d] = state[d]
        hl_ref[0, d:d + 1, :] = h0[d]


def _scan_call(feats, s0, l0, p):
    pm, mm, nm, gt, al, bl = feats
    _, bsz, t, _ = pm.shape
    tile = min(SCAN_TILE, t)
    nt = t // tile
    nc = t // CHUNK

    def dir_specs(d, tile_of):
        big = pl.BlockSpec((1, 1, tile, D_A), lambda b, i: (d, b, tile_of(i), 0))
        small = pl.BlockSpec((1, 1, tile // CHUNK, 1, D_A), lambda b, i: (d, b, tile_of(i), 0, 0))
        return big, small

    big_f, small_f = dir_specs(0, lambda i: i)
    big_b, small_b = dir_specs(1, lambda i: nt - 1 - i)
    in_specs = [big_f, big_f, big_f, small_f, small_f, small_f,
                big_b, big_b, big_b, small_b, small_b, small_b,
                pl.BlockSpec((1, 2, HEAD, D_A), lambda b, i: (b, 0, 0, 0)),
                pl.BlockSpec((1, 2, D_B), lambda b, i: (b, 0, 0)),
                pl.BlockSpec((GROUP, GROUP), lambda b, i: (0, 0))]
    args = [pm, mm, nm, gt, al, bl, pm, mm, nm, gt, al, bl, s0, l0, p["bdm_bf16"]]
    out_specs = [pl.BlockSpec((1, t, D_A), lambda b, i: (b, 0, 0)),
                 pl.BlockSpec((2, 1, nc, 1, D_B), lambda b, i: (0, b, 0, 0, 0)),
                 pl.BlockSpec((1, 2, HEAD, D_A), lambda b, i: (b, 0, 0, 0)),
                 pl.BlockSpec((1, 2, D_B), lambda b, i: (b, 0, 0))]
    out_shape = [jax.ShapeDtypeStruct((bsz, t, D_A), F32),
                 jax.ShapeDtypeStruct((2, bsz, nc, 1, D_B), F32),
                 jax.ShapeDtypeStruct((bsz, 2, HEAD, D_A), F32),
                 jax.ShapeDtypeStruct((bsz, 2, D_B), F32)]
    return pl.pallas_call(
        functools.partial(_scan_kernel, nt, tile),
        grid=(bsz, nt),
        in_specs=in_specs,
        out_specs=out_specs,
        out_shape=out_shape,
        compiler_params=pltpu.CompilerParams(dimension_semantics=("parallel", "arbitrary"),
                                             vmem_limit_bytes=VMEM_LIMIT),
        name="scan",
    )(*args)


def _out_kernel(has_pos, *refs):
    if has_pos:
        x_ref, pos_ref = refs[0], refs[1]
        refs = refs[2:]
    else:
        x_ref, pos_ref = refs[0], None
        refs = refs[1:]
    (y_ref, q_ref, ac_ref, hc_ref, bs_ref, g_ref, bonus_ref, gate_ref, mod_ref,
     avg_ref, lnxg_ref, lnxb_ref, wout_ref, gpost_ref, gpre2_ref, w1_ref, w2_ref, gpost2_ref,
     o_ref) = refs

    x = x_ref[0]
    if has_pos:
        x = x + pos_ref[...]
    mod = mod_ref[0]
    gate1, shift2, scale2, gate2 = mod[2:3], mod[3:4], mod[4:5], mod[5:6]

    y = y_ref[0] + q_ref[0]
    avg = avg_ref[...]

    def head_mean(a):
        return jnp.concatenate([_dot(a[:, t * MXU_TILE:(t + 1) * MXU_TILE], avg)
                                for t in range(D_A // MXU_TILE)], axis=1)

    yc = y - head_mean(y)
    var = head_mean(yc * yc)
    yn = yc * lax.rsqrt(var + LNX_EPS) * lnxg_ref[...] + lnxb_ref[...]
    out_a = (yn + bonus_ref[0]) * g_ref[0]
    hs = jnp.concatenate(
        [bs_ref[0, c * CHUNK:(c + 1) * CHUNK, :]
         + ac_ref[0, 0, c * CHUNK:(c + 1) * CHUNK, :] * hc_ref[0, 0, c]
         + ac_ref[1, 0, c * CHUNK:(c + 1) * CHUNK, :] * hc_ref[1, 0, c]
         for c in range(x_ref.shape[1] // CHUNK)], axis=0)
    out_b = hs * gate_ref[0]
    mix = _dot(jnp.concatenate([out_a, out_b], axis=1), wout_ref[...])
    ms = jnp.mean(mix * mix, axis=-1, keepdims=True)
    x = x + (mix * lax.rsqrt(ms + EPS)) * (gate1 * gpost_ref[...])

    ms = jnp.mean(x * x, axis=-1, keepdims=True)
    h = (x * lax.rsqrt(ms + EPS)) * (gpre2_ref[...] * (1.0 + scale2)) + shift2
    f = _dot(h, w1_ref[...])
    f = jnp.square(jnp.maximum(f, 0.0))
    f = _dot(f, w2_ref[...])
    ms = jnp.mean(f * f, axis=-1, keepdims=True)
    o_ref[0] = x + (f * lax.rsqrt(ms + EPS)) * (gate2 * gpost2_ref[...])


def _out_call(x, pos, mod, mod_row, y, q, ac, hc, bs, g, bonus, gate, p):
    bsz, t, _ = x.shape
    tm = min(OUT_TILE, t)
    nt = t // tm
    has_pos = pos is not None

    def const(shape):
        return pl.BlockSpec(shape, lambda i, b: (0,) * len(shape))

    tok = lambda width: pl.BlockSpec((1, tm, width), lambda i, b: (b, i, 0))
    in_specs = [tok(D_MODEL)]
    args = [x]
    if has_pos:
        in_specs.append(pl.BlockSpec((tm, D_MODEL), lambda i, b: (i, 0)))
        args.append(pos)
    in_specs += [tok(D_A), tok(D_A),
                 pl.BlockSpec((2, 1, tm, D_B), lambda i, b: (0, b, i, 0)),
                 pl.BlockSpec((2, 1, tm // CHUNK, 1, D_B), lambda i, b: (0, b, i, 0, 0))]
    in_specs += [tok(D_A)] * 4
    in_specs += [
        pl.BlockSpec((1, 6, D_MODEL), lambda i, b: (mod_row(b), 0, 0)),
        const((MXU_TILE, MXU_TILE)), const((1, D_A)), const((1, D_A)),
        const((D_MODEL, D_MODEL)), const((1, D_MODEL)), const((1, D_MODEL)),
        const((D_MODEL, D_FF)), const((D_FF, D_MODEL)), const((1, D_MODEL)),
    ]
    args += [y, q, ac, hc, bs, g, bonus, gate, mod,
             p["seg_avg"], p["lnx_g"], p["lnx_b"], p["w_out"], p["g_post_mix"], p["g_pre_mlp"],
             p["w_mlp1"], p["w_mlp2"], p["g_post_mlp"]]
    return pl.pallas_call(
        functools.partial(_out_kernel, has_pos),
        grid=(nt, bsz),
        in_specs=in_specs,
        out_specs=tok(D_MODEL),
        out_shape=jax.ShapeDtypeStruct((bsz, t, D_MODEL), F32),
        compiler_params=pltpu.CompilerParams(dimension_semantics=("parallel", "parallel"),
                                             vmem_limit_bytes=VMEM_LIMIT),
        name="out",
    )(*args)


def _sincos_1d(pos, dim):
    omega = 1.0 / (10000.0 ** (jnp.arange(dim // 2, dtype=F32) / (dim // 2)))
    ang = pos.astype(F32)[:, None] * omega[None, :]
    return jnp.concatenate([jnp.sin(ang), jnp.cos(ang)], axis=-1)


def _grid_pos_embed(n_tokens):
    rows = n_tokens // GRID_W
    half = D_MODEL // 2
    e_row = _sincos_1d(jnp.arange(rows), half)
    e_col = _sincos_1d(jnp.arange(GRID_W), half)
    emb = jnp.concatenate([jnp.broadcast_to(e_row[:, None, :], (rows, GRID_W, half)),
                           jnp.broadcast_to(e_col[None, :, :], (rows, GRID_W, half))], axis=-1)
    return emb.reshape(rows * GRID_W, D_MODEL)


def _blockdiag_pairs(w):
    z = jnp.zeros_like(w[0])
    return jnp.concatenate([jnp.concatenate([w[0], z], axis=1),
                            jnp.concatenate([z, w[1]], axis=1)], axis=0)


def _heads_to_blockdiag(w):
    lead = w.shape[:-3]
    w = w.reshape(lead + (N_GROUP, HEADS_PER_GROUP, HEAD, HEAD))
    eye = jnp.eye(HEADS_PER_GROUP, dtype=w.dtype)
    bd = jnp.einsum('...ghab,hj->...ghajb', w, eye)
    return bd.reshape(lead + (N_GROUP, GROUP, GROUP))


def _state_to_lanes(s):
    b = s.shape[0]
    return jnp.transpose(s, (0, 1, 3, 2, 4)).reshape(b, 2, HEAD, D_A)


def _state_from_lanes(s):
    b = s.shape[0]
    return jnp.transpose(s.reshape(b, 2, HEAD, N_HEAD, HEAD), (0, 1, 3, 2, 4))


def kernel(x_prompt, x_sample, c, state_rwkv, state_lru, c_ctx, w_mod, b_mod, g_pre_mix, g_post_mix,
           g_pre_mlp, g_post_mlp, w_in, rwkv_w0, rwkv_w_up, rwkv_a0, rwkv_a_up, rwkv_g_up, rwkv_k_k,
           rwkv_k_a, rwkv_r_k, rwkv_lnx_g, rwkv_lnx_b, lru_conv_w, lru_conv_b, lru_wa, lru_ba, lru_wx,
           lru_bx, lru_lambda, w_out, w_mlp1, w_mlp2):
    n_ctx = x_prompt.shape[0]
    n_lat = x_sample.shape[0]
    l = 0
    seg = _group_blockdiag_mask()
    seg512 = np.kron(np.eye(N_GROUP, dtype=np.float32), seg)
    p = {
        "g_pre_mix": g_pre_mix[l][None], "g_post_mix": g_post_mix[l][None],
        "g_pre_mlp": g_pre_mlp[l][None], "g_post_mlp": g_post_mlp[l][None],
        "w_in": w_in[l].astype(BF16), "w_out": w_out[l].astype(BF16),
        "w_mlp1": w_mlp1[l].astype(BF16), "w_mlp2": w_mlp2[l].astype(BF16),
        "w0": 0.5 * rwkv_w0[l].reshape(1, 2 * D_A), "a0": 0.5 * rwkv_a0[l].reshape(1, 2 * D_A),
        "wup_bd": (0.5 * _blockdiag_pairs(rwkv_w_up[l])).astype(BF16),
        "aup_bd": (0.5 * _blockdiag_pairs(rwkv_a_up[l])).astype(BF16),
        "g_up": rwkv_g_up[l].astype(BF16),
        "k_k": rwkv_k_k[l][None], "k_a": rwkv_k_a[l][None], "r_k": rwkv_r_k[l].reshape(1, D_A),
        "lnx_g": rwkv_lnx_g[l][None], "lnx_b": rwkv_lnx_b[l][None],
        "conv_w": lru_conv_w[l], "conv_b": lru_conv_b[l][None],
        "wa_bd": (0.5 * _heads_to_blockdiag(lru_wa[l])).astype(BF16), "ba": 0.5 * lru_ba[l],
        "wx_bd": (0.5 * _heads_to_blockdiag(lru_wx[l])).astype(BF16), "bx": 0.5 * lru_bx[l],
        "lam": lru_lambda[l],
        "seg_ones": jnp.asarray(seg512, BF16),
        "seg_avg": jnp.asarray(np.kron(np.eye(MXU_TILE // GROUP, dtype=np.float32), seg) / HEAD, BF16),
        "chunk_masks": jnp.asarray(_chunk_masks()),
        "bdm_bf16": jnp.asarray(seg, BF16),
    }

    m_rows = 16
    c_all = jnp.concatenate([c_ctx[None], c, jnp.zeros((m_rows - 1 - n_lat, D_MODEL), F32)], axis=0)
    mod = _mod_call(c_all, w_mod[l], b_mod[l]).reshape(m_rows, 6, D_MODEL)

    pos = _grid_pos_embed(x_sample.shape[1]).astype(x_sample.dtype)
    ctx_row = lambda b: 0
    lat_row = lambda b: b + 1

    pm, q, mm, nm, gt, ac, bs, bl, al, g, bonus, gate = _feat_call(x_prompt, None, mod, ctx_row, p)
    y, hc, s_ctx, l_ctx = _scan_call(
        (pm, mm, nm, gt, al, bl), jnp.zeros((n_ctx, 2, HEAD, D_A), F32), jnp.zeros((n_ctx, 2, D_B), F32), p)
    y_prompt = _out_call(x_prompt, None, mod, ctx_row, y, q, ac, hc, bs, g, bonus, gate, p)

    pm, q, mm, nm, gt, ac, bs, bl, al, g, bonus, gate = _feat_call(x_sample, pos, mod, lat_row, p)
    y, hc, _, _ = _scan_call((pm, mm, nm, gt, al, bl), _state_to_lanes(state_rwkv[:, l]), state_lru[:, l], p)
    y_sample = _out_call(x_sample, pos, mod, lat_row, y, q, ac, hc, bs, g, bonus, gate, p)

    new_state_rwkv = _state_from_lanes(s_ctx)[:, None].astype(x_prompt.dtype)
    new_state_lru = l_ctx[:, None].astype(x_prompt.dtype)
    return (y_prompt, y_sample, new_state_rwkv, new_state_lru)
```

```python
import functools

import numpy as np
import jax
import jax.numpy as jnp
from jax import lax
from jax.experimental import pallas as pl
from jax.experimental.pallas import tpu as pltpu

F32 = jnp.float32
BF16 = jnp.bfloat16

D_MODEL = 1024
D_A = 512
D_B = 512
HEAD = 64
N_HEAD = 8
R_W = 64
R_A = 64
R_G = 128
D_FF = 4096
D_IN = 2944
GRID_W = 64
CONV_W = 4
LRU_C = 8.0
EPS = 1e-6
LNX_EPS = 64e-5

CHUNK = 64
GROUP = 128
HEADS_PER_GROUP = GROUP // HEAD
N_GROUP = D_A // GROUP
SUB = 8
LANE = 128
N_SLAB = D_A // LANE
HALO = 8
TOKEN_TILE = 256
CPT = TOKEN_TILE // CHUNK
OUT_TILE = 512
SCAN_TILE = 1024
MXU_TILE = 256
V7X_VMEM_BYTES = 64 * 1024 * 1024
VMEM_LIMIT = V7X_VMEM_BYTES - 4 * 1024 * 1024

_O_R, _O_K, _O_V, _O_XW, _O_XA, _O_XG, _O_XB, _O_GB = 0, 512, 1024, 1536, 1664, 1792, 1920, 2432

_M_STRICT, _M_INCL, _M_LEV0 = 0, 1, 2
_LEVELS = (1, 2, 4, 8, 16, 32)
_M_PER_DIR = 2 + len(_LEVELS)
_M_EYE = 2 * _M_PER_DIR


def _dot(a, b):
    return jnp.dot(a.astype(BF16), b.astype(BF16), preferred_element_type=F32)


def _dot_nt(a, b):
    return lax.dot_general(a.astype(BF16), b.astype(BF16), (((1,), (1,)), ((), ())),
                           preferred_element_type=F32)


def _dot_tn(a, b):
    return lax.dot_general(a.astype(BF16), b.astype(BF16), (((0,), (0,)), ((), ())),
                           preferred_element_type=F32)


def _strided_rows(ref, lead, base):
    return [ref[lead + (pl.ds(base + j, SUB, stride=SUB), slice(None))] for j in range(SUB)]


def _natural_rows(ref, lead, base, pieces):
    for j, piece in enumerate(pieces):
        ref[lead + (pl.ds(base + SUB * j, SUB), slice(None))] = piece
    return jnp.concatenate(_strided_rows(ref, lead, base), axis=0)


def _sublane_shift(x, steps, reverse, fill):
    sub = lax.broadcasted_iota(jnp.int32, x.shape, 0)
    if reverse:
        return jnp.where(sub < SUB - steps, pltpu.roll(x, SUB - steps, 0), fill)
    return jnp.where(sub >= steps, pltpu.roll(x, steps, 0), fill)


def _scan_affine(a, b, reverse):
    a, b = list(a), list(b)
    order = range(SUB - 2, -1, -1) if reverse else range(1, SUB)
    for j in order:
        p = j + 1 if reverse else j - 1
        b[j] = a[j] * b[p] + b[j]
        a[j] = a[j] * a[p]
    ta, tb = (a[0], b[0]) if reverse else (a[SUB - 1], b[SUB - 1])
    s = 1
    while s < SUB:
        tb = ta * _sublane_shift(tb, s, reverse, 0.0) + tb
        ta = ta * _sublane_shift(ta, s, reverse, 1.0)
        s *= 2
    ea = _sublane_shift(ta, 1, reverse, 1.0)
    eb = _sublane_shift(tb, 1, reverse, 0.0)
    return [x * ea for x in a], [x * eb + y for x, y in zip(a, b)]


def _scan_sum(x, reverse):
    x = list(x)
    order = range(SUB - 2, -1, -1) if reverse else range(1, SUB)
    for j in order:
        x[j] = x[j] + x[j + 1 if reverse else j - 1]
    t = x[0] if reverse else x[SUB - 1]
    s = 1
    while s < SUB:
        t = t + _sublane_shift(t, s, reverse, 0.0)
        s *= 2
    e = _sublane_shift(t, 1, reverse, 0.0)
    return [y + e for y in x]


def _sigmoid(x):
    return 0.5 * jnp.tanh(0.5 * x) + 0.5


def _blockdiag(x, bdm):
    xb = x.astype(BF16)
    return jnp.concatenate([xb] * HEADS_PER_GROUP, axis=0) * bdm


def _mod_kernel(c_ref, w_ref, b_ref, o_ref):
    c = c_ref[...]
    s = c * _sigmoid(c)
    o_ref[...] = _dot(s, w_ref[...]) + b_ref[...]


def _mod_call(c_all, w_mod, b_mod):
    m = c_all.shape[0]
    n = w_mod.shape[1]
    tn = 1536
    return pl.pallas_call(
        _mod_kernel,
        grid=(n // tn,),
        in_specs=[pl.BlockSpec((m, D_MODEL), lambda j: (0, 0)),
                  pl.BlockSpec((D_MODEL, tn), lambda j: (0, j)),
                  pl.BlockSpec((1, tn), lambda j: (0, j))],
        out_specs=pl.BlockSpec((m, tn), lambda j: (0, j)),
        out_shape=jax.ShapeDtypeStruct((m, n), F32),
        compiler_params=pltpu.CompilerParams(dimension_semantics=("parallel",),
                                             vmem_limit_bytes=VMEM_LIMIT),
        name="mod",
    )(c_all, w_mod, b_mod.reshape(1, n))


def _chunk_masks():
    t = np.arange(CHUNK)[:, None]
    s = (np.arange(GROUP) % CHUNK)[None, :]
    rows = []
    for d in (0, 1):
        before = (s < t) if d == 0 else (s > t)
        rows.append(before)
        rows.append(before | (s == t))
        for b in _LEVELS:
            same = (t // (2 * b)) == (s // (2 * b))
            if d == 0:
                rows.append(same & ((t // b) % 2 == 1) & ((s // b) % 2 == 0))
            else:
                rows.append(same & ((t // b) % 2 == 0) & ((s // b) % 2 == 1))
    rows.append(s == t)
    return np.stack(rows).astype(np.float32)


def _group_blockdiag_mask():
    i = np.arange(GROUP)
    return ((i[:, None] // HEAD) == (i[None, :] // HEAD)).astype(np.float32)


def _feat_kernel(has_pos, nt, n_side, *refs):
    if has_pos:
        x_ref, xp_ref, xn_ref, pos_ref, pp_ref, pn_ref = refs[:6]
        refs = refs[6:]
    else:
        x_ref, xp_ref, xn_ref = refs[:3]
        pos_ref = pp_ref = pn_ref = None
        refs = refs[3:]
    (mod_ref, gpre_ref, win_ref, wup_ref, w0_ref, aup_ref, a0_ref, gup_ref,
     kk_ref, ka_ref, rk_ref, seg_ref,
     convw_ref, convb_ref, wa_ref, ba_ref, wx_ref, bx_ref, lam_ref,
     masks_ref, bdm_ref) = refs[:21]
    side_in, refs = refs[21:21 + n_side], refs[21 + n_side:]
    (p_o, q_o, m_o, n_o, gt_o, ac_o, bs_o, bl_o, al_o, g_o, bonus_o, gate_o) = refs[:12]
    side_out, refs = refs[12:12 + n_side], refs[12 + n_side:]
    lw_s, lc_s, la_s, lb_s, lo_s = refs

    for src, dst in zip(side_in, side_out):
        dst[...] = src[...].astype(BF16)

    i = pl.program_id(0)
    tm = TOKEN_TILE
    mod = mod_ref[0]
    shift1, scale1 = mod[0:1], mod[1:2]

    gain1 = gpre_ref[...] * (1.0 + scale1)

    def normmod(xv):
        ms = jnp.mean(xv * xv, axis=-1, keepdims=True)
        return (xv * lax.rsqrt(ms + EPS)) * gain1 + shift1

    x = x_ref[0]
    halo = jnp.concatenate([xp_ref[0], xn_ref[0]], axis=0)
    if has_pos:
        x = x + pos_ref[...]
        halo = halo + jnp.concatenate([pp_ref[...], pn_ref[...]], axis=0)
    z = _dot(normmod(x), win_ref[...])
    zh = _dot(normmod(halo), win_ref[:, _O_XB:_O_XB + D_B])

    r = z[:, _O_R:_O_R + D_A]
    k = z[:, _O_K:_O_K + D_A]
    v = z[:, _O_V:_O_V + D_A]
    xw = z[:, _O_XW:_O_XW + 2 * R_W]
    xa = z[:, _O_XA:_O_XA + 2 * R_A]
    xg = z[:, _O_XG:_O_XG + R_G]

    g_o[0] = _dot(_sigmoid(xg), gup_ref[...])
    gb = z[:, _O_GB:_O_GB + D_B]
    c_gelu = float(np.sqrt(2.0 / np.pi))
    gate_o[0] = gb * (0.5 + 0.5 * jnp.tanh(gb * (c_gelu + (0.044715 * c_gelu) * (gb * gb))))
    half_c = 0.5 * float(np.exp(-0.5))
    lw2 = -half_c * jnp.tanh(w0_ref[...] + _dot(jnp.tanh(xw), wup_ref[...])) - half_c
    a2 = 0.5 * jnp.tanh(a0_ref[...] + _dot(xa, aup_ref[...])) + 0.5

    kks = k * kk_ref[...]
    ss = _dot(kks * kks, seg_ref[...])
    kk = kks * lax.rsqrt(jnp.maximum(ss, 1e-24))
    ka = ka_ref[...]
    k_fix, k_var = k * (1.0 - ka), k * ka
    kd2 = [k_fix + k_var * a2[:, d * D_A:(d + 1) * D_A] for d in (0, 1)]
    bonus_o[0] = _dot(r * (kd2[0] + kd2[1]) * rk_ref[...], seg_ref[...]) * v

    m_prev = jnp.where(i > 0, 1.0, 0.0)
    m_next = jnp.where(i < nt - 1, 1.0, 0.0)
    ext = jnp.concatenate([zh[:HALO] * m_prev, z[:, _O_XB:_O_XB + D_B], zh[HALO:] * m_next], axis=0)
    n_ext = tm + 2 * HALO
    xc = convb_ref[...]
    for j in range(CONV_W):
        sh = (2 - j) % n_ext
        tap = ext if sh == 0 else pltpu.roll(ext, sh, 0)
        xc = xc + tap[HALO:HALO + tm] * convw_ref[j:j + 1, :]
    for d in (0, 1):
        rg = jnp.concatenate([_dot(xc[:, g * GROUP:(g + 1) * GROUP], wa_ref[d, g]) for g in range(N_GROUP)], 1)
        ig = jnp.concatenate([_dot(xc[:, g * GROUP:(g + 1) * GROUP], wx_ref[d, g]) for g in range(N_GROUP)], 1)
        rg = 0.5 * jnp.tanh(rg + ba_ref[d:d + 1, :]) + 0.5
        ig = 0.5 * jnp.tanh(ig + bx_ref[d:d + 1, :]) + 0.5
        neg_log_a = rg * (LRU_C * jax.nn.softplus(-lam_ref[d:d + 1, :]))
        a_lru = jnp.exp(-neg_log_a)
        b_lru = jnp.sqrt(jnp.tanh(neg_log_a) * (a_lru * a_lru + 1.0)) * (ig * xc)
        for q in range(N_SLAB):
            la_s[d, q] = a_lru[:, q * LANE:(q + 1) * LANE]
            lb_s[d, q] = b_lru[:, q * LANE:(q + 1) * LANE]
    for c in range(CPT):
        base = c * CHUNK
        rs = slice(base, base + CHUNK)
        for q in range(N_SLAB):
            ls = slice(q * LANE, (q + 1) * LANE)
            bsum = None
            for d in (0, 1):
                acum, bcum = _scan_affine(_strided_rows(la_s, (d, q), base), _strided_rows(lb_s, (d, q), base),
                                          d == 1)
                ac_o[d, 0, rs, ls] = _natural_rows(lo_s, (d, q), base, acum)
                bsum = bcum if bsum is None else [x + y for x, y in zip(bsum, bcum)]
                bl_o[d, 0, c, :, ls] = bcum[0][0:1, :] if d == 1 else bcum[SUB - 1][SUB - 1:SUB, :]
                al_o[d, 0, c, :, ls] = acum[0][0:1, :] if d == 1 else acum[SUB - 1][SUB - 1:SUB, :]
            bs_o[0, rs, ls] = _natural_rows(lo_s, (2, q), base, bsum)

    bdm = bdm_ref[...]
    bd = lambda xv: _blockdiag(xv, bdm)
    bdot = lambda lhs, rhs: jnp.dot(lhs.astype(BF16), bd(rhs), preferred_element_type=F32)

    a_t, r_t, b_t, k_t, g_tot = [], [], [], [], []
    neg_kk = -kk
    for d in (0, 1):
        lw = lw2[:, d * D_A:(d + 1) * D_A]
        for q in range(N_SLAB):
            lw_s[d, q] = lw[:, q * LANE:(q + 1) * LANE]
        g_tot.append([])
        lc_rows = []
        for c in range(CPT):
            base = c * CHUNK
            blocks, totals = [], []
            for q in range(N_SLAB):
                pieces = _scan_sum(_strided_rows(lw_s, (d, q), base), d == 1)
                totals.append(pieces[0][0:1, :] if d == 1 else pieces[SUB - 1][SUB - 1:SUB, :])
                blocks.append(_natural_rows(lc_s, (d, q), base, pieces))
            lc_rows.append(jnp.concatenate(blocks, axis=1))
            g_tot[d].append(jnp.exp(jnp.concatenate(totals, axis=1)))
            gt_o[d, 0, c] = g_tot[d][c]
        lc = jnp.concatenate(lc_rows, axis=0)
        e_neg = jnp.exp(-lc)
        a_t.append(neg_kk * jnp.exp(lc - lw))
        r_t.append(r * jnp.exp(lc))
        b_t.append(kk * a2[:, d * D_A:(d + 1) * D_A] * e_neg)
        k_t.append(kd2[d] * e_neg)

    combos = [(d, c, g) for d in (0, 1) for c in range(CPT) for g in range(N_GROUP)]

    def cut(arr, c, g):
        return arr[c * CHUNK:(c + 1) * CHUNK, g * GROUP:(g + 1) * GROUP]

    sc = {}
    for key in combos:
        d, c, g = key
        ar = jnp.concatenate([cut(a_t[d], c, g), cut(r_t[d], c, g)], axis=0)
        rhs = jnp.concatenate([bd(cut(b_t[d], c, g)), bd(cut(k_t[d], c, g))], axis=0)
        sc[key] = _dot_nt(ar, rhs)
    n_ab, n_ak, n_rb, n_rk, tinv = {}, {}, {}, {}, {}
    for key in combos:
        m0 = key[0] * _M_PER_DIR
        n_ab[key] = sc[key][:CHUNK, :GROUP] * masks_ref[m0 + _M_STRICT]
        n_ak[key] = sc[key][:CHUNK, GROUP:] * masks_ref[m0 + _M_STRICT]
        n_rb[key] = sc[key][CHUNK:, :GROUP] * masks_ref[m0 + _M_INCL]
        n_rk[key] = sc[key][CHUNK:, GROUP:] * masks_ref[m0 + _M_INCL]
        tinv[key] = masks_ref[_M_EYE] + n_ab[key] * masks_ref[m0 + _M_LEV0]
    for li in range(1, len(_LEVELS)):
        pm = {}
        for key in combos:
            pm[key] = bdot(n_ab[key] * masks_ref[key[0] * _M_PER_DIR + _M_LEV0 + li], tinv[key])
        for key in combos:
            tinv[key] = tinv[key] + bdot(tinv[key], pm[key])

    kv = {key: bdot(jnp.concatenate([n_ak[key], n_rk[key]], axis=0), cut(v, key[1], key[2])) for key in combos}
    gm = {key: bdot(n_rb[key], tinv[key]) for key in combos}
    wp = {}
    for key in combos:
        d, c, g = key
        lhs = jnp.concatenate([tinv[key], gm[key]], axis=0)
        rhs = jnp.concatenate([bd(cut(a_t[d], c, g)), bd(kv[key][:CHUNK])], axis=1)
        wp[key] = jnp.dot(lhs.astype(BF16), rhs, preferred_element_type=F32)

    lane = lax.broadcasted_iota(jnp.int32, (CHUNK, GROUP), 1)

    def fold(full):
        out = full[(HEADS_PER_GROUP - 1) * HEAD:]
        for h in range(HEADS_PER_GROUP - 2, -1, -1):
            out = jnp.where(lane < (h + 1) * HEAD, full[h * HEAD:(h + 1) * HEAD], out)
        return out

    for key in combos:
        d, c, g = key
        rs = slice(c * CHUNK, (c + 1) * CHUNK)
        ls = slice(g * GROUP, (g + 1) * GROUP)
        w_, u0 = wp[key][:CHUNK, :GROUP], wp[key][:CHUNK, GROUP:]
        vg = cut(v, c, g)
        gt = g_tot[d][c][:, ls]
        bh, kh = cut(b_t[d], c, g) * gt, cut(k_t[d], c, g) * gt
        p_o[d, 0, rs, ls] = (cut(r_t[d], c, g) + wp[key][CHUNK:, :GROUP]).astype(BF16)
        m_o[d, 0, rs, ls] = fold(_dot_tn(w_, bh)).astype(BF16)
        n_o[d, 0, rs, ls] = fold(_dot_tn(jnp.concatenate([u0, vg], axis=0), jnp.concatenate([bh, kh], axis=0)))
        if d == 1:
            other = (0, c, g)
            q_o[0, rs, ls] = ((wp[other][CHUNK:, GROUP:] + kv[other][CHUNK:])
                              + (wp[key][CHUNK:, GROUP:] + kv[key][CHUNK:]))


def _feat_call(x, pos, mod, mod_row, p, side_casts=()):
    bsz, t, _ = x.shape
    tm = TOKEN_TILE
    nt = t // tm
    nc = t // CHUNK
    hpt = tm // HALO
    has_pos = pos is not None

    def const(shape):
        return pl.BlockSpec(shape, lambda i, b: (0,) * len(shape))

    tok = lambda width: pl.BlockSpec((1, tm, width), lambda i, b: (b, i, 0))
    prev_i = lambda i: jnp.maximum(i * hpt - 1, 0)
    next_i = lambda i: jnp.minimum((i + 1) * hpt, t // HALO - 1)
    in_specs = [tok(D_MODEL),
                pl.BlockSpec((1, HALO, D_MODEL), lambda i, b: (b, prev_i(i), 0)),
                pl.BlockSpec((1, HALO, D_MODEL), lambda i, b: (b, next_i(i), 0))]
    args = [x, x, x]
    if has_pos:
        in_specs += [pl.BlockSpec((tm, D_MODEL), lambda i, b: (i, 0)),
                     pl.BlockSpec((HALO, D_MODEL), lambda i, b: (prev_i(i), 0)),
                     pl.BlockSpec((HALO, D_MODEL), lambda i, b: (next_i(i), 0))]
        args += [pos, pos, pos]
    n_masks = 2 * _M_PER_DIR + 1
    in_specs += [
        pl.BlockSpec((1, 6, D_MODEL), lambda i, b: (mod_row(b), 0, 0)),
        const((1, D_MODEL)), const((D_MODEL, D_IN)),
        const((2 * R_W, 2 * D_A)), const((1, 2 * D_A)),
        const((2 * R_A, 2 * D_A)), const((1, 2 * D_A)),
        const((R_G, D_A)),
        const((1, D_A)), const((1, D_A)), const((1, D_A)), const((D_A, D_A)),
        const((CONV_W, D_B)), const((1, D_B)),
        const((2, N_GROUP, GROUP, GROUP)), const((2, D_B)),
        const((2, N_GROUP, GROUP, GROUP)), const((2, D_B)), const((2, D_B)),
        const((n_masks, CHUNK, GROUP)), const((GROUP, GROUP)),
    ]
    args += [mod, p["g_pre_mix"], p["w_in"], p["wup_bd"], p["w0"], p["aup_bd"], p["a0"], p["g_up"],
             p["k_k"], p["k_a"], p["r_k"], p["seg_ones"],
             p["conv_w"], p["conv_b"], p["wa_bd"], p["ba"], p["wx_bd"], p["bx"], p["lam"],
             p["chunk_masks"], p["bdm_bf16"]]
    side_specs, side_shapes = [], []
    for w, axis in side_casts:
        block = list(w.shape)
        block[axis] //= nt * bsz
        assert block[axis] * nt * bsz == w.shape[axis] and block[axis] % LANE == 0
        index = (lambda i, b: (i * bsz + b, 0)) if axis == 0 else (lambda i, b: (0, i * bsz + b))
        side_specs.append(pl.BlockSpec(tuple(block), index))
        side_shapes.append(jax.ShapeDtypeStruct(w.shape, BF16))
        args.append(w)
    in_specs += side_specs
    tok_shape = jax.ShapeDtypeStruct((bsz, t, D_A), F32)
    dir_shape = jax.ShapeDtypeStruct((2, bsz, t, D_A), F32)
    row_shape = jax.ShapeDtypeStruct((2, bsz, nc, 1, D_A), F32)
    mxu_shape = jax.ShapeDtypeStruct((2, bsz, t, D_A), BF16)
    dir_spec = pl.BlockSpec((2, 1, tm, D_A), lambda i, b: (0, b, i, 0))
    row_spec = pl.BlockSpec((2, 1, CPT, 1, D_A), lambda i, b: (0, b, i, 0, 0))
    out_shape = [mxu_shape, tok_shape, mxu_shape, dir_shape, row_shape, dir_shape, tok_shape, row_shape, row_shape,
                 tok_shape, tok_shape, tok_shape]
    out_specs = [dir_spec, tok(D_A), dir_spec, dir_spec, row_spec, dir_spec, tok(D_B), row_spec, row_spec,
                 tok(D_A), tok(D_A), tok(D_B)]
    out_shape += side_shapes
    out_specs += side_specs
    return pl.pallas_call(
        functools.partial(_feat_kernel, has_pos, nt, len(side_casts)),
        grid=(nt, bsz),
        in_specs=in_specs,
        out_specs=out_specs,
        out_shape=out_shape,
        scratch_shapes=[pltpu.VMEM((2, N_SLAB, tm, LANE), F32)] * 4 + [pltpu.VMEM((3, N_SLAB, tm, LANE), F32)],
        compiler_params=pltpu.CompilerParams(dimension_semantics=("parallel", "parallel"),
                                             vmem_limit_bytes=VMEM_LIMIT),
        name="feat",
    )(*args)


def _scan_kernel(nt, tile, *refs):
    (pf_ref, mf_ref, nf_ref, gtf_ref, alf_ref, blf_ref,
     pb_ref, mb_ref, nb_ref, gtb_ref, alb_ref, blb_ref,
     s0_ref, l0_ref, bdm_ref,
     y_ref, hc_ref, s_ref, hl_ref) = refs

    i = pl.program_id(1)
    cpt = tile // CHUNK

    @pl.when(i == 0)
    def _init():
        y_ref[...] = jnp.zeros_like(y_ref)
        s_ref[...] = s0_ref[...]
        hl_ref[...] = l0_ref[...]

    bdm = bdm_ref[...]
    per_dir = ((0, pf_ref, mf_ref, nf_ref, gtf_ref, alf_ref, blf_ref),
               (1, pb_ref, mb_ref, nb_ref, gtb_ref, alb_ref, blb_ref))
    tiles = (i, nt - 1 - i)
    state = [s_ref[0, d] for d in (0, 1)]
    h0 = [hl_ref[0, d:d + 1, :] for d in (0, 1)]
    for cc in range(cpt):
        for d, p_ref, m_ref, n_ref, gt_ref, al_ref, bl_ref in per_dir:
            c = cc if d == 0 else cpt - 1 - cc
            rs = slice(c * CHUNK, (c + 1) * CHUNK)
            rows = pl.ds(pl.multiple_of(tiles[d] * tile + c * CHUNK, CHUNK), CHUNK)
            gt = gt_ref[0, 0, c]
            y_parts, s_parts = [], []
            for g in range(N_GROUP):
                ls = slice(g * GROUP, (g + 1) * GROUP)
                sg = state[d][:, ls]
                y_parts.append(_dot_nt(p_ref[0, 0, rs, ls], _blockdiag(sg, bdm)))
                s_parts.append(sg * gt[:, ls]
                               + jnp.dot(sg.astype(BF16), _blockdiag(m_ref[0, 0, rs, ls], bdm),
                                         preferred_element_type=F32)
                               + n_ref[0, 0, rs, ls])
            state[d] = jnp.concatenate(s_parts, axis=1)
            y_ref[0, rows, :] += jnp.concatenate(y_parts, axis=1)
            hc_ref[d, 0, tiles[d] * cpt + c] = h0[d]
            h0[d] = bl_ref[0, 0, c] + al_ref[0, 0, c] * h0[d]
    for d in (0, 1):
        s_ref[0, d] = state[d]
        hl_ref[0, d:d + 1, :] = h0[d]


def _scan_call(feats, s0, l0, p):
    pm, mm, nm, gt, al, bl = feats
    _, bsz, t, _ = pm.shape
    tile = min(SCAN_TILE, t)
    nt = t // tile
    nc = t // CHUNK

    def dir_specs(d, tile_of):
        big = pl.BlockSpec((1, 1, tile, D_A), lambda b, i: (d, b, tile_of(i), 0))
        small = pl.BlockSpec((1, 1, tile // CHUNK, 1, D_A), lambda b, i: (d, b, tile_of(i), 0, 0))
        return big, small

    big_f, small_f = dir_specs(0, lambda i: i)
    big_b, small_b = dir_specs(1, lambda i: nt - 1 - i)
    in_specs = [big_f, big_f, big_f, small_f, small_f, small_f,
                big_b, big_b, big_b, small_b, small_b, small_b,
                pl.BlockSpec((1, 2, HEAD, D_A), lambda b, i: (b, 0, 0, 0)),
                pl.BlockSpec((1, 2, D_B), lambda b, i: (b, 0, 0)),
                pl.BlockSpec((GROUP, GROUP), lambda b, i: (0, 0))]
    args = [pm, mm, nm, gt, al, bl, pm, mm, nm, gt, al, bl, s0, l0, p["bdm_bf16"]]
    out_specs = [pl.BlockSpec((1, t, D_A), lambda b, i: (b, 0, 0)),
                 pl.BlockSpec((2, 1, nc, 1, D_B), lambda b, i: (0, b, 0, 0, 0)),
                 pl.BlockSpec((1, 2, HEAD, D_A), lambda b, i: (b, 0, 0, 0)),
                 pl.BlockSpec((1, 2, D_B), lambda b, i: (b, 0, 0))]
    out_shape = [jax.ShapeDtypeStruct((bsz, t, D_A), F32),
                 jax.ShapeDtypeStruct((2, bsz, nc, 1, D_B), F32),
                 jax.ShapeDtypeStruct((bsz, 2, HEAD, D_A), F32),
                 jax.ShapeDtypeStruct((bsz, 2, D_B), F32)]
    return pl.pallas_call(
        functools.partial(_scan_kernel, nt, tile),
        grid=(bsz, nt),
        in_specs=in_specs,
        out_specs=out_specs,
        out_shape=out_shape,
        compiler_params=pltpu.CompilerParams(dimension_semantics=("parallel", "arbitrary"),
                                             vmem_limit_bytes=VMEM_LIMIT),
        name="scan",
    )(*args)


def _out_kernel(has_pos, *refs):
    if has_pos:
        x_ref, pos_ref = refs[0], refs[1]
        refs = refs[2:]
    else:
        x_ref, pos_ref = refs[0], None
        refs = refs[1:]
    (y_ref, q_ref, ac_ref, hc_ref, bs_ref, g_ref, bonus_ref, gate_ref, mod_ref,
     avg_ref, lnxg_ref, lnxb_ref, wout_ref, gpost_ref, gpre2_ref, w1_ref, w2_ref, gpost2_ref,
     o_ref) = refs

    x = x_ref[0]
    if has_pos:
        x = x + pos_ref[...]
    mod = mod_ref[0]
    gate1, shift2, scale2, gate2 = mod[2:3], mod[3:4], mod[4:5], mod[5:6]

    y = y_ref[0] + q_ref[0]
    avg = avg_ref[...]

    def head_mean(a):
        return jnp.concatenate([_dot(a[:, t * MXU_TILE:(t + 1) * MXU_TILE], avg)
                                for t in range(D_A // MXU_TILE)], axis=1)

    yc = y - head_mean(y)
    var = head_mean(yc * yc)
    yn = yc * lax.rsqrt(var + LNX_EPS) * lnxg_ref[...] + lnxb_ref[...]
    out_a = (yn + bonus_ref[0]) * g_ref[0]
    hs = jnp.concatenate(
        [bs_ref[0, c * CHUNK:(c + 1) * CHUNK, :]
         + ac_ref[0, 0, c * CHUNK:(c + 1) * CHUNK, :] * hc_ref[0, 0, c]
         + ac_ref[1, 0, c * CHUNK:(c + 1) * CHUNK, :] * hc_ref[1, 0, c]
         for c in range(x_ref.shape[1] // CHUNK)], axis=0)
    out_b = hs * gate_ref[0]
    mix = _dot(jnp.concatenate([out_a, out_b], axis=1), wout_ref[...])
    ms = jnp.mean(mix * mix, axis=-1, keepdims=True)
    x = x + (mix * lax.rsqrt(ms + EPS)) * (gate1 * gpost_ref[...])

    ms = jnp.mean(x * x, axis=-1, keepdims=True)
    h = (x * lax.rsqrt(ms + EPS)) * (gpre2_ref[...] * (1.0 + scale2)) + shift2
    f = _dot(h, w1_ref[...])
    f = jnp.square(jnp.maximum(f, 0.0))
    f = _dot(f, w2_ref[...])
    ms = jnp.mean(f * f, axis=-1, keepdims=True)
    o_ref[0] = x + (f * lax.rsqrt(ms + EPS)) * (gate2 * gpost2_ref[...])


def _out_call(x, pos, mod, mod_row, y, q, ac, hc, bs, g, bonus, gate, p):
    bsz, t, _ = x.shape
    tm = min(OUT_TILE, t)
    nt = t // tm
    has_pos = pos is not None

    def const(shape):
        return pl.BlockSpec(shape, lambda i, b: (0,) * len(shape))

    tok = lambda width: pl.BlockSpec((1, tm, width), lambda i, b: (b, i, 0))
    in_specs = [tok(D_MODEL)]
    args = [x]
    if has_pos:
        in_specs.append(pl.BlockSpec((tm, D_MODEL), lambda i, b: (i, 0)))
        args.append(pos)
    in_specs += [tok(D_A), tok(D_A),
                 pl.BlockSpec((2, 1, tm, D_B), lambda i, b: (0, b, i, 0)),
                 pl.BlockSpec((2, 1, tm // CHUNK, 1, D_B), lambda i, b: (0, b, i, 0, 0))]
    in_specs += [tok(D_A)] * 4
    in_specs += [
        pl.BlockSpec((1, 6, D_MODEL), lambda i, b: (mod_row(b), 0, 0)),
        const((MXU_TILE, MXU_TILE)), const((1, D_A)), const((1, D_A)),
        const((D_MODEL, D_MODEL)), const((1, D_MODEL)), const((1, D_MODEL)),
        const((D_MODEL, D_FF)), const((D_FF, D_MODEL)), const((1, D_MODEL)),
    ]
    args += [y, q, ac, hc, bs, g, bonus, gate, mod,
             p["seg_avg"], p["lnx_g"], p["lnx_b"], p["w_out"], p["g_post_mix"], p["g_pre_mlp"],
             p["w_mlp1"], p["w_mlp2"], p["g_post_mlp"]]
    return pl.pallas_call(
        functools.partial(_out_kernel, has_pos),
        grid=(nt, bsz),
        in_specs=in_specs,
        out_specs=tok(D_MODEL),
        out_shape=jax.ShapeDtypeStruct((bsz, t, D_MODEL), F32),
        compiler_params=pltpu.CompilerParams(dimension_semantics=("parallel", "parallel"),
                                             vmem_limit_bytes=VMEM_LIMIT),
        name="out",
    )(*args)


def _sincos_1d(pos, dim):
    omega = 1.0 / (10000.0 ** (jnp.arange(dim // 2, dtype=F32) / (dim // 2)))
    ang = pos.astype(F32)[:, None] * omega[None, :]
    return jnp.concatenate([jnp.sin(ang), jnp.cos(ang)], axis=-1)


def _grid_pos_embed(n_tokens):
    rows = n_tokens // GRID_W
    half = D_MODEL // 2
    e_row = _sincos_1d(jnp.arange(rows), half)
    e_col = _sincos_1d(jnp.arange(GRID_W), half)
    emb = jnp.concatenate([jnp.broadcast_to(e_row[:, None, :], (rows, GRID_W, half)),
                           jnp.broadcast_to(e_col[None, :, :], (rows, GRID_W, half))], axis=-1)
    return emb.reshape(rows * GRID_W, D_MODEL)


def _blockdiag_pairs(w):
    z = jnp.zeros_like(w[0])
    return jnp.concatenate([jnp.concatenate([w[0], z], axis=1),
                            jnp.concatenate([z, w[1]], axis=1)], axis=0)


def _heads_to_blockdiag(w):
    lead = w.shape[:-3]
    w = w.reshape(lead + (N_GROUP, HEADS_PER_GROUP, HEAD, HEAD))
    eye = jnp.eye(HEADS_PER_GROUP, dtype=w.dtype)
    bd = jnp.einsum('...ghab,hj->...ghajb', w, eye)
    return bd.reshape(lead + (N_GROUP, GROUP, GROUP))


def _state_to_lanes(s):
    b = s.shape[0]
    return jnp.transpose(s, (0, 1, 3, 2, 4)).reshape(b, 2, HEAD, D_A)


def _state_from_lanes(s):
    b = s.shape[0]
    return jnp.transpose(s.reshape(b, 2, HEAD, N_HEAD, HEAD), (0, 1, 3, 2, 4))


def kernel(x_prompt, x_sample, c, state_rwkv, state_lru, c_ctx, w_mod, b_mod, g_pre_mix, g_post_mix,
           g_pre_mlp, g_post_mlp, w_in, rwkv_w0, rwkv_w_up, rwkv_a0, rwkv_a_up, rwkv_g_up, rwkv_k_k,
           rwkv_k_a, rwkv_r_k, rwkv_lnx_g, rwkv_lnx_b, lru_conv_w, lru_conv_b, lru_wa, lru_ba, lru_wx,
           lru_bx, lru_lambda, w_out, w_mlp1, w_mlp2):
    n_ctx = x_prompt.shape[0]
    n_lat = x_sample.shape[0]
    l = 0
    seg = _group_blockdiag_mask()
    seg512 = np.kron(np.eye(N_GROUP, dtype=np.float32), seg)
    p = {
        "g_pre_mix": g_pre_mix[l][None], "g_post_mix": g_post_mix[l][None],
        "g_pre_mlp": g_pre_mlp[l][None], "g_post_mlp": g_post_mlp[l][None],
        "w_in": w_in[l].astype(BF16), "w_out": w_out[l].astype(BF16),
        "w0": 0.5 * rwkv_w0[l].reshape(1, 2 * D_A), "a0": 0.5 * rwkv_a0[l].reshape(1, 2 * D_A),
        "wup_bd": (0.5 * _blockdiag_pairs(rwkv_w_up[l])).astype(BF16),
        "aup_bd": (0.5 * _blockdiag_pairs(rwkv_a_up[l])).astype(BF16),
        "g_up": rwkv_g_up[l].astype(BF16),
        "k_k": rwkv_k_k[l][None], "k_a": rwkv_k_a[l][None], "r_k": rwkv_r_k[l].reshape(1, D_A),
        "lnx_g": rwkv_lnx_g[l][None], "lnx_b": rwkv_lnx_b[l][None],
        "conv_w": lru_conv_w[l], "conv_b": lru_conv_b[l][None],
        "wa_bd": (0.5 * _heads_to_blockdiag(lru_wa[l])).astype(BF16), "ba": 0.5 * lru_ba[l],
        "wx_bd": (0.5 * _heads_to_blockdiag(lru_wx[l])).astype(BF16), "bx": 0.5 * lru_bx[l],
        "lam": lru_lambda[l],
        "seg_ones": jnp.asarray(seg512, BF16),
        "seg_avg": jnp.asarray(np.kron(np.eye(MXU_TILE // GROUP, dtype=np.float32), seg) / HEAD, BF16),
        "chunk_masks": jnp.asarray(_chunk_masks()),
        "bdm_bf16": jnp.asarray(seg, BF16),
    }

    m_rows = 16
    c_all = jnp.concatenate([c_ctx[None], c, jnp.zeros((m_rows - 1 - n_lat, D_MODEL), F32)], axis=0)
    mod = _mod_call(c_all, w_mod[l], b_mod[l]).reshape(m_rows, 6, D_MODEL)

    pos = _grid_pos_embed(x_sample.shape[1]).astype(x_sample.dtype)
    ctx_row = lambda b: 0
    lat_row = lambda b: b + 1

    (pm, q, mm, nm, gt, ac, bs, bl, al, g, bonus, gate, p["w_mlp1"], p["w_mlp2"]) = _feat_call(
        x_prompt, None, mod, ctx_row, p, side_casts=((w_mlp1[l], 1), (w_mlp2[l], 0)))
    y, hc, s_ctx, l_ctx = _scan_call(
        (pm, mm, nm, gt, al, bl), jnp.zeros((n_ctx, 2, HEAD, D_A), F32), jnp.zeros((n_ctx, 2, D_B), F32), p)
    y_prompt = _out_call(x_prompt, None, mod, ctx_row, y, q, ac, hc, bs, g, bonus, gate, p)

    pm, q, mm, nm, gt, ac, bs, bl, al, g, bonus, gate = _feat_call(x_sample, pos, mod, lat_row, p)
    y, hc, _, _ = _scan_call((pm, mm, nm, gt, al, bl), _state_to_lanes(state_rwkv[:, l]), state_lru[:, l], p)
    y_sample = _out_call(x_sample, pos, mod, lat_row, y, q, ac, hc, bs, g, bonus, gate, p)

    new_state_rwkv = _state_from_lanes(s_ctx)[:, None].astype(x_prompt.dtype)
    new_state_lru = l_ctx[:, None].astype(x_prompt.dtype)
    return (y_prompt, y_sample, new_state_rwkv, new_state_lru)
```

```python
import functools

import numpy as np
import jax
import jax.numpy as jnp
from jax import lax
from jax.experimental import pallas as pl
from jax.experimental.pallas import tpu as pltpu

F32 = jnp.float32
BF16 = jnp.bfloat16

D_MODEL = 1024
D_A = 512
D_B = 512
HEAD = 64
N_HEAD = 8
R_W = 64
R_A = 64
R_G = 128
D_FF = 4096
D_IN = 2944
GRID_W = 64
CONV_W = 4
LRU_C = 8.0
EPS = 1e-6
LNX_EPS = 64e-5

CHUNK = 64
GROUP = 128
HEADS_PER_GROUP = GROUP // HEAD
N_GROUP = D_A // GROUP
SUB = 8
LANE = 128
N_SLAB = D_A // LANE
HALO = 8
TOKEN_TILE = 256
CPT = TOKEN_TILE // CHUNK
OUT_TILE = 512
SCAN_TILE = 1024
MXU_TILE = 256
V7X_VMEM_BYTES = 64 * 1024 * 1024
VMEM_LIMIT = V7X_VMEM_BYTES - 4 * 1024 * 1024

_O_R, _O_K, _O_V, _O_XW, _O_XA, _O_XG, _O_XB, _O_GB = 0, 512, 1024, 1536, 1664, 1792, 1920, 2432

_M_STRICT, _M_INCL, _M_LEV0 = 0, 1, 2
_LEVELS = (1, 2, 4, 8, 16, 32)
_M_PER_DIR = 2 + len(_LEVELS)
_M_EYE = 2 * _M_PER_DIR


def _dot(a, b):
    return jnp.dot(a.astype(BF16), b.astype(BF16), preferred_element_type=F32)


def _dot_nt(a, b):
    return lax.dot_general(a.astype(BF16), b.astype(BF16), (((1,), (1,)), ((), ())),
                           preferred_element_type=F32)


def _dot_tn(a, b):
    return lax.dot_general(a.astype(BF16), b.astype(BF16), (((0,), (0,)), ((), ())),
                           preferred_element_type=F32)


def _strided_rows(ref, lead, base):
    return [ref[lead + (pl.ds(base + j, SUB, stride=SUB), slice(None))] for j in range(SUB)]


def _natural_rows(ref, lead, base, pieces):
    for j, piece in enumerate(pieces):
        ref[lead + (pl.ds(base + SUB * j, SUB), slice(None))] = piece
    return jnp.concatenate(_strided_rows(ref, lead, base), axis=0)


def _sublane_shift(x, steps, reverse, fill):
    sub = lax.broadcasted_iota(jnp.int32, x.shape, 0)
    if reverse:
        return jnp.where(sub < SUB - steps, pltpu.roll(x, SUB - steps, 0), fill)
    return jnp.where(sub >= steps, pltpu.roll(x, steps, 0), fill)


def _scan_affine(a, b, reverse):
    a, b = list(a), list(b)
    order = range(SUB - 2, -1, -1) if reverse else range(1, SUB)
    for j in order:
        p = j + 1 if reverse else j - 1
        b[j] = a[j] * b[p] + b[j]
        a[j] = a[j] * a[p]
    ta, tb = (a[0], b[0]) if reverse else (a[SUB - 1], b[SUB - 1])
    s = 1
    while s < SUB:
        tb = ta * _sublane_shift(tb, s, reverse, 0.0) + tb
        ta = ta * _sublane_shift(ta, s, reverse, 1.0)
        s *= 2
    ea = _sublane_shift(ta, 1, reverse, 1.0)
    eb = _sublane_shift(tb, 1, reverse, 0.0)
    return [x * ea for x in a], [x * eb + y for x, y in zip(a, b)]


def _scan_sum(x, reverse):
    x = list(x)
    order = range(SUB - 2, -1, -1) if reverse else range(1, SUB)
    for j in order:
        x[j] = x[j] + x[j + 1 if reverse else j - 1]
    t = x[0] if reverse else x[SUB - 1]
    s = 1
    while s < SUB:
        t = t + _sublane_shift(t, s, reverse, 0.0)
        s *= 2
    e = _sublane_shift(t, 1, reverse, 0.0)
    return [y + e for y in x]


def _sigmoid(x):
    return 0.5 * jnp.tanh(0.5 * x) + 0.5


def _blockdiag(x, bdm):
    xb = x.astype(BF16)
    return jnp.concatenate([xb] * HEADS_PER_GROUP, axis=0) * bdm


def _mod_kernel(c_ref, w_ref, b_ref, o_ref):
    c = c_ref[...]
    s = c * _sigmoid(c)
    o_ref[...] = _dot(s, w_ref[...]) + b_ref[...]


def _mod_call(c_all, w_mod, b_mod):
    m = c_all.shape[0]
    n = w_mod.shape[1]
    tn = 1536
    return pl.pallas_call(
        _mod_kernel,
        grid=(n // tn,),
        in_specs=[pl.BlockSpec((m, D_MODEL), lambda j: (0, 0)),
                  pl.BlockSpec((D_MODEL, tn), lambda j: (0, j)),
                  pl.BlockSpec((1, tn), lambda j: (0, j))],
        out_specs=pl.BlockSpec((m, tn), lambda j: (0, j)),
        out_shape=jax.ShapeDtypeStruct((m, n), F32),
        compiler_params=pltpu.CompilerParams(dimension_semantics=("parallel",),
                                             vmem_limit_bytes=VMEM_LIMIT),
        name="mod",
    )(c_all, w_mod, b_mod.reshape(1, n))


def _chunk_masks():
    t = np.arange(CHUNK)[:, None]
    s = (np.arange(GROUP) % CHUNK)[None, :]
    rows = []
    for d in (0, 1):
        before = (s < t) if d == 0 else (s > t)
        rows.append(before)
        rows.append(before | (s == t))
        for b in _LEVELS:
            same = (t // (2 * b)) == (s // (2 * b))
            if d == 0:
                rows.append(same & ((t // b) % 2 == 1) & ((s // b) % 2 == 0))
            else:
                rows.append(same & ((t // b) % 2 == 0) & ((s // b) % 2 == 1))
    rows.append(s == t)
    return np.stack(rows).astype(np.float32)


def _group_blockdiag_mask():
    i = np.arange(GROUP)
    return ((i[:, None] // HEAD) == (i[None, :] // HEAD)).astype(np.float32)


def _feat_kernel(has_pos, nt, n_side, *refs):
    if has_pos:
        x_ref, xp_ref, xn_ref, pos_ref, pp_ref, pn_ref = refs[:6]
        refs = refs[6:]
    else:
        x_ref, xp_ref, xn_ref = refs[:3]
        pos_ref = pp_ref = pn_ref = None
        refs = refs[3:]
    (mod_ref, gpre_ref, win_ref, wup_ref, w0_ref, aup_ref, a0_ref, gup_ref,
     kk_ref, ka_ref, rk_ref, seg_ref,
     convw_ref, convb_ref, wa_ref, ba_ref, wx_ref, bx_ref, lam_ref,
     masks_ref, bdm_ref) = refs[:21]
    side_in, refs = refs[21:21 + n_side], refs[21 + n_side:]
    (p_o, q_o, m_o, n_o, gt_o, ac_o, bs_o, bl_o, al_o, g_o, bonus_o, gate_o) = refs[:12]
    side_out, refs = refs[12:12 + n_side], refs[12 + n_side:]
    lw_s, lc_s, la_s, lb_s, lo_s = refs

    for src, dst in zip(side_in, side_out):
        dst[...] = src[...].astype(BF16)

    i = pl.program_id(0)
    tm = TOKEN_TILE
    mod = mod_ref[0]
    shift1, scale1 = mod[0:1], mod[1:2]

    gain1 = gpre_ref[...] * (1.0 + scale1)

    def normmod(xv):
        ms = jnp.mean(xv * xv, axis=-1, keepdims=True)
        return (xv * lax.rsqrt(ms + EPS)) * gain1 + shift1

    x = x_ref[0]
    halo = jnp.concatenate([xp_ref[0], xn_ref[0]], axis=0)
    if has_pos:
        x = x + pos_ref[...]
        halo = halo + jnp.concatenate([pp_ref[...], pn_ref[...]], axis=0)
    z = _dot(normmod(x), win_ref[...])
    zh = _dot(normmod(halo), win_ref[:, _O_XB:_O_XB + D_B])

    r = z[:, _O_R:_O_R + D_A]
    k = z[:, _O_K:_O_K + D_A]
    v = z[:, _O_V:_O_V + D_A]
    xw = z[:, _O_XW:_O_XW + 2 * R_W]
    xa = z[:, _O_XA:_O_XA + 2 * R_A]
    xg = z[:, _O_XG:_O_XG + R_G]

    g_o[0] = _dot(_sigmoid(xg), gup_ref[...])
    gb = z[:, _O_GB:_O_GB + D_B]
    c_gelu = float(np.sqrt(2.0 / np.pi))
    gate_o[0] = gb * (0.5 + 0.5 * jnp.tanh(gb * (c_gelu + (0.044715 * c_gelu) * (gb * gb))))
    half_c = 0.5 * float(np.exp(-0.5))
    lw2 = -half_c * jnp.tanh(w0_ref[...] + _dot(jnp.tanh(xw), wup_ref[...])) - half_c
    a2 = 0.5 * jnp.tanh(a0_ref[...] + _dot(xa, aup_ref[...])) + 0.5

    kks = k * kk_ref[...]
    ss = _dot(kks * kks, seg_ref[...])
    kk = kks * lax.rsqrt(jnp.maximum(ss, 1e-24))
    ka = ka_ref[...]
    k_fix, k_var = k * (1.0 - ka), k * ka
    kd2 = [k_fix + k_var * a2[:, d * D_A:(d + 1) * D_A] for d in (0, 1)]
    bonus_o[0] = _dot(r * (kd2[0] + kd2[1]) * rk_ref[...], seg_ref[...]) * v

    m_prev = jnp.where(i > 0, 1.0, 0.0)
    m_next = jnp.where(i < nt - 1, 1.0, 0.0)
    ext = jnp.concatenate([zh[:HALO] * m_prev, z[:, _O_XB:_O_XB + D_B], zh[HALO:] * m_next], axis=0)
    n_ext = tm + 2 * HALO
    xc = convb_ref[...]
    for j in range(CONV_W):
        sh = (2 - j) % n_ext
        tap = ext if sh == 0 else pltpu.roll(ext, sh, 0)
        xc = xc + tap[HALO:HALO + tm] * convw_ref[j:j + 1, :]
    for d in (0, 1):
        rg = jnp.concatenate([_dot(xc[:, g * GROUP:(g + 1) * GROUP], wa_ref[d, g]) for g in range(N_GROUP)], 1)
        ig = jnp.concatenate([_dot(xc[:, g * GROUP:(g + 1) * GROUP], wx_ref[d, g]) for g in range(N_GROUP)], 1)
        rg = 0.5 * jnp.tanh(rg + ba_ref[d:d + 1, :]) + 0.5
        ig = 0.5 * jnp.tanh(ig + bx_ref[d:d + 1, :]) + 0.5
        neg_log_a = rg * (LRU_C * jax.nn.softplus(-lam_ref[d:d + 1, :]))
        a_lru = jnp.exp(-neg_log_a)
        b_lru = jnp.sqrt(jnp.tanh(neg_log_a) * (a_lru * a_lru + 1.0)) * (ig * xc)
        for q in range(N_SLAB):
            la_s[d, q] = a_lru[:, q * LANE:(q + 1) * LANE]
            lb_s[d, q] = b_lru[:, q * LANE:(q + 1) * LANE]
    for c in range(CPT):
        base = c * CHUNK
        rs = slice(base, base + CHUNK)
        for q in range(N_SLAB):
            ls = slice(q * LANE, (q + 1) * LANE)
            bsum = None
            for d in (0, 1):
                acum, bcum = _scan_affine(_strided_rows(la_s, (d, q), base), _strided_rows(lb_s, (d, q), base),
                                          d == 1)
                ac_o[d, 0, rs, ls] = _natural_rows(lo_s, (d, q), base, acum)
                bsum = bcum if bsum is None else [x + y for x, y in zip(bsum, bcum)]
                bl_o[d, 0, c, :, ls] = bcum[0][0:1, :] if d == 1 else bcum[SUB - 1][SUB - 1:SUB, :]
                al_o[d, 0, c, :, ls] = acum[0][0:1, :] if d == 1 else acum[SUB - 1][SUB - 1:SUB, :]
            bs_o[0, rs, ls] = _natural_rows(lo_s, (2, q), base, bsum)

    bdm = bdm_ref[...]
    bd = lambda xv: _blockdiag(xv, bdm)
    bdot = lambda lhs, rhs: jnp.dot(lhs.astype(BF16), bd(rhs), preferred_element_type=F32)

    a_t, r_t, b_t, k_t, g_tot = [], [], [], [], []
    neg_kk = -kk
    for d in (0, 1):
        lw = lw2[:, d * D_A:(d + 1) * D_A]
        for q in range(N_SLAB):
            lw_s[d, q] = lw[:, q * LANE:(q + 1) * LANE]
        g_tot.append([])
        lc_rows = []
        for c in range(CPT):
            base = c * CHUNK
            blocks, totals = [], []
            for q in range(N_SLAB):
                pieces = _scan_sum(_strided_rows(lw_s, (d, q), base), d == 1)
                totals.append(pieces[0][0:1, :] if d == 1 else pieces[SUB - 1][SUB - 1:SUB, :])
                blocks.append(_natural_rows(lc_s, (d, q), base, pieces))
            lc_rows.append(jnp.concatenate(blocks, axis=1))
            g_tot[d].append(jnp.exp(jnp.concatenate(totals, axis=1)))
            gt_o[d, 0, c] = g_tot[d][c]
        lc = jnp.concatenate(lc_rows, axis=0)
        e_neg = jnp.exp(-lc)
        a_t.append(neg_kk * jnp.exp(lc - lw))
        r_t.append(r * jnp.exp(lc))
        b_t.append(kk * a2[:, d * D_A:(d + 1) * D_A] * e_neg)
        k_t.append(kd2[d] * e_neg)

    combos = [(d, c, g) for d in (0, 1) for c in range(CPT) for g in range(N_GROUP)]

    def cut(arr, c, g):
        return arr[c * CHUNK:(c + 1) * CHUNK, g * GROUP:(g + 1) * GROUP]

    sc = {}
    for key in combos:
        d, c, g = key
        ar = jnp.concatenate([cut(a_t[d], c, g), cut(r_t[d], c, g)], axis=0)
        rhs = jnp.concatenate([bd(cut(b_t[d], c, g)), bd(cut(k_t[d], c, g))], axis=0)
        sc[key] = _dot_nt(ar, rhs)
    n_ab, n_ak, n_rb, n_rk, tinv = {}, {}, {}, {}, {}
    for key in combos:
        m0 = key[0] * _M_PER_DIR
        n_ab[key] = sc[key][:CHUNK, :GROUP] * masks_ref[m0 + _M_STRICT]
        n_ak[key] = sc[key][:CHUNK, GROUP:] * masks_ref[m0 + _M_STRICT]
        n_rb[key] = sc[key][CHUNK:, :GROUP] * masks_ref[m0 + _M_INCL]
        n_rk[key] = sc[key][CHUNK:, GROUP:] * masks_ref[m0 + _M_INCL]
        tinv[key] = masks_ref[_M_EYE] + n_ab[key] * masks_ref[m0 + _M_LEV0]
    for li in range(1, len(_LEVELS)):
        pm = {}
        for key in combos:
            pm[key] = bdot(n_ab[key] * masks_ref[key[0] * _M_PER_DIR + _M_LEV0 + li], tinv[key])
        for key in combos:
            tinv[key] = tinv[key] + bdot(tinv[key], pm[key])

    kv = {key: bdot(jnp.concatenate([n_ak[key], n_rk[key]], axis=0), cut(v, key[1], key[2])) for key in combos}
    gm = {key: bdot(n_rb[key], tinv[key]) for key in combos}
    wp = {}
    for key in combos:
        d, c, g = key
        lhs = jnp.concatenate([tinv[key], gm[key]], axis=0)
        rhs = jnp.concatenate([bd(cut(a_t[d], c, g)), bd(kv[key][:CHUNK])], axis=1)
        wp[key] = jnp.dot(lhs.astype(BF16), rhs, preferred_element_type=F32)

    lane = lax.broadcasted_iota(jnp.int32, (CHUNK, GROUP), 1)

    def fold(full):
        out = full[(HEADS_PER_GROUP - 1) * HEAD:]
        for h in range(HEADS_PER_GROUP - 2, -1, -1):
            out = jnp.where(lane < (h + 1) * HEAD, full[h * HEAD:(h + 1) * HEAD], out)
        return out

    for key in combos:
        d, c, g = key
        rs = slice(c * CHUNK, (c + 1) * CHUNK)
        ls = slice(g * GROUP, (g + 1) * GROUP)
        w_, u0 = wp[key][:CHUNK, :GROUP], wp[key][:CHUNK, GROUP:]
        vg = cut(v, c, g)
        gt = g_tot[d][c][:, ls]
        bh, kh = cut(b_t[d], c, g) * gt, cut(k_t[d], c, g) * gt
        p_o[d, 0, rs, ls] = (cut(r_t[d], c, g) + wp[key][CHUNK:, :GROUP]).astype(BF16)
        m_o[d, 0, rs, ls] = fold(_dot_tn(w_, bh)).astype(BF16)
        n_o[d, 0, rs, ls] = fold(_dot_tn(jnp.concatenate([u0, vg], axis=0), jnp.concatenate([bh, kh], axis=0)))
        if d == 1:
            other = (0, c, g)
            q_o[0, rs, ls] = ((wp[other][CHUNK:, GROUP:] + kv[other][CHUNK:])
                              + (wp[key][CHUNK:, GROUP:] + kv[key][CHUNK:]))


def _feat_call(x, pos, mod, mod_row, p, side_casts=()):
    bsz, t, _ = x.shape
    tm = TOKEN_TILE
    nt = t // tm
    nc = t // CHUNK
    hpt = tm // HALO
    has_pos = pos is not None

    def const(shape):
        return pl.BlockSpec(shape, lambda i, b: (0,) * len(shape))

    tok = lambda width: pl.BlockSpec((1, tm, width), lambda i, b: (b, i, 0))
    prev_i = lambda i: jnp.maximum(i * hpt - 1, 0)
    next_i = lambda i: jnp.minimum((i + 1) * hpt, t // HALO - 1)
    in_specs = [tok(D_MODEL),
                pl.BlockSpec((1, HALO, D_MODEL), lambda i, b: (b, prev_i(i), 0)),
                pl.BlockSpec((1, HALO, D_MODEL), lambda i, b: (b, next_i(i), 0))]
    args = [x, x, x]
    if has_pos:
        in_specs += [pl.BlockSpec((tm, D_MODEL), lambda i, b: (i, 0)),
                     pl.BlockSpec((HALO, D_MODEL), lambda i, b: (prev_i(i), 0)),
                     pl.BlockSpec((HALO, D_MODEL), lambda i, b: (next_i(i), 0))]
        args += [pos, pos, pos]
    n_masks = 2 * _M_PER_DIR + 1
    in_specs += [
        pl.BlockSpec((1, 6, D_MODEL), lambda i, b: (mod_row(b), 0, 0)),
        const((1, D_MODEL)), const((D_MODEL, D_IN)),
        const((2 * R_W, 2 * D_A)), const((1, 2 * D_A)),
        const((2 * R_A, 2 * D_A)), const((1, 2 * D_A)),
        const((R_G, D_A)),
        const((1, D_A)), const((1, D_A)), const((1, D_A)), const((D_A, D_A)),
        const((CONV_W, D_B)), const((1, D_B)),
        const((2, N_GROUP, GROUP, GROUP)), const((2, D_B)),
        const((2, N_GROUP, GROUP, GROUP)), const((2, D_B)), const((2, D_B)),
        const((n_masks, CHUNK, GROUP)), const((GROUP, GROUP)),
    ]
    args += [mod, p["g_pre_mix"], p["w_in"], p["wup_bd"], p["w0"], p["aup_bd"], p["a0"], p["g_up"],
             p["k_k"], p["k_a"], p["r_k"], p["seg_ones"],
             p["conv_w"], p["conv_b"], p["wa_bd"], p["ba"], p["wx_bd"], p["bx"], p["lam"],
             p["chunk_masks"], p["bdm_bf16"]]
    side_specs, side_shapes = [], []
    for w, axis in side_casts:
        block = list(w.shape)
        block[axis] //= nt * bsz
        assert block[axis] * nt * bsz == w.shape[axis] and block[axis] % LANE == 0
        index = (lambda i, b: (i * bsz + b, 0)) if axis == 0 else (lambda i, b: (0, i * bsz + b))
        side_specs.append(pl.BlockSpec(tuple(block), index))
        side_shapes.append(jax.ShapeDtypeStruct(w.shape, BF16))
        args.append(w)
    in_specs += side_specs
    tok_shape = jax.ShapeDtypeStruct((bsz, t, D_A), F32)
    dir_shape = jax.ShapeDtypeStruct((2, bsz, t, D_A), F32)
    row_shape = jax.ShapeDtypeStruct((2, bsz, nc, 1, D_A), F32)
    mxu_shape = jax.ShapeDtypeStruct((2, bsz, t, D_A), BF16)
    dir_spec = pl.BlockSpec((2, 1, tm, D_A), lambda i, b: (0, b, i, 0))
    row_spec = pl.BlockSpec((2, 1, CPT, 1, D_A), lambda i, b: (0, b, i, 0, 0))
    out_shape = [mxu_shape, tok_shape, mxu_shape, dir_shape, row_shape, dir_shape, tok_shape, row_shape, row_shape,
                 tok_shape, tok_shape, tok_shape]
    out_specs = [dir_spec, tok(D_A), dir_spec, dir_spec, row_spec, dir_spec, tok(D_B), row_spec, row_spec,
                 tok(D_A), tok(D_A), tok(D_B)]
    out_shape += side_shapes
    out_specs += side_specs
    return pl.pallas_call(
        functools.partial(_feat_kernel, has_pos, nt, len(side_casts)),
        grid=(nt, bsz),
        in_specs=in_specs,
        out_specs=out_specs,
        out_shape=out_shape,
        scratch_shapes=[pltpu.VMEM((2, N_SLAB, tm, LANE), F32)] * 4 + [pltpu.VMEM((3, N_SLAB, tm, LANE), F32)],
        compiler_params=pltpu.CompilerParams(dimension_semantics=("parallel", "parallel"),
                                             vmem_limit_bytes=VMEM_LIMIT),
        name="feat",
    )(*args)


def _scan_kernel(nt, tile, *refs):
    (pf_ref, mf_ref, nf_ref, gtf_ref, alf_ref, blf_ref,
     pb_ref, mb_ref, nb_ref, gtb_ref, alb_ref, blb_ref,
     s0_ref, l0_ref, bdm_ref,
     y_ref, hc_ref, s_ref, hl_ref, st_s) = refs

    i = pl.program_id(1)
    cpt = tile // CHUNK

    @pl.when(i == 0)
    def _init():
        y_ref[...] = jnp.zeros_like(y_ref)
        for d in (0, 1):
            st_s[d] = jnp.concatenate([s0_ref[0, d, h] for h in range(N_HEAD)], axis=1)
        hl_ref[...] = l0_ref[...]

    bdm = bdm_ref[...]
    per_dir = ((0, pf_ref, mf_ref, nf_ref, gtf_ref, alf_ref, blf_ref),
               (1, pb_ref, mb_ref, nb_ref, gtb_ref, alb_ref, blb_ref))
    tiles = (i, nt - 1 - i)
    state = [st_s[d] for d in (0, 1)]
    h0 = [hl_ref[0, d:d + 1, :] for d in (0, 1)]
    for cc in range(cpt):
        for d, p_ref, m_ref, n_ref, gt_ref, al_ref, bl_ref in per_dir:
            c = cc if d == 0 else cpt - 1 - cc
            rs = slice(c * CHUNK, (c + 1) * CHUNK)
            rows = pl.ds(pl.multiple_of(tiles[d] * tile + c * CHUNK, CHUNK), CHUNK)
            gt = gt_ref[0, 0, c]
            y_parts, s_parts = [], []
            for g in range(N_GROUP):
                ls = slice(g * GROUP, (g + 1) * GROUP)
                sg = state[d][:, ls]
                y_parts.append(_dot_nt(p_ref[0, 0, rs, ls], _blockdiag(sg, bdm)))
                s_parts.append(sg * gt[:, ls]
                               + jnp.dot(sg.astype(BF16), _blockdiag(m_ref[0, 0, rs, ls], bdm),
                                         preferred_element_type=F32)
                               + n_ref[0, 0, rs, ls])
            state[d] = jnp.concatenate(s_parts, axis=1)
            y_ref[0, rows, :] += jnp.concatenate(y_parts, axis=1)
            hc_ref[d, 0, tiles[d] * cpt + c] = h0[d]
            h0[d] = bl_ref[0, 0, c] + al_ref[0, 0, c] * h0[d]
    for d in (0, 1):
        st_s[d] = state[d]
        hl_ref[0, d:d + 1, :] = h0[d]

    @pl.when(i == nt - 1)
    def _final():
        for d in (0, 1):
            for h in range(N_HEAD):
                s_ref[0, d, h] = st_s[d, :, h * HEAD:(h + 1) * HEAD]


def _scan_call(feats, s0, l0, p):
    pm, mm, nm, gt, al, bl = feats
    _, bsz, t, _ = pm.shape
    tile = min(SCAN_TILE, t)
    nt = t // tile
    nc = t // CHUNK

    def dir_specs(d, tile_of):
        big = pl.BlockSpec((1, 1, tile, D_A), lambda b, i: (d, b, tile_of(i), 0))
        small = pl.BlockSpec((1, 1, tile // CHUNK, 1, D_A), lambda b, i: (d, b, tile_of(i), 0, 0))
        return big, small

    big_f, small_f = dir_specs(0, lambda i: i)
    big_b, small_b = dir_specs(1, lambda i: nt - 1 - i)
    in_specs = [big_f, big_f, big_f, small_f, small_f, small_f,
                big_b, big_b, big_b, small_b, small_b, small_b,
                pl.BlockSpec((1, 2, N_HEAD, HEAD, HEAD), lambda b, i: (b, 0, 0, 0, 0)),
                pl.BlockSpec((1, 2, D_B), lambda b, i: (b, 0, 0)),
                pl.BlockSpec((GROUP, GROUP), lambda b, i: (0, 0))]
    args = [pm, mm, nm, gt, al, bl, pm, mm, nm, gt, al, bl, s0, l0, p["bdm_bf16"]]
    out_specs = [pl.BlockSpec((1, t, D_A), lambda b, i: (b, 0, 0)),
                 pl.BlockSpec((2, 1, nc, 1, D_B), lambda b, i: (0, b, 0, 0, 0)),
                 pl.BlockSpec((1, 2, N_HEAD, HEAD, HEAD), lambda b, i: (b, 0, 0, 0, 0)),
                 pl.BlockSpec((1, 2, D_B), lambda b, i: (b, 0, 0))]
    out_shape = [jax.ShapeDtypeStruct((bsz, t, D_A), F32),
                 jax.ShapeDtypeStruct((2, bsz, nc, 1, D_B), F32),
                 jax.ShapeDtypeStruct((bsz, 2, N_HEAD, HEAD, HEAD), F32),
                 jax.ShapeDtypeStruct((bsz, 2, D_B), F32)]
    return pl.pallas_call(
        functools.partial(_scan_kernel, nt, tile),
        grid=(bsz, nt),
        in_specs=in_specs,
        out_specs=out_specs,
        out_shape=out_shape,
        scratch_shapes=[pltpu.VMEM((2, HEAD, D_A), F32)],
        compiler_params=pltpu.CompilerParams(dimension_semantics=("parallel", "arbitrary"),
                                             vmem_limit_bytes=VMEM_LIMIT),
        name="scan",
    )(*args)


def _out_kernel(has_pos, *refs):
    if has_pos:
        x_ref, pos_ref = refs[0], refs[1]
        refs = refs[2:]
    else:
        x_ref, pos_ref = refs[0], None
        refs = refs[1:]
    (y_ref, q_ref, ac_ref, hc_ref, bs_ref, g_ref, bonus_ref, gate_ref, mod_ref,
     avg_ref, lnxg_ref, lnxb_ref, wout_ref, gpost_ref, gpre2_ref, w1_ref, w2_ref, gpost2_ref,
     o_ref) = refs

    x = x_ref[0]
    if has_pos:
        x = x + pos_ref[...]
    mod = mod_ref[0]
    gate1, shift2, scale2, gate2 = mod[2:3], mod[3:4], mod[4:5], mod[5:6]

    y = y_ref[0] + q_ref[0]
    avg = avg_ref[...]

    def head_mean(a):
        return jnp.concatenate([_dot(a[:, t * MXU_TILE:(t + 1) * MXU_TILE], avg)
                                for t in range(D_A // MXU_TILE)], axis=1)

    yc = y - head_mean(y)
    var = head_mean(yc * yc)
    yn = yc * lax.rsqrt(var + LNX_EPS) * lnxg_ref[...] + lnxb_ref[...]
    out_a = (yn + bonus_ref[0]) * g_ref[0]
    hs = jnp.concatenate(
        [bs_ref[0, c * CHUNK:(c + 1) * CHUNK, :]
         + ac_ref[0, 0, c * CHUNK:(c + 1) * CHUNK, :] * hc_ref[0, 0, c]
         + ac_ref[1, 0, c * CHUNK:(c + 1) * CHUNK, :] * hc_ref[1, 0, c]
         for c in range(x_ref.shape[1] // CHUNK)], axis=0)
    out_b = hs * gate_ref[0]
    mix = _dot(jnp.concatenate([out_a, out_b], axis=1), wout_ref[...])
    ms = jnp.mean(mix * mix, axis=-1, keepdims=True)
    x = x + (mix * lax.rsqrt(ms + EPS)) * (gate1 * gpost_ref[...])

    ms = jnp.mean(x * x, axis=-1, keepdims=True)
    h = (x * lax.rsqrt(ms + EPS)) * (gpre2_ref[...] * (1.0 + scale2)) + shift2
    f = _dot(h, w1_ref[...])
    f = jnp.square(jnp.maximum(f, 0.0))
    f = _dot(f, w2_ref[...])
    ms = jnp.mean(f * f, axis=-1, keepdims=True)
    o_ref[0] = x + (f * lax.rsqrt(ms + EPS)) * (gate2 * gpost2_ref[...])


def _out_call(x, pos, mod, mod_row, y, q, ac, hc, bs, g, bonus, gate, p):
    bsz, t, _ = x.shape
    tm = min(OUT_TILE, t)
    nt = t // tm
    has_pos = pos is not None

    def const(shape):
        return pl.BlockSpec(shape, lambda i, b: (0,) * len(shape))

    tok = lambda width: pl.BlockSpec((1, tm, width), lambda i, b: (b, i, 0))
    in_specs = [tok(D_MODEL)]
    args = [x]
    if has_pos:
        in_specs.append(pl.BlockSpec((tm, D_MODEL), lambda i, b: (i, 0)))
        args.append(pos)
    in_specs += [tok(D_A), tok(D_A),
                 pl.BlockSpec((2, 1, tm, D_B), lambda i, b: (0, b, i, 0)),
                 pl.BlockSpec((2, 1, tm // CHUNK, 1, D_B), lambda i, b: (0, b, i, 0, 0))]
    in_specs += [tok(D_A)] * 4
    in_specs += [
        pl.BlockSpec((1, 6, D_MODEL), lambda i, b: (mod_row(b), 0, 0)),
        const((MXU_TILE, MXU_TILE)), const((1, D_A)), const((1, D_A)),
        const((D_MODEL, D_MODEL)), const((1, D_MODEL)), const((1, D_MODEL)),
        const((D_MODEL, D_FF)), const((D_FF, D_MODEL)), const((1, D_MODEL)),
    ]
    args += [y, q, ac, hc, bs, g, bonus, gate, mod,
             p["seg_avg"], p["lnx_g"], p["lnx_b"], p["w_out"], p["g_post_mix"], p["g_pre_mlp"],
             p["w_mlp1"], p["w_mlp2"], p["g_post_mlp"]]
    return pl.pallas_call(
        functools.partial(_out_kernel, has_pos),
        grid=(nt, bsz),
        in_specs=in_specs,
        out_specs=tok(D_MODEL),
        out_shape=jax.ShapeDtypeStruct((bsz, t, D_MODEL), F32),
        compiler_params=pltpu.CompilerParams(dimension_semantics=("parallel", "parallel"),
                                             vmem_limit_bytes=VMEM_LIMIT),
        name="out",
    )(*args)


def _sincos_1d(pos, dim):
    omega = 1.0 / (10000.0 ** (jnp.arange(dim // 2, dtype=F32) / (dim // 2)))
    ang = pos.astype(F32)[:, None] * omega[None, :]
    return jnp.concatenate([jnp.sin(ang), jnp.cos(ang)], axis=-1)


def _grid_pos_embed(n_tokens):
    rows = n_tokens // GRID_W
    half = D_MODEL // 2
    e_row = _sincos_1d(jnp.arange(rows), half)
    e_col = _sincos_1d(jnp.arange(GRID_W), half)
    emb = jnp.concatenate([jnp.broadcast_to(e_row[:, None, :], (rows, GRID_W, half)),
                           jnp.broadcast_to(e_col[None, :, :], (rows, GRID_W, half))], axis=-1)
    return emb.reshape(rows * GRID_W, D_MODEL)


def _blockdiag_pairs(w):
    z = jnp.zeros_like(w[0])
    return jnp.concatenate([jnp.concatenate([w[0], z], axis=1),
                            jnp.concatenate([z, w[1]], axis=1)], axis=0)


def _heads_to_blockdiag(w):
    lead = w.shape[:-3]
    w = w.reshape(lead + (N_GROUP, HEADS_PER_GROUP, HEAD, HEAD))
    eye = jnp.eye(HEADS_PER_GROUP, dtype=w.dtype)
    bd = jnp.einsum('...ghab,hj->...ghajb', w, eye)
    return bd.reshape(lead + (N_GROUP, GROUP, GROUP))


def kernel(x_prompt, x_sample, c, state_rwkv, state_lru, c_ctx, w_mod, b_mod, g_pre_mix, g_post_mix,
           g_pre_mlp, g_post_mlp, w_in, rwkv_w0, rwkv_w_up, rwkv_a0, rwkv_a_up, rwkv_g_up, rwkv_k_k,
           rwkv_k_a, rwkv_r_k, rwkv_lnx_g, rwkv_lnx_b, lru_conv_w, lru_conv_b, lru_wa, lru_ba, lru_wx,
           lru_bx, lru_lambda, w_out, w_mlp1, w_mlp2):
    n_ctx = x_prompt.shape[0]
    n_lat = x_sample.shape[0]
    l = 0
    seg = _group_blockdiag_mask()
    seg512 = np.kron(np.eye(N_GROUP, dtype=np.float32), seg)
    p = {
        "g_pre_mix": g_pre_mix[l][None], "g_post_mix": g_post_mix[l][None],
        "g_pre_mlp": g_pre_mlp[l][None], "g_post_mlp": g_post_mlp[l][None],
        "w_in": w_in[l].astype(BF16), "w_out": w_out[l].astype(BF16),
        "w0": 0.5 * rwkv_w0[l].reshape(1, 2 * D_A), "a0": 0.5 * rwkv_a0[l].reshape(1, 2 * D_A),
        "wup_bd": (0.5 * _blockdiag_pairs(rwkv_w_up[l])).astype(BF16),
        "aup_bd": (0.5 * _blockdiag_pairs(rwkv_a_up[l])).astype(BF16),
        "g_up": rwkv_g_up[l].astype(BF16),
        "k_k": rwkv_k_k[l][None], "k_a": rwkv_k_a[l][None], "r_k": rwkv_r_k[l].reshape(1, D_A),
        "lnx_g": rwkv_lnx_g[l][None], "lnx_b": rwkv_lnx_b[l][None],
        "conv_w": lru_conv_w[l], "conv_b": lru_conv_b[l][None],
        "wa_bd": (0.5 * _heads_to_blockdiag(lru_wa[l])).astype(BF16), "ba": 0.5 * lru_ba[l],
        "wx_bd": (0.5 * _heads_to_blockdiag(lru_wx[l])).astype(BF16), "bx": 0.5 * lru_bx[l],
        "lam": lru_lambda[l],
        "seg_ones": jnp.asarray(seg512, BF16),
        "seg_avg": jnp.asarray(np.kron(np.eye(MXU_TILE // GROUP, dtype=np.float32), seg) / HEAD, BF16),
        "chunk_masks": jnp.asarray(_chunk_masks()),
        "bdm_bf16": jnp.asarray(seg, BF16),
    }

    m_rows = 16
    c_all = jnp.concatenate([c_ctx[None], c, jnp.zeros((m_rows - 1 - n_lat, D_MODEL), F32)], axis=0)
    mod = _mod_call(c_all, w_mod[l], b_mod[l]).reshape(m_rows, 6, D_MODEL)

    pos = _grid_pos_embed(x_sample.shape[1]).astype(x_sample.dtype)
    ctx_row = lambda b: 0
    lat_row = lambda b: b + 1

    (pm, q, mm, nm, gt, ac, bs, bl, al, g, bonus, gate, p["w_mlp1"], p["w_mlp2"]) = _feat_call(
        x_prompt, None, mod, ctx_row, p, side_casts=((w_mlp1[l], 1), (w_mlp2[l], 0)))
    y, hc, s_ctx, l_ctx = _scan_call(
        (pm, mm, nm, gt, al, bl), jnp.zeros((n_ctx, 2, N_HEAD, HEAD, HEAD), F32), jnp.zeros((n_ctx, 2, D_B), F32), p)
    y_prompt = _out_call(x_prompt, None, mod, ctx_row, y, q, ac, hc, bs, g, bonus, gate, p)

    pm, q, mm, nm, gt, ac, bs, bl, al, g, bonus, gate = _feat_call(x_sample, pos, mod, lat_row, p)
    y, hc, _, _ = _scan_call((pm, mm, nm, gt, al, bl), state_rwkv[:, l], state_lru[:, l], p)
    y_sample = _out_call(x_sample, pos, mod, lat_row, y, q, ac, hc, bs, g, bonus, gate, p)

    new_state_rwkv = s_ctx[:, None].astype(x_prompt.dtype)
    new_state_lru = l_ctx[:, None].astype(x_prompt.dtype)
    return (y_prompt, y_sample, new_state_rwkv, new_state_lru)
```

```python
import functools

import numpy as np
import jax
import jax.numpy as jnp
from jax import lax
from jax.experimental import pallas as pl
from jax.experimental.pallas import tpu as pltpu

F32 = jnp.float32
BF16 = jnp.bfloat16

D_MODEL = 1024
D_A = 512
D_B = 512
HEAD = 64
N_HEAD = 8
R_W = 64
R_A = 64
R_G = 128
D_FF = 4096
D_IN = 2944
GRID_W = 64
CONV_W = 4
LRU_C = 8.0
EPS = 1e-6
LNX_EPS = 64e-5

CHUNK = 64
GROUP = 128
HEADS_PER_GROUP = GROUP // HEAD
N_GROUP = D_A // GROUP
SUB = 8
LANE = 128
N_SLAB = D_A // LANE
HALO = 8
TOKEN_TILE = 256
CPT = TOKEN_TILE // CHUNK
OUT_TILE = 512
SCAN_TILE = 1024
MXU_TILE = 256
V7X_VMEM_BYTES = 64 * 1024 * 1024
VMEM_LIMIT = V7X_VMEM_BYTES - 4 * 1024 * 1024

_O_R, _O_K, _O_V, _O_XW, _O_XA, _O_XG, _O_XB, _O_GB = 0, 512, 1024, 1536, 1664, 1792, 1920, 2432

_M_STRICT, _M_INCL, _M_LEV0 = 0, 1, 2
_LEVELS = (1, 2, 4, 8, 16, 32)
_M_PER_DIR = 2 + len(_LEVELS)
_M_EYE = 2 * _M_PER_DIR


def _dot(a, b):
    return jnp.dot(a.astype(BF16), b.astype(BF16), preferred_element_type=F32)


def _dot_nt(a, b):
    return lax.dot_general(a.astype(BF16), b.astype(BF16), (((1,), (1,)), ((), ())),
                           preferred_element_type=F32)


def _dot_tn(a, b):
    return lax.dot_general(a.astype(BF16), b.astype(BF16), (((0,), (0,)), ((), ())),
                           preferred_element_type=F32)


def _strided_rows(ref, lead, base):
    return [ref[lead + (pl.ds(base + j, SUB, stride=SUB), slice(None))] for j in range(SUB)]


def _natural_rows(ref, lead, base, pieces):
    for j, piece in enumerate(pieces):
        ref[lead + (pl.ds(base + SUB * j, SUB), slice(None))] = piece
    return jnp.concatenate(_strided_rows(ref, lead, base), axis=0)


def _sublane_shift(x, steps, reverse, fill):
    sub = lax.broadcasted_iota(jnp.int32, x.shape, 0)
    if reverse:
        return jnp.where(sub < SUB - steps, pltpu.roll(x, SUB - steps, 0), fill)
    return jnp.where(sub >= steps, pltpu.roll(x, steps, 0), fill)


def _scan_affine(a, b, reverse):
    a, b = list(a), list(b)
    order = range(SUB - 2, -1, -1) if reverse else range(1, SUB)
    for j in order:
        p = j + 1 if reverse else j - 1
        b[j] = a[j] * b[p] + b[j]
        a[j] = a[j] * a[p]
    ta, tb = (a[0], b[0]) if reverse else (a[SUB - 1], b[SUB - 1])
    s = 1
    while s < SUB:
        tb = ta * _sublane_shift(tb, s, reverse, 0.0) + tb
        ta = ta * _sublane_shift(ta, s, reverse, 1.0)
        s *= 2
    ea = _sublane_shift(ta, 1, reverse, 1.0)
    eb = _sublane_shift(tb, 1, reverse, 0.0)
    return [x * ea for x in a], [x * eb + y for x, y in zip(a, b)]


def _scan_sum(x, reverse):
    x = list(x)
    order = range(SUB - 2, -1, -1) if reverse else range(1, SUB)
    for j in order:
        x[j] = x[j] + x[j + 1 if reverse else j - 1]
    t = x[0] if reverse else x[SUB - 1]
    s = 1
    while s < SUB:
        t = t + _sublane_shift(t, s, reverse, 0.0)
        s *= 2
    e = _sublane_shift(t, 1, reverse, 0.0)
    return [y + e for y in x]


def _sigmoid(x):
    return 0.5 * jnp.tanh(0.5 * x) + 0.5


def _blockdiag(x, bdm):
    xb = x.astype(BF16)
    return jnp.concatenate([xb] * HEADS_PER_GROUP, axis=0) * bdm


def _mod_kernel(c_ref, w_ref, b_ref, o_ref):
    c = c_ref[...]
    s = c * _sigmoid(c)
    o_ref[...] = _dot(s, w_ref[...]) + b_ref[...]


def _mod_call(c_all, w_mod, b_mod):
    m = c_all.shape[0]
    n = w_mod.shape[1]
    tn = 1536
    return pl.pallas_call(
        _mod_kernel,
        grid=(n // tn,),
        in_specs=[pl.BlockSpec((m, D_MODEL), lambda j: (0, 0)),
                  pl.BlockSpec((D_MODEL, tn), lambda j: (0, j)),
                  pl.BlockSpec((1, tn), lambda j: (0, j))],
        out_specs=pl.BlockSpec((m, tn), lambda j: (0, j)),
        out_shape=jax.ShapeDtypeStruct((m, n), F32),
        compiler_params=pltpu.CompilerParams(dimension_semantics=("parallel",),
                                             vmem_limit_bytes=VMEM_LIMIT),
        name="mod",
    )(c_all, w_mod, b_mod.reshape(1, n))


def _chunk_masks():
    t = np.arange(CHUNK)[:, None]
    s = (np.arange(GROUP) % CHUNK)[None, :]
    rows = []
    for d in (0, 1):
        before = (s < t) if d == 0 else (s > t)
        rows.append(before)
        rows.append(before | (s == t))
        for b in _LEVELS:
            same = (t // (2 * b)) == (s // (2 * b))
            if d == 0:
                rows.append(same & ((t // b) % 2 == 1) & ((s // b) % 2 == 0))
            else:
                rows.append(same & ((t // b) % 2 == 0) & ((s // b) % 2 == 1))
    rows.append(s == t)
    return np.stack(rows).astype(np.float32)


def _group_blockdiag_mask():
    i = np.arange(GROUP)
    return ((i[:, None] // HEAD) == (i[None, :] // HEAD)).astype(np.float32)


def _feat_kernel(has_pos, nt, n_side, *refs):
    if has_pos:
        x_ref, xp_ref, xn_ref, pos_ref, pp_ref, pn_ref = refs[:6]
        refs = refs[6:]
    else:
        x_ref, xp_ref, xn_ref = refs[:3]
        pos_ref = pp_ref = pn_ref = None
        refs = refs[3:]
    (mod_ref, gpre_ref, win_ref, wup_ref, w0_ref, aup_ref, a0_ref, gup_ref,
     kk_ref, ka_ref, rk_ref, seg_ref,
     convw_ref, convb_ref, wa_ref, ba_ref, wx_ref, bx_ref, lam_ref,
     masks_ref, bdm_ref) = refs[:21]
    side_in, refs = refs[21:21 + n_side], refs[21 + n_side:]
    (p_o, q_o, m_o, n_o, gt_o, ac_o, bs_o, bl_o, al_o, g_o, bonus_o, gate_o) = refs[:12]
    side_out, refs = refs[12:12 + n_side], refs[12 + n_side:]
    lw_s, lc_s, la_s, lb_s, lo_s = refs

    for src, dst in zip(side_in, side_out):
        dst[...] = src[...].astype(BF16)

    i = pl.program_id(0)
    tm = TOKEN_TILE
    mod = mod_ref[0]
    shift1, scale1 = mod[0:1], mod[1:2]

    gain1 = gpre_ref[...] * (1.0 + scale1)

    def normmod(xv):
        ms = jnp.mean(xv * xv, axis=-1, keepdims=True)
        return (xv * lax.rsqrt(ms + EPS)) * gain1 + shift1

    x = x_ref[0]
    halo = jnp.concatenate([xp_ref[0], xn_ref[0]], axis=0)
    if has_pos:
        x = x + pos_ref[...]
        halo = halo + jnp.concatenate([pp_ref[...], pn_ref[...]], axis=0)
    z = _dot(normmod(x), win_ref[...])
    zh = _dot(normmod(halo), win_ref[:, _O_XB:_O_XB + D_B])

    r = z[:, _O_R:_O_R + D_A]
    k = z[:, _O_K:_O_K + D_A]
    v = z[:, _O_V:_O_V + D_A]
    xw = z[:, _O_XW:_O_XW + 2 * R_W]
    xa = z[:, _O_XA:_O_XA + 2 * R_A]
    xg = z[:, _O_XG:_O_XG + R_G]

    g_o[0] = _dot(_sigmoid(xg), gup_ref[...])
    gb = z[:, _O_GB:_O_GB + D_B]
    c_gelu = float(np.sqrt(2.0 / np.pi))
    gate_o[0] = gb * (0.5 + 0.5 * jnp.tanh(gb * (c_gelu + (0.044715 * c_gelu) * (gb * gb))))
    half_c = 0.5 * float(np.exp(-0.5))
    lw2 = -half_c * jnp.tanh(w0_ref[...] + _dot(jnp.tanh(xw), wup_ref[...])) - half_c
    a2 = 0.5 * jnp.tanh(a0_ref[...] + _dot(xa, aup_ref[...])) + 0.5

    kks = k * kk_ref[...]
    ss = _dot(kks * kks, seg_ref[...])
    kk = kks * lax.rsqrt(jnp.maximum(ss, 1e-24))
    ka = ka_ref[...]
    k_fix, k_var = k * (1.0 - ka), k * ka
    kd2 = [k_fix + k_var * a2[:, d * D_A:(d + 1) * D_A] for d in (0, 1)]
    bonus_o[0] = _dot(r * (kd2[0] + kd2[1]) * rk_ref[...], seg_ref[...]) * v

    m_prev = jnp.where(i > 0, 1.0, 0.0)
    m_next = jnp.where(i < nt - 1, 1.0, 0.0)
    ext = jnp.concatenate([zh[:HALO] * m_prev, z[:, _O_XB:_O_XB + D_B], zh[HALO:] * m_next], axis=0)
    n_ext = tm + 2 * HALO
    xc = convb_ref[...]
    for j in range(CONV_W):
        sh = (2 - j) % n_ext
        tap = ext if sh == 0 else pltpu.roll(ext, sh, 0)
        xc = xc + tap[HALO:HALO + tm] * convw_ref[j:j + 1, :]
    for d in (0, 1):
        rg = jnp.concatenate([_dot(xc[:, g * GROUP:(g + 1) * GROUP], wa_ref[d, g]) for g in range(N_GROUP)], 1)
        ig = jnp.concatenate([_dot(xc[:, g * GROUP:(g + 1) * GROUP], wx_ref[d, g]) for g in range(N_GROUP)], 1)
        rg = 0.5 * jnp.tanh(rg + ba_ref[d:d + 1, :]) + 0.5
        ig = 0.5 * jnp.tanh(ig + bx_ref[d:d + 1, :]) + 0.5
        neg_log_a = rg * (LRU_C * jax.nn.softplus(-lam_ref[d:d + 1, :]))
        a_lru = jnp.exp(-neg_log_a)
        b_lru = jnp.sqrt(jnp.tanh(neg_log_a) * (a_lru * a_lru + 1.0)) * (ig * xc)
        for q in range(N_SLAB):
            la_s[d, q] = a_lru[:, q * LANE:(q + 1) * LANE]
            lb_s[d, q] = b_lru[:, q * LANE:(q + 1) * LANE]
    for c in range(CPT):
        base = c * CHUNK
        rs = slice(base, base + CHUNK)
        for q in range(N_SLAB):
            ls = slice(q * LANE, (q + 1) * LANE)
            bsum = None
            for d in (0, 1):
                acum, bcum = _scan_affine(_strided_rows(la_s, (d, q), base), _strided_rows(lb_s, (d, q), base),
                                          d == 1)
                ac_o[d, 0, rs, ls] = _natural_rows(lo_s, (d, q), base, acum)
                bsum = bcum if bsum is None else [x + y for x, y in zip(bsum, bcum)]
                bl_o[d, 0, c, :, ls] = bcum[0][0:1, :] if d == 1 else bcum[SUB - 1][SUB - 1:SUB, :]
                al_o[d, 0, c, :, ls] = acum[0][0:1, :] if d == 1 else acum[SUB - 1][SUB - 1:SUB, :]
            bs_o[0, rs, ls] = _natural_rows(lo_s, (2, q), base, bsum)

    bdm = bdm_ref[...]
    bd = lambda xv: _blockdiag(xv, bdm)
    bdot = lambda lhs, rhs: jnp.dot(lhs.astype(BF16), bd(rhs), preferred_element_type=F32)

    a_t, r_t, b_t, k_t, g_tot = [], [], [], [], []
    neg_kk = -kk
    for d in (0, 1):
        lw = lw2[:, d * D_A:(d + 1) * D_A]
        for q in range(N_SLAB):
            lw_s[d, q] = lw[:, q * LANE:(q + 1) * LANE]
        g_tot.append([])
        lc_rows = []
        for c in range(CPT):
            base = c * CHUNK
            blocks, totals = [], []
            for q in range(N_SLAB):
                pieces = _scan_sum(_strided_rows(lw_s, (d, q), base), d == 1)
                totals.append(pieces[0][0:1, :] if d == 1 else pieces[SUB - 1][SUB - 1:SUB, :])
                blocks.append(_natural_rows(lc_s, (d, q), base, pieces))
            lc_rows.append(jnp.concatenate(blocks, axis=1))
            g_tot[d].append(jnp.exp(jnp.concatenate(totals, axis=1)))
            gt_o[d, 0, c] = g_tot[d][c]
        lc = jnp.concatenate(lc_rows, axis=0)
        e_neg = jnp.exp(-lc)
        a_t.append(neg_kk * jnp.exp(lc - lw))
        r_t.append(r * jnp.exp(lc))
        b_t.append(kk * a2[:, d * D_A:(d + 1) * D_A] * e_neg)
        k_t.append(kd2[d] * e_neg)

    combos = [(d, c, g) for d in (0, 1) for c in range(CPT) for g in range(N_GROUP)]

    def cut(arr, c, g):
        return arr[c * CHUNK:(c + 1) * CHUNK, g * GROUP:(g + 1) * GROUP]

    sc = {}
    for key in combos:
        d, c, g = key
        ar = jnp.concatenate([cut(a_t[d], c, g), cut(r_t[d], c, g)], axis=0)
        rhs = jnp.concatenate([bd(cut(b_t[d], c, g)), bd(cut(k_t[d], c, g))], axis=0)
        sc[key] = _dot_nt(ar, rhs)
    n_ab, n_ak, n_rb, n_rk, tinv = {}, {}, {}, {}, {}
    for key in combos:
        m0 = key[0] * _M_PER_DIR
        n_ab[key] = sc[key][:CHUNK, :GROUP] * masks_ref[m0 + _M_STRICT]
        n_ak[key] = sc[key][:CHUNK, GROUP:] * masks_ref[m0 + _M_STRICT]
        n_rb[key] = sc[key][CHUNK:, :GROUP] * masks_ref[m0 + _M_INCL]
        n_rk[key] = sc[key][CHUNK:, GROUP:] * masks_ref[m0 + _M_INCL]
        tinv[key] = masks_ref[_M_EYE] + n_ab[key] * masks_ref[m0 + _M_LEV0]
    for li in range(1, len(_LEVELS)):
        pm = {}
        for key in combos:
            pm[key] = bdot(n_ab[key] * masks_ref[key[0] * _M_PER_DIR + _M_LEV0 + li], tinv[key])
        for key in combos:
            tinv[key] = tinv[key] + bdot(tinv[key], pm[key])

    kv = {key: bdot(jnp.concatenate([n_ak[key], n_rk[key]], axis=0), cut(v, key[1], key[2])) for key in combos}
    gm = {key: bdot(n_rb[key], tinv[key]) for key in combos}
    wp = {}
    for key in combos:
        d, c, g = key
        lhs = jnp.concatenate([tinv[key], gm[key]], axis=0)
        rhs = jnp.concatenate([bd(cut(a_t[d], c, g)), bd(kv[key][:CHUNK])], axis=1)
        wp[key] = jnp.dot(lhs.astype(BF16), rhs, preferred_element_type=F32)

    lane = lax.broadcasted_iota(jnp.int32, (CHUNK, GROUP), 1)

    def fold(full):
        out = full[(HEADS_PER_GROUP - 1) * HEAD:]
        for h in range(HEADS_PER_GROUP - 2, -1, -1):
            out = jnp.where(lane < (h + 1) * HEAD, full[h * HEAD:(h + 1) * HEAD], out)
        return out

    for key in combos:
        d, c, g = key
        rs = slice(c * CHUNK, (c + 1) * CHUNK)
        ls = slice(g * GROUP, (g + 1) * GROUP)
        w_, u0 = wp[key][:CHUNK, :GROUP], wp[key][:CHUNK, GROUP:]
        vg = cut(v, c, g)
        gt = g_tot[d][c][:, ls]
        bh, kh = cut(b_t[d], c, g) * gt, cut(k_t[d], c, g) * gt
        p_o[d, 0, rs, ls] = (cut(r_t[d], c, g) + wp[key][CHUNK:, :GROUP]).astype(BF16)
        m_o[d, 0, rs, ls] = fold(_dot_tn(w_, bh)).astype(BF16)
        n_o[d, 0, rs, ls] = fold(_dot_tn(jnp.concatenate([u0, vg], axis=0), jnp.concatenate([bh, kh], axis=0)))
        if d == 1:
            other = (0, c, g)
            q_o[0, rs, ls] = ((wp[other][CHUNK:, GROUP:] + kv[other][CHUNK:])
                              + (wp[key][CHUNK:, GROUP:] + kv[key][CHUNK:]))


def _feat_call(x, pos, mod, mod_row, p, side_casts=()):
    bsz, t, _ = x.shape
    tm = TOKEN_TILE
    nt = t // tm
    nc = t // CHUNK
    hpt = tm // HALO
    has_pos = pos is not None

    def const(shape):
        return pl.BlockSpec(shape, lambda i, b: (0,) * len(shape))

    tok = lambda width: pl.BlockSpec((1, tm, width), lambda i, b: (b, i, 0))
    prev_i = lambda i: jnp.maximum(i * hpt - 1, 0)
    next_i = lambda i: jnp.minimum((i + 1) * hpt, t // HALO - 1)
    in_specs = [tok(D_MODEL),
                pl.BlockSpec((1, HALO, D_MODEL), lambda i, b: (b, prev_i(i), 0)),
                pl.BlockSpec((1, HALO, D_MODEL), lambda i, b: (b, next_i(i), 0))]
    args = [x, x, x]
    if has_pos:
        in_specs += [pl.BlockSpec((tm, D_MODEL), lambda i, b: (i, 0)),
                     pl.BlockSpec((HALO, D_MODEL), lambda i, b: (prev_i(i), 0)),
                     pl.BlockSpec((HALO, D_MODEL), lambda i, b: (next_i(i), 0))]
        args += [pos, pos, pos]
    n_masks = 2 * _M_PER_DIR + 1
    in_specs += [
        pl.BlockSpec((1, 6, D_MODEL), lambda i, b: (mod_row(b), 0, 0)),
        const((1, D_MODEL)), const((D_MODEL, D_IN)),
        const((2 * R_W, 2 * D_A)), const((1, 2 * D_A)),
        const((2 * R_A, 2 * D_A)), const((1, 2 * D_A)),
        const((R_G, D_A)),
        const((1, D_A)), const((1, D_A)), const((1, D_A)), const((D_A, D_A)),
        const((CONV_W, D_B)), const((1, D_B)),
        const((2, N_GROUP, GROUP, GROUP)), const((2, D_B)),
        const((2, N_GROUP, GROUP, GROUP)), const((2, D_B)), const((2, D_B)),
        const((n_masks, CHUNK, GROUP)), const((GROUP, GROUP)),
    ]
    args += [mod, p["g_pre_mix"], p["w_in"], p["wup_bd"], p["w0"], p["aup_bd"], p["a0"], p["g_up"],
             p["k_k"], p["k_a"], p["r_k"], p["seg_ones"],
             p["conv_w"], p["conv_b"], p["wa_bd"], p["ba"], p["wx_bd"], p["bx"], p["lam"],
             p["chunk_masks"], p["bdm_bf16"]]
    side_specs, side_shapes = [], []
    for w, axis in side_casts:
        block = list(w.shape)
        block[axis] //= nt * bsz
        assert block[axis] * nt * bsz == w.shape[axis] and block[axis] % LANE == 0
        index = (lambda i, b: (i * bsz + b, 0)) if axis == 0 else (lambda i, b: (0, i * bsz + b))
        side_specs.append(pl.BlockSpec(tuple(block), index))
        side_shapes.append(jax.ShapeDtypeStruct(w.shape, BF16))
        args.append(w)
    in_specs += side_specs
    tok_shape = jax.ShapeDtypeStruct((bsz, t, D_A), F32)
    dir_shape = jax.ShapeDtypeStruct((2, bsz, t, D_A), F32)
    row_shape = jax.ShapeDtypeStruct((2, bsz, nc, 1, D_A), F32)
    mxu_shape = jax.ShapeDtypeStruct((2, bsz, t, D_A), BF16)
    dir_spec = pl.BlockSpec((2, 1, tm, D_A), lambda i, b: (0, b, i, 0))
    row_spec = pl.BlockSpec((2, 1, CPT, 1, D_A), lambda i, b: (0, b, i, 0, 0))
    out_shape = [mxu_shape, tok_shape, mxu_shape, dir_shape, row_shape, dir_shape, tok_shape, row_shape, row_shape,
                 tok_shape, tok_shape, tok_shape]
    out_specs = [dir_spec, tok(D_A), dir_spec, dir_spec, row_spec, dir_spec, tok(D_B), row_spec, row_spec,
                 tok(D_A), tok(D_A), tok(D_B)]
    out_shape += side_shapes
    out_specs += side_specs
    return pl.pallas_call(
        functools.partial(_feat_kernel, has_pos, nt, len(side_casts)),
        grid=(nt, bsz),
        in_specs=in_specs,
        out_specs=out_specs,
        out_shape=out_shape,
        scratch_shapes=[pltpu.VMEM((2, N_SLAB, tm, LANE), F32)] * 4 + [pltpu.VMEM((3, N_SLAB, tm, LANE), F32)],
        compiler_params=pltpu.CompilerParams(dimension_semantics=("parallel", "parallel"),
                                             vmem_limit_bytes=VMEM_LIMIT),
        name="feat",
    )(*args)


def _scan_kernel(nt, tile, has_init, *refs):
    (pf_ref, mf_ref, nf_ref, gtf_ref, alf_ref, blf_ref,
     pb_ref, mb_ref, nb_ref, gtb_ref, alb_ref, blb_ref) = refs[:12]
    if has_init:
        s0_ref, l0_ref = refs[12:14]
    bdm_ref, y_ref, hc_ref, s_ref, hl_ref, st_s = refs[-6:]

    i = pl.program_id(1)
    cpt = tile // CHUNK

    @pl.when(i == 0)
    def _init():
        y_ref[...] = jnp.zeros_like(y_ref)
        if has_init:
            for d in (0, 1):
                st_s[d] = jnp.concatenate([s0_ref[0, d, h] for h in range(N_HEAD)], axis=1)
            hl_ref[...] = l0_ref[...]
        else:
            st_s[...] = jnp.zeros_like(st_s)
            hl_ref[...] = jnp.zeros_like(hl_ref)

    bdm = bdm_ref[...]
    per_dir = ((0, pf_ref, mf_ref, nf_ref, gtf_ref, alf_ref, blf_ref),
               (1, pb_ref, mb_ref, nb_ref, gtb_ref, alb_ref, blb_ref))
    tiles = (i, nt - 1 - i)
    state = [st_s[d] for d in (0, 1)]
    h0 = [hl_ref[0, d:d + 1, :] for d in (0, 1)]
    for cc in range(cpt):
        for d, p_ref, m_ref, n_ref, gt_ref, al_ref, bl_ref in per_dir:
            c = cc if d == 0 else cpt - 1 - cc
            rs = slice(c * CHUNK, (c + 1) * CHUNK)
            rows = pl.ds(pl.multiple_of(tiles[d] * tile + c * CHUNK, CHUNK), CHUNK)
            gt = gt_ref[0, 0, c]
            y_parts, s_parts = [], []
            for g in range(N_GROUP):
                ls = slice(g * GROUP, (g + 1) * GROUP)
                sg = state[d][:, ls]
                y_parts.append(_dot_nt(p_ref[0, 0, rs, ls], _blockdiag(sg, bdm)))
                s_parts.append(sg * gt[:, ls]
                               + jnp.dot(sg.astype(BF16), _blockdiag(m_ref[0, 0, rs, ls], bdm),
                                         preferred_element_type=F32)
                               + n_ref[0, 0, rs, ls])
            state[d] = jnp.concatenate(s_parts, axis=1)
            y_ref[0, rows, :] += jnp.concatenate(y_parts, axis=1)
            hc_ref[d, 0, tiles[d] * cpt + c] = h0[d]
            h0[d] = bl_ref[0, 0, c] + al_ref[0, 0, c] * h0[d]
    for d in (0, 1):
        st_s[d] = state[d]
        hl_ref[0, d:d + 1, :] = h0[d]

    @pl.when(i == nt - 1)
    def _final():
        for d in (0, 1):
            for h in range(N_HEAD):
                s_ref[0, d, h] = st_s[d, :, h * HEAD:(h + 1) * HEAD]


def _scan_call(feats, s0, l0, p):
    pm, mm, nm, gt, al, bl = feats
    _, bsz, t, _ = pm.shape
    tile = min(SCAN_TILE, t)
    nt = t // tile
    nc = t // CHUNK

    def dir_specs(d, tile_of):
        big = pl.BlockSpec((1, 1, tile, D_A), lambda b, i: (d, b, tile_of(i), 0))
        small = pl.BlockSpec((1, 1, tile // CHUNK, 1, D_A), lambda b, i: (d, b, tile_of(i), 0, 0))
        return big, small

    big_f, small_f = dir_specs(0, lambda i: i)
    big_b, small_b = dir_specs(1, lambda i: nt - 1 - i)
    has_init = s0 is not None
    in_specs = [big_f, big_f, big_f, small_f, small_f, small_f,
                big_b, big_b, big_b, small_b, small_b, small_b]
    args = [pm, mm, nm, gt, al, bl, pm, mm, nm, gt, al, bl]
    if has_init:
        in_specs += [pl.BlockSpec((1, 2, N_HEAD, HEAD, HEAD), lambda b, i: (b, 0, 0, 0, 0)),
                     pl.BlockSpec((1, 2, D_B), lambda b, i: (b, 0, 0))]
        args += [s0, l0]
    in_specs.append(pl.BlockSpec((GROUP, GROUP), lambda b, i: (0, 0)))
    args.append(p["bdm_bf16"])
    out_specs = [pl.BlockSpec((1, t, D_A), lambda b, i: (b, 0, 0)),
                 pl.BlockSpec((2, 1, nc, 1, D_B), lambda b, i: (0, b, 0, 0, 0)),
                 pl.BlockSpec((1, 2, N_HEAD, HEAD, HEAD), lambda b, i: (b, 0, 0, 0, 0)),
                 pl.BlockSpec((1, 2, D_B), lambda b, i: (b, 0, 0))]
    out_shape = [jax.ShapeDtypeStruct((bsz, t, D_A), F32),
                 jax.ShapeDtypeStruct((2, bsz, nc, 1, D_B), F32),
                 jax.ShapeDtypeStruct((bsz, 2, N_HEAD, HEAD, HEAD), F32),
                 jax.ShapeDtypeStruct((bsz, 2, D_B), F32)]
    return pl.pallas_call(
        functools.partial(_scan_kernel, nt, tile, has_init),
        grid=(bsz, nt),
        in_specs=in_specs,
        out_specs=out_specs,
        out_shape=out_shape,
        scratch_shapes=[pltpu.VMEM((2, HEAD, D_A), F32)],
        compiler_params=pltpu.CompilerParams(dimension_semantics=("parallel", "arbitrary"),
                                             vmem_limit_bytes=VMEM_LIMIT),
        name="scan",
    )(*args)


def _out_kernel(has_pos, *refs):
    if has_pos:
        x_ref, pos_ref = refs[0], refs[1]
        refs = refs[2:]
    else:
        x_ref, pos_ref = refs[0], None
        refs = refs[1:]
    (y_ref, q_ref, ac_ref, hc_ref, bs_ref, g_ref, bonus_ref, gate_ref, mod_ref,
     avg_ref, lnxg_ref, lnxb_ref, wout_ref, gpost_ref, gpre2_ref, w1_ref, w2_ref, gpost2_ref,
     o_ref) = refs

    x = x_ref[0]
    if has_pos:
        x = x + pos_ref[...]
    mod = mod_ref[0]
    gate1, shift2, scale2, gate2 = mod[2:3], mod[3:4], mod[4:5], mod[5:6]

    y = y_ref[0] + q_ref[0]
    avg = avg_ref[...]

    def head_mean(a):
        return jnp.concatenate([_dot(a[:, t * MXU_TILE:(t + 1) * MXU_TILE], avg)
                                for t in range(D_A // MXU_TILE)], axis=1)

    yc = y - head_mean(y)
    var = head_mean(yc * yc)
    yn = yc * lax.rsqrt(var + LNX_EPS) * lnxg_ref[...] + lnxb_ref[...]
    out_a = (yn + bonus_ref[0]) * g_ref[0]
    hs = jnp.concatenate(
        [bs_ref[0, c * CHUNK:(c + 1) * CHUNK, :]
         + ac_ref[0, 0, c * CHUNK:(c + 1) * CHUNK, :] * hc_ref[0, 0, c]
         + ac_ref[1, 0, c * CHUNK:(c + 1) * CHUNK, :] * hc_ref[1, 0, c]
         for c in range(x_ref.shape[1] // CHUNK)], axis=0)
    out_b = hs * gate_ref[0]
    mix = _dot(jnp.concatenate([out_a, out_b], axis=1), wout_ref[...])
    ms = jnp.mean(mix * mix, axis=-1, keepdims=True)
    x = x + (mix * lax.rsqrt(ms + EPS)) * (gate1 * gpost_ref[...])

    ms = jnp.mean(x * x, axis=-1, keepdims=True)
    h = (x * lax.rsqrt(ms + EPS)) * (gpre2_ref[...] * (1.0 + scale2)) + shift2
    f = _dot(h, w1_ref[...])
    f = jnp.square(jnp.maximum(f, 0.0))
    f = _dot(f, w2_ref[...])
    ms = jnp.mean(f * f, axis=-1, keepdims=True)
    o_ref[0] = x + (f * lax.rsqrt(ms + EPS)) * (gate2 * gpost2_ref[...])


def _out_call(x, pos, mod, mod_row, y, q, ac, hc, bs, g, bonus, gate, p):
    bsz, t, _ = x.shape
    tm = min(OUT_TILE, t)
    nt = t // tm
    has_pos = pos is not None

    def const(shape):
        return pl.BlockSpec(shape, lambda i, b: (0,) * len(shape))

    tok = lambda width: pl.BlockSpec((1, tm, width), lambda i, b: (b, i, 0))
    in_specs = [tok(D_MODEL)]
    args = [x]
    if has_pos:
        in_specs.append(pl.BlockSpec((tm, D_MODEL), lambda i, b: (i, 0)))
        args.append(pos)
    in_specs += [tok(D_A), tok(D_A),
                 pl.BlockSpec((2, 1, tm, D_B), lambda i, b: (0, b, i, 0)),
                 pl.BlockSpec((2, 1, tm // CHUNK, 1, D_B), lambda i, b: (0, b, i, 0, 0))]
    in_specs += [tok(D_A)] * 4
    in_specs += [
        pl.BlockSpec((1, 6, D_MODEL), lambda i, b: (mod_row(b), 0, 0)),
        const((MXU_TILE, MXU_TILE)), const((1, D_A)), const((1, D_A)),
        const((D_MODEL, D_MODEL)), const((1, D_MODEL)), const((1, D_MODEL)),
        const((D_MODEL, D_FF)), const((D_FF, D_MODEL)), const((1, D_MODEL)),
    ]
    args += [y, q, ac, hc, bs, g, bonus, gate, mod,
             p["seg_avg"], p["lnx_g"], p["lnx_b"], p["w_out"], p["g_post_mix"], p["g_pre_mlp"],
             p["w_mlp1"], p["w_mlp2"], p["g_post_mlp"]]
    return pl.pallas_call(
        functools.partial(_out_kernel, has_pos),
        grid=(nt, bsz),
        in_specs=in_specs,
        out_specs=tok(D_MODEL),
        out_shape=jax.ShapeDtypeStruct((bsz, t, D_MODEL), F32),
        compiler_params=pltpu.CompilerParams(dimension_semantics=("parallel", "parallel"),
                                             vmem_limit_bytes=VMEM_LIMIT),
        name="out",
    )(*args)


def _sincos_1d(pos, dim):
    omega = 1.0 / (10000.0 ** (jnp.arange(dim // 2, dtype=F32) / (dim // 2)))
    ang = pos.astype(F32)[:, None] * omega[None, :]
    return jnp.concatenate([jnp.sin(ang), jnp.cos(ang)], axis=-1)


def _grid_pos_embed(n_tokens):
    rows = n_tokens // GRID_W
    half = D_MODEL // 2
    e_row = _sincos_1d(jnp.arange(rows), half)
    e_col = _sincos_1d(jnp.arange(GRID_W), half)
    emb = jnp.concatenate([jnp.broadcast_to(e_row[:, None, :], (rows, GRID_W, half)),
                           jnp.broadcast_to(e_col[None, :, :], (rows, GRID_W, half))], axis=-1)
    return emb.reshape(rows * GRID_W, D_MODEL)


def _blockdiag_pairs(w):
    z = jnp.zeros_like(w[0])
    return jnp.concatenate([jnp.concatenate([w[0], z], axis=1),
                            jnp.concatenate([z, w[1]], axis=1)], axis=0)


def _heads_to_blockdiag(w):
    lead = w.shape[:-3]
    w = w.reshape(lead + (N_GROUP, HEADS_PER_GROUP, HEAD, HEAD))
    eye = jnp.eye(HEADS_PER_GROUP, dtype=w.dtype)
    bd = jnp.einsum('...ghab,hj->...ghajb', w, eye)
    return bd.reshape(lead + (N_GROUP, GROUP, GROUP))


def kernel(x_prompt, x_sample, c, state_rwkv, state_lru, c_ctx, w_mod, b_mod, g_pre_mix, g_post_mix,
           g_pre_mlp, g_post_mlp, w_in, rwkv_w0, rwkv_w_up, rwkv_a0, rwkv_a_up, rwkv_g_up, rwkv_k_k,
           rwkv_k_a, rwkv_r_k, rwkv_lnx_g, rwkv_lnx_b, lru_conv_w, lru_conv_b, lru_wa, lru_ba, lru_wx,
           lru_bx, lru_lambda, w_out, w_mlp1, w_mlp2):
    n_ctx = x_prompt.shape[0]
    n_lat = x_sample.shape[0]
    l = 0
    seg = _group_blockdiag_mask()
    seg512 = np.kron(np.eye(N_GROUP, dtype=np.float32), seg)
    p = {
        "g_pre_mix": g_pre_mix[l][None], "g_post_mix": g_post_mix[l][None],
        "g_pre_mlp": g_pre_mlp[l][None], "g_post_mlp": g_post_mlp[l][None],
        "w_in": w_in[l].astype(BF16), "w_out": w_out[l].astype(BF16),
        "w0": 0.5 * rwkv_w0[l].reshape(1, 2 * D_A), "a0": 0.5 * rwkv_a0[l].reshape(1, 2 * D_A),
        "wup_bd": (0.5 * _blockdiag_pairs(rwkv_w_up[l])).astype(BF16),
        "aup_bd": (0.5 * _blockdiag_pairs(rwkv_a_up[l])).astype(BF16),
        "g_up": rwkv_g_up[l].astype(BF16),
        "k_k": rwkv_k_k[l][None], "k_a": rwkv_k_a[l][None], "r_k": rwkv_r_k[l].reshape(1, D_A),
        "lnx_g": rwkv_lnx_g[l][None], "lnx_b": rwkv_lnx_b[l][None],
        "conv_w": lru_conv_w[l], "conv_b": lru_conv_b[l][None],
        "wa_bd": (0.5 * _heads_to_blockdiag(lru_wa[l])).astype(BF16), "ba": 0.5 * lru_ba[l],
        "wx_bd": (0.5 * _heads_to_blockdiag(lru_wx[l])).astype(BF16), "bx": 0.5 * lru_bx[l],
        "lam": lru_lambda[l],
        "seg_ones": jnp.asarray(seg512, BF16),
        "seg_avg": jnp.asarray(np.kron(np.eye(MXU_TILE // GROUP, dtype=np.float32), seg) / HEAD, BF16),
        "chunk_masks": jnp.asarray(_chunk_masks()),
        "bdm_bf16": jnp.asarray(seg, BF16),
    }

    m_rows = 16
    c_all = jnp.concatenate([c_ctx[None], c, jnp.zeros((m_rows - 1 - n_lat, D_MODEL), F32)], axis=0)
    mod = _mod_call(c_all, w_mod[l], b_mod[l]).reshape(m_rows, 6, D_MODEL)

    pos = _grid_pos_embed(x_sample.shape[1]).astype(x_sample.dtype)
    ctx_row = lambda b: 0
    lat_row = lambda b: b + 1

    (pm, q, mm, nm, gt, ac, bs, bl, al, g, bonus, gate, p["w_mlp1"], p["w_mlp2"]) = _feat_call(
        x_prompt, None, mod, ctx_row, p, side_casts=((w_mlp1[l], 1), (w_mlp2[l], 0)))
    y, hc, s_ctx, l_ctx = _scan_call((pm, mm, nm, gt, al, bl), None, None, p)
    y_prompt = _out_call(x_prompt, None, mod, ctx_row, y, q, ac, hc, bs, g, bonus, gate, p)

    pm, q, mm, nm, gt, ac, bs, bl, al, g, bonus, gate = _feat_call(x_sample, pos, mod, lat_row, p)
    y, hc, _, _ = _scan_call((pm, mm, nm, gt, al, bl), state_rwkv[:, l], state_lru[:, l], p)
    y_sample = _out_call(x_sample, pos, mod, lat_row, y, q, ac, hc, bs, g, bonus, gate, p)

    new_state_rwkv = s_ctx[:, None].astype(x_prompt.dtype)
    new_state_lru = l_ctx[:, None].astype(x_prompt.dtype)
    return (y_prompt, y_sample, new_state_rwkv, new_state_lru)
```
